```python
import math
import jax, jax.numpy as jnp
from jax import lax
import numpy as np

D_MODEL = 1024
BATCH = 8
SEQ = 2048
DEPTH = 2
DEC_BATCH = 128
DEC_SEQ = 4
PAST_LEN = 16384
PAGE_SIZE = 128

N_MEM = 256
RET_HEADS = 4
RET_DK = 128
RET_DV = 128
RET_W = RET_HEADS * RET_DV
RET_CHUNK = 128
CONV_W = 256
CONV_K = 31
XA_HEADS = 4
XA_DH = 64
XA_W = XA_HEADS * XA_DH
D_MIX = RET_W + CONV_W + XA_W
ROPE_BASE = 10000.0
EPS = 1e-6
SPLIT_SIZES = (RET_HEADS * RET_DK, RET_HEADS * RET_DK, RET_W, RET_W,
               CONV_W, CONV_W, CONV_W, XA_W, XA_W)
D_IN = sum(SPLIT_SIZES)

kernel_name = "hymba_retention_conformer_xattn_step"

F32 = jnp.float32


def rmsnorm(x, g):
    xf = x.astype(F32)
    y = xf * lax.rsqrt(jnp.mean(xf * xf, axis=-1, keepdims=True) + EPS)
    return (y * g.astype(F32)).astype(x.dtype)


def standardize(x):
    xf = x.astype(F32)
    mu = jnp.mean(xf, axis=-1, keepdims=True)
    var = jnp.mean(jnp.square(xf - mu), axis=-1, keepdims=True)
    return (xf - mu) * lax.rsqrt(var + EPS)


def rope(x, pos):
    half = x.shape[-1] // 2
    inv = ROPE_BASE ** (-jnp.arange(half, dtype=F32) / half)
    ang = pos.astype(F32)[:, None] * inv[None, :]
    cos = jnp.cos(ang)[None, :, None, :]
    sin = jnp.sin(ang)[None, :, None, :]
    xf = x.astype(F32)
    x1, x2 = xf[..., :half], xf[..., half:]
    return jnp.concatenate([x1 * cos - x2 * sin, x2 * cos + x1 * sin], axis=-1)


def log_gamma():
    return jnp.log(1.0 - jnp.exp2(-5.0 - jnp.arange(RET_HEADS, dtype=F32)))


def retention(q, k, v, s0):
    B, L, H, DK = q.shape
    DV = v.shape[-1]
    C = RET_CHUNK if L % RET_CHUNK == 0 else L
    NC = L // C
    lg = log_gamma()
    q = q.reshape(B, NC, C, H, DK)
    k = k.reshape(B, NC, C, H, DK)
    v = v.reshape(B, NC, C, H, DV)
    idx = jnp.arange(C, dtype=F32)
    diff = idx[:, None] - idx[None, :]
    decay = jnp.where(diff[None] >= 0,
                      jnp.exp(jnp.maximum(diff, 0.0)[None] * lg[:, None, None]), 0.0)
    scores = jnp.einsum('bnihd,bnjhd->bnhij', q, k) * decay[None, None]
    inner = jnp.einsum('bnhij,bnjhe->bnihe', scores, v)
    w_k = jnp.exp((C - 1.0 - idx)[:, None] * lg[None, :])
    kv = jnp.einsum('bnjhd,jh,bnjhe->bnhde', k, w_k, v)
    chunk_decay = jnp.exp(C * lg)[None, :, None, None]

    def step(s, kv_c):
        return chunk_decay * s + kv_c, s

    s_last, s_prev = lax.scan(step, s0, jnp.moveaxis(kv, 1, 0))
    s_prev = jnp.moveaxis(s_prev, 0, 1)
    w_q = jnp.exp((idx + 1.0)[:, None] * lg[None, :])
    cross = jnp.einsum('bnihd,bnhde->bnihe', q, s_prev) * w_q[None, None, :, :, None]
    return (inner + cross).reshape(B, L, H, DV), s_last


def mem_kv(mem, g, w_mk, w_mv):
    B = mem.shape[0]
    m = rmsnorm(mem, g)
    return ((m @ w_mk).reshape(B, N_MEM, XA_HEADS, XA_DH),
            (m @ w_mv).reshape(B, N_MEM, XA_HEADS, XA_DH))


def layer_mix(h, pos, s_ret, conv_buf, mk, mv, w_in, ret_gn, conv_w, conv_b,
              conv_ln_g, conv_ln_b, xa_norm):
    B, L, _ = h.shape
    z = h @ w_in
    bounds, acc = [], 0
    for s in SPLIT_SIZES[:-1]:
        acc += s
        bounds.append(acc)
    rq, rk, rv, rg, ca, cb, cg, xq, xg = jnp.split(z, bounds, axis=-1)

    q = rope(rq.reshape(B, L, RET_HEADS, RET_DK), pos)
    k = rope(rk.reshape(B, L, RET_HEADS, RET_DK), pos) * (RET_DK ** -0.5)
    v = rv.reshape(B, L, RET_HEADS, RET_DV).astype(F32)
    o, s_new = retention(q, k, v, s_ret.astype(F32))
    o = standardize(o) * ret_gn.reshape(RET_HEADS, RET_DV).astype(F32)
    o = o.reshape(B, L, RET_W).astype(h.dtype) * jax.nn.silu(rg)

    u = ca * jax.nn.sigmoid(cb)
    ext = jnp.concatenate([conv_buf.astype(u.dtype), u], axis=1)
    c = lax.conv_general_dilated(ext, conv_w[:, None, :].astype(u.dtype), (1,), 'VALID',
                                 dimension_numbers=('NWC', 'WIO', 'NWC'),
                                 feature_group_count=CONV_W) + conv_b
    new_buf = ext[:, -(CONV_K - 1):]
    c = standardize(c) * conv_ln_g.astype(F32) + conv_ln_b.astype(F32)
    c = jax.nn.silu(c).astype(h.dtype) * jax.nn.silu(cg)

    qx = xq.reshape(B, L, XA_HEADS, XA_DH)
    sc = jnp.einsum('blhd,bmhd->bhlm', qx, mk).astype(F32) * (XA_DH ** -0.5)
    p = jax.nn.softmax(sc, axis=-1).astype(mv.dtype)
    a = jnp.einsum('bhlm,bmhd->blhd', p, mv)
    a = rmsnorm(a, xa_norm.reshape(XA_HEADS, XA_DH)).reshape(B, L, XA_W) * jax.nn.silu(xg)

    return jnp.concatenate([o, c, a], axis=-1), s_new, new_buf


def setup_inputs(seed: int = 0) -> dict:
    key = jax.random.key(seed)
    ks = jax.random.split(key, 24)
    nrm = lambda k, shp, s: jax.random.normal(k, shp, F32) * s
    return {
        "x_prompt": nrm(ks[0], (BATCH, SEQ, D_MODEL), 1.0),
        "x_sample": nrm(ks[1], (DEC_BATCH, DEC_SEQ, D_MODEL), 1.0),
        "mem_prompt": nrm(ks[2], (BATCH, N_MEM, D_MODEL), 1.0),
        "state_ret": nrm(ks[3], (DEPTH, DEC_BATCH, RET_HEADS, RET_DK, RET_DV), 1.0),
        "state_conv": nrm(ks[4], (DEPTH, DEC_BATCH, CONV_K - 1, CONV_W), 0.5),
        "cache_mem_k": nrm(ks[5], (DEPTH, DEC_BATCH, N_MEM, XA_HEADS, XA_DH), 1.0),
        "cache_mem_v": nrm(ks[6], (DEPTH, DEC_BATCH, N_MEM, XA_HEADS, XA_DH), 1.0),
        "norm_g": 1.0 + nrm(ks[7], (DEPTH, D_MODEL), 0.01),
        "w_in": nrm(ks[8], (DEPTH, D_MODEL, D_IN), D_MODEL ** -0.5),
        "ret_gn_g": 1.0 + nrm(ks[9], (DEPTH, RET_W), 0.01),
        "conv_w": nrm(ks[10], (DEPTH, CONV_K, CONV_W), CONV_K ** -0.5),
        "conv_b": nrm(ks[11], (DEPTH, CONV_W), 0.01),
        "conv_ln_g": 1.0 + nrm(ks[12], (DEPTH, CONV_W), 0.01),
        "conv_ln_b": nrm(ks[13], (DEPTH, CONV_W), 0.01),
        "xa_norm_g": 1.0 + nrm(ks[14], (DEPTH, XA_W), 0.01),
        "mem_norm_g": 1.0 + nrm(ks[15], (DEPTH, D_MODEL), 0.01),
        "w_mk": nrm(ks[16], (DEPTH, D_MODEL, XA_W), D_MODEL ** -0.5),
        "w_mv": nrm(ks[17], (DEPTH, D_MODEL, XA_W), D_MODEL ** -0.5),
        "w_out": nrm(ks[18], (DEPTH, D_MIX, D_MODEL), D_MIX ** -0.5),
        "final_norm_g": 1.0 + nrm(ks[19], (D_MODEL,), 0.01),
    }


def reference(x_prompt, x_sample, mem_prompt, state_ret, state_conv, cache_mem_k, cache_mem_v,
              norm_g, w_in, ret_gn_g, conv_w, conv_b, conv_ln_g, conv_ln_b, xa_norm_g,
              mem_norm_g, w_mk, w_mv, w_out, final_norm_g):
    pos_p = jnp.arange(SEQ)
    pos_s = PAST_LEN + jnp.arange(DEC_SEQ)
    hp, hs = x_prompt, x_sample
    p_ret, p_conv, p_mk, p_mv, s_ret, s_conv = [], [], [], [], [], []
    zero_ret = jnp.zeros((BATCH, RET_HEADS, RET_DK, RET_DV), F32)
    zero_conv = jnp.zeros((BATCH, CONV_K - 1, CONV_W), x_prompt.dtype)
    for l in range(DEPTH):
        mk, mv = mem_kv(mem_prompt, mem_norm_g[l], w_mk[l], w_mv[l])
        mix, sr, sc = layer_mix(rmsnorm(hp, norm_g[l]), pos_p, zero_ret, zero_conv, mk, mv,
                                w_in[l], ret_gn_g[l], conv_w[l], conv_b[l], conv_ln_g[l],
                                conv_ln_b[l], xa_norm_g[l])
        hp = hp + mix @ w_out[l]
        p_ret.append(sr.astype(x_prompt.dtype))
        p_conv.append(sc)
        p_mk.append(mk)
        p_mv.append(mv)
        mix, sr, sc = layer_mix(rmsnorm(hs, norm_g[l]), pos_s, state_ret[l], state_conv[l],
                                cache_mem_k[l], cache_mem_v[l],
                                w_in[l], ret_gn_g[l], conv_w[l], conv_b[l], conv_ln_g[l],
                                conv_ln_b[l], xa_norm_g[l])
        hs = hs + mix @ w_out[l]
        s_ret.append(sr.astype(state_ret.dtype))
        s_conv.append(sc.astype(state_conv.dtype))
    y_prompt = rmsnorm(hp, final_norm_g)
    y_sample = rmsnorm(hs, final_norm_g)
    return (y_prompt, y_sample, jnp.stack(p_ret), jnp.stack(p_conv), jnp.stack(p_mk),
            jnp.stack(p_mv), jnp.stack(s_ret), jnp.stack(s_conv))
```

```python
import functools
import math

import jax
import jax.numpy as jnp
from jax import lax
from jax.experimental import pallas as pl
from jax.experimental.pallas import tpu as pltpu

F32 = jnp.float32
BF16 = jnp.bfloat16

D_MODEL = 1024
N_MEM = 256
RET_HEADS = 4
RET_DK = 128
RET_DV = 128
RET_W = RET_HEADS * RET_DV
RET_CHUNK = 128
CONV_W = 256
CONV_K = 31
XA_HEADS = 4
XA_DH = 64
XA_W = XA_HEADS * XA_DH
D_MIX = RET_W + CONV_W + XA_W
ROPE_BASE = 10000.0
EPS = 1e-6
PAST_LEN = 16384

C_RQ, C_RK, C_RV, C_RG = 0, 512, 1024, 1536
C_CA, C_CB, C_CG = 2048, 2304, 2560
C_XQ, C_XG = 2816, 3072
D_IN = 3328

VMEM_LIMIT_BYTES = 56 * 1024 * 1024
SUBLANES = 8
CONV_PAD = 32
CONV_HIST = CONV_K - 1
PROMPT_TILE = 256
SAMPLE_BLOCK = 8


def _rms(x, g):
    return x * lax.rsqrt(jnp.mean(x * x, axis=-1, keepdims=True) + EPS) * g


def _standardize(x):
    mu = jnp.mean(x, axis=-1, keepdims=True)
    d = x - mu
    var = jnp.mean(d * d, axis=-1, keepdims=True)
    return d * lax.rsqrt(var + EPS)


def _sigmoid(x):
    return 1.0 / (1.0 + jnp.exp(-x))


def _silu(x):
    return x * _sigmoid(x)


def _dot(a, b):
    return jnp.dot(a, b, preferred_element_type=F32)


def _dot_nt(a, b):
    return lax.dot_general(a, b, (((1,), (1,)), ((), ())), preferred_element_type=F32)


def _dot_tn(a, b):
    return lax.dot_general(a, b, (((0,), (0,)), ((), ())), preferred_element_type=F32)


def _rope(x, cosf, sins):
    return x * cosf + pltpu.roll(x, RET_DK // 2, 1) * sins


def _head_lane_ids():
    return lax.broadcasted_iota(jnp.int32, (1, XA_W), 1) // XA_DH


def _xattn_heads(q, mkb, mvb, xan, gate, rows):
    head = _head_lane_ids()
    a = jnp.zeros((rows, XA_W), F32)
    for hh in range(XA_HEADS):
        m = head == hh
        qm = jnp.where(m, q, 0.0).astype(BF16)
        sc = _dot_nt(qm, mkb) * (XA_DH ** -0.5)
        e = jnp.exp(sc - jnp.max(sc, axis=-1, keepdims=True))
        s = jnp.sum(e, axis=-1, keepdims=True)
        pv = _dot(e.astype(BF16), mvb)
        a = jnp.where(m, pv / s, a)
    a2 = a * a
    ms = jnp.zeros((rows, XA_W), F32)
    for hh in range(XA_HEADS):
        m = head == hh
        ssq = jnp.sum(jnp.where(m, a2, 0.0), axis=-1, keepdims=True) * (1.0 / XA_DH)
        ms = jnp.where(m, ssq, ms)
    an = a * lax.rsqrt(ms + EPS) * xan
    return an * gate


def _mem_kv_kernel(mem_ref, g_ref, wk_ref, wv_ref, k_ref, v_ref):
    m = _rms(mem_ref[0], g_ref[0]).astype(BF16)
    k_ref[0, 0] = _dot(m, wk_ref[0])
    v_ref[0, 0] = _dot(m, wv_ref[0])


def _mem_kv(mem, g, wk, wv):
    depth, batch = g.shape[0], mem.shape[0]
    out = jax.ShapeDtypeStruct((depth, batch, N_MEM, XA_W), F32)
    return pl.pallas_call(
        _mem_kv_kernel,
        grid=(depth, batch),
        in_specs=[
            pl.BlockSpec((1, N_MEM, D_MODEL), lambda l, b: (b, 0, 0)),
            pl.BlockSpec((1, 1, D_MODEL), lambda l, b: (l, 0, 0)),
            pl.BlockSpec((1, D_MODEL, XA_W), lambda l, b: (l, 0, 0)),
            pl.BlockSpec((1, D_MODEL, XA_W), lambda l, b: (l, 0, 0)),
        ],
        out_specs=[
            pl.BlockSpec((1, 1, N_MEM, XA_W), lambda l, b: (l, b, 0, 0)),
            pl.BlockSpec((1, 1, N_MEM, XA_W), lambda l, b: (l, b, 0, 0)),
        ],
        out_shape=[out, out],
        compiler_params=pltpu.CompilerParams(
            dimension_semantics=("arbitrary", "arbitrary"),
            vmem_limit_bytes=VMEM_LIMIT_BYTES),
        name="mem_kv",
    )(mem, g.reshape(depth, 1, D_MODEL), wk, wv)


def _prompt_layer_kernel(final, n_tiles,
                         cd_ref, x_ref, cos_ref, sin_ref, mk_ref, mv_ref, ng_ref, win_ref,
                         gn_ref, cw_ref, cb_ref, lng_ref, lnb_ref, xan_ref, wout_ref,
                         dec_ref, wq_ref, wk_ref, fng_ref,
                         y_ref, sret_ref, sconv_ref,
                         s_scr, ext_scr, q_scr, k_scr, v_scr, g_scr, mix_scr):
    tm = PROMPT_TILE
    t = pl.program_id(1)

    @pl.when(t == 0)
    def _():
        s_scr[...] = jnp.zeros_like(s_scr)
        ext_scr[0:CONV_PAD, :] = jnp.zeros((CONV_PAD, CONV_W), F32)

    x = x_ref[0]
    hb = _rms(x, ng_ref[...]).astype(BF16)

    def proj(a, b):
        return _dot(hb, win_ref[:, a:b])

    cosf = cos_ref[...]
    sins = sin_ref[...]

    zq = proj(C_RQ, C_RQ + RET_W)
    zk = proj(C_RK, C_RK + RET_W)
    for hh in range(RET_HEADS):
        cols = slice(RET_DK * hh, RET_DK * (hh + 1))
        q_scr[:, cols] = _rope(zq[:, cols], cosf, sins).astype(BF16)
        k_scr[:, cols] = _rope(zk[:, cols], cosf, sins) * (RET_DK ** -0.5)
    v_scr[...] = proj(C_RV, C_RV + RET_W).astype(BF16)
    g_scr[...] = proj(C_RG, C_RG + RET_W)

    for c in range(tm // RET_CHUNK):
        rows = slice(RET_CHUNK * c, RET_CHUNK * (c + 1))
        for hh in range(RET_HEADS):
            cols = slice(RET_DK * hh, RET_DK * (hh + 1))
            qh = q_scr[rows, cols]
            kf = k_scr[rows, cols]
            vh = v_scr[rows, cols]
            s_prev = s_scr[hh]
            sc = _dot_nt(qh, kf.astype(BF16)) * dec_ref[hh]
            inner = _dot(sc.astype(BF16), vh)
            cross = _dot(qh, s_prev.astype(BF16)) * wq_ref[hh]
            kv = _dot_tn((kf * wk_ref[hh]).astype(BF16), vh)
            s_scr[hh] = cd_ref[hh] * s_prev + kv
            o = _standardize(inner + cross) * gn_ref[:, cols]
            mix_scr[rows, cols] = (o * _silu(g_scr[rows, cols])).astype(BF16)

    u = proj(C_CA, C_CA + CONV_W) * _sigmoid(proj(C_CB, C_CB + CONV_W))
    ext_scr[CONV_PAD:CONV_PAD + tm, :] = u
    first = CONV_PAD - CONV_HIST
    acc = jnp.zeros((tm, CONV_W), F32) + cb_ref[...]
    for j in range(CONV_K):
        acc = acc + ext_scr[first + j:first + j + tm, :] * cw_ref[j:j + 1, :]

    @pl.when(t == n_tiles - 1)
    def _():
        sconv_ref[0, 0] = ext_scr[tm + first:tm + CONV_PAD, :]

    ext_scr[0:CONV_PAD, :] = ext_scr[tm:tm + CONV_PAD, :]
    cn = _standardize(acc) * lng_ref[...] + lnb_ref[...]
    cgate = _silu(proj(C_CG, C_CG + CONV_W))
    mix_scr[:, RET_W:RET_W + CONV_W] = (_silu(cn) * cgate).astype(BF16)

    xq = proj(C_XQ, C_XQ + XA_W)
    xgate = _silu(proj(C_XG, C_XG + XA_W))
    a = _xattn_heads(xq, mk_ref[0, 0].astype(BF16), mv_ref[0, 0].astype(BF16),
                     xan_ref[...], xgate, tm)
    mix_scr[:, RET_W + CONV_W:D_MIX] = a.astype(BF16)

    out = x + _dot(mix_scr[...], wout_ref[...])
    if final:
        out = _rms(out, fng_ref[...])
    y_ref[0] = out

    @pl.when(t == n_tiles - 1)
    def _():
        sret_ref[0, 0] = s_scr[...]


def _prompt_layer(layer, final, x, cosf, sins, mk, mv, ng, win, gn, cw, cb, lng, lnb, xan,
                  wout, dec, wq, wk, cd, fng):
    batch, seq, _ = x.shape
    tm = PROMPT_TILE
    n_tiles = seq // tm
    row = lambda n: pl.BlockSpec((1, n), lambda b, t: (0, 0))
    const3 = lambda shp: pl.BlockSpec(shp, lambda b, t: (0, 0, 0))
    in_specs = [
        pl.BlockSpec(memory_space=pltpu.SMEM),
        pl.BlockSpec((1, tm, D_MODEL), lambda b, t: (b, t, 0)),
        pl.BlockSpec((tm, RET_DK), lambda b, t: (t, 0)),
        pl.BlockSpec((tm, RET_DK), lambda b, t: (t, 0)),
        pl.BlockSpec((1, 1, N_MEM, XA_W), lambda b, t: (layer, b, 0, 0)),
        pl.BlockSpec((1, 1, N_MEM, XA_W), lambda b, t: (layer, b, 0, 0)),
        row(D_MODEL),
        pl.BlockSpec((D_MODEL, D_IN), lambda b, t: (0, 0)),
        row(RET_W),
        pl.BlockSpec((CONV_K, CONV_W), lambda b, t: (0, 0)),
        row(CONV_W), row(CONV_W), row(CONV_W),
        row(XA_W),
        pl.BlockSpec((D_MIX, D_MODEL), lambda b, t: (0, 0)),
        const3((RET_HEADS, RET_CHUNK, RET_CHUNK)),
        const3((RET_HEADS, RET_CHUNK, RET_DV)),
        const3((RET_HEADS, RET_CHUNK, RET_DK)),
        row(D_MODEL),
    ]
    out_specs = [
        pl.BlockSpec((1, tm, D_MODEL), lambda b, t: (b, t, 0)),
        pl.BlockSpec((1, 1, RET_HEADS, RET_DK, RET_DV), lambda b, t: (0, b, 0, 0, 0)),
        pl.BlockSpec((1, 1, CONV_HIST, CONV_W), lambda b, t: (0, b, 0, 0)),
    ]
    out_shape = [
        jax.ShapeDtypeStruct((batch, seq, D_MODEL), F32),
        jax.ShapeDtypeStruct((1, batch, RET_HEADS, RET_DK, RET_DV), F32),
        jax.ShapeDtypeStruct((1, batch, CONV_HIST, CONV_W), F32),
    ]
    scratch = [
        pltpu.VMEM((RET_HEADS, RET_DK, RET_DV), F32),
        pltpu.VMEM((CONV_PAD + tm, CONV_W), F32),
        pltpu.VMEM((tm, RET_W), BF16),
        pltpu.VMEM((tm, RET_W), F32),
        pltpu.VMEM((tm, RET_W), BF16),
        pltpu.VMEM((tm, RET_W), F32),
        pltpu.VMEM((tm, D_MIX), BF16),
    ]
    return pl.pallas_call(
        functools.partial(_prompt_layer_kernel, final, n_tiles),
        grid=(batch, n_tiles),
        in_specs=in_specs,
        out_specs=out_specs,
        out_shape=out_shape,
        scratch_shapes=scratch,
        compiler_params=pltpu.CompilerParams(
            dimension_semantics=("arbitrary", "arbitrary"),
            vmem_limit_bytes=VMEM_LIMIT_BYTES),
        name=f"prompt_layer{layer}",
    )(cd, x, cosf, sins, mk, mv, ng, win, gn, cw, cb, lng, lnb, xan, wout, dec, wq, wk, fng)


def _sample_in_kernel(x_ref, ng_ref, win_ref, z_ref):
    z_ref[...] = _dot(_rms(x_ref[...], ng_ref[...]).astype(BF16), win_ref[...])


def _sample_in(x, ng, win):
    n = x.shape[0]
    return pl.pallas_call(
        _sample_in_kernel,
        out_shape=jax.ShapeDtypeStruct((n, D_IN), F32),
        compiler_params=pltpu.CompilerParams(vmem_limit_bytes=VMEM_LIMIT_BYTES),
        name="sample_in",
    )(x, ng, win)


def _sample_mid_kernel(x_ref, mix_ref, wout_ref, ng_ref, win_ref, h_ref, z_ref):
    h = x_ref[...] + _dot(mix_ref[...], wout_ref[...])
    h_ref[...] = h
    z_ref[...] = _dot(_rms(h, ng_ref[...]).astype(BF16), win_ref[...])


def _sample_mid(x, mix, wout, ng, win):
    n = x.shape[0]
    return pl.pallas_call(
        _sample_mid_kernel,
        out_shape=[jax.ShapeDtypeStruct((n, D_MODEL), F32),
                   jax.ShapeDtypeStruct((n, D_IN), F32)],
        compiler_params=pltpu.CompilerParams(vmem_limit_bytes=VMEM_LIMIT_BYTES),
        name="sample_mid",
    )(x, mix, wout, ng, win)


def _sample_out_kernel(x_ref, mix_ref, wout_ref, fng_ref, y_ref):
    y_ref[...] = _rms(x_ref[...] + _dot(mix_ref[...], wout_ref[...]), fng_ref[...])


def _sample_out(x, mix, wout, fng):
    n = x.shape[0]
    return pl.pallas_call(
        _sample_out_kernel,
        out_shape=jax.ShapeDtypeStruct((n, D_MODEL), F32),
        compiler_params=pltpu.CompilerParams(vmem_limit_bytes=VMEM_LIMIT_BYTES),
        name="sample_out",
    )(x, mix, wout, fng)


def _sample_mix_kernel(n_tok, n_alias,
                       cd_ref, z_ref, s0_ref, cbuf_ref, mk_ref, mv_ref, cos_ref, sin_ref,
                       gn_ref, cw_ref, cb_ref, lng_ref, lnb_ref, xan_ref,
                       dec_ref, wq_ref, wk_ref, *refs):
    mix_ref, sret_ref, sconv_ref, ext_scr, c_scr, q4_scr, o4_scr = refs[n_alias:]
    cosf = cos_ref[...]
    sins = sin_ref[...]
    head = _head_lane_ids()

    def body(bb, carry):
        z = z_ref[bb]

        for hh in range(RET_HEADS):
            cq = slice(C_RQ + RET_DK * hh, C_RQ + RET_DK * (hh + 1))
            ck = slice(C_RK + RET_DK * hh, C_RK + RET_DK * (hh + 1))
            cv = slice(C_RV + RET_DV * hh, C_RV + RET_DV * (hh + 1))
            cg = slice(C_RG + RET_DV * hh, C_RG + RET_DV * (hh + 1))
            qh = _rope(z[:, cq], cosf, sins).astype(BF16)
            kf = _rope(z[:, ck], cosf, sins) * (RET_DK ** -0.5)
            vh = z[:, cv].astype(BF16)
            s_prev = s0_ref[0, bb, hh]
            sc = _dot_nt(qh, kf.astype(BF16)) * dec_ref[hh]
            inner = _dot(sc.astype(BF16), vh)
            cross = _dot(qh, s_prev.astype(BF16)) * wq_ref[hh]
            kv = _dot_tn((kf * wk_ref[hh]).astype(BF16), vh)
            sret_ref[0, bb, hh] = cd_ref[hh] * s_prev + kv
            co = slice(RET_DV * hh, RET_DV * (hh + 1))
            o = _standardize(inner + cross) * gn_ref[:, co]
            mix_ref[bb, :, co] = (o * _silu(z[:, cg])).astype(BF16)

        u = z[:, C_CA:C_CA + CONV_W] * _sigmoid(z[:, C_CB:C_CB + CONV_W])
        ext_scr[0:CONV_HIST, :] = cbuf_ref[0, bb]
        ext_scr[CONV_HIST:CONV_HIST + n_tok, :] = u
        w = cw_ref[...]
        for i in range(n_tok):
            c_scr[i:i + 1, :] = jnp.sum(ext_scr[i:i + CONV_K, :] * w, axis=0, keepdims=True)
        sconv_ref[0, bb] = ext_scr[n_tok:n_tok + CONV_HIST, :]
        cn = _standardize(c_scr[0:n_tok, :] + cb_ref[...]) * lng_ref[...] + lnb_ref[...]
        mix_ref[bb, :, RET_W:RET_W + CONV_W] = (
            _silu(cn) * _silu(z[:, C_CG:C_CG + CONV_W])).astype(BF16)

        xq = z[:, C_XQ:C_XQ + XA_W]
        for hh in range(XA_HEADS):
            q4_scr[n_tok * hh:n_tok * (hh + 1), :] = jnp.where(head == hh, xq, 0.0)
        sc = _dot_nt(q4_scr[...].astype(BF16), mk_ref[0, bb].astype(BF16)) * (XA_DH ** -0.5)
        e = jnp.exp(sc - jnp.max(sc, axis=-1, keepdims=True))
        s = jnp.sum(e, axis=-1, keepdims=True)
        o4_scr[...] = _dot(e.astype(BF16), mv_ref[0, bb].astype(BF16)) / s
        a = jnp.zeros((n_tok, XA_W), F32)
        for hh in range(XA_HEADS):
            a = jnp.where(head == hh, o4_scr[n_tok * hh:n_tok * (hh + 1), :], a)
        a2 = a * a
        ms = jnp.zeros((n_tok, XA_W), F32)
        for hh in range(XA_HEADS):
            m = head == hh
            ssq = jnp.sum(jnp.where(m, a2, 0.0), axis=-1, keepdims=True) * (1.0 / XA_DH)
            ms = jnp.where(m, ssq, ms)
        an = a * lax.rsqrt(ms + EPS) * xan_ref[...]
        mix_ref[bb, :, RET_W + CONV_W:D_MIX] = (
            an * _silu(z[:, C_XG:C_XG + XA_W])).astype(BF16)
        return carry

    lax.fori_loop(0, SAMPLE_BLOCK, body, 0)


def _sample_mix(layer, z, s0, cbuf, mk, mv, cosf, sins, gn, cw, cb, lng, lnb, xan,
                dec, wq, wk, cd, prev_states):
    batch, n_tok, _ = z.shape
    depth = s0.shape[0]
    bb = SAMPLE_BLOCK
    row = lambda n: pl.BlockSpec((1, n), lambda i: (0, 0))
    const3 = lambda shp: pl.BlockSpec(shp, lambda i: (0, 0, 0))
    in_specs = [
        pl.BlockSpec(memory_space=pltpu.SMEM),
        pl.BlockSpec((bb, n_tok, D_IN), lambda i: (i, 0, 0)),
        pl.BlockSpec((1, bb, RET_HEADS, RET_DK, RET_DV), lambda i: (layer, i, 0, 0, 0)),
        pl.BlockSpec((1, bb, CONV_HIST, CONV_W), lambda i: (layer, i, 0, 0)),
        pl.BlockSpec((1, bb, N_MEM, XA_W), lambda i: (layer, i, 0, 0)),
        pl.BlockSpec((1, bb, N_MEM, XA_W), lambda i: (layer, i, 0, 0)),
        pl.BlockSpec((n_tok, RET_DK), lambda i: (0, 0)),
        pl.BlockSpec((n_tok, RET_DK), lambda i: (0, 0)),
        row(RET_W),
        pl.BlockSpec((CONV_K, CONV_W), lambda i: (0, 0)),
        row(CONV_W), row(CONV_W), row(CONV_W),
        row(XA_W),
        const3((RET_HEADS, n_tok, n_tok)),
        const3((RET_HEADS, n_tok, RET_DV)),
        const3((RET_HEADS, n_tok, RET_DK)),
    ]
    n_fixed = len(in_specs)
    n_alias = len(prev_states)
    in_specs += [pl.BlockSpec(memory_space=pl.ANY)] * n_alias
    out_specs = [
        pl.BlockSpec((bb, n_tok, D_MIX), lambda i: (i, 0, 0)),
        pl.BlockSpec((1, bb, RET_HEADS, RET_DK, RET_DV), lambda i: (layer, i, 0, 0, 0)),
        pl.BlockSpec((1, bb, CONV_HIST, CONV_W), lambda i: (layer, i, 0, 0)),
    ]
    out_shape = [
        jax.ShapeDtypeStruct((batch, n_tok, D_MIX), BF16),
        jax.ShapeDtypeStruct((depth, batch, RET_HEADS, RET_DK, RET_DV), F32),
        jax.ShapeDtypeStruct((depth, batch, CONV_HIST, CONV_W), F32),
    ]
    rows4 = XA_HEADS * n_tok
    scratch = [
        pltpu.VMEM((CONV_HIST + n_tok + SUBLANES, CONV_W), F32),
        pltpu.VMEM((SUBLANES, CONV_W), F32),
        pltpu.VMEM((rows4, XA_W), F32),
        pltpu.VMEM((rows4, XA_W), F32),
    ]
    return pl.pallas_call(
        functools.partial(_sample_mix_kernel, n_tok, n_alias),
        grid=(batch // bb,),
        in_specs=in_specs,
        out_specs=out_specs,
        out_shape=out_shape,
        scratch_shapes=scratch,
        input_output_aliases={n_fixed + k: 1 + k for k in range(n_alias)},
        compiler_params=pltpu.CompilerParams(
            dimension_semantics=("arbitrary",),
            vmem_limit_bytes=VMEM_LIMIT_BYTES),
        name=f"sample_mix{layer}",
    )(cd, z, s0, cbuf, mk, mv, cosf, sins, gn, cw, cb, lng, lnb, xan, dec, wq, wk,
      *prev_states)


def _rope_tables(pos):
    half = RET_DK // 2
    inv = ROPE_BASE ** (-jnp.arange(half, dtype=F32) / half)
    ang = pos.astype(F32)[:, None] * inv[None, :]
    cos, sin = jnp.cos(ang), jnp.sin(ang)
    return jnp.concatenate([cos, cos], axis=-1), jnp.concatenate([-sin, sin], axis=-1)


def _decay_tables(chunk):
    lg = jnp.log(1.0 - jnp.exp2(-5.0 - jnp.arange(RET_HEADS, dtype=F32)))
    idx = jnp.arange(chunk, dtype=F32)
    diff = idx[:, None] - idx[None, :]
    dec = jnp.where(diff[None] >= 0,
                    jnp.exp(jnp.maximum(diff, 0.0)[None] * lg[:, None, None]), 0.0)
    wk = jnp.exp((chunk - 1.0 - idx)[None, :] * lg[:, None])
    wq = jnp.exp((idx + 1.0)[None, :] * lg[:, None])
    cd = jnp.exp(chunk * lg)
    wk = jnp.broadcast_to(wk[:, :, None], (RET_HEADS, chunk, RET_DK))
    wq = jnp.broadcast_to(wq[:, :, None], (RET_HEADS, chunk, RET_DV))
    return dec, wq, wk, cd


def kernel(x_prompt, x_sample, mem_prompt, state_ret, state_conv, cache_mem_k, cache_mem_v,
           norm_g, w_in, ret_gn_g, conv_w, conv_b, conv_ln_g, conv_ln_b, xa_norm_g,
           mem_norm_g, w_mk, w_mv, w_out, final_norm_g):
    depth = w_in.shape[0]
    batch, seq, _ = x_prompt.shape
    dbatch, dseq, _ = x_sample.shape

    w_in_b = w_in.astype(BF16)
    w_out_b = w_out.astype(BF16)
    w_mk_b = w_mk.astype(BF16)
    w_mv_b = w_mv.astype(BF16)
    row = lambda a: a.reshape(1, -1)
    fng = row(final_norm_g)

    p_mk, p_mv = _mem_kv(mem_prompt, mem_norm_g, w_mk_b, w_mv_b)
    cos_p, sin_p = _rope_tables(jnp.arange(seq))
    dec_p, wq_p, wk_p, cd_p = _decay_tables(RET_CHUNK if seq % RET_CHUNK == 0 else seq)
    hp = x_prompt
    p_ret, p_conv = [], []
    for l in range(depth):
        hp, sr, sc = _prompt_layer(
            l, l == depth - 1, hp, cos_p, sin_p, p_mk, p_mv, row(norm_g[l]), w_in_b[l],
            row(ret_gn_g[l]), conv_w[l], row(conv_b[l]), row(conv_ln_g[l]),
            row(conv_ln_b[l]), row(xa_norm_g[l]), w_out_b[l], dec_p, wq_p, wk_p, cd_p, fng)
        p_ret.append(sr)
        p_conv.append(sc)
    y_prompt = hp
    prompt_state_ret = jnp.concatenate(p_ret, axis=0)
    prompt_state_conv = jnp.concatenate(p_conv, axis=0)

    cos_s, sin_s = _rope_tables(PAST_LEN + jnp.arange(dseq))
    dec_s, wq_s, wk_s, cd_s = _decay_tables(RET_CHUNK if dseq % RET_CHUNK == 0 else dseq)
    mk_s = cache_mem_k.reshape(depth, dbatch, N_MEM, XA_W)
    mv_s = cache_mem_v.reshape(depth, dbatch, N_MEM, XA_W)
    hs = x_sample.reshape(dbatch * dseq, D_MODEL)
    z = _sample_in(hs, row(norm_g[0]), w_in_b[0])
    states = ()
    y_sample = None
    for l in range(depth):
        mix, *states = _sample_mix(
            l, z.reshape(dbatch, dseq, D_IN), state_ret, state_conv, mk_s, mv_s,
            cos_s, sin_s, row(ret_gn_g[l]), conv_w[l], row(conv_b[l]), row(conv_ln_g[l]),
            row(conv_ln_b[l]), row(xa_norm_g[l]), dec_s, wq_s, wk_s, cd_s, tuple(states))
        mix = mix.reshape(dbatch * dseq, D_MIX)
        if l + 1 < depth:
            hs, z = _sample_mid(hs, mix, w_out_b[l], row(norm_g[l + 1]), w_in_b[l + 1])
        else:
            y_sample = _sample_out(hs, mix, w_out_b[l], fng).reshape(dbatch, dseq, D_MODEL)

    return (y_prompt, y_sample, prompt_state_ret, prompt_state_conv,
            p_mk.reshape(depth, batch, N_MEM, XA_HEADS, XA_DH),
            p_mv.reshape(depth, batch, N_MEM, XA_HEADS, XA_DH),
            states[0], states[1])
```

```python
import functools
import math

import jax
import jax.numpy as jnp
from jax import lax
from jax.experimental import pallas as pl
from jax.experimental.pallas import tpu as pltpu

F32 = jnp.float32
BF16 = jnp.bfloat16

D_MODEL = 1024
N_MEM = 256
RET_HEADS = 4
RET_DK = 128
RET_DV = 128
RET_W = RET_HEADS * RET_DV
RET_CHUNK = 128
CONV_W = 256
CONV_K = 31
XA_HEADS = 4
XA_DH = 64
XA_W = XA_HEADS * XA_DH
D_MIX = RET_W + CONV_W + XA_W
ROPE_BASE = 10000.0
EPS = 1e-6
PAST_LEN = 16384

C_RQ, C_RK, C_RV, C_RG = 0, 512, 1024, 1536
C_CA, C_CB, C_CG = 2048, 2304, 2560
C_XQ, C_XG = 2816, 3072
D_IN = 3328

VMEM_LIMIT_BYTES = 56 * 1024 * 1024
SUBLANES = 8
CONV_PAD = 32
CONV_HIST = CONV_K - 1
PROMPT_TILE = 256
SAMPLE_BLOCK = 8


def _rms(x, g):
    return x * lax.rsqrt(jnp.mean(x * x, axis=-1, keepdims=True) + EPS) * g


def _standardize(x):
    mu = jnp.mean(x, axis=-1, keepdims=True)
    d = x - mu
    var = jnp.mean(d * d, axis=-1, keepdims=True)
    return d * lax.rsqrt(var + EPS)


def _sigmoid(x):
    return 1.0 / (1.0 + jnp.exp(-x))


def _silu(x):
    return x * _sigmoid(x)


def _dot(a, b):
    return jnp.dot(a, b, preferred_element_type=F32)


def _dot_nt(a, b):
    return lax.dot_general(a, b, (((1,), (1,)), ((), ())), preferred_element_type=F32)


def _dot_tn(a, b):
    return lax.dot_general(a, b, (((0,), (0,)), ((), ())), preferred_element_type=F32)


def _rope(x, cosf, sins):
    return x * cosf + pltpu.roll(x, RET_DK // 2, 1) * sins


def _head_lane_ids():
    return lax.broadcasted_iota(jnp.int32, (1, XA_W), 1) // XA_DH


def _xattn_norm_gate(a, xan, gate, rows):
    head = _head_lane_ids()
    a2 = a * a
    ms = jnp.zeros((rows, XA_W), F32)
    for hh in range(XA_HEADS):
        m = head == hh
        ssq = jnp.sum(jnp.where(m, a2, 0.0), axis=-1, keepdims=True) * (1.0 / XA_DH)
        ms = jnp.where(m, ssq, ms)
    return a * lax.rsqrt(ms + EPS) * xan * gate


def _xattn_heads(q, mkt, mvt, xan, gate, rows):
    head = _head_lane_ids()
    a = jnp.zeros((rows, XA_W), F32)
    for hh in range(XA_HEADS):
        m = head == hh
        qm = jnp.where(m, q, 0.0).astype(BF16)
        sc = _dot(qm, mkt) * (XA_DH ** -0.5)
        e = jnp.exp(sc - jnp.max(sc, axis=-1, keepdims=True))
        s = jnp.sum(e, axis=-1, keepdims=True)
        pv = _dot_nt(e.astype(BF16), mvt)
        a = jnp.where(m, pv / s, a)
    return _xattn_norm_gate(a, xan, gate, rows)


def _mem_kv_kernel(mem_ref, g_ref, wkt_ref, wvt_ref, kt_ref, vt_ref):
    m = _rms(mem_ref[0], g_ref[0]).astype(BF16)
    kt_ref[0, 0] = _dot_nt(wkt_ref[0], m)
    vt_ref[0, 0] = _dot_nt(wvt_ref[0], m)


def _mem_kv(mem, g, wkt, wvt):
    depth, batch = g.shape[0], mem.shape[0]
    out = jax.ShapeDtypeStruct((depth, batch, XA_W, N_MEM), F32)
    return pl.pallas_call(
        _mem_kv_kernel,
        grid=(depth, batch),
        in_specs=[
            pl.BlockSpec((1, N_MEM, D_MODEL), lambda l, b: (b, 0, 0)),
            pl.BlockSpec((1, 1, D_MODEL), lambda l, b: (l, 0, 0)),
            pl.BlockSpec((1, XA_W, D_MODEL), lambda l, b: (l, 0, 0)),
            pl.BlockSpec((1, XA_W, D_MODEL), lambda l, b: (l, 0, 0)),
        ],
        out_specs=[
            pl.BlockSpec((1, 1, XA_W, N_MEM), lambda l, b: (l, b, 0, 0)),
            pl.BlockSpec((1, 1, XA_W, N_MEM), lambda l, b: (l, b, 0, 0)),
        ],
        out_shape=[out, out],
        compiler_params=pltpu.CompilerParams(
            dimension_semantics=("arbitrary", "arbitrary"),
            vmem_limit_bytes=VMEM_LIMIT_BYTES),
        name="mem_kv",
    )(mem, g.reshape(depth, 1, D_MODEL), wkt, wvt)


def _prompt_layer_kernel(final, n_tiles,
                         cd_ref, x_ref, cos_ref, sin_ref, mk_ref, mv_ref, ng_ref, win_ref,
                         gn_ref, cw_ref, cb_ref, lng_ref, lnb_ref, xan_ref, wout_ref,
                         dec_ref, wq_ref, wk_ref, fng_ref,
                         y_ref, sret_ref, sconv_ref,
                         s_scr, ext_scr, q_scr, k_scr, v_scr, g_scr, mix_scr):
    tm = PROMPT_TILE
    t = pl.program_id(1)

    @pl.when(t == 0)
    def _():
        s_scr[...] = jnp.zeros_like(s_scr)
        ext_scr[0:CONV_PAD, :] = jnp.zeros((CONV_PAD, CONV_W), F32)

    x = x_ref[0]
    hb = _rms(x, ng_ref[...]).astype(BF16)

    def proj(a, b):
        return _dot(hb, win_ref[:, a:b])

    cosf = cos_ref[...]
    sins = sin_ref[...]

    zq = proj(C_RQ, C_RQ + RET_W)
    zk = proj(C_RK, C_RK + RET_W)
    for hh in range(RET_HEADS):
        cols = slice(RET_DK * hh, RET_DK * (hh + 1))
        q_scr[:, cols] = _rope(zq[:, cols], cosf, sins).astype(BF16)
        k_scr[:, cols] = _rope(zk[:, cols], cosf, sins) * (RET_DK ** -0.5)
    v_scr[...] = proj(C_RV, C_RV + RET_W).astype(BF16)
    g_scr[...] = proj(C_RG, C_RG + RET_W)

    for c in range(tm // RET_CHUNK):
        rows = slice(RET_CHUNK * c, RET_CHUNK * (c + 1))
        for hh in range(RET_HEADS):
            cols = slice(RET_DK * hh, RET_DK * (hh + 1))
            qh = q_scr[rows, cols]
            kf = k_scr[rows, cols]
            vh = v_scr[rows, cols]
            s_prev = s_scr[hh]
            sc = _dot_nt(qh, kf.astype(BF16)) * dec_ref[hh]
            inner = _dot(sc.astype(BF16), vh)
            cross = _dot(qh, s_prev.astype(BF16)) * wq_ref[hh]
            kv = _dot_tn((kf * wk_ref[hh]).astype(BF16), vh)
            s_scr[hh] = cd_ref[hh] * s_prev + kv
            o = _standardize(inner + cross) * gn_ref[:, cols]
            mix_scr[rows, cols] = (o * _silu(g_scr[rows, cols])).astype(BF16)

    u = proj(C_CA, C_CA + CONV_W) * _sigmoid(proj(C_CB, C_CB + CONV_W))
    ext_scr[CONV_PAD:CONV_PAD + tm, :] = u
    first = CONV_PAD - CONV_HIST
    acc = jnp.zeros((tm, CONV_W), F32) + cb_ref[...]
    for j in range(CONV_K):
        acc = acc + ext_scr[first + j:first + j + tm, :] * cw_ref[j:j + 1, :]

    @pl.when(t == n_tiles - 1)
    def _():
        sconv_ref[0, 0] = ext_scr[tm + first:tm + CONV_PAD, :]

    ext_scr[0:CONV_PAD, :] = ext_scr[tm:tm + CONV_PAD, :]
    cn = _standardize(acc) * lng_ref[...] + lnb_ref[...]
    cgate = _silu(proj(C_CG, C_CG + CONV_W))
    mix_scr[:, RET_W:RET_W + CONV_W] = (_silu(cn) * cgate).astype(BF16)

    xq = proj(C_XQ, C_XQ + XA_W)
    xgate = _silu(proj(C_XG, C_XG + XA_W))
    a = _xattn_heads(xq, mk_ref[0, 0].astype(BF16), mv_ref[0, 0].astype(BF16),
                     xan_ref[...], xgate, tm)
    mix_scr[:, RET_W + CONV_W:D_MIX] = a.astype(BF16)

    out = x + _dot(mix_scr[...], wout_ref[...])
    if final:
        out = _rms(out, fng_ref[...])
    y_ref[0] = out

    @pl.when(t == n_tiles - 1)
    def _():
        sret_ref[0, 0] = s_scr[...]


def _prompt_layer(layer, final, x, cosf, sins, mk, mv, ng, win, gn, cw, cb, lng, lnb, xan,
                  wout, dec, wq, wk, cd, fng):
    batch, seq, _ = x.shape
    tm = PROMPT_TILE
    n_tiles = seq // tm
    row = lambda n: pl.BlockSpec((1, n), lambda b, t: (0, 0))
    const3 = lambda shp: pl.BlockSpec(shp, lambda b, t: (0, 0, 0))
    in_specs = [
        pl.BlockSpec(memory_space=pltpu.SMEM),
        pl.BlockSpec((1, tm, D_MODEL), lambda b, t: (b, t, 0)),
        pl.BlockSpec((tm, RET_DK), lambda b, t: (t, 0)),
        pl.BlockSpec((tm, RET_DK), lambda b, t: (t, 0)),
        pl.BlockSpec((1, 1, N_MEM, XA_W), lambda b, t: (layer, b, 0, 0)),
        pl.BlockSpec((1, 1, N_MEM, XA_W), lambda b, t: (layer, b, 0, 0)),
        row(D_MODEL),
        pl.BlockSpec((D_MODEL, D_IN), lambda b, t: (0, 0)),
        row(RET_W),
        pl.BlockSpec((CONV_K, CONV_W), lambda b, t: (0, 0)),
        row(CONV_W), row(CONV_W), row(CONV_W),
        row(XA_W),
        pl.BlockSpec((D_MIX, D_MODEL), lambda b, t: (0, 0)),
        const3((RET_HEADS, RET_CHUNK, RET_CHUNK)),
        const3((RET_HEADS, RET_CHUNK, RET_DV)),
        const3((RET_HEADS, RET_CHUNK, RET_DK)),
        row(D_MODEL),
    ]
    out_specs = [
        pl.BlockSpec((1, tm, D_MODEL), lambda b, t: (b, t, 0)),
        pl.BlockSpec((1, 1, RET_HEADS, RET_DK, RET_DV), lambda b, t: (0, b, 0, 0, 0)),
        pl.BlockSpec((1, 1, CONV_HIST, CONV_W), lambda b, t: (0, b, 0, 0)),
    ]
    out_shape = [
        jax.ShapeDtypeStruct((batch, seq, D_MODEL), F32),
        jax.ShapeDtypeStruct((1, batch, RET_HEADS, RET_DK, RET_DV), F32),
        jax.ShapeDtypeStruct((1, batch, CONV_HIST, CONV_W), F32),
    ]
    scratch = [
        pltpu.VMEM((RET_HEADS, RET_DK, RET_DV), F32),
        pltpu.VMEM((CONV_PAD + tm, CONV_W), F32),
        pltpu.VMEM((tm, RET_W), BF16),
        pltpu.VMEM((tm, RET_W), F32),
        pltpu.VMEM((tm, RET_W), BF16),
        pltpu.VMEM((tm, RET_W), F32),
        pltpu.VMEM((tm, D_MIX), BF16),
    ]
    return pl.pallas_call(
        functools.partial(_prompt_layer_kernel, final, n_tiles),
        grid=(batch, n_tiles),
        in_specs=in_specs,
        out_specs=out_specs,
        out_shape=out_shape,
        scratch_shapes=scratch,
        compiler_params=pltpu.CompilerParams(
            dimension_semantics=("arbitrary", "arbitrary"),
            vmem_limit_bytes=VMEM_LIMIT_BYTES),
        name=f"prompt_layer{layer}",
    )(cd, x, cosf, sins, mk, mv, ng, win, gn, cw, cb, lng, lnb, xan, wout, dec, wq, wk, fng)


def _sample_in_kernel(x_ref, ng_ref, win_ref, z_ref):
    z_ref[...] = _dot(_rms(x_ref[...], ng_ref[...]).astype(BF16), win_ref[...])


def _sample_in(x, ng, win):
    n = x.shape[0]
    return pl.pallas_call(
        _sample_in_kernel,
        out_shape=jax.ShapeDtypeStruct((n, D_IN), F32),
        compiler_params=pltpu.CompilerParams(vmem_limit_bytes=VMEM_LIMIT_BYTES),
        name="sample_in",
    )(x, ng, win)


def _sample_mid_kernel(x_ref, mix_ref, wout_ref, ng_ref, win_ref, h_ref, z_ref):
    h = x_ref[...] + _dot(mix_ref[...], wout_ref[...])
    h_ref[...] = h
    z_ref[...] = _dot(_rms(h, ng_ref[...]).astype(BF16), win_ref[...])


def _sample_mid(x, mix, wout, ng, win):
    n = x.shape[0]
    return pl.pallas_call(
        _sample_mid_kernel,
        out_shape=[jax.ShapeDtypeStruct((n, D_MODEL), F32),
                   jax.ShapeDtypeStruct((n, D_IN), F32)],
        compiler_params=pltpu.CompilerParams(vmem_limit_bytes=VMEM_LIMIT_BYTES),
        name="sample_mid",
    )(x, mix, wout, ng, win)


def _sample_out_kernel(x_ref, mix_ref, wout_ref, fng_ref, y_ref):
    y_ref[...] = _rms(x_ref[...] + _dot(mix_ref[...], wout_ref[...]), fng_ref[...])


def _sample_out(x, mix, wout, fng):
    n = x.shape[0]
    return pl.pallas_call(
        _sample_out_kernel,
        out_shape=jax.ShapeDtypeStruct((n, D_MODEL), F32),
        compiler_params=pltpu.CompilerParams(vmem_limit_bytes=VMEM_LIMIT_BYTES),
        name="sample_out",
    )(x, mix, wout, fng)


def _sample_mix_kernel(n_tok, n_alias,
                       cd_ref, z_ref, s0_ref, cbuf_ref, mkt_ref, mvt_ref, cos_ref, sin_ref,
                       gn_ref, wh_ref, wl_ref, cb_ref, lng_ref, lnb_ref, xan_ref,
                       dec_ref, wq_ref, wk_ref, *refs):
    mix_ref, sret_ref, sconv_ref = refs[n_alias:]
    grp = SUBLANES // n_tok
    rows = grp * n_tok
    cosf = cos_ref[...]
    sins = sin_ref[...]
    head = _head_lane_ids()
    row_id = lax.broadcasted_iota(jnp.int32, (rows, 1), 0)
    row_seq = row_id // n_tok
    row_tok = row_id - row_seq * n_tok
    row4_seq = lax.broadcasted_iota(jnp.int32, (XA_HEADS * rows, 1), 0) % rows // n_tok

    def pick(parts, seq_of_row):
        out = parts[0]
        for s in range(1, grp):
            out = jnp.where(seq_of_row == s, parts[s], out)
        return out

    for g in range(SAMPLE_BLOCK // grp):
        seqs = [g * grp + s for s in range(grp)]
        rs = slice(rows * g, rows * (g + 1))
        z = z_ref[rs, :]

        for hh in range(RET_HEADS):
            cq = slice(C_RQ + RET_DK * hh, C_RQ + RET_DK * (hh + 1))
            ck = slice(C_RK + RET_DK * hh, C_RK + RET_DK * (hh + 1))
            cv = slice(C_RV + RET_DV * hh, C_RV + RET_DV * (hh + 1))
            cg = slice(C_RG + RET_DV * hh, C_RG + RET_DV * (hh + 1))
            co = slice(RET_DV * hh, RET_DV * (hh + 1))
            qh = _rope(z[:, cq], cosf, sins).astype(BF16)
            kf = _rope(z[:, ck], cosf, sins) * (RET_DK ** -0.5)
            vh = z[:, cv].astype(BF16)
            sc = _dot_nt(qh, kf.astype(BF16)) * dec_ref[hh]
            inner = _dot(sc.astype(BF16), vh)
            kw = kf * wk_ref[hh]
            crosses = []
            for s, b in enumerate(seqs):
                s_prev = s0_ref[0, b, hh]
                crosses.append(_dot(qh, s_prev.astype(BF16)))
                kv = _dot_tn(jnp.where(row_seq == s, kw, 0.0).astype(BF16), vh)
                sret_ref[0, b, hh] = cd_ref[hh] * s_prev + kv
            cross = pick(crosses, row_seq) * wq_ref[hh]
            o = _standardize(inner + cross) * gn_ref[:, co]
            mix_ref[rs, co] = (o * _silu(z[:, cg])).astype(BF16)

        u = z[:, C_CA:C_CA + CONV_W] * _sigmoid(z[:, C_CB:C_CB + CONV_W])
        c = u * wl_ref[0:1, :] + cb_ref[...]
        for lag in range(1, n_tok):
            c = c + jnp.where(row_tok >= lag, pltpu.roll(u, lag, 0), 0.0) * wl_ref[lag:lag + 1, :]
        for s, b in enumerate(seqs):
            hist = cbuf_ref[0, b]
            for i in range(n_tok):
                hi = jnp.sum(hist * wh_ref[i], axis=0, keepdims=True)
                c = c + jnp.where(row_id == s * n_tok + i, hi, 0.0)
            sconv_ref[0, b, 0:CONV_HIST - n_tok, :] = cbuf_ref[0, b, n_tok:CONV_HIST, :]
            sconv_ref[0, b, CONV_HIST - n_tok:CONV_HIST, :] = u[s * n_tok:(s + 1) * n_tok, :]
        cn = _standardize(c) * lng_ref[...] + lnb_ref[...]
        mix_ref[rs, RET_W:RET_W + CONV_W] = (
            _silu(cn) * _silu(z[:, C_CG:C_CG + CONV_W])).astype(BF16)

        xq = z[:, C_XQ:C_XQ + XA_W]
        q4 = jnp.concatenate([jnp.where(head == hh, xq, 0.0) for hh in range(XA_HEADS)],
                             axis=0).astype(BF16)
        sc = pick([_dot(q4, mkt_ref[0, b].astype(BF16)) for b in seqs], row4_seq)
        sc = sc * (XA_DH ** -0.5)
        e = jnp.exp(sc - jnp.max(sc, axis=-1, keepdims=True))
        ssum = jnp.sum(e, axis=-1, keepdims=True)
        eb = e.astype(BF16)
        o4 = pick([_dot_nt(eb, mvt_ref[0, b].astype(BF16)) for b in seqs], row4_seq) / ssum
        a = jnp.zeros((rows, XA_W), F32)
        for hh in range(XA_HEADS):
            a = jnp.where(head == hh, o4[rows * hh:rows * (hh + 1), :], a)
        gate = _silu(z[:, C_XG:C_XG + XA_W])
        mix_ref[rs, RET_W + CONV_W:D_MIX] = _xattn_norm_gate(
            a, xan_ref[...], gate, rows).astype(BF16)


def _sample_mix(layer, n_tok, z, s0, cbuf, mkt, mvt, cosf, sins, gn, wh, wl, cb, lng, lnb,
                xan, dec, wq, wk, cd, prev_states):
    depth, batch = s0.shape[0], s0.shape[1]
    bb = SAMPLE_BLOCK
    rows = SUBLANES
    row = lambda n: pl.BlockSpec((1, n), lambda i: (0, 0))
    const2 = lambda shp: pl.BlockSpec(shp, lambda i: (0, 0))
    const3 = lambda shp: pl.BlockSpec(shp, lambda i: (0, 0, 0))
    in_specs = [
        pl.BlockSpec(memory_space=pltpu.SMEM),
        pl.BlockSpec((bb * n_tok, D_IN), lambda i: (i, 0)),
        pl.BlockSpec((1, bb, RET_HEADS, RET_DK, RET_DV), lambda i: (layer, i, 0, 0, 0)),
        pl.BlockSpec((1, bb, CONV_HIST, CONV_W), lambda i: (layer, i, 0, 0)),
        pl.BlockSpec((1, bb, XA_W, N_MEM), lambda i: (layer, i, 0, 0)),
        pl.BlockSpec((1, bb, XA_W, N_MEM), lambda i: (layer, i, 0, 0)),
        const2((rows, RET_DK)), const2((rows, RET_DK)),
        row(RET_W),
        const3((n_tok, CONV_HIST, CONV_W)),
        const2((n_tok, CONV_W)),
        row(CONV_W), row(CONV_W), row(CONV_W),
        row(XA_W),
        const3((RET_HEADS, rows, rows)),
        const3((RET_HEADS, rows, RET_DV)),
        const3((RET_HEADS, rows, RET_DK)),
    ]
    n_fixed = len(in_specs)
    n_alias = len(prev_states)
    in_specs += [pl.BlockSpec(memory_space=pl.ANY)] * n_alias
    out_specs = [
        pl.BlockSpec((bb * n_tok, D_MIX), lambda i: (i, 0)),
        pl.BlockSpec((1, bb, RET_HEADS, RET_DK, RET_DV), lambda i: (layer, i, 0, 0, 0)),
        pl.BlockSpec((1, bb, CONV_HIST, CONV_W), lambda i: (layer, i, 0, 0)),
    ]
    out_shape = [
        jax.ShapeDtypeStruct((batch * n_tok, D_MIX), BF16),
        jax.ShapeDtypeStruct((depth, batch, RET_HEADS, RET_DK, RET_DV), F32),
        jax.ShapeDtypeStruct((depth, batch, CONV_HIST, CONV_W), F32),
    ]
    return pl.pallas_call(
        functools.partial(_sample_mix_kernel, n_tok, n_alias),
        grid=(batch // bb,),
        in_specs=in_specs,
        out_specs=out_specs,
        out_shape=out_shape,
        input_output_aliases={n_fixed + k: 1 + k for k in range(n_alias)},
        compiler_params=pltpu.CompilerParams(
            dimension_semantics=("arbitrary",),
            vmem_limit_bytes=VMEM_LIMIT_BYTES),
        name=f"sample_mix{layer}",
    )(cd, z, s0, cbuf, mkt, mvt, cosf, sins, gn, wh, wl, cb, lng, lnb, xan, dec, wq, wk,
      *prev_states)


def _rope_tables(pos):
    half = RET_DK // 2
    inv = ROPE_BASE ** (-jnp.arange(half, dtype=F32) / half)
    ang = pos.astype(F32)[:, None] * inv[None, :]
    cos, sin = jnp.cos(ang), jnp.sin(ang)
    return jnp.concatenate([cos, cos], axis=-1), jnp.concatenate([-sin, sin], axis=-1)


def _decay_tables(chunk):
    lg = jnp.log(1.0 - jnp.exp2(-5.0 - jnp.arange(RET_HEADS, dtype=F32)))
    idx = jnp.arange(chunk, dtype=F32)
    diff = idx[:, None] - idx[None, :]
    dec = jnp.where(diff[None] >= 0,
                    jnp.exp(jnp.maximum(diff, 0.0)[None] * lg[:, None, None]), 0.0)
    wk = jnp.exp((chunk - 1.0 - idx)[None, :] * lg[:, None])
    wq = jnp.exp((idx + 1.0)[None, :] * lg[:, None])
    cd = jnp.exp(chunk * lg)
    wk = jnp.broadcast_to(wk[:, :, None], (RET_HEADS, chunk, RET_DK))
    wq = jnp.broadcast_to(wq[:, :, None], (RET_HEADS, chunk, RET_DV))
    return dec, wq, wk, cd


def _group_tables(n_tok, pos0):
    grp = SUBLANES // n_tok
    cosf, sins = _rope_tables(pos0 + jnp.arange(n_tok))
    dec, wq, wk, cd = _decay_tables(n_tok)
    tile_rows = lambda a: jnp.concatenate([a] * grp, axis=-2)
    eye = jnp.eye(grp, dtype=F32)
    dec = jnp.einsum("st,hij->hsitj", eye, dec).reshape(RET_HEADS, grp * n_tok, grp * n_tok)
    return tile_rows(cosf), tile_rows(sins), dec, tile_rows(wq), tile_rows(wk), cd


def _conv_step_taps(cw, n_tok):
    wh = jnp.stack([jnp.pad(cw[:CONV_HIST - i], ((i, 0), (0, 0))) for i in range(n_tok)])
    wl = jnp.stack([cw[CONV_HIST - l] for l in range(n_tok)])
    return wh, wl


def kernel(x_prompt, x_sample, mem_prompt, state_ret, state_conv, cache_mem_k, cache_mem_v,
           norm_g, w_in, ret_gn_g, conv_w, conv_b, conv_ln_g, conv_ln_b, xa_norm_g,
           mem_norm_g, w_mk, w_mv, w_out, final_norm_g):
    depth = w_in.shape[0]
    batch, seq, _ = x_prompt.shape
    dbatch, dseq, _ = x_sample.shape

    w_in_b = w_in.astype(BF16)
    w_out_b = w_out.astype(BF16)
    w_mkt_b = w_mk.swapaxes(1, 2).astype(BF16)
    w_mvt_b = w_mv.swapaxes(1, 2).astype(BF16)
    row = lambda a: a.reshape(1, -1)
    fng = row(final_norm_g)

    p_mk, p_mv = _mem_kv(mem_prompt, mem_norm_g, w_mkt_b, w_mvt_b)
    cos_p, sin_p = _rope_tables(jnp.arange(seq))
    dec_p, wq_p, wk_p, cd_p = _decay_tables(RET_CHUNK if seq % RET_CHUNK == 0 else seq)
    hp = x_prompt
    p_ret, p_conv = [], []
    for l in range(depth):
        hp, sr, sc = _prompt_layer(
            l, l == depth - 1, hp, cos_p, sin_p, p_mk, p_mv, row(norm_g[l]), w_in_b[l],
            row(ret_gn_g[l]), conv_w[l], row(conv_b[l]), row(conv_ln_g[l]),
            row(conv_ln_b[l]), row(xa_norm_g[l]), w_out_b[l], dec_p, wq_p, wk_p, cd_p, fng)
        p_ret.append(sr)
        p_conv.append(sc)
    y_prompt = hp
    prompt_state_ret = jnp.concatenate(p_ret, axis=0)
    prompt_state_conv = jnp.concatenate(p_conv, axis=0)

    assert SUBLANES % dseq == 0 and dseq % RET_CHUNK != 0
    cos_s, sin_s, dec_s, wq_s, wk_s, cd_s = _group_tables(dseq, PAST_LEN)
    to_hd_m = lambda c: c.transpose(0, 1, 3, 4, 2).reshape(depth, dbatch, XA_W, N_MEM)
    mkt_s, mvt_s = to_hd_m(cache_mem_k), to_hd_m(cache_mem_v)
    hs = x_sample.reshape(dbatch * dseq, D_MODEL)
    z = _sample_in(hs, row(norm_g[0]), w_in_b[0])
    states = ()
    y_sample = None
    for l in range(depth):
        wh, wl = _conv_step_taps(conv_w[l], dseq)
        mix, *states = _sample_mix(
            l, dseq, z, state_ret, state_conv, mkt_s, mvt_s, cos_s, sin_s, row(ret_gn_g[l]),
            wh, wl, row(conv_b[l]), row(conv_ln_g[l]), row(conv_ln_b[l]), row(xa_norm_g[l]),
            dec_s, wq_s, wk_s, cd_s, tuple(states))
        if l + 1 < depth:
            hs, z = _sample_mid(hs, mix, w_out_b[l], row(norm_g[l + 1]), w_in_b[l + 1])
        else:
            y_sample = _sample_out(hs, mix, w_out_b[l], fng).reshape(dbatch, dseq, D_MODEL)

    from_hd_m = lambda c: c.reshape(depth, batch, XA_HEADS, XA_DH, N_MEM).transpose(0, 1, 4, 2, 3)
    return (y_prompt, y_sample, prompt_state_ret, prompt_state_conv,
            from_hd_m(p_mk), from_hd_m(p_mv), states[0], states[1])
```

```python
import functools
import math

import jax
import jax.numpy as jnp
from jax import lax
from jax.experimental import pallas as pl
from jax.experimental.pallas import tpu as pltpu

F32 = jnp.float32
BF16 = jnp.bfloat16

D_MODEL = 1024
N_MEM = 256
RET_HEADS = 4
RET_DK = 128
RET_DV = 128
RET_W = RET_HEADS * RET_DV
RET_CHUNK = 128
CONV_W = 256
CONV_K = 31
XA_HEADS = 4
XA_DH = 64
XA_W = XA_HEADS * XA_DH
D_MIX = RET_W + CONV_W + XA_W
ROPE_BASE = 10000.0
EPS = 1e-6
PAST_LEN = 16384

C_RQ, C_RK, C_RV, C_RG = 0, 512, 1024, 1536
C_CA, C_CB, C_CG = 2048, 2304, 2560
C_XQ, C_XG = 2816, 3072
D_IN = 3328

VMEM_LIMIT_BYTES = 56 * 1024 * 1024
SUBLANES = 8
CONV_PAD = 32
CONV_HIST = CONV_K - 1
PROMPT_TILE = 256
CONV_ROWS = 64
PROJ_BLOCK = 256
SAMPLE_BLOCK = 8


def _rms(x, g):
    return x * lax.rsqrt(jnp.mean(x * x, axis=-1, keepdims=True) + EPS) * g


def _standardize(x):
    mu = jnp.mean(x, axis=-1, keepdims=True)
    d = x - mu
    var = jnp.mean(d * d, axis=-1, keepdims=True)
    return d * lax.rsqrt(var + EPS)


def _sigmoid(x):
    return 1.0 / (1.0 + jnp.exp(-x))


def _silu(x):
    return x * _sigmoid(x)


def _dot(a, b):
    return jnp.dot(a, b, preferred_element_type=F32)


def _dot_nt(a, b):
    return lax.dot_general(a, b, (((1,), (1,)), ((), ())), preferred_element_type=F32)


def _dot_tn(a, b):
    return lax.dot_general(a, b, (((0,), (0,)), ((), ())), preferred_element_type=F32)


def _rope(x, cosf, sins):
    return x * cosf + pltpu.roll(x, RET_DK // 2, 1) * sins


def _zero_row_after(z):
    tile = z[z.shape[0] - SUBLANES:, z.shape[1] - 128:]
    bits = pltpu.bitcast(tile, jnp.uint32)
    sixteen = jnp.uint32(16)
    zero = lax.shift_right_logical(lax.shift_right_logical(bits, sixteen), sixteen)
    row = pltpu.bitcast(zero, F32)[0:1, :]
    return jnp.concatenate([row, row], axis=1)


def _head_lane_ids():
    return lax.broadcasted_iota(jnp.int32, (1, XA_W), 1) // XA_DH


def _xattn_norm_gate(a, xan, gate, rows):
    head = _head_lane_ids()
    a2 = a * a
    ms = jnp.zeros((rows, XA_W), F32)
    for hh in range(XA_HEADS):
        m = head == hh
        ssq = jnp.sum(jnp.where(m, a2, 0.0), axis=-1, keepdims=True) * (1.0 / XA_DH)
        ms = jnp.where(m, ssq, ms)
    return a * lax.rsqrt(ms + EPS) * xan * gate


def _xattn_heads(q, mkt, mvt, xan, gate, rows):
    head = _head_lane_ids()
    a = jnp.zeros((rows, XA_W), F32)
    for hh in range(XA_HEADS):
        m = head == hh
        qm = jnp.where(m, q, 0.0).astype(BF16)
        sc = _dot(qm, mkt) * (XA_DH ** -0.5)
        e = jnp.exp(sc - jnp.max(sc, axis=-1, keepdims=True))
        s = jnp.sum(e, axis=-1, keepdims=True)
        pv = _dot_nt(e.astype(BF16), mvt)
        a = jnp.where(m, pv / s, a)
    return _xattn_norm_gate(a, xan, gate, rows)


def _mem_kv_kernel(mem_ref, g_ref, wkt_ref, wvt_ref, kt_ref, vt_ref):
    m = _rms(mem_ref[0], g_ref[0]).astype(BF16)
    kt_ref[0, 0] = _dot_nt(wkt_ref[0], m)
    vt_ref[0, 0] = _dot_nt(wvt_ref[0], m)


def _mem_kv(mem, g, wkt, wvt):
    depth, batch = g.shape[0], mem.shape[0]
    out = jax.ShapeDtypeStruct((depth, batch, XA_W, N_MEM), F32)
    return pl.pallas_call(
        _mem_kv_kernel,
        grid=(depth, batch),
        in_specs=[
            pl.BlockSpec((1, N_MEM, D_MODEL), lambda l, b: (b, 0, 0)),
            pl.BlockSpec((1, 1, D_MODEL), lambda l, b: (l, 0, 0)),
            pl.BlockSpec((1, XA_W, D_MODEL), lambda l, b: (l, 0, 0)),
            pl.BlockSpec((1, XA_W, D_MODEL), lambda l, b: (l, 0, 0)),
        ],
        out_specs=[
            pl.BlockSpec((1, 1, XA_W, N_MEM), lambda l, b: (l, b, 0, 0)),
            pl.BlockSpec((1, 1, XA_W, N_MEM), lambda l, b: (l, b, 0, 0)),
        ],
        out_shape=[out, out],
        compiler_params=pltpu.CompilerParams(
            dimension_semantics=("arbitrary", "arbitrary"),
            vmem_limit_bytes=VMEM_LIMIT_BYTES),
        name="mem_kv",
    )(mem, g.reshape(depth, 1, D_MODEL), wkt, wvt)


def _prompt_layer_kernel(final, n_tiles,
                         cd_ref, x_ref, cos_ref, sin_ref, mk_ref, mv_ref, ng_ref, win_ref,
                         gn_ref, cw_ref, cb_ref, lng_ref, lnb_ref, xan_ref, wout_ref,
                         dec_ref, wq_ref, wk_ref, fng_ref,
                         y_ref, sret_ref, sconv_ref,
                         s_scr, ext_scr, c_scr, q_scr, k_scr, v_scr, g_scr, xg_scr, mix_scr):
    tm = PROMPT_TILE
    t = pl.program_id(1)

    @pl.when(t == 0)
    def _():
        s_scr[...] = jnp.zeros_like(s_scr)
        ext_scr[0:CONV_PAD, :] = jnp.zeros((CONV_PAD, CONV_W), F32)

    x = x_ref[0]
    hb = _rms(x, ng_ref[...]).astype(BF16)

    def proj(a, b):
        return _dot(hb, win_ref[:, a:b])

    cosf = cos_ref[...]
    sins = sin_ref[...]

    u = proj(C_CA, C_CA + CONV_W) * _sigmoid(proj(C_CB, C_CB + CONV_W))
    ext_scr[CONV_PAD:CONV_PAD + tm, :] = u

    pb = PROJ_BLOCK

    def post_q(z, c):
        for i in range(pb // RET_DK):
            cols = slice(c + RET_DK * i, c + RET_DK * (i + 1))
            zs = z[:, RET_DK * i:RET_DK * (i + 1)]
            q_scr[:, cols] = _rope(zs, cosf, sins).astype(BF16)

    def post_k(z, c):
        for i in range(pb // RET_DK):
            cols = slice(c + RET_DK * i, c + RET_DK * (i + 1))
            zs = z[:, RET_DK * i:RET_DK * (i + 1)]
            k_scr[:, cols] = _rope(zs, cosf, sins) * (RET_DK ** -0.5)

    def post_v(z, c):
        v_scr[:, c:c + pb] = z.astype(BF16)

    def post_g(z, c):
        g_scr[:, c:c + pb] = _silu(z)

    def post_cg(z, c):
        xg_scr[:, c:c + pb] = _silu(z)

    def post_xq(z, c):
        xg_scr[:, CONV_W + c:CONV_W + c + pb] = z

    def post_xg(z, c):
        xg_scr[:, CONV_W + XA_W + c:CONV_W + XA_W + c + pb] = _silu(z)

    blocks = []
    for col0, width, post in ((C_RQ, RET_W, post_q), (C_RK, RET_W, post_k),
                              (C_RV, RET_W, post_v), (C_RG, RET_W, post_g),
                              (C_CG, CONV_W, post_cg), (C_XQ, XA_W, post_xq),
                              (C_XG, XA_W, post_xg)):
        blocks += [(col0, c, post) for c in range(0, width, pb)]

    first = CONV_PAD - CONV_HIST
    conv_acc = {}

    def tap_group(c0, r, wait_zero):
        n_rows = CONV_ROWS if r == 0 else CONV_ROWS + SUBLANES
        part = None
        for a in range((first + CONV_K - 1 - r) // SUBLANES + 1):
            j = SUBLANES * a + r - first
            if 0 <= j < CONV_K:
                w_j = cw_ref[j:j + 1, :]
                if wait_zero is not None:
                    w_j = w_j + wait_zero
                lo = c0 + SUBLANES * a
                term = ext_scr[lo:lo + n_rows, :] * w_j
                part = term if part is None else part + term
        if r == 0:
            conv_acc[c0] = part + cb_ref[...]
        else:
            conv_acc[c0] = conv_acc[c0] + part[r:r + CONV_ROWS, :]
        if r == SUBLANES - 1:
            c_scr[c0:c0 + CONV_ROWS, :] = conv_acc.pop(c0)

    groups = [(c0, r) for c0 in range(0, tm, CONV_ROWS) for r in range(SUBLANES)]
    per_block = -(-len(groups) // len(blocks))
    after_proj = None
    for k, (col0, c, post) in enumerate(blocks):
        for c0, r in groups[per_block * k:per_block * (k + 1)]:
            tap_group(c0, r, after_proj)
        z = proj(col0 + c, col0 + c + pb)
        post(z, c)
        after_proj = _zero_row_after(z)

    for c in range(tm // RET_CHUNK):
        rows = slice(RET_CHUNK * c, RET_CHUNK * (c + 1))
        for hh in range(RET_HEADS):
            cols = slice(RET_DK * hh, RET_DK * (hh + 1))
            qh = q_scr[rows, cols]
            kf = k_scr[rows, cols]
            vh = v_scr[rows, cols]
            s_prev = s_scr[hh]
            sc = _dot_nt(qh, kf.astype(BF16)) * dec_ref[hh]
            inner = _dot(sc.astype(BF16), vh)
            cross = _dot(qh, s_prev.astype(BF16)) * wq_ref[hh]
            kv = _dot_tn((kf * wk_ref[hh]).astype(BF16), vh)
            s_scr[hh] = cd_ref[hh] * s_prev + kv
            o = _standardize(inner + cross) * gn_ref[:, cols]
            mix_scr[rows, cols] = (o * g_scr[rows, cols]).astype(BF16)

    @pl.when(t == n_tiles - 1)
    def _():
        sconv_ref[0, 0] = ext_scr[tm + first:tm + CONV_PAD, :]

    ext_scr[0:CONV_PAD, :] = ext_scr[tm:tm + CONV_PAD, :]
    cn = _standardize(c_scr[...]) * lng_ref[...] + lnb_ref[...]
    mix_scr[:, RET_W:RET_W + CONV_W] = (_silu(cn) * xg_scr[:, 0:CONV_W]).astype(BF16)

    a = _xattn_heads(xg_scr[:, CONV_W:CONV_W + XA_W],
                     mk_ref[0, 0].astype(BF16), mv_ref[0, 0].astype(BF16),
                     xan_ref[...], xg_scr[:, CONV_W + XA_W:CONV_W + 2 * XA_W], tm)
    mix_scr[:, RET_W + CONV_W:D_MIX] = a.astype(BF16)

    out = x + _dot(mix_scr[...], wout_ref[...])
    if final:
        out = _rms(out, fng_ref[...])
    y_ref[0] = out

    @pl.when(t == n_tiles - 1)
    def _():
        sret_ref[0, 0] = s_scr[...]


def _prompt_layer(layer, final, x, cosf, sins, mk, mv, ng, win, gn, cw, cb, lng, lnb, xan,
                  wout, dec, wq, wk, cd, fng):
    batch, seq, _ = x.shape
    tm = PROMPT_TILE
    n_tiles = seq // tm
    row = lambda n: pl.BlockSpec((1, n), lambda b, t: (0, 0))
    const3 = lambda shp: pl.BlockSpec(shp, lambda b, t: (0, 0, 0))
    in_specs = [
        pl.BlockSpec(memory_space=pltpu.SMEM),
        pl.BlockSpec((1, tm, D_MODEL), lambda b, t: (b, t, 0)),
        pl.BlockSpec((tm, RET_DK), lambda b, t: (t, 0)),
        pl.BlockSpec((tm, RET_DK), lambda b, t: (t, 0)),
        pl.BlockSpec((1, 1, N_MEM, XA_W), lambda b, t: (layer, b, 0, 0)),
        pl.BlockSpec((1, 1, N_MEM, XA_W), lambda b, t: (layer, b, 0, 0)),
        row(D_MODEL),
        pl.BlockSpec((D_MODEL, D_IN), lambda b, t: (0, 0)),
        row(RET_W),
        pl.BlockSpec((CONV_K, CONV_W), lambda b, t: (0, 0)),
        row(CONV_W), row(CONV_W), row(CONV_W),
        row(XA_W),
        pl.BlockSpec((D_MIX, D_MODEL), lambda b, t: (0, 0)),
        const3((RET_HEADS, RET_CHUNK, RET_CHUNK)),
        const3((RET_HEADS, RET_CHUNK, RET_DV)),
        const3((RET_HEADS, RET_CHUNK, RET_DK)),
        row(D_MODEL),
    ]
    out_specs = [
        pl.BlockSpec((1, tm, D_MODEL), lambda b, t: (b, t, 0)),
        pl.BlockSpec((1, 1, RET_HEADS, RET_DK, RET_DV), lambda b, t: (0, b, 0, 0, 0)),
        pl.BlockSpec((1, 1, CONV_HIST, CONV_W), lambda b, t: (0, b, 0, 0)),
    ]
    out_shape = [
        jax.ShapeDtypeStruct((batch, seq, D_MODEL), F32),
        jax.ShapeDtypeStruct((1, batch, RET_HEADS, RET_DK, RET_DV), F32),
        jax.ShapeDtypeStruct((1, batch, CONV_HIST, CONV_W), F32),
    ]
    scratch = [
        pltpu.VMEM((RET_HEADS, RET_DK, RET_DV), F32),
        pltpu.VMEM((CONV_PAD + tm, CONV_W), F32),
        pltpu.VMEM((tm, CONV_W), F32),
        pltpu.VMEM((tm, RET_W), BF16),
        pltpu.VMEM((tm, RET_W), F32),
        pltpu.VMEM((tm, RET_W), BF16),
        pltpu.VMEM((tm, RET_W), F32),
        pltpu.VMEM((tm, CONV_W + 2 * XA_W), F32),
        pltpu.VMEM((tm, D_MIX), BF16),
    ]
    return pl.pallas_call(
        functools.partial(_prompt_layer_kernel, final, n_tiles),
        grid=(batch, n_tiles),
        in_specs=in_specs,
        out_specs=out_specs,
        out_shape=out_shape,
        scratch_shapes=scratch,
        compiler_params=pltpu.CompilerParams(
            dimension_semantics=("arbitrary", "arbitrary"),
            vmem_limit_bytes=VMEM_LIMIT_BYTES),
        name=f"prompt_layer{layer}",
    )(cd, x, cosf, sins, mk, mv, ng, win, gn, cw, cb, lng, lnb, xan, wout, dec, wq, wk, fng)


def _sample_in_kernel(x_ref, ng_ref, win_ref, z_ref):
    z_ref[...] = _dot(_rms(x_ref[...], ng_ref[...]).astype(BF16), win_ref[...])


def _sample_in(x, ng, win):
    n = x.shape[0]
    return pl.pallas_call(
        _sample_in_kernel,
        out_shape=jax.ShapeDtypeStruct((n, D_IN), F32),
        compiler_params=pltpu.CompilerParams(vmem_limit_bytes=VMEM_LIMIT_BYTES),
        name="sample_in",
    )(x, ng, win)


def _sample_mid_kernel(x_ref, mix_ref, wout_ref, ng_ref, win_ref, h_ref, z_ref):
    h = x_ref[...] + _dot(mix_ref[...], wout_ref[...])
    h_ref[...] = h
    z_ref[...] = _dot(_rms(h, ng_ref[...]).astype(BF16), win_ref[...])


def _sample_mid(x, mix, wout, ng, win):
    n = x.shape[0]
    return pl.pallas_call(
        _sample_mid_kernel,
        out_shape=[jax.ShapeDtypeStruct((n, D_MODEL), F32),
                   jax.ShapeDtypeStruct((n, D_IN), F32)],
        compiler_params=pltpu.CompilerParams(vmem_limit_bytes=VMEM_LIMIT_BYTES),
        name="sample_mid",
    )(x, mix, wout, ng, win)


def _sample_out_kernel(x_ref, mix_ref, wout_ref, fng_ref, y_ref):
    y_ref[...] = _rms(x_ref[...] + _dot(mix_ref[...], wout_ref[...]), fng_ref[...])


def _sample_out(x, mix, wout, fng):
    n = x.shape[0]
    return pl.pallas_call(
        _sample_out_kernel,
        out_shape=jax.ShapeDtypeStruct((n, D_MODEL), F32),
        compiler_params=pltpu.CompilerParams(vmem_limit_bytes=VMEM_LIMIT_BYTES),
        name="sample_out",
    )(x, mix, wout, fng)


def _sample_mix_kernel(n_tok, n_alias,
                       cd_ref, z_ref, s0_ref, cbuf_ref, mkt_ref, mvt_ref, cos_ref, sin_ref,
                       gn_ref, wh_ref, wl_ref, cb_ref, lng_ref, lnb_ref, xan_ref,
                       dec_ref, wq_ref, wk_ref, *refs):
    mix_ref, sret_ref, sconv_ref = refs[n_alias:]
    grp = SUBLANES // n_tok
    rows = grp * n_tok
    cosf = cos_ref[...]
    sins = sin_ref[...]
    head = _head_lane_ids()
    row_id = lax.broadcasted_iota(jnp.int32, (rows, 1), 0)
    row_seq = row_id // n_tok
    row_tok = row_id - row_seq * n_tok
    row4_seq = lax.broadcasted_iota(jnp.int32, (XA_HEADS * rows, 1), 0) % rows // n_tok

    def pick(parts, seq_of_row):
        out = parts[0]
        for s in range(1, grp):
            out = jnp.where(seq_of_row == s, parts[s], out)
        return out

    for g in range(SAMPLE_BLOCK // grp):
        seqs = [g * grp + s for s in range(grp)]
        rs = slice(rows * g, rows * (g + 1))
        z = z_ref[rs, :]

        for hh in range(RET_HEADS):
            cq = slice(C_RQ + RET_DK * hh, C_RQ + RET_DK * (hh + 1))
            ck = slice(C_RK + RET_DK * hh, C_RK + RET_DK * (hh + 1))
            cv = slice(C_RV + RET_DV * hh, C_RV + RET_DV * (hh + 1))
            cg = slice(C_RG + RET_DV * hh, C_RG + RET_DV * (hh + 1))
            co = slice(RET_DV * hh, RET_DV * (hh + 1))
            qh = _rope(z[:, cq], cosf, sins).astype(BF16)
            kf = _rope(z[:, ck], cosf, sins) * (RET_DK ** -0.5)
            vh = z[:, cv].astype(BF16)
            sc = _dot_nt(qh, kf.astype(BF16)) * dec_ref[hh]
            inner = _dot(sc.astype(BF16), vh)
            kw = kf * wk_ref[hh]
            crosses = []
            for s, b in enumerate(seqs):
                s_prev = s0_ref[0, b, hh]
                crosses.append(_dot(qh, s_prev.astype(BF16)))
                kv = _dot_tn(jnp.where(row_seq == s, kw, 0.0).astype(BF16), vh)
                sret_ref[0, b, hh] = cd_ref[hh] * s_prev + kv
            cross = pick(crosses, row_seq) * wq_ref[hh]
            o = _standardize(inner + cross) * gn_ref[:, co]
            mix_ref[rs, co] = (o * _silu(z[:, cg])).astype(BF16)

        u = z[:, C_CA:C_CA + CONV_W] * _sigmoid(z[:, C_CB:C_CB + CONV_W])
        c = u * wl_ref[0:1, :] + cb_ref[...]
        for lag in range(1, n_tok):
            c = c + jnp.where(row_tok >= lag, pltpu.roll(u, lag, 0), 0.0) * wl_ref[lag:lag + 1, :]
        for s, b in enumerate(seqs):
            hist = cbuf_ref[0, b]
            for i in range(n_tok):
                hi = jnp.sum(hist * wh_ref[i], axis=0, keepdims=True)
                c = c + jnp.where(row_id == s * n_tok + i, hi, 0.0)
            sconv_ref[0, b, 0:CONV_HIST - n_tok, :] = cbuf_ref[0, b, n_tok:CONV_HIST, :]
            sconv_ref[0, b, CONV_HIST - n_tok:CONV_HIST, :] = u[s * n_tok:(s + 1) * n_tok, :]
        cn = _standardize(c) * lng_ref[...] + lnb_ref[...]
        mix_ref[rs, RET_W:RET_W + CONV_W] = (
            _silu(cn) * _silu(z[:, C_CG:C_CG + CONV_W])).astype(BF16)

        xq = z[:, C_XQ:C_XQ + XA_W]
        q4 = jnp.concatenate([jnp.where(head == hh, xq, 0.0) for hh in range(XA_HEADS)],
                             axis=0).astype(BF16)
        sc = pick([_dot(q4, mkt_ref[0, b].astype(BF16)) for b in seqs], row4_seq)
        sc = sc * (XA_DH ** -0.5)
        e = jnp.exp(sc - jnp.max(sc, axis=-1, keepdims=True))
        ssum = jnp.sum(e, axis=-1, keepdims=True)
        eb = e.astype(BF16)
        o4 = pick([_dot_nt(eb, mvt_ref[0, b].astype(BF16)) for b in seqs], row4_seq) / ssum
        a = jnp.zeros((rows, XA_W), F32)
        for hh in range(XA_HEADS):
            a = jnp.where(head == hh, o4[rows * hh:rows * (hh + 1), :], a)
        gate = _silu(z[:, C_XG:C_XG + XA_W])
        mix_ref[rs, RET_W + CONV_W:D_MIX] = _xattn_norm_gate(
            a, xan_ref[...], gate, rows).astype(BF16)


def _sample_mix(layer, n_tok, z, s0, cbuf, mkt, mvt, cosf, sins, gn, wh, wl, cb, lng, lnb,
                xan, dec, wq, wk, cd, prev_states):
    depth, batch = s0.shape[0], s0.shape[1]
    bb = SAMPLE_BLOCK
    rows = SUBLANES
    row = lambda n: pl.BlockSpec((1, n), lambda i: (0, 0))
    const2 = lambda shp: pl.BlockSpec(shp, lambda i: (0, 0))
    const3 = lambda shp: pl.BlockSpec(shp, lambda i: (0, 0, 0))
    in_specs = [
        pl.BlockSpec(memory_space=pltpu.SMEM),
        pl.BlockSpec((bb * n_tok, D_IN), lambda i: (i, 0)),
        pl.BlockSpec((1, bb, RET_HEADS, RET_DK, RET_DV), lambda i: (layer, i, 0, 0, 0)),
        pl.BlockSpec((1, bb, CONV_HIST, CONV_W), lambda i: (layer, i, 0, 0)),
        pl.BlockSpec((1, bb, XA_W, N_MEM), lambda i: (layer, i, 0, 0)),
        pl.BlockSpec((1, bb, XA_W, N_MEM), lambda i: (layer, i, 0, 0)),
        const2((rows, RET_DK)), const2((rows, RET_DK)),
        row(RET_W),
        const3((n_tok, CONV_HIST, CONV_W)),
        const2((n_tok, CONV_W)),
        row(CONV_W), row(CONV_W), row(CONV_W),
        row(XA_W),
        const3((RET_HEADS, rows, rows)),
        const3((RET_HEADS, rows, RET_DV)),
        const3((RET_HEADS, rows, RET_DK)),
    ]
    n_fixed = len(in_specs)
    n_alias = len(prev_states)
    in_specs += [pl.BlockSpec(memory_space=pl.ANY)] * n_alias
    out_specs = [
        pl.BlockSpec((bb * n_tok, D_MIX), lambda i: (i, 0)),
        pl.BlockSpec((1, bb, RET_HEADS, RET_DK, RET_DV), lambda i: (layer, i, 0, 0, 0)),
        pl.BlockSpec((1, bb, CONV_HIST, CONV_W), lambda i: (layer, i, 0, 0)),
    ]
    out_shape = [
        jax.ShapeDtypeStruct((batch * n_tok, D_MIX), BF16),
        jax.ShapeDtypeStruct((depth, batch, RET_HEADS, RET_DK, RET_DV), F32),
        jax.ShapeDtypeStruct((depth, batch, CONV_HIST, CONV_W), F32),
    ]
    return pl.pallas_call(
        functools.partial(_sample_mix_kernel, n_tok, n_alias),
        grid=(batch // bb,),
        in_specs=in_specs,
        out_specs=out_specs,
        out_shape=out_shape,
        input_output_aliases={n_fixed + k: 1 + k for k in range(n_alias)},
        compiler_params=pltpu.CompilerParams(
            dimension_semantics=("arbitrary",),
            vmem_limit_bytes=VMEM_LIMIT_BYTES),
        name=f"sample_mix{layer}",
    )(cd, z, s0, cbuf, mkt, mvt, cosf, sins, gn, wh, wl, cb, lng, lnb, xan, dec, wq, wk,
      *prev_states)


def _rope_tables(pos):
    half = RET_DK // 2
    inv = ROPE_BASE ** (-jnp.arange(half, dtype=F32) / half)
    ang = pos.astype(F32)[:, None] * inv[None, :]
    cos, sin = jnp.cos(ang), jnp.sin(ang)
    return jnp.concatenate([cos, cos], axis=-1), jnp.concatenate([-sin, sin], axis=-1)


def _decay_tables(chunk):
    lg = jnp.log(1.0 - jnp.exp2(-5.0 - jnp.arange(RET_HEADS, dtype=F32)))
    idx = jnp.arange(chunk, dtype=F32)
    diff = idx[:, None] - idx[None, :]
    dec = jnp.where(diff[None] >= 0,
                    jnp.exp(jnp.maximum(diff, 0.0)[None] * lg[:, None, None]), 0.0)
    wk = jnp.exp((chunk - 1.0 - idx)[None, :] * lg[:, None])
    wq = jnp.exp((idx + 1.0)[None, :] * lg[:, None])
    cd = jnp.exp(chunk * lg)
    wk = jnp.broadcast_to(wk[:, :, None], (RET_HEADS, chunk, RET_DK))
    wq = jnp.broadcast_to(wq[:, :, None], (RET_HEADS, chunk, RET_DV))
    return dec, wq, wk, cd


def _group_tables(n_tok, pos0):
    grp = SUBLANES // n_tok
    cosf, sins = _rope_tables(pos0 + jnp.arange(n_tok))
    dec, wq, wk, cd = _decay_tables(n_tok)
    tile_rows = lambda a: jnp.concatenate([a] * grp, axis=-2)
    eye = jnp.eye(grp, dtype=F32)
    dec = jnp.einsum("st,hij->hsitj", eye, dec).reshape(RET_HEADS, grp * n_tok, grp * n_tok)
    return tile_rows(cosf), tile_rows(sins), dec, tile_rows(wq), tile_rows(wk), cd


def _conv_step_taps(cw, n_tok):
    wh = jnp.stack([jnp.pad(cw[:CONV_HIST - i], ((i, 0), (0, 0))) for i in range(n_tok)])
    wl = jnp.stack([cw[CONV_HIST - l] for l in range(n_tok)])
    return wh, wl


def kernel(x_prompt, x_sample, mem_prompt, state_ret, state_conv, cache_mem_k, cache_mem_v,
           norm_g, w_in, ret_gn_g, conv_w, conv_b, conv_ln_g, conv_ln_b, xa_norm_g,
           mem_norm_g, w_mk, w_mv, w_out, final_norm_g):
    depth = w_in.shape[0]
    batch, seq, _ = x_prompt.shape
    dbatch, dseq, _ = x_sample.shape

    w_in_b = w_in.astype(BF16)
    w_out_b = w_out.astype(BF16)
    w_mkt_b = w_mk.swapaxes(1, 2).astype(BF16)
    w_mvt_b = w_mv.swapaxes(1, 2).astype(BF16)
    row = lambda a: a.reshape(1, -1)
    fng = row(final_norm_g)

    p_mk, p_mv = _mem_kv(mem_prompt, mem_norm_g, w_mkt_b, w_mvt_b)
    cos_p, sin_p = _rope_tables(jnp.arange(seq))
    dec_p, wq_p, wk_p, cd_p = _decay_tables(RET_CHUNK if seq % RET_CHUNK == 0 else seq)
    hp = x_prompt
    p_ret, p_conv = [], []
    for l in range(depth):
        hp, sr, sc = _prompt_layer(
            l, l == depth - 1, hp, cos_p, sin_p, p_mk, p_mv, row(norm_g[l]), w_in_b[l],
            row(ret_gn_g[l]), conv_w[l], row(conv_b[l]), row(conv_ln_g[l]),
            row(conv_ln_b[l]), row(xa_norm_g[l]), w_out_b[l], dec_p, wq_p, wk_p, cd_p, fng)
        p_ret.append(sr)
        p_conv.append(sc)
    y_prompt = hp
    prompt_state_ret = jnp.concatenate(p_ret, axis=0)
    prompt_state_conv = jnp.concatenate(p_conv, axis=0)

    assert SUBLANES % dseq == 0 and dseq % RET_CHUNK != 0
    cos_s, sin_s, dec_s, wq_s, wk_s, cd_s = _group_tables(dseq, PAST_LEN)
    to_hd_m = lambda c: c.transpose(0, 1, 3, 4, 2).reshape(depth, dbatch, XA_W, N_MEM)
    mkt_s, mvt_s = to_hd_m(cache_mem_k), to_hd_m(cache_mem_v)
    hs = x_sample.reshape(dbatch * dseq, D_MODEL)
    z = _sample_in(hs, row(norm_g[0]), w_in_b[0])
    states = ()
    y_sample = None
    for l in range(depth):
        wh, wl = _conv_step_taps(conv_w[l], dseq)
        mix, *states = _sample_mix(
            l, dseq, z, state_ret, state_conv, mkt_s, mvt_s, cos_s, sin_s, row(ret_gn_g[l]),
            wh, wl, row(conv_b[l]), row(conv_ln_g[l]), row(conv_ln_b[l]), row(xa_norm_g[l]),
            dec_s, wq_s, wk_s, cd_s, tuple(states))
        if l + 1 < depth:
            hs, z = _sample_mid(hs, mix, w_out_b[l], row(norm_g[l + 1]), w_in_b[l + 1])
        else:
            y_sample = _sample_out(hs, mix, w_out_b[l], fng).reshape(dbatch, dseq, D_MODEL)

    from_hd_m = lambda c: c.reshape(depth, batch, XA_HEADS, XA_DH, N_MEM).transpose(0, 1, 4, 2, 3)
    return (y_prompt, y_sample, prompt_state_ret, prompt_state_conv,
            from_hd_m(p_mk), from_hd_m(p_mv), states[0], states[1])
```

```python
import functools
import math

import jax
import jax.numpy as jnp
from jax import lax
from jax.experimental import pallas as pl
from jax.experimental.pallas import tpu as pltpu

F32 = jnp.float32
BF16 = jnp.bfloat16

D_MODEL = 1024
N_MEM = 256
RET_HEADS = 4
RET_DK = 128
RET_DV = 128
RET_W = RET_HEADS * RET_DV
RET_CHUNK = 128
CONV_W = 256
CONV_K = 31
XA_HEADS = 4
XA_DH = 64
XA_W = XA_HEADS * XA_DH
D_MIX = RET_W + CONV_W + XA_W
ROPE_BASE = 10000.0
EPS = 1e-6
PAST_LEN = 16384

C_RQ, C_RK, C_RV, C_RG = 0, 512, 1024, 1536
C_CA, C_CB, C_CG = 2048, 2304, 2560
C_XQ, C_XG = 2816, 3072
D_IN = 3328

VMEM_LIMIT_BYTES = 56 * 1024 * 1024
SUBLANES = 8
CONV_PAD = 32
CONV_HIST = CONV_K - 1
PROMPT_TILE = 512
CONV_ROWS = 64
PROJ_BLOCK = 256
SAMPLE_BLOCK = 8


def _rms(x, g):
    return x * lax.rsqrt(jnp.mean(x * x, axis=-1, keepdims=True) + EPS) * g


def _standardize(x):
    mu = jnp.mean(x, axis=-1, keepdims=True)
    d = x - mu
    var = jnp.mean(d * d, axis=-1, keepdims=True)
    return d * lax.rsqrt(var + EPS)


def _sigmoid(x):
    return 1.0 / (1.0 + jnp.exp(-x))


def _silu(x):
    return x * _sigmoid(x)


def _dot(a, b):
    return jnp.dot(a, b, preferred_element_type=F32)


def _dot_nt(a, b):
    return lax.dot_general(a, b, (((1,), (1,)), ((), ())), preferred_element_type=F32)


def _dot_tn(a, b):
    return lax.dot_general(a, b, (((0,), (0,)), ((), ())), preferred_element_type=F32)


def _rope(x, cosf, sins):
    return x * cosf + pltpu.roll(x, RET_DK // 2, 1) * sins


def _zero_row_after(z):
    tile = z[z.shape[0] - SUBLANES:, z.shape[1] - 128:]
    bits = pltpu.bitcast(tile, jnp.uint32)
    sixteen = jnp.uint32(16)
    zero = lax.shift_right_logical(lax.shift_right_logical(bits, sixteen), sixteen)
    row = pltpu.bitcast(zero, F32)[0:1, :]
    return jnp.concatenate([row, row], axis=1)


def _head_lane_ids():
    return lax.broadcasted_iota(jnp.int32, (1, XA_W), 1) // XA_DH


def _xattn_norm_gate(a, xan, gate, rows):
    head = _head_lane_ids()
    a2 = a * a
    ms = jnp.zeros((rows, XA_W), F32)
    for hh in range(XA_HEADS):
        m = head == hh
        ssq = jnp.sum(jnp.where(m, a2, 0.0), axis=-1, keepdims=True) * (1.0 / XA_DH)
        ms = jnp.where(m, ssq, ms)
    return a * lax.rsqrt(ms + EPS) * xan * gate


def _xattn_heads(q, mkt, mvt, xan, gate, rows):
    head = _head_lane_ids()
    a = jnp.zeros((rows, XA_W), F32)
    for hh in range(XA_HEADS):
        m = head == hh
        qm = jnp.where(m, q, 0.0).astype(BF16)
        sc = _dot(qm, mkt) * (XA_DH ** -0.5)
        e = jnp.exp(sc - jnp.max(sc, axis=-1, keepdims=True))
        s = jnp.sum(e, axis=-1, keepdims=True)
        pv = _dot_nt(e.astype(BF16), mvt)
        a = jnp.where(m, pv / s, a)
    return _xattn_norm_gate(a, xan, gate, rows)


def _mem_kv_kernel(mem_ref, g_ref, wkt_ref, wvt_ref, kt_ref, vt_ref):
    m = _rms(mem_ref[0], g_ref[0]).astype(BF16)
    kt_ref[0, 0] = _dot_nt(wkt_ref[0], m)
    vt_ref[0, 0] = _dot_nt(wvt_ref[0], m)


def _mem_kv(mem, g, wkt, wvt):
    depth, batch = g.shape[0], mem.shape[0]
    out = jax.ShapeDtypeStruct((depth, batch, XA_W, N_MEM), F32)
    return pl.pallas_call(
        _mem_kv_kernel,
        grid=(depth, batch),
        in_specs=[
            pl.BlockSpec((1, N_MEM, D_MODEL), lambda l, b: (b, 0, 0)),
            pl.BlockSpec((1, 1, D_MODEL), lambda l, b: (l, 0, 0)),
            pl.BlockSpec((1, XA_W, D_MODEL), lambda l, b: (l, 0, 0)),
            pl.BlockSpec((1, XA_W, D_MODEL), lambda l, b: (l, 0, 0)),
        ],
        out_specs=[
            pl.BlockSpec((1, 1, XA_W, N_MEM), lambda l, b: (l, b, 0, 0)),
            pl.BlockSpec((1, 1, XA_W, N_MEM), lambda l, b: (l, b, 0, 0)),
        ],
        out_shape=[out, out],
        compiler_params=pltpu.CompilerParams(
            dimension_semantics=("arbitrary", "arbitrary"),
            vmem_limit_bytes=VMEM_LIMIT_BYTES),
        name="mem_kv",
    )(mem, g.reshape(depth, 1, D_MODEL), wkt, wvt)


def _prompt_layer_kernel(final, n_tiles,
                         cd_ref, x_ref, cos_ref, sin_ref, mk_ref, mv_ref, ng_ref, win_ref,
                         gn_ref, cw_ref, cb_ref, lng_ref, lnb_ref, xan_ref, wout_ref,
                         dec_ref, wq_ref, wk_ref, fng_ref,
                         y_ref, sret_ref, sconv_ref,
                         s_scr, ext_scr, c_scr, q_scr, k_scr, v_scr, g_scr, xg_scr, mix_scr):
    tm = PROMPT_TILE
    t = pl.program_id(1)

    @pl.when(t == 0)
    def _():
        s_scr[...] = jnp.zeros_like(s_scr)
        ext_scr[0:CONV_PAD, :] = jnp.zeros((CONV_PAD, CONV_W), F32)

    x = x_ref[0]
    hb = _rms(x, ng_ref[...]).astype(BF16)

    def proj(a, b):
        return _dot(hb, win_ref[:, a:b])

    cosf = cos_ref[...]
    sins = sin_ref[...]

    u = proj(C_CA, C_CA + CONV_W) * _sigmoid(proj(C_CB, C_CB + CONV_W))
    ext_scr[CONV_PAD:CONV_PAD + tm, :] = u

    pb = PROJ_BLOCK

    def post_q(z, c):
        for i in range(pb // RET_DK):
            cols = slice(c + RET_DK * i, c + RET_DK * (i + 1))
            zs = z[:, RET_DK * i:RET_DK * (i + 1)]
            q_scr[:, cols] = _rope(zs, cosf, sins).astype(BF16)

    def post_k(z, c):
        for i in range(pb // RET_DK):
            cols = slice(c + RET_DK * i, c + RET_DK * (i + 1))
            zs = z[:, RET_DK * i:RET_DK * (i + 1)]
            k_scr[:, cols] = _rope(zs, cosf, sins) * (RET_DK ** -0.5)

    def post_v(z, c):
        v_scr[:, c:c + pb] = z.astype(BF16)

    def post_g(z, c):
        g_scr[:, c:c + pb] = _silu(z)

    def post_cg(z, c):
        xg_scr[:, c:c + pb] = _silu(z)

    def post_xq(z, c):
        xg_scr[:, CONV_W + c:CONV_W + c + pb] = z

    def post_xg(z, c):
        xg_scr[:, CONV_W + XA_W + c:CONV_W + XA_W + c + pb] = _silu(z)

    blocks = []
    for col0, width, post in ((C_RQ, RET_W, post_q), (C_RK, RET_W, post_k),
                              (C_RV, RET_W, post_v), (C_RG, RET_W, post_g),
                              (C_CG, CONV_W, post_cg), (C_XQ, XA_W, post_xq),
                              (C_XG, XA_W, post_xg)):
        blocks += [(col0, c, post) for c in range(0, width, pb)]

    first = CONV_PAD - CONV_HIST
    conv_acc = {}

    def tap_group(c0, r, wait_zero):
        n_rows = CONV_ROWS if r == 0 else CONV_ROWS + SUBLANES
        part = None
        for a in range((first + CONV_K - 1 - r) // SUBLANES + 1):
            j = SUBLANES * a + r - first
            if 0 <= j < CONV_K:
                w_j = cw_ref[j:j + 1, :]
                if wait_zero is not None:
                    w_j = w_j + wait_zero
                lo = c0 + SUBLANES * a
                term = ext_scr[lo:lo + n_rows, :] * w_j
                part = term if part is None else part + term
        if r == 0:
            conv_acc[c0] = part + cb_ref[...]
        else:
            conv_acc[c0] = conv_acc[c0] + part[r:r + CONV_ROWS, :]
        if r == SUBLANES - 1:
            c_scr[c0:c0 + CONV_ROWS, :] = conv_acc.pop(c0)

    groups = [(c0, r) for c0 in range(0, tm, CONV_ROWS) for r in range(SUBLANES)]
    per_block = -(-len(groups) // len(blocks))
    after_proj = None
    for k, (col0, c, post) in enumerate(blocks):
        for c0, r in groups[per_block * k:per_block * (k + 1)]:
            tap_group(c0, r, after_proj)
        z = proj(col0 + c, col0 + c + pb)
        post(z, c)
        after_proj = _zero_row_after(z)

    for c in range(tm // RET_CHUNK):
        rows = slice(RET_CHUNK * c, RET_CHUNK * (c + 1))
        for hh in range(RET_HEADS):
            cols = slice(RET_DK * hh, RET_DK * (hh + 1))
            qh = q_scr[rows, cols]
            kf = k_scr[rows, cols]
            vh = v_scr[rows, cols]
            s_prev = s_scr[hh]
            sc = _dot_nt(qh, kf.astype(BF16)) * dec_ref[hh]
            inner = _dot(sc.astype(BF16), vh)
            cross = _dot(qh, s_prev.astype(BF16)) * wq_ref[hh]
            kv = _dot_tn((kf * wk_ref[hh]).astype(BF16), vh)
            s_scr[hh] = cd_ref[hh] * s_prev + kv
            o = _standardize(inner + cross) * gn_ref[:, cols]
            mix_scr[rows, cols] = (o * g_scr[rows, cols]).astype(BF16)

    @pl.when(t == n_tiles - 1)
    def _():
        sconv_ref[0, 0] = ext_scr[tm + first:tm + CONV_PAD, :]

    ext_scr[0:CONV_PAD, :] = ext_scr[tm:tm + CONV_PAD, :]
    cn = _standardize(c_scr[...]) * lng_ref[...] + lnb_ref[...]
    mix_scr[:, RET_W:RET_W + CONV_W] = (_silu(cn) * xg_scr[:, 0:CONV_W]).astype(BF16)

    a = _xattn_heads(xg_scr[:, CONV_W:CONV_W + XA_W],
                     mk_ref[0, 0].astype(BF16), mv_ref[0, 0].astype(BF16),
                     xan_ref[...], xg_scr[:, CONV_W + XA_W:CONV_W + 2 * XA_W], tm)
    mix_scr[:, RET_W + CONV_W:D_MIX] = a.astype(BF16)

    out = x + _dot(mix_scr[...], wout_ref[...])
    if final:
        out = _rms(out, fng_ref[...])
    y_ref[0] = out

    @pl.when(t == n_tiles - 1)
    def _():
        sret_ref[0, 0] = s_scr[...]


def _prompt_layer(layer, final, x, cosf, sins, mk, mv, ng, win, gn, cw, cb, lng, lnb, xan,
                  wout, dec, wq, wk, cd, fng):
    batch, seq, _ = x.shape
    tm = PROMPT_TILE
    n_tiles = seq // tm
    row = lambda n: pl.BlockSpec((1, n), lambda b, t: (0, 0))
    const3 = lambda shp: pl.BlockSpec(shp, lambda b, t: (0, 0, 0))
    in_specs = [
        pl.BlockSpec(memory_space=pltpu.SMEM),
        pl.BlockSpec((1, tm, D_MODEL), lambda b, t: (b, t, 0)),
        pl.BlockSpec((tm, RET_DK), lambda b, t: (t, 0)),
        pl.BlockSpec((tm, RET_DK), lambda b, t: (t, 0)),
        pl.BlockSpec((1, 1, N_MEM, XA_W), lambda b, t: (layer, b, 0, 0)),
        pl.BlockSpec((1, 1, N_MEM, XA_W), lambda b, t: (layer, b, 0, 0)),
        row(D_MODEL),
        pl.BlockSpec((D_MODEL, D_IN), lambda b, t: (0, 0)),
        row(RET_W),
        pl.BlockSpec((CONV_K, CONV_W), lambda b, t: (0, 0)),
        row(CONV_W), row(CONV_W), row(CONV_W),
        row(XA_W),
        pl.BlockSpec((D_MIX, D_MODEL), lambda b, t: (0, 0)),
        const3((RET_HEADS, RET_CHUNK, RET_CHUNK)),
        const3((RET_HEADS, RET_CHUNK, RET_DV)),
        const3((RET_HEADS, RET_CHUNK, RET_DK)),
        row(D_MODEL),
    ]
    out_specs = [
        pl.BlockSpec((1, tm, D_MODEL), lambda b, t: (b, t, 0)),
        pl.BlockSpec((1, 1, RET_HEADS, RET_DK, RET_DV), lambda b, t: (0, b, 0, 0, 0)),
        pl.BlockSpec((1, 1, CONV_HIST, CONV_W), lambda b, t: (0, b, 0, 0)),
    ]
    out_shape = [
        jax.ShapeDtypeStruct((batch, seq, D_MODEL), F32),
        jax.ShapeDtypeStruct((1, batch, RET_HEADS, RET_DK, RET_DV), F32),
        jax.ShapeDtypeStruct((1, batch, CONV_HIST, CONV_W), F32),
    ]
    scratch = [
        pltpu.VMEM((RET_HEADS, RET_DK, RET_DV), F32),
        pltpu.VMEM((CONV_PAD + tm, CONV_W), F32),
        pltpu.VMEM((tm, CONV_W), F32),
        pltpu.VMEM((tm, RET_W), BF16),
        pltpu.VMEM((tm, RET_W), F32),
        pltpu.VMEM((tm, RET_W), BF16),
        pltpu.VMEM((tm, RET_W), F32),
        pltpu.VMEM((tm, CONV_W + 2 * XA_W), F32),
        pltpu.VMEM((tm, D_MIX), BF16),
    ]
    return pl.pallas_call(
        functools.partial(_prompt_layer_kernel, final, n_tiles),
        grid=(batch, n_tiles),
        in_specs=in_specs,
        out_specs=out_specs,
        out_shape=out_shape,
        scratch_shapes=scratch,
        compiler_params=pltpu.CompilerParams(
            dimension_semantics=("arbitrary", "arbitrary"),
            vmem_limit_bytes=VMEM_LIMIT_BYTES),
        name=f"prompt_layer{layer}",
    )(cd, x, cosf, sins, mk, mv, ng, win, gn, cw, cb, lng, lnb, xan, wout, dec, wq, wk, fng)


def _sample_in_kernel(x_ref, ng_ref, win_ref, z_ref):
    z_ref[...] = _dot(_rms(x_ref[...], ng_ref[...]).astype(BF16), win_ref[...])


def _sample_in(x, ng, win):
    n = x.shape[0]
    return pl.pallas_call(
        _sample_in_kernel,
        out_shape=jax.ShapeDtypeStruct((n, D_IN), F32),
        compiler_params=pltpu.CompilerParams(vmem_limit_bytes=VMEM_LIMIT_BYTES),
        name="sample_in",
    )(x, ng, win)


def _sample_mid_kernel(x_ref, mix_ref, wout_ref, ng_ref, win_ref, h_ref, z_ref):
    h = x_ref[...] + _dot(mix_ref[...], wout_ref[...])
    h_ref[...] = h
    z_ref[...] = _dot(_rms(h, ng_ref[...]).astype(BF16), win_ref[...])


def _sample_mid(x, mix, wout, ng, win):
    n = x.shape[0]
    return pl.pallas_call(
        _sample_mid_kernel,
        out_shape=[jax.ShapeDtypeStruct((n, D_MODEL), F32),
                   jax.ShapeDtypeStruct((n, D_IN), F32)],
        compiler_params=pltpu.CompilerParams(vmem_limit_bytes=VMEM_LIMIT_BYTES),
        name="sample_mid",
    )(x, mix, wout, ng, win)


def _sample_out_kernel(x_ref, mix_ref, wout_ref, fng_ref, y_ref):
    y_ref[...] = _rms(x_ref[...] + _dot(mix_ref[...], wout_ref[...]), fng_ref[...])


def _sample_out(x, mix, wout, fng):
    n = x.shape[0]
    return pl.pallas_call(
        _sample_out_kernel,
        out_shape=jax.ShapeDtypeStruct((n, D_MODEL), F32),
        compiler_params=pltpu.CompilerParams(vmem_limit_bytes=VMEM_LIMIT_BYTES),
        name="sample_out",
    )(x, mix, wout, fng)


def _sample_mix_kernel(n_tok, n_alias,
                       cd_ref, z_ref, s0_ref, cbuf_ref, mkt_ref, mvt_ref, cos_ref, sin_ref,
                       gn_ref, wh_ref, wl_ref, cb_ref, lng_ref, lnb_ref, xan_ref,
                       dec_ref, wq_ref, wk_ref, *refs):
    mix_ref, sret_ref, sconv_ref = refs[n_alias:]
    grp = SUBLANES // n_tok
    rows = grp * n_tok
    cosf = cos_ref[...]
    sins = sin_ref[...]
    head = _head_lane_ids()
    row_id = lax.broadcasted_iota(jnp.int32, (rows, 1), 0)
    row_seq = row_id // n_tok
    row_tok = row_id - row_seq * n_tok
    row4_seq = lax.broadcasted_iota(jnp.int32, (XA_HEADS * rows, 1), 0) % rows // n_tok

    def pick(parts, seq_of_row):
        out = parts[0]
        for s in range(1, grp):
            out = jnp.where(seq_of_row == s, parts[s], out)
        return out

    for g in range(SAMPLE_BLOCK // grp):
        seqs = [g * grp + s for s in range(grp)]
        rs = slice(rows * g, rows * (g + 1))
        z = z_ref[rs, :]

        for hh in range(RET_HEADS):
            cq = slice(C_RQ + RET_DK * hh, C_RQ + RET_DK * (hh + 1))
            ck = slice(C_RK + RET_DK * hh, C_RK + RET_DK * (hh + 1))
            cv = slice(C_RV + RET_DV * hh, C_RV + RET_DV * (hh + 1))
            cg = slice(C_RG + RET_DV * hh, C_RG + RET_DV * (hh + 1))
            co = slice(RET_DV * hh, RET_DV * (hh + 1))
            qh = _rope(z[:, cq], cosf, sins).astype(BF16)
            kf = _rope(z[:, ck], cosf, sins) * (RET_DK ** -0.5)
            vh = z[:, cv].astype(BF16)
            sc = _dot_nt(qh, kf.astype(BF16)) * dec_ref[hh]
            inner = _dot(sc.astype(BF16), vh)
            kw = kf * wk_ref[hh]
            crosses = []
            for s, b in enumerate(seqs):
                s_prev = s0_ref[0, b, hh]
                crosses.append(_dot(qh, s_prev.astype(BF16)))
                kv = _dot_tn(jnp.where(row_seq == s, kw, 0.0).astype(BF16), vh)
                sret_ref[0, b, hh] = cd_ref[hh] * s_prev + kv
            cross = pick(crosses, row_seq) * wq_ref[hh]
            o = _standardize(inner + cross) * gn_ref[:, co]
            mix_ref[rs, co] = (o * _silu(z[:, cg])).astype(BF16)

        u = z[:, C_CA:C_CA + CONV_W] * _sigmoid(z[:, C_CB:C_CB + CONV_W])
        c = u * wl_ref[0:1, :] + cb_ref[...]
        for lag in range(1, n_tok):
            c = c + jnp.where(row_tok >= lag, pltpu.roll(u, lag, 0), 0.0) * wl_ref[lag:lag + 1, :]
        for s, b in enumerate(seqs):
            hist = cbuf_ref[0, b]
            for i in range(n_tok):
                hi = jnp.sum(hist * wh_ref[i], axis=0, keepdims=True)
                c = c + jnp.where(row_id == s * n_tok + i, hi, 0.0)
            sconv_ref[0, b, 0:CONV_HIST - n_tok, :] = cbuf_ref[0, b, n_tok:CONV_HIST, :]
            sconv_ref[0, b, CONV_HIST - n_tok:CONV_HIST, :] = u[s * n_tok:(s + 1) * n_tok, :]
        cn = _standardize(c) * lng_ref[...] + lnb_ref[...]
        mix_ref[rs, RET_W:RET_W + CONV_W] = (
            _silu(cn) * _silu(z[:, C_CG:C_CG + CONV_W])).astype(BF16)

        xq = z[:, C_XQ:C_XQ + XA_W]
        q4 = jnp.concatenate([jnp.where(head == hh, xq, 0.0) for hh in range(XA_HEADS)],
                             axis=0).astype(BF16)
        sc = pick([_dot(q4, mkt_ref[0, b].astype(BF16)) for b in seqs], row4_seq)
        sc = sc * (XA_DH ** -0.5)
        e = jnp.exp(sc - jnp.max(sc, axis=-1, keepdims=True))
        ssum = jnp.sum(e, axis=-1, keepdims=True)
        eb = e.astype(BF16)
        o4 = pick([_dot_nt(eb, mvt_ref[0, b].astype(BF16)) for b in seqs], row4_seq) / ssum
        a = jnp.zeros((rows, XA_W), F32)
        for hh in range(XA_HEADS):
            a = jnp.where(head == hh, o4[rows * hh:rows * (hh + 1), :], a)
        gate = _silu(z[:, C_XG:C_XG + XA_W])
        mix_ref[rs, RET_W + CONV_W:D_MIX] = _xattn_norm_gate(
            a, xan_ref[...], gate, rows).astype(BF16)


def _sample_mix(layer, n_tok, z, s0, cbuf, mkt, mvt, cosf, sins, gn, wh, wl, cb, lng, lnb,
                xan, dec, wq, wk, cd, prev_states):
    depth, batch = s0.shape[0], s0.shape[1]
    bb = SAMPLE_BLOCK
    rows = SUBLANES
    row = lambda n: pl.BlockSpec((1, n), lambda i: (0, 0))
    const2 = lambda shp: pl.BlockSpec(shp, lambda i: (0, 0))
    const3 = lambda shp: pl.BlockSpec(shp, lambda i: (0, 0, 0))
    in_specs = [
        pl.BlockSpec(memory_space=pltpu.SMEM),
        pl.BlockSpec((bb * n_tok, D_IN), lambda i: (i, 0)),
        pl.BlockSpec((1, bb, RET_HEADS, RET_DK, RET_DV), lambda i: (layer, i, 0, 0, 0)),
        pl.BlockSpec((1, bb, CONV_HIST, CONV_W), lambda i: (layer, i, 0, 0)),
        pl.BlockSpec((1, bb, XA_W, N_MEM), lambda i: (layer, i, 0, 0)),
        pl.BlockSpec((1, bb, XA_W, N_MEM), lambda i: (layer, i, 0, 0)),
        const2((rows, RET_DK)), const2((rows, RET_DK)),
        row(RET_W),
        const3((n_tok, CONV_HIST, CONV_W)),
        const2((n_tok, CONV_W)),
        row(CONV_W), row(CONV_W), row(CONV_W),
        row(XA_W),
        const3((RET_HEADS, rows, rows)),
        const3((RET_HEADS, rows, RET_DV)),
        const3((RET_HEADS, rows, RET_DK)),
    ]
    n_fixed = len(in_specs)
    n_alias = len(prev_states)
    in_specs += [pl.BlockSpec(memory_space=pl.ANY)] * n_alias
    out_specs = [
        pl.BlockSpec((bb * n_tok, D_MIX), lambda i: (i, 0)),
        pl.BlockSpec((1, bb, RET_HEADS, RET_DK, RET_DV), lambda i: (layer, i, 0, 0, 0)),
        pl.BlockSpec((1, bb, CONV_HIST, CONV_W), lambda i: (layer, i, 0, 0)),
    ]
    out_shape = [
        jax.ShapeDtypeStruct((batch * n_tok, D_MIX), BF16),
        jax.ShapeDtypeStruct((depth, batch, RET_HEADS, RET_DK, RET_DV), F32),
        jax.ShapeDtypeStruct((depth, batch, CONV_HIST, CONV_W), F32),
    ]
    return pl.pallas_call(
        functools.partial(_sample_mix_kernel, n_tok, n_alias),
        grid=(batch // bb,),
        in_specs=in_specs,
        out_specs=out_specs,
        out_shape=out_shape,
        input_output_aliases={n_fixed + k: 1 + k for k in range(n_alias)},
        compiler_params=pltpu.CompilerParams(
            dimension_semantics=("arbitrary",),
            vmem_limit_bytes=VMEM_LIMIT_BYTES),
        name=f"sample_mix{layer}",
    )(cd, z, s0, cbuf, mkt, mvt, cosf, sins, gn, wh, wl, cb, lng, lnb, xan, dec, wq, wk,
      *prev_states)


def _rope_tables(pos):
    half = RET_DK // 2
    inv = ROPE_BASE ** (-jnp.arange(half, dtype=F32) / half)
    ang = pos.astype(F32)[:, None] * inv[None, :]
    cos, sin = jnp.cos(ang), jnp.sin(ang)
    return jnp.concatenate([cos, cos], axis=-1), jnp.concatenate([-sin, sin], axis=-1)


def _decay_tables(chunk):
    lg = jnp.log(1.0 - jnp.exp2(-5.0 - jnp.arange(RET_HEADS, dtype=F32)))
    idx = jnp.arange(chunk, dtype=F32)
    diff = idx[:, None] - idx[None, :]
    dec = jnp.where(diff[None] >= 0,
                    jnp.exp(jnp.maximum(diff, 0.0)[None] * lg[:, None, None]), 0.0)
    wk = jnp.exp((chunk - 1.0 - idx)[None, :] * lg[:, None])
    wq = jnp.exp((idx + 1.0)[None, :] * lg[:, None])
    cd = jnp.exp(chunk * lg)
    wk = jnp.broadcast_to(wk[:, :, None], (RET_HEADS, chunk, RET_DK))
    wq = jnp.broadcast_to(wq[:, :, None], (RET_HEADS, chunk, RET_DV))
    return dec, wq, wk, cd


def _group_tables(n_tok, pos0):
    grp = SUBLANES // n_tok
    cosf, sins = _rope_tables(pos0 + jnp.arange(n_tok))
    dec, wq, wk, cd = _decay_tables(n_tok)
    tile_rows = lambda a: jnp.concatenate([a] * grp, axis=-2)
    eye = jnp.eye(grp, dtype=F32)
    dec = jnp.einsum("st,hij->hsitj", eye, dec).reshape(RET_HEADS, grp * n_tok, grp * n_tok)
    return tile_rows(cosf), tile_rows(sins), dec, tile_rows(wq), tile_rows(wk), cd


def _conv_step_taps(cw, n_tok):
    wh = jnp.stack([jnp.pad(cw[:CONV_HIST - i], ((i, 0), (0, 0))) for i in range(n_tok)])
    wl = jnp.stack([cw[CONV_HIST - l] for l in range(n_tok)])
    return wh, wl


def kernel(x_prompt, x_sample, mem_prompt, state_ret, state_conv, cache_mem_k, cache_mem_v,
           norm_g, w_in, ret_gn_g, conv_w, conv_b, conv_ln_g, conv_ln_b, xa_norm_g,
           mem_norm_g, w_mk, w_mv, w_out, final_norm_g):
    depth = w_in.shape[0]
    batch, seq, _ = x_prompt.shape
    dbatch, dseq, _ = x_sample.shape

    w_in_b = w_in.astype(BF16)
    w_out_b = w_out.astype(BF16)
    w_mkt_b = w_mk.swapaxes(1, 2).astype(BF16)
    w_mvt_b = w_mv.swapaxes(1, 2).astype(BF16)
    row = lambda a: a.reshape(1, -1)
    fng = row(final_norm_g)

    p_mk, p_mv = _mem_kv(mem_prompt, mem_norm_g, w_mkt_b, w_mvt_b)
    cos_p, sin_p = _rope_tables(jnp.arange(seq))
    dec_p, wq_p, wk_p, cd_p = _decay_tables(RET_CHUNK if seq % RET_CHUNK == 0 else seq)
    hp = x_prompt
    p_ret, p_conv = [], []
    for l in range(depth):
        hp, sr, sc = _prompt_layer(
            l, l == depth - 1, hp, cos_p, sin_p, p_mk, p_mv, row(norm_g[l]), w_in_b[l],
            row(ret_gn_g[l]), conv_w[l], row(conv_b[l]), row(conv_ln_g[l]),
            row(conv_ln_b[l]), row(xa_norm_g[l]), w_out_b[l], dec_p, wq_p, wk_p, cd_p, fng)
        p_ret.append(sr)
        p_conv.append(sc)
    y_prompt = hp
    prompt_state_ret = jnp.concatenate(p_ret, axis=0)
    prompt_state_conv = jnp.concatenate(p_conv, axis=0)

    assert SUBLANES % dseq == 0 and dseq % RET_CHUNK != 0
    cos_s, sin_s, dec_s, wq_s, wk_s, cd_s = _group_tables(dseq, PAST_LEN)
    to_hd_m = lambda c: c.transpose(0, 1, 3, 4, 2).reshape(depth, dbatch, XA_W, N_MEM)
    mkt_s, mvt_s = to_hd_m(cache_mem_k), to_hd_m(cache_mem_v)
    hs = x_sample.reshape(dbatch * dseq, D_MODEL)
    z = _sample_in(hs, row(norm_g[0]), w_in_b[0])
    states = ()
    y_sample = None
    for l in range(depth):
        wh, wl = _conv_step_taps(conv_w[l], dseq)
        mix, *states = _sample_mix(
            l, dseq, z, state_ret, state_conv, mkt_s, mvt_s, cos_s, sin_s, row(ret_gn_g[l]),
            wh, wl, row(conv_b[l]), row(conv_ln_g[l]), row(conv_ln_b[l]), row(xa_norm_g[l]),
            dec_s, wq_s, wk_s, cd_s, tuple(states))
        if l + 1 < depth:
            hs, z = _sample_mid(hs, mix, w_out_b[l], row(norm_g[l + 1]), w_in_b[l + 1])
        else:
            y_sample = _sample_out(hs, mix, w_out_b[l], fng).reshape(dbatch, dseq, D_MODEL)

    from_hd_m = lambda c: c.reshape(depth, batch, XA_HEADS, XA_DH, N_MEM).transpose(0, 1, 4, 2, 3)
    return (y_prompt, y_sample, prompt_state_ret, prompt_state_conv,
            from_hd_m(p_mk), from_hd_m(p_mv), states[0], states[1])
```

```python
import functools

import numpy as np

import jax
import jax.numpy as jnp
from jax import lax
from jax.experimental import pallas as pl
from jax.experimental.pallas import tpu as pltpu

F32 = jnp.float32
BF16 = jnp.bfloat16

D_MODEL = 1024
N_MEM = 256
RET_HEADS = 4
RET_DK = 128
RET_DV = 128
RET_W = RET_HEADS * RET_DV
RET_CHUNK = 128
CONV_W = 256
CONV_K = 31
XA_HEADS = 4
XA_DH = 64
XA_W = XA_HEADS * XA_DH
D_MIX = RET_W + CONV_W + XA_W
ROPE_BASE = 10000.0
EPS = 1e-6
PAST_LEN = 16384
LOG2_E = 1.4426950408889634

C_RQ, C_RK, C_RV, C_RG = 0, 512, 1024, 1536
C_CA, C_CB, C_CG = 2048, 2304, 2560
C_XQ, C_XG = 2816, 3072
D_IN = 3328

VMEM_LIMIT_BYTES = 56 * 1024 * 1024
SUBLANES = 8
LANES = 128
CONV_PAD = 32
CONV_HIST = CONV_K - 1
PROMPT_TILE = 512
CONV_ROWS = 64
PROJ_BLOCK = 256
SAMPLE_BLOCK = 16


def _rms(x, g):
    return x * lax.rsqrt(jnp.mean(x * x, axis=-1, keepdims=True) + EPS) * g


def _standardize(x):
    mu = jnp.mean(x, axis=-1, keepdims=True)
    d = x - mu
    var = jnp.mean(d * d, axis=-1, keepdims=True)
    return d * lax.rsqrt(var + EPS)


def _sigmoid(x):
    return 1.0 / (1.0 + jnp.exp(-x))


def _silu(x):
    return x * _sigmoid(x)


def _dot(a, b):
    return jnp.dot(a, b, preferred_element_type=F32)


def _dot_nt(a, b):
    return lax.dot_general(a, b, (((1,), (1,)), ((), ())), preferred_element_type=F32)


def _dot_tn(a, b):
    return lax.dot_general(a, b, (((0,), (0,)), ((), ())), preferred_element_type=F32)


def _rope(x, cosf, sins):
    return x * cosf + pltpu.roll(x, RET_DK // 2, 1) * sins


def _zero_row_after(z):
    tile = z[z.shape[0] - SUBLANES:, z.shape[1] - LANES:]
    bits = pltpu.bitcast(tile, jnp.uint32)
    sixteen = jnp.uint32(16)
    zero = lax.shift_right_logical(lax.shift_right_logical(bits, sixteen), sixteen)
    row = pltpu.bitcast(zero, F32)[0:1, :]
    return jnp.concatenate([row, row], axis=1)


def _head_lane_ids():
    return lax.broadcasted_iota(jnp.int32, (1, XA_W), 1) // XA_DH


def _xattn_norm_gate(a, xan, gate, rows):
    head = _head_lane_ids()
    a2 = a * a
    ms = jnp.zeros((rows, XA_W), F32)
    for hh in range(XA_HEADS):
        m = head == hh
        ssq = jnp.sum(jnp.where(m, a2, 0.0), axis=-1, keepdims=True) * (1.0 / XA_DH)
        ms = jnp.where(m, ssq, ms)
    return a * lax.rsqrt(ms + EPS) * xan * gate


def _xattn_heads(q, mkt, mvt, xan, gate, rows):
    head = _head_lane_ids()
    q = q * (LOG2_E * XA_DH ** -0.5)
    scores = [_dot(jnp.where(head == hh, q, 0.0).astype(BF16), mkt) for hh in range(XA_HEADS)]
    pvs, sums = [], []
    for sc in scores:
        e = jnp.exp2(sc - jnp.max(sc, axis=-1, keepdims=True))
        sums.append(jnp.sum(e, axis=-1, keepdims=True))
        pvs.append(_dot_nt(e.astype(BF16), mvt))
    a = jnp.zeros((rows, XA_W), F32)
    for hh in range(XA_HEADS):
        a = jnp.where(head == hh, pvs[hh] / sums[hh], a)
    return _xattn_norm_gate(a, xan, gate, rows)


def _layer_row(ref, layer):
    return ref[layer:layer + 1, :]


def _params(**kw):
    return pltpu.CompilerParams(vmem_limit_bytes=VMEM_LIMIT_BYTES, **kw)


def _mem_kv_kernel(mem_ref, g_ref, wkt_ref, wvt_ref, kt_ref, vt_ref):
    layer = pl.program_id(0)
    m = _rms(mem_ref[0], g_ref[pl.ds(layer, 1), :]).astype(BF16)
    kt_ref[0, 0] = _dot_nt(wkt_ref[...], m)
    vt_ref[0, 0] = _dot_nt(wvt_ref[...], m)


def _mem_kv(mem, g, wkt, wvt):
    depth, batch = g.shape[0], mem.shape[0]
    out = jax.ShapeDtypeStruct((depth, batch, XA_W, N_MEM), F32)
    return pl.pallas_call(
        _mem_kv_kernel,
        grid=(depth, batch),
        in_specs=[
            pl.BlockSpec((1, N_MEM, D_MODEL), lambda l, b: (b, 0, 0)),
            pl.BlockSpec((depth, D_MODEL), lambda l, b: (0, 0)),
            pl.BlockSpec((None, XA_W, D_MODEL), lambda l, b: (l, 0, 0)),
            pl.BlockSpec((None, XA_W, D_MODEL), lambda l, b: (l, 0, 0)),
        ],
        out_specs=[
            pl.BlockSpec((1, 1, XA_W, N_MEM), lambda l, b: (l, b, 0, 0)),
            pl.BlockSpec((1, 1, XA_W, N_MEM), lambda l, b: (l, b, 0, 0)),
        ],
        out_shape=[out, out],
        compiler_params=_params(dimension_semantics=("arbitrary", "arbitrary")),
        name="mem_kv",
    )(mem, g, wkt, wvt)


def _prompt_layer_kernel(layer, final, n_tiles, n_alias,
                         cd_ref, x_ref, cos_ref, sin_ref, mk_ref, mv_ref, ng_ref, win_ref,
                         gn_ref, cw_ref, cb_ref, lng_ref, lnb_ref, xan_ref, wout_ref,
                         dec_ref, wq_ref, wk_ref, fng_ref, *refs):
    (y_ref, sret_ref, sconv_ref,
     s_scr, ext_scr, c_scr, q_scr, k_scr, v_scr, g_scr, xg_scr, mix_scr) = refs[n_alias:]
    tm = PROMPT_TILE
    t = pl.program_id(1)

    @pl.when(t == 0)
    def _():
        s_scr[...] = jnp.zeros_like(s_scr)
        ext_scr[0:CONV_PAD, :] = jnp.zeros((CONV_PAD, CONV_W), F32)

    x = x_ref[0]
    hb = _rms(x, _layer_row(ng_ref, layer)).astype(BF16)

    def proj(a, b):
        return _dot(hb, win_ref[:, a:b])

    cosf = cos_ref[...]
    sins = sin_ref[...]

    u = proj(C_CA, C_CA + CONV_W) * _sigmoid(proj(C_CB, C_CB + CONV_W))
    ext_scr[CONV_PAD:CONV_PAD + tm, :] = u

    pb = PROJ_BLOCK

    def post_q(z, c):
        for i in range(pb // RET_DK):
            cols = slice(c + RET_DK * i, c + RET_DK * (i + 1))
            zs = z[:, RET_DK * i:RET_DK * (i + 1)]
            q_scr[:, cols] = _rope(zs, cosf, sins).astype(BF16)

    def post_k(z, c):
        for i in range(pb // RET_DK):
            cols = slice(c + RET_DK * i, c + RET_DK * (i + 1))
            zs = z[:, RET_DK * i:RET_DK * (i + 1)]
            k_scr[:, cols] = _rope(zs, cosf, sins) * (RET_DK ** -0.5)

    def post_v(z, c):
        v_scr[:, c:c + pb] = z.astype(BF16)

    def post_g(z, c):
        g_scr[:, c:c + pb] = _silu(z)

    def post_cg(z, c):
        xg_scr[:, c:c + pb] = _silu(z)

    def post_xq(z, c):
        xg_scr[:, CONV_W + c:CONV_W + c + pb] = z

    def post_xg(z, c):
        xg_scr[:, CONV_W + XA_W + c:CONV_W + XA_W + c + pb] = _silu(z)

    blocks = []
    for col0, width, post in ((C_RQ, RET_W, post_q), (C_RK, RET_W, post_k),
                              (C_RV, RET_W, post_v), (C_RG, RET_W, post_g),
                              (C_CG, CONV_W, post_cg), (C_XQ, XA_W, post_xq),
                              (C_XG, XA_W, post_xg)):
        blocks += [(col0, c, post) for c in range(0, width, pb)]

    first = CONV_PAD - CONV_HIST
    conv_acc = {}
    conv_bias = _layer_row(cb_ref, layer)

    def tap_group(c0, r, wait_zero):
        n_rows = CONV_ROWS if r == 0 else CONV_ROWS + SUBLANES
        part = None
        for a in range((first + CONV_K - 1 - r) // SUBLANES + 1):
            j = SUBLANES * a + r - first
            if 0 <= j < CONV_K:
                w_j = cw_ref[j:j + 1, :]
                if wait_zero is not None:
                    w_j = w_j + wait_zero
                lo = c0 + SUBLANES * a
                term = ext_scr[lo:lo + n_rows, :] * w_j
                part = term if part is None else part + term
        if r == 0:
            conv_acc[c0] = part + conv_bias
        else:
            conv_acc[c0] = conv_acc[c0] + part[r:r + CONV_ROWS, :]
        if r == SUBLANES - 1:
            c_scr[c0:c0 + CONV_ROWS, :] = conv_acc.pop(c0)

    groups = [(c0, r) for c0 in range(0, tm, CONV_ROWS) for r in range(SUBLANES)]
    per_block = -(-len(groups) // len(blocks))
    after_proj = None
    for k, (col0, c, post) in enumerate(blocks):
        for c0, r in groups[per_block * k:per_block * (k + 1)]:
            tap_group(c0, r, after_proj)
        z = proj(col0 + c, col0 + c + pb)
        post(z, c)
        after_proj = _zero_row_after(z)

    chunks = range(tm // RET_CHUNK)
    heads = range(RET_HEADS)
    crow = lambda c: slice(RET_CHUNK * c, RET_CHUNK * (c + 1))
    hcol = lambda hh: slice(RET_DK * hh, RET_DK * (hh + 1))
    sc, kv, inner, cross = {}, {}, {}, {}
    for c in chunks:
        for hh in heads:
            kf = k_scr[crow(c), hcol(hh)]
            vh = v_scr[crow(c), hcol(hh)]
            sc[c, hh] = _dot_nt(q_scr[crow(c), hcol(hh)], kf.astype(BF16))
            kv[c, hh] = _dot_tn((kf * wk_ref[hh]).astype(BF16), vh)
    for hh in heads:
        state = s_scr[hh]
        for c in chunks:
            cross[c, hh] = _dot(q_scr[crow(c), hcol(hh)], state.astype(BF16)) * wq_ref[hh]
            state = cd_ref[hh] * state + kv[c, hh]
        s_scr[hh] = state
    for c in chunks:
        for hh in heads:
            scaled = (sc[c, hh] * dec_ref[hh]).astype(BF16)
            inner[c, hh] = _dot(scaled, v_scr[crow(c), hcol(hh)])
    for c in chunks:
        for hh in heads:
            o = _standardize(inner[c, hh] + cross[c, hh]) * gn_ref[layer:layer + 1, hcol(hh)]
            mix_scr[crow(c), hcol(hh)] = (o * g_scr[crow(c), hcol(hh)]).astype(BF16)

    @pl.when(t == n_tiles - 1)
    def _():
        sconv_ref[0, 0] = ext_scr[tm + first:tm + CONV_PAD, :]

    ext_scr[0:CONV_PAD, :] = ext_scr[tm:tm + CONV_PAD, :]
    cn = _standardize(c_scr[...]) * _layer_row(lng_ref, layer) + _layer_row(lnb_ref, layer)
    mix_scr[:, RET_W:RET_W + CONV_W] = (_silu(cn) * xg_scr[:, 0:CONV_W]).astype(BF16)

    a = _xattn_heads(xg_scr[:, CONV_W:CONV_W + XA_W],
                     mk_ref[0, 0].astype(BF16), mv_ref[0, 0].astype(BF16),
                     _layer_row(xan_ref, layer),
                     xg_scr[:, CONV_W + XA_W:CONV_W + 2 * XA_W], tm)
    mix_scr[:, RET_W + CONV_W:D_MIX] = a.astype(BF16)

    out = x + _dot(mix_scr[...], wout_ref[...])
    if final:
        out = _rms(out, fng_ref[...])
    y_ref[0] = out

    @pl.when(t == n_tiles - 1)
    def _():
        sret_ref[0, 0] = s_scr[...]


def _prompt_layer(layer, final, x, cosf, sins, mk, mv, ng, win, gn, cw, cb, lng, lnb, xan,
                  wout, dec, wq, wk, cd, fng, prev_states):
    batch, seq, _ = x.shape
    depth = win.shape[0]
    tm = PROMPT_TILE
    n_tiles = seq // tm
    whole = lambda a: pl.BlockSpec(a.shape, lambda b, t: (0,) * a.ndim)
    in_specs = [
        pl.BlockSpec(memory_space=pltpu.SMEM),
        pl.BlockSpec((1, tm, D_MODEL), lambda b, t: (b, t, 0)),
        pl.BlockSpec((tm, RET_DK), lambda b, t: (t, 0)),
        pl.BlockSpec((tm, RET_DK), lambda b, t: (t, 0)),
        pl.BlockSpec((1, 1, XA_W, N_MEM), lambda b, t: (layer, b, 0, 0)),
        pl.BlockSpec((1, 1, XA_W, N_MEM), lambda b, t: (layer, b, 0, 0)),
        whole(ng),
        pl.BlockSpec((None, D_MODEL, D_IN), lambda b, t: (layer, 0, 0)),
        whole(gn),
        pl.BlockSpec((None, CONV_K, CONV_W), lambda b, t: (layer, 0, 0)),
        whole(cb), whole(lng), whole(lnb),
        whole(xan),
        pl.BlockSpec((None, D_MIX, D_MODEL), lambda b, t: (layer, 0, 0)),
        whole(dec), whole(wq), whole(wk),
        whole(fng),
    ]
    n_fixed = len(in_specs)
    n_alias = len(prev_states)
    in_specs += [pl.BlockSpec(memory_space=pl.ANY)] * n_alias
    out_specs = [
        pl.BlockSpec((1, tm, D_MODEL), lambda b, t: (b, t, 0)),
        pl.BlockSpec((1, 1, RET_HEADS, RET_DK, RET_DV), lambda b, t: (layer, b, 0, 0, 0)),
        pl.BlockSpec((1, 1, CONV_HIST, CONV_W), lambda b, t: (layer, b, 0, 0)),
    ]
    out_shape = [
        jax.ShapeDtypeStruct((batch, seq, D_MODEL), F32),
        jax.ShapeDtypeStruct((depth, batch, RET_HEADS, RET_DK, RET_DV), F32),
        jax.ShapeDtypeStruct((depth, batch, CONV_HIST, CONV_W), F32),
    ]
    scratch = [
        pltpu.VMEM((RET_HEADS, RET_DK, RET_DV), F32),
        pltpu.VMEM((CONV_PAD + tm, CONV_W), F32),
        pltpu.VMEM((tm, CONV_W), F32),
        pltpu.VMEM((tm, RET_W), BF16),
        pltpu.VMEM((tm, RET_W), F32),
        pltpu.VMEM((tm, RET_W), BF16),
        pltpu.VMEM((tm, RET_W), F32),
        pltpu.VMEM((tm, CONV_W + 2 * XA_W), F32),
        pltpu.VMEM((tm, D_MIX), BF16),
    ]
    return pl.pallas_call(
        functools.partial(_prompt_layer_kernel, layer, final, n_tiles, n_alias),
        grid=(batch, n_tiles),
        in_specs=in_specs,
        out_specs=out_specs,
        out_shape=out_shape,
        scratch_shapes=scratch,
        input_output_aliases={n_fixed + k: 1 + k for k in range(n_alias)},
        compiler_params=_params(dimension_semantics=("arbitrary", "arbitrary")),
        name=f"prompt_layer{layer}",
    )(cd, x, cosf, sins, mk, mv, ng, win, gn, cw, cb, lng, lnb, xan, wout, dec, wq, wk, fng,
      *prev_states)


def _sample_in_kernel(layer, x_ref, ng_ref, win_ref, z_ref):
    hb = _rms(x_ref[...], _layer_row(ng_ref, layer)).astype(BF16)
    z_ref[...] = _dot(hb, win_ref[...])


def _sample_in(layer, x, ng, win):
    n = x.shape[0]
    whole = lambda a: pl.BlockSpec(a.shape, lambda j: (0,) * a.ndim)
    return pl.pallas_call(
        functools.partial(_sample_in_kernel, layer),
        grid=(1,),
        in_specs=[whole(x), whole(ng),
                  pl.BlockSpec((None, D_MODEL, D_IN), lambda j: (layer, 0, 0))],
        out_specs=pl.BlockSpec((n, D_IN), lambda j: (0, 0)),
        out_shape=jax.ShapeDtypeStruct((n, D_IN), F32),
        compiler_params=_params(dimension_semantics=("arbitrary",)),
        name="sample_in",
    )(x, ng, win)


def _sample_mid_kernel(layer, x_ref, mix_ref, wout_ref, ng_ref, win_ref, h_ref, z_ref):
    h = x_ref[...] + _dot(mix_ref[...], wout_ref[...])
    h_ref[...] = h
    z_ref[...] = _dot(_rms(h, _layer_row(ng_ref, layer + 1)).astype(BF16), win_ref[...])


def _sample_mid(layer, x, mix, wout, ng, win):
    n = x.shape[0]
    whole = lambda a: pl.BlockSpec(a.shape, lambda j: (0,) * a.ndim)
    return pl.pallas_call(
        functools.partial(_sample_mid_kernel, layer),
        grid=(1,),
        in_specs=[whole(x), whole(mix),
                  pl.BlockSpec((None, D_MIX, D_MODEL), lambda j: (layer, 0, 0)),
                  whole(ng),
                  pl.BlockSpec((None, D_MODEL, D_IN), lambda j: (layer + 1, 0, 0))],
        out_specs=[pl.BlockSpec((n, D_MODEL), lambda j: (0, 0)),
                   pl.BlockSpec((n, D_IN), lambda j: (0, 0))],
        out_shape=[jax.ShapeDtypeStruct((n, D_MODEL), F32),
                   jax.ShapeDtypeStruct((n, D_IN), F32)],
        compiler_params=_params(dimension_semantics=("arbitrary",)),
        name="sample_mid",
    )(x, mix, wout, ng, win)


def _sample_out_kernel(x_ref, mix_ref, wout_ref, fng_ref, y_ref):
    y_ref[...] = _rms(x_ref[...] + _dot(mix_ref[...], wout_ref[...]), fng_ref[...])


def _sample_out(layer, x, mix, wout, fng):
    n = x.shape[0]
    whole = lambda a: pl.BlockSpec(a.shape, lambda j: (0,) * a.ndim)
    return pl.pallas_call(
        _sample_out_kernel,
        grid=(1,),
        in_specs=[whole(x), whole(mix),
                  pl.BlockSpec((None, D_MIX, D_MODEL), lambda j: (layer, 0, 0)), whole(fng)],
        out_specs=pl.BlockSpec((n, D_MODEL), lambda j: (0, 0)),
        out_shape=jax.ShapeDtypeStruct((n, D_MODEL), F32),
        compiler_params=_params(dimension_semantics=("arbitrary",)),
        name="sample_out",
    )(x, mix, wout, fng)


def _sample_mix_kernel(layer, n_tok, n_alias,
                       cd_ref, z_ref, s0_ref, cbuf_ref, mkt_ref, mvt_ref, cos_ref, sin_ref,
                       gn_ref, cw_ref, cb_ref, lng_ref, lnb_ref, xan_ref,
                       dec_ref, wq_ref, wk_ref, *refs):
    mix_ref, sret_ref, sconv_ref = refs[n_alias:]
    grp = SUBLANES // n_tok
    rows = grp * n_tok
    keep = CONV_HIST - n_tok
    cosf = cos_ref[...]
    sins = sin_ref[...]
    head = _head_lane_ids()
    row_id = lax.broadcasted_iota(jnp.int32, (rows, 1), 0)
    row_seq = row_id // n_tok
    row_tok = row_id - row_seq * n_tok
    row4_seq = lax.broadcasted_iota(jnp.int32, (XA_HEADS * rows, 1), 0) % rows // n_tok

    def pick(parts, seq_of_row):
        out = parts[0]
        for s in range(1, grp):
            out = jnp.where(seq_of_row == s, parts[s], out)
        return out

    groups = range(SAMPLE_BLOCK // grp)
    heads = range(RET_HEADS)
    seqs = [[g * grp + s for s in range(grp)] for g in groups]
    rs = [slice(rows * g, rows * (g + 1)) for g in groups]
    hcol = lambda c0, hh: slice(c0 + RET_DK * hh, c0 + RET_DK * (hh + 1))

    qh, kf, vh, sc, xsc = {}, {}, {}, {}, {}
    for g in groups:
        for hh in heads:
            qh[g, hh] = _rope(z_ref[rs[g], hcol(C_RQ, hh)], cosf, sins).astype(BF16)
            kf[g, hh] = _rope(z_ref[rs[g], hcol(C_RK, hh)], cosf, sins) * (RET_DK ** -0.5)
            vh[g, hh] = z_ref[rs[g], hcol(C_RV, hh)].astype(BF16)
            sc[g, hh] = _dot_nt(qh[g, hh], kf[g, hh].astype(BF16))
        xq = z_ref[rs[g], C_XQ:C_XQ + XA_W] * (LOG2_E * XA_DH ** -0.5)
        q4 = jnp.concatenate([jnp.where(head == hh, xq, 0.0) for hh in range(XA_HEADS)],
                             axis=0).astype(BF16)
        xsc[g] = [_dot(q4, mkt_ref[0, b].astype(BF16)) for b in seqs[g]]

    for g in groups:
        z_a = z_ref[rs[g], C_CA:C_CA + CONV_W]
        u = z_a * _sigmoid(z_ref[rs[g], C_CB:C_CB + CONV_W])
        c = u * cw_ref[CONV_HIST:CONV_K, :] + _layer_row(cb_ref, layer)
        for lag in range(1, n_tok):
            tap = cw_ref[CONV_HIST - lag:CONV_K - lag, :]
            c = c + jnp.where(row_tok >= lag, pltpu.roll(u, lag, 0), 0.0) * tap
        for s, b in enumerate(seqs[g]):
            for i in range(n_tok):
                window = cbuf_ref[0, b, i:CONV_HIST, :] * cw_ref[0:CONV_HIST - i, :]
                hi = jnp.sum(window, axis=0, keepdims=True)
                c = c + jnp.where(row_id == s * n_tok + i, hi, 0.0)
            sconv_ref[0, b, 0:keep, :] = cbuf_ref[0, b, n_tok:CONV_HIST, :]
            sconv_ref[0, b, keep:CONV_HIST, :] = u[s * n_tok:(s + 1) * n_tok, :]
        cn = _standardize(c) * _layer_row(lng_ref, layer) + _layer_row(lnb_ref, layer)
        gate = _silu(z_ref[rs[g], C_CG:C_CG + CONV_W])
        mix_ref[rs[g], RET_W:RET_W + CONV_W] = (_silu(cn) * gate).astype(BF16)

    inner, cross, xo, xsum = {}, {}, {}, {}
    for g in groups:
        for hh in heads:
            inner[g, hh] = _dot((sc[g, hh] * dec_ref[hh]).astype(BF16), vh[g, hh])
            kw = kf[g, hh] * wk_ref[hh]
            parts = []
            for s, b in enumerate(seqs[g]):
                s_prev = s0_ref[0, b, hh]
                parts.append(_dot(qh[g, hh], s_prev.astype(BF16)))
                kv = _dot_tn(jnp.where(row_seq == s, kw, 0.0).astype(BF16), vh[g, hh])
                sret_ref[0, b, hh] = cd_ref[hh] * s_prev + kv
            cross[g, hh] = pick(parts, row_seq) * wq_ref[hh]
        s4 = pick(xsc[g], row4_seq)
        e = jnp.exp2(s4 - jnp.max(s4, axis=-1, keepdims=True))
        xsum[g] = jnp.sum(e, axis=-1, keepdims=True)
        eb = e.astype(BF16)
        xo[g] = [_dot_nt(eb, mvt_ref[0, b].astype(BF16)) for b in seqs[g]]

    for g in groups:
        for hh in heads:
            co = hcol(0, hh)
            o = _standardize(inner[g, hh] + cross[g, hh]) * gn_ref[layer:layer + 1, co]
            gate = _silu(z_ref[rs[g], hcol(C_RG, hh)])
            mix_ref[rs[g], co] = (o * gate).astype(BF16)
        o4 = pick(xo[g], row4_seq) / xsum[g]
        a = jnp.zeros((rows, XA_W), F32)
        for hh in range(XA_HEADS):
            a = jnp.where(head == hh, o4[rows * hh:rows * (hh + 1), :], a)
        gate = _silu(z_ref[rs[g], C_XG:C_XG + XA_W])
        mix_ref[rs[g], RET_W + CONV_W:D_MIX] = _xattn_norm_gate(
            a, _layer_row(xan_ref, layer), gate, rows).astype(BF16)


def _sample_mix(layer, n_tok, z, s0, cbuf, mkt, mvt, cosf, sins, gn, cw, cb, lng, lnb,
                xan, dec, wq, wk, cd, prev_states):
    depth, batch = s0.shape[0], s0.shape[1]
    bb = SAMPLE_BLOCK
    whole = lambda a: pl.BlockSpec(a.shape, lambda i: (0,) * a.ndim)
    in_specs = [
        pl.BlockSpec(memory_space=pltpu.SMEM),
        pl.BlockSpec((bb * n_tok, D_IN), lambda i: (i, 0)),
        pl.BlockSpec((1, bb, RET_HEADS, RET_DK, RET_DV), lambda i: (layer, i, 0, 0, 0)),
        pl.BlockSpec((1, bb, CONV_HIST, CONV_W), lambda i: (layer, i, 0, 0)),
        pl.BlockSpec((1, bb, XA_W, N_MEM), lambda i: (layer, i, 0, 0)),
        pl.BlockSpec((1, bb, XA_W, N_MEM), lambda i: (layer, i, 0, 0)),
        whole(cosf), whole(sins),
        whole(gn),
        pl.BlockSpec((None, CONV_K, CONV_W), lambda i: (layer, 0, 0)),
        whole(cb), whole(lng), whole(lnb),
        whole(xan),
        whole(dec), whole(wq), whole(wk),
    ]
    n_fixed = len(in_specs)
    n_alias = len(prev_states)
    in_specs += [pl.BlockSpec(memory_space=pl.ANY)] * n_alias
    out_specs = [
        pl.BlockSpec((bb * n_tok, D_MIX), lambda i: (i, 0)),
        pl.BlockSpec((1, bb, RET_HEADS, RET_DK, RET_DV), lambda i: (layer, i, 0, 0, 0)),
        pl.BlockSpec((1, bb, CONV_HIST, CONV_W), lambda i: (layer, i, 0, 0)),
    ]
    out_shape = [
        jax.ShapeDtypeStruct((batch * n_tok, D_MIX), BF16),
        jax.ShapeDtypeStruct((depth, batch, RET_HEADS, RET_DK, RET_DV), F32),
        jax.ShapeDtypeStruct((depth, batch, CONV_HIST, CONV_W), F32),
    ]
    return pl.pallas_call(
        functools.partial(_sample_mix_kernel, layer, n_tok, n_alias),
        grid=(batch // bb,),
        in_specs=in_specs,
        out_specs=out_specs,
        out_shape=out_shape,
        input_output_aliases={n_fixed + k: 1 + k for k in range(n_alias)},
        compiler_params=_params(dimension_semantics=("arbitrary",)),
        name=f"sample_mix{layer}",
    )(cd, z, s0, cbuf, mkt, mvt, cosf, sins, gn, cw, cb, lng, lnb, xan, dec, wq, wk,
      *prev_states)


def _rope_tables(pos):
    half = RET_DK // 2
    inv = np.float64(ROPE_BASE) ** (-np.arange(half, dtype=np.float64) / half)
    ang = pos.astype(np.float64)[:, None] * inv[None, :]
    cos, sin = np.cos(ang), np.sin(ang)
    return (np.concatenate([cos, cos], axis=-1).astype(np.float32),
            np.concatenate([-sin, sin], axis=-1).astype(np.float32))


def _decay_tables(chunk):
    lg = np.log(1.0 - np.exp2(-5.0 - np.arange(RET_HEADS, dtype=np.float64)))
    idx = np.arange(chunk, dtype=np.float64)
    diff = idx[:, None] - idx[None, :]
    dec = np.where(diff[None] >= 0, np.exp(np.maximum(diff, 0.0)[None] * lg[:, None, None]), 0.0)
    wk = np.exp((chunk - 1.0 - idx)[None, :] * lg[:, None])
    wq = np.exp((idx + 1.0)[None, :] * lg[:, None])
    cd = np.exp(chunk * lg)
    wk = np.broadcast_to(wk[:, :, None], (RET_HEADS, chunk, RET_DK))
    wq = np.broadcast_to(wq[:, :, None], (RET_HEADS, chunk, RET_DV))
    f32 = lambda a: np.ascontiguousarray(a, dtype=np.float32)
    return f32(dec), f32(wq), f32(wk), f32(cd)


def _group_tables(n_tok, pos0):
    grp = SUBLANES // n_tok
    cosf, sins = _rope_tables(pos0 + np.arange(n_tok))
    dec, wq, wk, cd = _decay_tables(n_tok)
    tile_rows = lambda a: np.concatenate([a] * grp, axis=-2)
    eye = np.eye(grp, dtype=np.float32)
    dec = np.einsum("st,hij->hsitj", eye, dec).reshape(RET_HEADS, grp * n_tok, grp * n_tok)
    return tile_rows(cosf), tile_rows(sins), dec, tile_rows(wq), tile_rows(wk), cd


def kernel(x_prompt, x_sample, mem_prompt, state_ret, state_conv, cache_mem_k, cache_mem_v,
           norm_g, w_in, ret_gn_g, conv_w, conv_b, conv_ln_g, conv_ln_b, xa_norm_g,
           mem_norm_g, w_mk, w_mv, w_out, final_norm_g):
    depth = w_in.shape[0]
    batch, seq, _ = x_prompt.shape
    dbatch, dseq, _ = x_sample.shape

    w_in_b = w_in.astype(BF16)
    w_out_b = w_out.astype(BF16)
    w_mkt_b = w_mk.swapaxes(1, 2).astype(BF16)
    w_mvt_b = w_mv.swapaxes(1, 2).astype(BF16)
    fng = final_norm_g.reshape(1, D_MODEL)

    p_mk, p_mv = _mem_kv(mem_prompt, mem_norm_g, w_mkt_b, w_mvt_b)
    cos_p, sin_p = _rope_tables(np.arange(seq))
    dec_p, wq_p, wk_p, cd_p = _decay_tables(RET_CHUNK if seq % RET_CHUNK == 0 else seq)
    hp = x_prompt
    p_states = ()
    for l in range(depth):
        hp, *p_states = _prompt_layer(
            l, l == depth - 1, hp, cos_p, sin_p, p_mk, p_mv, norm_g, w_in_b, ret_gn_g,
            conv_w, conv_b, conv_ln_g, conv_ln_b, xa_norm_g, w_out_b, dec_p, wq_p, wk_p,
            cd_p, fng, tuple(p_states))
    y_prompt = hp

    assert SUBLANES % dseq == 0 and dseq % RET_CHUNK != 0
    cos_s, sin_s, dec_s, wq_s, wk_s, cd_s = _group_tables(dseq, PAST_LEN)
    to_hd_m = lambda c: c.transpose(0, 1, 3, 4, 2).reshape(depth, dbatch, XA_W, N_MEM)
    mkt_s, mvt_s = to_hd_m(cache_mem_k), to_hd_m(cache_mem_v)
    hs = x_sample.reshape(dbatch * dseq, D_MODEL)
    z = _sample_in(0, hs, norm_g, w_in_b)
    s_states = ()
    y_sample = None
    for l in range(depth):
        mix, *s_states = _sample_mix(
            l, dseq, z, state_ret, state_conv, mkt_s, mvt_s, cos_s, sin_s, ret_gn_g, conv_w,
            conv_b, conv_ln_g, conv_ln_b, xa_norm_g, dec_s, wq_s, wk_s, cd_s, tuple(s_states))
        if l + 1 < depth:
            hs, z = _sample_mid(l, hs, mix, w_out_b, norm_g, w_in_b)
        else:
            y_sample = _sample_out(l, hs, mix, w_out_b, fng).reshape(dbatch, dseq, D_MODEL)

    from_hd_m = lambda c: c.reshape(depth, batch, XA_HEADS, XA_DH, N_MEM).transpose(0, 1, 4, 2, 3)
    return (y_prompt, y_sample, p_states[0], p_states[1],
            from_hd_m(p_mk), from_hd_m(p_mv), s_states[0], s_states[1])
```

```python
import functools

import numpy as np

import jax
import jax.numpy as jnp
from jax import lax
from jax.experimental import pallas as pl
from jax.experimental.pallas import tpu as pltpu

F32 = jnp.float32
BF16 = jnp.bfloat16

D_MODEL = 1024
N_MEM = 256
RET_HEADS = 4
RET_DK = 128
RET_DV = 128
RET_W = RET_HEADS * RET_DV
RET_CHUNK = 128
CONV_W = 256
CONV_K = 31
XA_HEADS = 4
XA_DH = 64
XA_W = XA_HEADS * XA_DH
D_MIX = RET_W + CONV_W + XA_W
ROPE_BASE = 10000.0
EPS = 1e-6
PAST_LEN = 16384
LOG2_E = 1.4426950408889634

C_RQ, C_RK, C_RV, C_RG = 0, 512, 1024, 1536
C_CA, C_CB, C_CG = 2048, 2304, 2560
C_XQ, C_XG = 2816, 3072
D_IN = 3328

VMEM_LIMIT_BYTES = 56 * 1024 * 1024
SUBLANES = 8
LANES = 128
CONV_PAD = 32
CONV_HIST = CONV_K - 1
PROMPT_TILE = 512
CONV_ROWS = 64
PROJ_BLOCK = 256
RET_STAGE_CHUNKS = 2


def _rms(x, g):
    return x * lax.rsqrt(jnp.mean(x * x, axis=-1, keepdims=True) + EPS) * g


def _standardize(x):
    mu = jnp.mean(x, axis=-1, keepdims=True)
    d = x - mu
    var = jnp.mean(d * d, axis=-1, keepdims=True)
    return d * lax.rsqrt(var + EPS)


def _sigmoid(x):
    return 1.0 / (1.0 + jnp.exp(-x))


def _silu(x):
    return x * _sigmoid(x)


def _dot(a, b):
    return jnp.dot(a, b, preferred_element_type=F32)


def _dot_nt(a, b):
    return lax.dot_general(a, b, (((1,), (1,)), ((), ())), preferred_element_type=F32)


def _dot_tn(a, b):
    return lax.dot_general(a, b, (((0,), (0,)), ((), ())), preferred_element_type=F32)


def _rope(x, cosf, sins):
    return x * cosf + pltpu.roll(x, RET_DK // 2, 1) * sins


def _zero_row_after(z):
    tile = z[z.shape[0] - SUBLANES:, z.shape[1] - LANES:]
    bits = pltpu.bitcast(tile, jnp.uint32)
    sixteen = jnp.uint32(16)
    zero = lax.shift_right_logical(lax.shift_right_logical(bits, sixteen), sixteen)
    row = pltpu.bitcast(zero, F32)[0:1, :]
    return jnp.concatenate([row, row], axis=1)


def _head_lane_ids():
    return lax.broadcasted_iota(jnp.int32, (1, XA_W), 1) // XA_DH


def _xattn_norm_gate(a, xan, gate, rows):
    head = _head_lane_ids()
    a2 = a * a
    ms = jnp.zeros((rows, XA_W), F32)
    for hh in range(XA_HEADS):
        m = head == hh
        ssq = jnp.sum(jnp.where(m, a2, 0.0), axis=-1, keepdims=True) * (1.0 / XA_DH)
        ms = jnp.where(m, ssq, ms)
    return a * lax.rsqrt(ms + EPS) * xan * gate


def _xattn_heads(q, mkt, mvt, xan, gate, rows):
    head = _head_lane_ids()
    q = q * (LOG2_E * XA_DH ** -0.5)
    scores = [_dot(jnp.where(head == hh, q, 0.0).astype(BF16), mkt) for hh in range(XA_HEADS)]
    pvs, sums = [], []
    for sc in scores:
        e = jnp.exp2(sc - jnp.max(sc, axis=-1, keepdims=True))
        sums.append(jnp.sum(e, axis=-1, keepdims=True))
        pvs.append(_dot_nt(e.astype(BF16), mvt))
    a = jnp.zeros((rows, XA_W), F32)
    for hh in range(XA_HEADS):
        a = jnp.where(head == hh, pvs[hh] / sums[hh], a)
    return _xattn_norm_gate(a, xan, gate, rows)


def _layer_row(ref, layer):
    return ref[layer:layer + 1, :]


def _params(**kw):
    return pltpu.CompilerParams(vmem_limit_bytes=VMEM_LIMIT_BYTES, **kw)


def _mem_kv_kernel(mem_ref, g_ref, wkt_ref, wvt_ref, kt_ref, vt_ref):
    layer = pl.program_id(0)
    gain = g_ref[pl.ds(layer, 1), :]
    for b in range(mem_ref.shape[0]):
        m = _rms(mem_ref[b], gain).astype(BF16)
        kt_ref[0, b] = _dot_nt(wkt_ref[...], m)
        vt_ref[0, b] = _dot_nt(wvt_ref[...], m)


def _mem_kv(mem, g, wkt, wvt):
    depth, batch = g.shape[0], mem.shape[0]
    out = jax.ShapeDtypeStruct((depth, batch, XA_W, N_MEM), F32)
    return pl.pallas_call(
        _mem_kv_kernel,
        grid=(depth,),
        in_specs=[
            pl.BlockSpec((batch, N_MEM, D_MODEL), lambda l: (0, 0, 0)),
            pl.BlockSpec((depth, D_MODEL), lambda l: (0, 0)),
            pl.BlockSpec((None, XA_W, D_MODEL), lambda l: (l, 0, 0)),
            pl.BlockSpec((None, XA_W, D_MODEL), lambda l: (l, 0, 0)),
        ],
        out_specs=[
            pl.BlockSpec((1, batch, XA_W, N_MEM), lambda l: (l, 0, 0, 0)),
            pl.BlockSpec((1, batch, XA_W, N_MEM), lambda l: (l, 0, 0, 0)),
        ],
        out_shape=[out, out],
        compiler_params=_params(dimension_semantics=("arbitrary",)),
        name="mem_kv",
    )(mem, g, wkt, wvt)


def _layer_kernel(layer, final, n_tiles, n_alias, n_tok, n_seq,
                  cd_ref, x_ref, cos_ref, sin_ref, mk_ref, mv_ref, ng_ref, win_ref,
                  gn_ref, cw_ref, cb_ref, lng_ref, lnb_ref, xan_ref, wout_ref,
                  dec_ref, wq_ref, wk_ref, fng_ref,
                  scd_ref, sz_ref, ss0_ref, scbuf_ref, smkt_ref, smvt_ref, scos_ref, ssin_ref,
                  sdec_ref, swq_ref, swk_ref, *refs):
    (y_ref, sret_ref, sconv_ref, smix_ref, ssret_ref, ssconv_ref,
     s_scr, ext_scr, c_scr, q_scr, k_scr, v_scr, g_scr, xg_scr, mix_scr) = refs[n_alias:]

    sample_stages = _sample_mix_stages(
        layer, n_tok, n_seq, scd_ref, sz_ref, ss0_ref, scbuf_ref, smkt_ref, smvt_ref,
        scos_ref, ssin_ref, gn_ref, cw_ref, cb_ref, lng_ref, lnb_ref, xan_ref, sdec_ref,
        swq_ref, swk_ref, smix_ref, ssret_ref, ssconv_ref)

    tm = PROMPT_TILE
    t = pl.program_id(1)

    @pl.when(t == 0)
    def _():
        s_scr[...] = jnp.zeros_like(s_scr)
        ext_scr[0:CONV_PAD, :] = jnp.zeros((CONV_PAD, CONV_W), F32)

    sample_stages[0]()
    sample_stages[1]()

    x = x_ref[0]
    hb = _rms(x, _layer_row(ng_ref, layer)).astype(BF16)

    def proj(a, b):
        return _dot(hb, win_ref[:, a:b])

    cosf = cos_ref[...]
    sins = sin_ref[...]

    u = proj(C_CA, C_CA + CONV_W) * _sigmoid(proj(C_CB, C_CB + CONV_W))
    ext_scr[CONV_PAD:CONV_PAD + tm, :] = u

    pb = PROJ_BLOCK

    def post_q(z, c):
        for i in range(pb // RET_DK):
            cols = slice(c + RET_DK * i, c + RET_DK * (i + 1))
            zs = z[:, RET_DK * i:RET_DK * (i + 1)]
            q_scr[:, cols] = _rope(zs, cosf, sins).astype(BF16)

    def post_k(z, c):
        for i in range(pb // RET_DK):
            cols = slice(c + RET_DK * i, c + RET_DK * (i + 1))
            zs = z[:, RET_DK * i:RET_DK * (i + 1)]
            k_scr[:, cols] = _rope(zs, cosf, sins) * (RET_DK ** -0.5)

    def post_v(z, c):
        v_scr[:, c:c + pb] = z.astype(BF16)

    def post_g(z, c):
        g_scr[:, c:c + pb] = _silu(z)

    def post_cg(z, c):
        xg_scr[:, c:c + pb] = _silu(z)

    def post_xq(z, c):
        xg_scr[:, CONV_W + c:CONV_W + c + pb] = z

    def post_xg(z, c):
        xg_scr[:, CONV_W + XA_W + c:CONV_W + XA_W + c + pb] = _silu(z)

    blocks = []
    for col0, width, post in ((C_RQ, RET_W, post_q), (C_RK, RET_W, post_k),
                              (C_RV, RET_W, post_v), (C_RG, RET_W, post_g),
                              (C_CG, CONV_W, post_cg), (C_XQ, XA_W, post_xq),
                              (C_XG, XA_W, post_xg)):
        blocks += [(col0, c, post) for c in range(0, width, pb)]

    first = CONV_PAD - CONV_HIST
    conv_acc = {}
    conv_bias = _layer_row(cb_ref, layer)

    def tap_group(c0, r, wait_zero):
        n_rows = CONV_ROWS if r == 0 else CONV_ROWS + SUBLANES
        part = None
        for a in range((first + CONV_K - 1 - r) // SUBLANES + 1):
            j = SUBLANES * a + r - first
            if 0 <= j < CONV_K:
                w_j = cw_ref[j:j + 1, :]
                if wait_zero is not None:
                    w_j = w_j + wait_zero
                lo = c0 + SUBLANES * a
                term = ext_scr[lo:lo + n_rows, :] * w_j
                part = term if part is None else part + term
        if r == 0:
            conv_acc[c0] = part + conv_bias
        else:
            conv_acc[c0] = conv_acc[c0] + part[r:r + CONV_ROWS, :]
        if r == SUBLANES - 1:
            c_scr[c0:c0 + CONV_ROWS, :] = conv_acc.pop(c0)

    groups = [(c0, r) for c0 in range(0, tm, CONV_ROWS) for r in range(SUBLANES)]
    per_block = -(-len(groups) // len(blocks))
    after_proj = None
    for k, (col0, c, post) in enumerate(blocks):
        for c0, r in groups[per_block * k:per_block * (k + 1)]:
            tap_group(c0, r, after_proj)
        z = proj(col0 + c, col0 + c + pb)
        post(z, c)
        after_proj = _zero_row_after(z)

    sample_stages[2]()

    chunks = range(tm // RET_CHUNK)
    heads = range(RET_HEADS)
    crow = lambda c: slice(RET_CHUNK * c, RET_CHUNK * (c + 1))
    hcol = lambda hh: slice(RET_DK * hh, RET_DK * (hh + 1))
    for c0 in range(0, tm // RET_CHUNK, RET_STAGE_CHUNKS):
        part = chunks[c0:c0 + RET_STAGE_CHUNKS]
        sc, kv, inner, cross = {}, {}, {}, {}
        for c in part:
            for hh in heads:
                kf = k_scr[crow(c), hcol(hh)]
                vh = v_scr[crow(c), hcol(hh)]
                sc[c, hh] = _dot_nt(q_scr[crow(c), hcol(hh)], kf.astype(BF16))
                kv[c, hh] = _dot_tn((kf * wk_ref[hh]).astype(BF16), vh)
        for hh in heads:
            state = s_scr[hh]
            for c in part:
                cross[c, hh] = _dot(q_scr[crow(c), hcol(hh)], state.astype(BF16)) * wq_ref[hh]
                state = cd_ref[hh] * state + kv[c, hh]
            s_scr[hh] = state
        for c in part:
            for hh in heads:
                scaled = (sc[c, hh] * dec_ref[hh]).astype(BF16)
                inner[c, hh] = _dot(scaled, v_scr[crow(c), hcol(hh)])
        for c in part:
            for hh in heads:
                o = _standardize(inner[c, hh] + cross[c, hh]) * gn_ref[layer:layer + 1, hcol(hh)]
                mix_scr[crow(c), hcol(hh)] = (o * g_scr[crow(c), hcol(hh)]).astype(BF16)

    @pl.when(t == n_tiles - 1)
    def _():
        sconv_ref[0, 0] = ext_scr[tm + first:tm + CONV_PAD, :]

    ext_scr[0:CONV_PAD, :] = ext_scr[tm:tm + CONV_PAD, :]
    cn = _standardize(c_scr[...]) * _layer_row(lng_ref, layer) + _layer_row(lnb_ref, layer)
    mix_scr[:, RET_W:RET_W + CONV_W] = (_silu(cn) * xg_scr[:, 0:CONV_W]).astype(BF16)

    sample_stages[3]()

    a = _xattn_heads(xg_scr[:, CONV_W:CONV_W + XA_W],
                     mk_ref[0, 0].astype(BF16), mv_ref[0, 0].astype(BF16),
                     _layer_row(xan_ref, layer),
                     xg_scr[:, CONV_W + XA_W:CONV_W + 2 * XA_W], tm)
    mix_scr[:, RET_W + CONV_W:D_MIX] = a.astype(BF16)

    out = x + _dot(mix_scr[...], wout_ref[...])
    if final:
        out = _rms(out, fng_ref[...])
    y_ref[0] = out

    @pl.when(t == n_tiles - 1)
    def _():
        sret_ref[0, 0] = s_scr[...]


def _layer(layer, final, n_tok, x, cosf, sins, mk, mv, ng, win, gn, cw, cb, lng, lnb, xan,
           wout, dec, wq, wk, cd, fng,
           z_s, s0, cbuf, mkt_s, mvt_s, cos_s, sin_s, dec_s, wq_s, wk_s, cd_s, prev_states):
    batch, seq, _ = x.shape
    depth, sbatch = s0.shape[0], s0.shape[1]
    tm = PROMPT_TILE
    n_tiles = seq // tm
    n_seq = sbatch // (batch * n_tiles)
    assert n_seq * batch * n_tiles == sbatch and (n_seq * n_tok) % (2 * SUBLANES) == 0
    whole = lambda a: pl.BlockSpec(a.shape, lambda b, t: (0,) * a.ndim)
    step = lambda b, t: b * n_tiles + t
    sample_specs = [
        pl.BlockSpec(memory_space=pltpu.SMEM),
        pl.BlockSpec((n_seq * n_tok, D_IN), lambda b, t: (step(b, t), 0)),
        pl.BlockSpec((1, n_seq, RET_HEADS, RET_DK, RET_DV),
                     lambda b, t: (layer, step(b, t), 0, 0, 0)),
        pl.BlockSpec((1, n_seq, CONV_HIST, CONV_W), lambda b, t: (layer, step(b, t), 0, 0)),
        pl.BlockSpec((1, n_seq, XA_W, N_MEM), lambda b, t: (layer, step(b, t), 0, 0)),
        pl.BlockSpec((1, n_seq, XA_W, N_MEM), lambda b, t: (layer, step(b, t), 0, 0)),
        whole(cos_s), whole(sin_s), whole(dec_s), whole(wq_s), whole(wk_s),
    ]
    in_specs = [
        pl.BlockSpec(memory_space=pltpu.SMEM),
        pl.BlockSpec((1, tm, D_MODEL), lambda b, t: (b, t, 0)),
        pl.BlockSpec((tm, RET_DK), lambda b, t: (t, 0)),
        pl.BlockSpec((tm, RET_DK), lambda b, t: (t, 0)),
        pl.BlockSpec((1, 1, XA_W, N_MEM), lambda b, t: (layer, b, 0, 0)),
        pl.BlockSpec((1, 1, XA_W, N_MEM), lambda b, t: (layer, b, 0, 0)),
        whole(ng),
        pl.BlockSpec((None, D_MODEL, D_IN), lambda b, t: (layer, 0, 0)),
        whole(gn),
        pl.BlockSpec((None, CONV_K, CONV_W), lambda b, t: (layer, 0, 0)),
        whole(cb), whole(lng), whole(lnb),
        whole(xan),
        pl.BlockSpec((None, D_MIX, D_MODEL), lambda b, t: (layer, 0, 0)),
        whole(dec), whole(wq), whole(wk),
        whole(fng),
    ] + sample_specs
    n_fixed = len(in_specs)
    n_alias = len(prev_states)
    in_specs += [pl.BlockSpec(memory_space=pl.ANY)] * n_alias
    out_specs = [
        pl.BlockSpec((1, tm, D_MODEL), lambda b, t: (b, t, 0)),
        pl.BlockSpec((1, 1, RET_HEADS, RET_DK, RET_DV), lambda b, t: (layer, b, 0, 0, 0)),
        pl.BlockSpec((1, 1, CONV_HIST, CONV_W), lambda b, t: (layer, b, 0, 0)),
        pl.BlockSpec((n_seq * n_tok, D_MIX), lambda b, t: (step(b, t), 0)),
        pl.BlockSpec((1, n_seq, RET_HEADS, RET_DK, RET_DV),
                     lambda b, t: (layer, step(b, t), 0, 0, 0)),
        pl.BlockSpec((1, n_seq, CONV_HIST, CONV_W), lambda b, t: (layer, step(b, t), 0, 0)),
    ]
    out_shape = [
        jax.ShapeDtypeStruct((batch, seq, D_MODEL), F32),
        jax.ShapeDtypeStruct((depth, batch, RET_HEADS, RET_DK, RET_DV), F32),
        jax.ShapeDtypeStruct((depth, batch, CONV_HIST, CONV_W), F32),
        jax.ShapeDtypeStruct((sbatch * n_tok, D_MIX), BF16),
        jax.ShapeDtypeStruct((depth, sbatch, RET_HEADS, RET_DK, RET_DV), F32),
        jax.ShapeDtypeStruct((depth, sbatch, CONV_HIST, CONV_W), F32),
    ]
    state_outputs = (1, 2, 4, 5)
    scratch = [
        pltpu.VMEM((RET_HEADS, RET_DK, RET_DV), F32),
        pltpu.VMEM((CONV_PAD + tm, CONV_W), F32),
        pltpu.VMEM((tm, CONV_W), F32),
        pltpu.VMEM((tm, RET_W), BF16),
        pltpu.VMEM((tm, RET_W), F32),
        pltpu.VMEM((tm, RET_W), BF16),
        pltpu.VMEM((tm, RET_W), F32),
        pltpu.VMEM((tm, CONV_W + 2 * XA_W), F32),
        pltpu.VMEM((tm, D_MIX), BF16),
    ]
    return pl.pallas_call(
        functools.partial(_layer_kernel, layer, final, n_tiles, n_alias, n_tok, n_seq),
        grid=(batch, n_tiles),
        in_specs=in_specs,
        out_specs=out_specs,
        out_shape=out_shape,
        scratch_shapes=scratch,
        input_output_aliases={n_fixed + k: state_outputs[k] for k in range(n_alias)},
        compiler_params=_params(dimension_semantics=("arbitrary", "arbitrary")),
        name=f"layer{layer}",
    )(cd, x, cosf, sins, mk, mv, ng, win, gn, cw, cb, lng, lnb, xan, wout, dec, wq, wk, fng,
      cd_s, z_s, s0, cbuf, mkt_s, mvt_s, cos_s, sin_s, dec_s, wq_s, wk_s, *prev_states)


def _sample_in_kernel(layer, x_ref, ng_ref, win_ref, z_ref):
    hb = _rms(x_ref[...], _layer_row(ng_ref, layer)).astype(BF16)
    z_ref[...] = _dot(hb, win_ref[...])


def _sample_in(layer, x, ng, win):
    n = x.shape[0]
    whole = lambda a: pl.BlockSpec(a.shape, lambda j: (0,) * a.ndim)
    return pl.pallas_call(
        functools.partial(_sample_in_kernel, layer),
        grid=(1,),
        in_specs=[whole(x), whole(ng),
                  pl.BlockSpec((None, D_MODEL, D_IN), lambda j: (layer, 0, 0))],
        out_specs=pl.BlockSpec((n, D_IN), lambda j: (0, 0)),
        out_shape=jax.ShapeDtypeStruct((n, D_IN), F32),
        compiler_params=_params(dimension_semantics=("arbitrary",)),
        name="sample_in",
    )(x, ng, win)


def _sample_mid_kernel(layer, x_ref, mix_ref, wout_ref, ng_ref, win_ref, h_ref, z_ref):
    h = x_ref[...] + _dot(mix_ref[...], wout_ref[...])
    h_ref[...] = h
    z_ref[...] = _dot(_rms(h, _layer_row(ng_ref, layer + 1)).astype(BF16), win_ref[...])


def _sample_mid(layer, x, mix, wout, ng, win):
    n = x.shape[0]
    whole = lambda a: pl.BlockSpec(a.shape, lambda j: (0,) * a.ndim)
    return pl.pallas_call(
        functools.partial(_sample_mid_kernel, layer),
        grid=(1,),
        in_specs=[whole(x), whole(mix),
                  pl.BlockSpec((None, D_MIX, D_MODEL), lambda j: (layer, 0, 0)),
                  whole(ng),
                  pl.BlockSpec((None, D_MODEL, D_IN), lambda j: (layer + 1, 0, 0))],
        out_specs=[pl.BlockSpec((n, D_MODEL), lambda j: (0, 0)),
                   pl.BlockSpec((n, D_IN), lambda j: (0, 0))],
        out_shape=[jax.ShapeDtypeStruct((n, D_MODEL), F32),
                   jax.ShapeDtypeStruct((n, D_IN), F32)],
        compiler_params=_params(dimension_semantics=("arbitrary",)),
        name="sample_mid",
    )(x, mix, wout, ng, win)


def _sample_out_kernel(x_ref, mix_ref, wout_ref, fng_ref, y_ref):
    y_ref[...] = _rms(x_ref[...] + _dot(mix_ref[...], wout_ref[...]), fng_ref[...])


def _sample_out(layer, x, mix, wout, fng):
    n = x.shape[0]
    whole = lambda a: pl.BlockSpec(a.shape, lambda j: (0,) * a.ndim)
    return pl.pallas_call(
        _sample_out_kernel,
        grid=(1,),
        in_specs=[whole(x), whole(mix),
                  pl.BlockSpec((None, D_MIX, D_MODEL), lambda j: (layer, 0, 0)), whole(fng)],
        out_specs=pl.BlockSpec((n, D_MODEL), lambda j: (0, 0)),
        out_shape=jax.ShapeDtypeStruct((n, D_MODEL), F32),
        compiler_params=_params(dimension_semantics=("arbitrary",)),
        name="sample_out",
    )(x, mix, wout, fng)


def _sample_mix_stages(layer, n_tok, n_seq,
                       cd_ref, z_ref, s0_ref, cbuf_ref, mkt_ref, mvt_ref, cos_ref, sin_ref,
                       gn_ref, cw_ref, cb_ref, lng_ref, lnb_ref, xan_ref,
                       dec_ref, wq_ref, wk_ref, mix_ref, sret_ref, sconv_ref):
    grp = SUBLANES // n_tok
    rows = grp * n_tok
    keep = CONV_HIST - n_tok
    cosf = cos_ref[...]
    sins = sin_ref[...]
    head = _head_lane_ids()
    row_id = lax.broadcasted_iota(jnp.int32, (rows, 1), 0)
    row_seq = row_id // n_tok
    row_tok = row_id - row_seq * n_tok
    row4_seq = lax.broadcasted_iota(jnp.int32, (XA_HEADS * rows, 1), 0) % rows // n_tok

    def pick(parts, seq_of_row):
        out = parts[0]
        for s in range(1, grp):
            out = jnp.where(seq_of_row == s, parts[s], out)
        return out

    groups = range(n_seq // grp)
    heads = range(RET_HEADS)
    seqs = [[g * grp + s for s in range(grp)] for g in groups]
    rs = [slice(rows * g, rows * (g + 1)) for g in groups]
    hcol = lambda c0, hh: slice(c0 + RET_DK * hh, c0 + RET_DK * (hh + 1))

    qh, kf, vh, sc, xsc = {}, {}, {}, {}, {}
    inner, cross, xo, xsum = {}, {}, {}, {}

    def first_matmuls():
        for g in groups:
            for hh in heads:
                qh[g, hh] = _rope(z_ref[rs[g], hcol(C_RQ, hh)], cosf, sins).astype(BF16)
                kf[g, hh] = _rope(z_ref[rs[g], hcol(C_RK, hh)], cosf, sins) * (RET_DK ** -0.5)
                vh[g, hh] = z_ref[rs[g], hcol(C_RV, hh)].astype(BF16)
                sc[g, hh] = _dot_nt(qh[g, hh], kf[g, hh].astype(BF16))
            xq = z_ref[rs[g], C_XQ:C_XQ + XA_W] * (LOG2_E * XA_DH ** -0.5)
            q4 = jnp.concatenate([jnp.where(head == hh, xq, 0.0) for hh in range(XA_HEADS)],
                                 axis=0).astype(BF16)
            xsc[g] = [_dot(q4, mkt_ref[0, b].astype(BF16)) for b in seqs[g]]

    def conv_module():
        for g in groups:
            z_a = z_ref[rs[g], C_CA:C_CA + CONV_W]
            u = z_a * _sigmoid(z_ref[rs[g], C_CB:C_CB + CONV_W])
            c = u * cw_ref[CONV_HIST:CONV_K, :] + _layer_row(cb_ref, layer)
            for lag in range(1, n_tok):
                tap = cw_ref[CONV_HIST - lag:CONV_K - lag, :]
                c = c + jnp.where(row_tok >= lag, pltpu.roll(u, lag, 0), 0.0) * tap
            for s, b in enumerate(seqs[g]):
                for i in range(n_tok):
                    window = cbuf_ref[0, b, i:CONV_HIST, :] * cw_ref[0:CONV_HIST - i, :]
                    hi = jnp.sum(window, axis=0, keepdims=True)
                    c = c + jnp.where(row_id == s * n_tok + i, hi, 0.0)
                sconv_ref[0, b, 0:keep, :] = cbuf_ref[0, b, n_tok:CONV_HIST, :]
                sconv_ref[0, b, keep:CONV_HIST, :] = u[s * n_tok:(s + 1) * n_tok, :]
            cn = _standardize(c) * _layer_row(lng_ref, layer) + _layer_row(lnb_ref, layer)
            gate = _silu(z_ref[rs[g], C_CG:C_CG + CONV_W])
            mix_ref[rs[g], RET_W:RET_W + CONV_W] = (_silu(cn) * gate).astype(BF16)

    def second_matmuls():
        for g in groups:
            for hh in heads:
                inner[g, hh] = _dot((sc[g, hh] * dec_ref[hh]).astype(BF16), vh[g, hh])
                kw = kf[g, hh] * wk_ref[hh]
                parts = []
                for s, b in enumerate(seqs[g]):
                    s_prev = s0_ref[0, b, hh]
                    parts.append(_dot(qh[g, hh], s_prev.astype(BF16)))
                    kv = _dot_tn(jnp.where(row_seq == s, kw, 0.0).astype(BF16), vh[g, hh])
                    sret_ref[0, b, hh] = cd_ref[hh] * s_prev + kv
                cross[g, hh] = pick(parts, row_seq) * wq_ref[hh]
            s4 = pick(xsc[g], row4_seq)
            e = jnp.exp2(s4 - jnp.max(s4, axis=-1, keepdims=True))
            xsum[g] = jnp.sum(e, axis=-1, keepdims=True)
            eb = e.astype(BF16)
            xo[g] = [_dot_nt(eb, mvt_ref[0, b].astype(BF16)) for b in seqs[g]]

    def norms_and_stores():
        for g in groups:
            for hh in heads:
                co = hcol(0, hh)
                o = _standardize(inner[g, hh] + cross[g, hh]) * gn_ref[layer:layer + 1, co]
                gate = _silu(z_ref[rs[g], hcol(C_RG, hh)])
                mix_ref[rs[g], co] = (o * gate).astype(BF16)
            o4 = pick(xo[g], row4_seq) / xsum[g]
            a = jnp.zeros((rows, XA_W), F32)
            for hh in range(XA_HEADS):
                a = jnp.where(head == hh, o4[rows * hh:rows * (hh + 1), :], a)
            gate = _silu(z_ref[rs[g], C_XG:C_XG + XA_W])
            mix_ref[rs[g], RET_W + CONV_W:D_MIX] = _xattn_norm_gate(
                a, _layer_row(xan_ref, layer), gate, rows).astype(BF16)

    return [first_matmuls, conv_module, second_matmuls, norms_and_stores]


def _rope_tables(pos):
    half = RET_DK // 2
    inv = np.float64(ROPE_BASE) ** (-np.arange(half, dtype=np.float64) / half)
    ang = pos.astype(np.float64)[:, None] * inv[None, :]
    cos, sin = np.cos(ang), np.sin(ang)
    return (np.concatenate([cos, cos], axis=-1).astype(np.float32),
            np.concatenate([-sin, sin], axis=-1).astype(np.float32))


def _decay_tables(chunk):
    lg = np.log(1.0 - np.exp2(-5.0 - np.arange(RET_HEADS, dtype=np.float64)))
    idx = np.arange(chunk, dtype=np.float64)
    diff = idx[:, None] - idx[None, :]
    dec = np.where(diff[None] >= 0, np.exp(np.maximum(diff, 0.0)[None] * lg[:, None, None]), 0.0)
    wk = np.exp((chunk - 1.0 - idx)[None, :] * lg[:, None])
    wq = np.exp((idx + 1.0)[None, :] * lg[:, None])
    cd = np.exp(chunk * lg)
    wk = np.broadcast_to(wk[:, :, None], (RET_HEADS, chunk, RET_DK))
    wq = np.broadcast_to(wq[:, :, None], (RET_HEADS, chunk, RET_DV))
    f32 = lambda a: np.ascontiguousarray(a, dtype=np.float32)
    return f32(dec), f32(wq), f32(wk), f32(cd)


def _group_tables(n_tok, pos0):
    grp = SUBLANES // n_tok
    cosf, sins = _rope_tables(pos0 + np.arange(n_tok))
    dec, wq, wk, cd = _decay_tables(n_tok)
    tile_rows = lambda a: np.concatenate([a] * grp, axis=-2)
    eye = np.eye(grp, dtype=np.float32)
    dec = np.einsum("st,hij->hsitj", eye, dec).reshape(RET_HEADS, grp * n_tok, grp * n_tok)
    return tile_rows(cosf), tile_rows(sins), dec, tile_rows(wq), tile_rows(wk), cd


def kernel(x_prompt, x_sample, mem_prompt, state_ret, state_conv, cache_mem_k, cache_mem_v,
           norm_g, w_in, ret_gn_g, conv_w, conv_b, conv_ln_g, conv_ln_b, xa_norm_g,
           mem_norm_g, w_mk, w_mv, w_out, final_norm_g):
    depth = w_in.shape[0]
    batch, seq, _ = x_prompt.shape
    dbatch, dseq, _ = x_sample.shape

    w_in_b = w_in.astype(BF16)
    w_out_b = w_out.astype(BF16)
    w_mkt_b = w_mk.swapaxes(1, 2).astype(BF16)
    w_mvt_b = w_mv.swapaxes(1, 2).astype(BF16)
    fng = final_norm_g.reshape(1, D_MODEL)

    p_mk, p_mv = _mem_kv(mem_prompt, mem_norm_g, w_mkt_b, w_mvt_b)
    cos_p, sin_p = _rope_tables(np.arange(seq))
    dec_p, wq_p, wk_p, cd_p = _decay_tables(RET_CHUNK if seq % RET_CHUNK == 0 else seq)

    assert SUBLANES % dseq == 0 and dseq % RET_CHUNK != 0
    cos_s, sin_s, dec_s, wq_s, wk_s, cd_s = _group_tables(dseq, PAST_LEN)
    to_hd_m = lambda c: c.transpose(0, 1, 3, 4, 2).reshape(depth, dbatch, XA_W, N_MEM)
    mkt_s, mvt_s = to_hd_m(cache_mem_k), to_hd_m(cache_mem_v)
    hs = x_sample.reshape(dbatch * dseq, D_MODEL)
    z = _sample_in(0, hs, norm_g, w_in_b)

    hp = x_prompt
    states = ()
    y_sample = None
    for l in range(depth):
        hp, p_ret, p_conv, mix, s_ret, s_conv = _layer(
            l, l == depth - 1, dseq, hp, cos_p, sin_p, p_mk, p_mv, norm_g, w_in_b, ret_gn_g,
            conv_w, conv_b, conv_ln_g, conv_ln_b, xa_norm_g, w_out_b, dec_p, wq_p, wk_p,
            cd_p, fng, z, state_ret, state_conv, mkt_s, mvt_s, cos_s, sin_s, dec_s, wq_s,
            wk_s, cd_s, states)
        states = (p_ret, p_conv, s_ret, s_conv)
        if l + 1 < depth:
            hs, z = _sample_mid(l, hs, mix, w_out_b, norm_g, w_in_b)
        else:
            y_sample = _sample_out(l, hs, mix, w_out_b, fng).reshape(dbatch, dseq, D_MODEL)
    y_prompt = hp

    from_hd_m = lambda c: c.reshape(depth, batch, XA_HEADS, XA_DH, N_MEM).transpose(0, 1, 4, 2, 3)
    return (y_prompt, y_sample, states[0], states[1],
            from_hd_m(p_mk), from_hd_m(p_mv), states[2], states[3])
```

```python
import functools

import numpy as np

import jax
import jax.numpy as jnp
from jax import lax
from jax.experimental import pallas as pl
from jax.experimental.pallas import tpu as pltpu

F32 = jnp.float32
BF16 = jnp.bfloat16

D_MODEL = 1024
N_MEM = 256
RET_HEADS = 4
RET_DK = 128
RET_DV = 128
RET_W = RET_HEADS * RET_DV
RET_CHUNK = 128
CONV_W = 256
CONV_K = 31
XA_HEADS = 4
XA_DH = 64
XA_W = XA_HEADS * XA_DH
D_MIX = RET_W + CONV_W + XA_W
ROPE_BASE = 10000.0
EPS = 1e-6
PAST_LEN = 16384
LOG2_E = 1.4426950408889634

C_RQ, C_RK, C_RV, C_RG = 0, 512, 1024, 1536
C_CA, C_CB, C_CG = 2048, 2304, 2560
C_XQ, C_XG = 2816, 3072
D_IN = 3328

VMEM_LIMIT_BYTES = 56 * 1024 * 1024
SUBLANES = 8
LANES = 128
CONV_PAD = 32
CONV_HIST = CONV_K - 1
PROMPT_TILE = 512
CONV_ROWS = 64
PROJ_BLOCK = 256
RET_STAGE_CHUNKS = 2
SAMPLE_PROJ_STEPS = 2


def _rms(x, g):
    return x * lax.rsqrt(jnp.mean(x * x, axis=-1, keepdims=True) + EPS) * g


def _standardize(x):
    mu = jnp.mean(x, axis=-1, keepdims=True)
    d = x - mu
    var = jnp.mean(d * d, axis=-1, keepdims=True)
    return d * lax.rsqrt(var + EPS)


def _sigmoid(x):
    return 1.0 / (1.0 + jnp.exp(-x))


def _silu(x):
    return x * _sigmoid(x)


def _dot(a, b):
    return jnp.dot(a, b, preferred_element_type=F32)


def _dot_nt(a, b):
    return lax.dot_general(a, b, (((1,), (1,)), ((), ())), preferred_element_type=F32)


def _dot_tn(a, b):
    return lax.dot_general(a, b, (((0,), (0,)), ((), ())), preferred_element_type=F32)


def _rope(x, cosf, sins):
    return x * cosf + pltpu.roll(x, RET_DK // 2, 1) * sins


def _zero_row_after(z):
    tile = z[z.shape[0] - SUBLANES:, z.shape[1] - LANES:]
    bits = pltpu.bitcast(tile, jnp.uint32)
    sixteen = jnp.uint32(16)
    zero = lax.shift_right_logical(lax.shift_right_logical(bits, sixteen), sixteen)
    row = pltpu.bitcast(zero, F32)[0:1, :]
    return jnp.concatenate([row, row], axis=1)


def _head_lane_ids():
    return lax.broadcasted_iota(jnp.int32, (1, XA_W), 1) // XA_DH


def _xattn_norm_gate(a, xan, gate, rows):
    head = _head_lane_ids()
    a2 = a * a
    ms = jnp.zeros((rows, XA_W), F32)
    for hh in range(XA_HEADS):
        m = head == hh
        ssq = jnp.sum(jnp.where(m, a2, 0.0), axis=-1, keepdims=True) * (1.0 / XA_DH)
        ms = jnp.where(m, ssq, ms)
    return a * lax.rsqrt(ms + EPS) * xan * gate


def _xattn_heads(q, mkt, mvt, xan, gate, rows):
    head = _head_lane_ids()
    q = q * (LOG2_E * XA_DH ** -0.5)
    scores = [_dot(jnp.where(head == hh, q, 0.0).astype(BF16), mkt) for hh in range(XA_HEADS)]
    pvs, sums = [], []
    for sc in scores:
        e = jnp.exp2(sc - jnp.max(sc, axis=-1, keepdims=True))
        sums.append(jnp.sum(e, axis=-1, keepdims=True))
        pvs.append(_dot_nt(e.astype(BF16), mvt))
    a = jnp.zeros((rows, XA_W), F32)
    for hh in range(XA_HEADS):
        a = jnp.where(head == hh, pvs[hh] / sums[hh], a)
    return _xattn_norm_gate(a, xan, gate, rows)


def _layer_row(ref, layer):
    return ref[layer:layer + 1, :]


def _params(**kw):
    return pltpu.CompilerParams(vmem_limit_bytes=VMEM_LIMIT_BYTES, **kw)


def _mem_kv_kernel(mem_ref, g_ref, wkt_ref, wvt_ref, kt_ref, vt_ref):
    layer = pl.program_id(0)
    gain = g_ref[pl.ds(layer, 1), :]
    for b in range(mem_ref.shape[0]):
        m = _rms(mem_ref[b], gain).astype(BF16)
        kt_ref[0, b] = _dot_nt(wkt_ref[...], m)
        vt_ref[0, b] = _dot_nt(wvt_ref[...], m)


def _mem_kv(mem, g, wkt, wvt):
    depth, batch = g.shape[0], mem.shape[0]
    out = jax.ShapeDtypeStruct((depth, batch, XA_W, N_MEM), F32)
    return pl.pallas_call(
        _mem_kv_kernel,
        grid=(depth,),
        in_specs=[
            pl.BlockSpec((batch, N_MEM, D_MODEL), lambda l: (0, 0, 0)),
            pl.BlockSpec((depth, D_MODEL), lambda l: (0, 0)),
            pl.BlockSpec((None, XA_W, D_MODEL), lambda l: (l, 0, 0)),
            pl.BlockSpec((None, XA_W, D_MODEL), lambda l: (l, 0, 0)),
        ],
        out_specs=[
            pl.BlockSpec((1, batch, XA_W, N_MEM), lambda l: (l, 0, 0, 0)),
            pl.BlockSpec((1, batch, XA_W, N_MEM), lambda l: (l, 0, 0, 0)),
        ],
        out_shape=[out, out],
        compiler_params=_params(dimension_semantics=("arbitrary",)),
        name="mem_kv",
    )(mem, g, wkt, wvt)


def _layer_kernel(layer, final, n_tiles, n_alias, n_tok, n_seq,
                  cd_ref, x_ref, cos_ref, sin_ref, mk_ref, mv_ref, ng_ref, win_ref,
                  gn_ref, cw_ref, cb_ref, lng_ref, lnb_ref, xan_ref, wout_ref,
                  dec_ref, wq_ref, wk_ref, fng_ref,
                  scd_ref, sz_ref, ss0_ref, scbuf_ref, smkt_ref, smvt_ref, scos_ref, ssin_ref,
                  sdec_ref, swq_ref, swk_ref, *refs):
    (y_ref, sret_ref, sconv_ref, smix_ref, ssret_ref, ssconv_ref,
     s_scr, ext_scr, c_scr, q_scr, qw_scr, k_scr, v_scr, g_scr, xg_scr, mix_scr) = refs[n_alias:]

    sample_stages = _sample_mix_stages(
        layer, n_tok, n_seq, scd_ref, sz_ref, ss0_ref, scbuf_ref, smkt_ref, smvt_ref,
        scos_ref, ssin_ref, gn_ref, cw_ref, cb_ref, lng_ref, lnb_ref, xan_ref, sdec_ref,
        swq_ref, swk_ref, smix_ref, ssret_ref, ssconv_ref)

    tm = PROMPT_TILE
    t = pl.program_id(1)

    @pl.when(t == 0)
    def _():
        s_scr[...] = jnp.zeros_like(s_scr)
        ext_scr[0:CONV_PAD, :] = jnp.zeros((CONV_PAD, CONV_W), F32)

    sample_stages[0]()
    sample_stages[1]()

    hb = _rms(x_ref[0], _layer_row(ng_ref, layer)).astype(BF16)

    def proj(a, b):
        return _dot(hb, win_ref[:, a:b])

    cosf = cos_ref[...]
    sins = sin_ref[...]

    u = proj(C_CA, C_CA + CONV_W) * _sigmoid(proj(C_CB, C_CB + CONV_W))
    ext_scr[CONV_PAD:CONV_PAD + tm, :] = u

    pb = PROJ_BLOCK

    def post_q(z, c):
        for i in range(pb // RET_DK):
            hh = c // RET_DK + i
            cols = slice(RET_DK * hh, RET_DK * (hh + 1))
            q = _rope(z[:, RET_DK * i:RET_DK * (i + 1)], cosf, sins)
            q_scr[:, cols] = q.astype(BF16)
            for ch in range(tm // RET_CHUNK):
                rows = slice(RET_CHUNK * ch, RET_CHUNK * (ch + 1))
                qw_scr[rows, cols] = (q[rows, :] * wq_ref[hh]).astype(BF16)

    def post_k(z, c):
        for i in range(pb // RET_DK):
            cols = slice(c + RET_DK * i, c + RET_DK * (i + 1))
            zs = z[:, RET_DK * i:RET_DK * (i + 1)]
            k_scr[:, cols] = _rope(zs, cosf, sins) * (RET_DK ** -0.5)

    def post_v(z, c):
        v_scr[:, c:c + pb] = z.astype(BF16)

    def post_g(z, c):
        g_scr[:, c:c + pb] = _silu(z)

    def post_cg(z, c):
        xg_scr[:, c:c + pb] = _silu(z)

    def post_xq(z, c):
        xg_scr[:, CONV_W + c:CONV_W + c + pb] = z

    def post_xg(z, c):
        xg_scr[:, CONV_W + XA_W + c:CONV_W + XA_W + c + pb] = _silu(z)

    blocks = []
    for col0, width, post in ((C_RQ, RET_W, post_q), (C_RK, RET_W, post_k),
                              (C_RV, RET_W, post_v), (C_RG, RET_W, post_g),
                              (C_CG, CONV_W, post_cg), (C_XQ, XA_W, post_xq),
                              (C_XG, XA_W, post_xg)):
        blocks += [(col0, c, post) for c in range(0, width, pb)]

    first = CONV_PAD - CONV_HIST
    conv_acc = {}
    conv_bias = _layer_row(cb_ref, layer)

    def tap_group(c0, r, wait_zero):
        n_rows = CONV_ROWS if r == 0 else CONV_ROWS + SUBLANES
        part = None
        for a in range((first + CONV_K - 1 - r) // SUBLANES + 1):
            j = SUBLANES * a + r - first
            if 0 <= j < CONV_K:
                w_j = cw_ref[j:j + 1, :]
                if wait_zero is not None:
                    w_j = w_j + wait_zero
                lo = c0 + SUBLANES * a
                term = ext_scr[lo:lo + n_rows, :] * w_j
                part = term if part is None else part + term
        if r == 0:
            conv_acc[c0] = part + conv_bias
        else:
            conv_acc[c0] = conv_acc[c0] + part[r:r + CONV_ROWS, :]
        if r == SUBLANES - 1:
            c_scr[c0:c0 + CONV_ROWS, :] = conv_acc.pop(c0)

    groups = [(c0, r) for c0 in range(0, tm, CONV_ROWS) for r in range(SUBLANES)]
    per_block = -(-len(groups) // len(blocks))
    after_proj = None
    for k, (col0, c, post) in enumerate(blocks):
        for c0, r in groups[per_block * k:per_block * (k + 1)]:
            tap_group(c0, r, after_proj)
        z = proj(col0 + c, col0 + c + pb)
        post(z, c)
        after_proj = _zero_row_after(z)

    sample_stages[2]()

    chunks = range(tm // RET_CHUNK)
    heads = range(RET_HEADS)
    crow = lambda c: slice(RET_CHUNK * c, RET_CHUNK * (c + 1))
    hcol = lambda hh: slice(RET_DK * hh, RET_DK * (hh + 1))
    for c0 in range(0, tm // RET_CHUNK, RET_STAGE_CHUNKS):
        part = chunks[c0:c0 + RET_STAGE_CHUNKS]
        sc, kv, before, out = {}, {}, {}, {}
        for c in part:
            for hh in heads:
                kf = k_scr[crow(c), hcol(hh)]
                vh = v_scr[crow(c), hcol(hh)]
                sc[c, hh] = _dot_nt(q_scr[crow(c), hcol(hh)], kf.astype(BF16))
                kv[c, hh] = _dot_tn((kf * wk_ref[hh]).astype(BF16), vh)
        for hh in heads:
            state = s_scr[hh]
            for c in part:
                before[c, hh] = state.astype(BF16)
                state = cd_ref[hh] * state + kv[c, hh]
            s_scr[hh] = state
        for c in part:
            for hh in heads:
                lhs = jnp.concatenate([(sc[c, hh] * dec_ref[hh]).astype(BF16),
                                       qw_scr[crow(c), hcol(hh)]], axis=1)
                rhs = jnp.concatenate([v_scr[crow(c), hcol(hh)], before[c, hh]], axis=0)
                out[c, hh] = _dot(lhs, rhs)
        for c in part:
            for hh in heads:
                o = _standardize(out[c, hh]) * gn_ref[layer:layer + 1, hcol(hh)]
                mix_scr[crow(c), hcol(hh)] = (o * g_scr[crow(c), hcol(hh)]).astype(BF16)

    @pl.when(t == n_tiles - 1)
    def _():
        sconv_ref[0, 0] = ext_scr[tm + first:tm + CONV_PAD, :]

    ext_scr[0:CONV_PAD, :] = ext_scr[tm:tm + CONV_PAD, :]
    cn = _standardize(c_scr[...]) * _layer_row(lng_ref, layer) + _layer_row(lnb_ref, layer)
    mix_scr[:, RET_W:RET_W + CONV_W] = (_silu(cn) * xg_scr[:, 0:CONV_W]).astype(BF16)

    sample_stages[3]()

    a = _xattn_heads(xg_scr[:, CONV_W:CONV_W + XA_W],
                     mk_ref[0, 0].astype(BF16), mv_ref[0, 0].astype(BF16),
                     _layer_row(xan_ref, layer),
                     xg_scr[:, CONV_W + XA_W:CONV_W + 2 * XA_W], tm)
    mix_scr[:, RET_W + CONV_W:D_MIX] = a.astype(BF16)

    out = x_ref[0] + _dot(mix_scr[...], wout_ref[...])
    if final:
        out = _rms(out, fng_ref[...])
    y_ref[0] = out

    @pl.when(t == n_tiles - 1)
    def _():
        sret_ref[0, 0] = s_scr[...]


def _layer(layer, final, n_tok, x, cosf, sins, mk, mv, ng, win, gn, cw, cb, lng, lnb, xan,
           wout, dec, wq, wk, cd, fng,
           z_s, s0, cbuf, mkt_s, mvt_s, cos_s, sin_s, dec_s, wq_s, wk_s, cd_s, prev_states):
    batch, seq, _ = x.shape
    depth, sbatch = s0.shape[0], s0.shape[1]
    tm = PROMPT_TILE
    n_tiles = seq // tm
    n_seq = sbatch // (batch * n_tiles)
    assert n_seq * batch * n_tiles == sbatch and (n_seq * n_tok) % (2 * SUBLANES) == 0
    whole = lambda a: pl.BlockSpec(a.shape, lambda b, t: (0,) * a.ndim)
    step = lambda b, t: b * n_tiles + t
    sample_specs = [
        pl.BlockSpec(memory_space=pltpu.SMEM),
        pl.BlockSpec((n_seq * n_tok, D_IN), lambda b, t: (step(b, t), 0)),
        pl.BlockSpec((1, n_seq, RET_HEADS, RET_DK, RET_DV),
                     lambda b, t: (layer, step(b, t), 0, 0, 0)),
        pl.BlockSpec((1, n_seq, CONV_HIST, CONV_W), lambda b, t: (layer, step(b, t), 0, 0)),
        pl.BlockSpec((1, n_seq, XA_W, N_MEM), lambda b, t: (layer, step(b, t), 0, 0)),
        pl.BlockSpec((1, n_seq, XA_W, N_MEM), lambda b, t: (layer, step(b, t), 0, 0)),
        whole(cos_s), whole(sin_s), whole(dec_s), whole(wq_s), whole(wk_s),
    ]
    in_specs = [
        pl.BlockSpec(memory_space=pltpu.SMEM),
        pl.BlockSpec((1, tm, D_MODEL), lambda b, t: (b, t, 0)),
        pl.BlockSpec((tm, RET_DK), lambda b, t: (t, 0)),
        pl.BlockSpec((tm, RET_DK), lambda b, t: (t, 0)),
        pl.BlockSpec((1, 1, XA_W, N_MEM), lambda b, t: (layer, b, 0, 0)),
        pl.BlockSpec((1, 1, XA_W, N_MEM), lambda b, t: (layer, b, 0, 0)),
        whole(ng),
        pl.BlockSpec((None, D_MODEL, D_IN), lambda b, t: (layer, 0, 0)),
        whole(gn),
        pl.BlockSpec((None, CONV_K, CONV_W), lambda b, t: (layer, 0, 0)),
        whole(cb), whole(lng), whole(lnb),
        whole(xan),
        pl.BlockSpec((None, D_MIX, D_MODEL), lambda b, t: (layer, 0, 0)),
        whole(dec), whole(wq), whole(wk),
        whole(fng),
    ] + sample_specs
    n_fixed = len(in_specs)
    n_alias = len(prev_states)
    in_specs += [pl.BlockSpec(memory_space=pl.ANY)] * n_alias
    out_specs = [
        pl.BlockSpec((1, tm, D_MODEL), lambda b, t: (b, t, 0)),
        pl.BlockSpec((1, 1, RET_HEADS, RET_DK, RET_DV), lambda b, t: (layer, b, 0, 0, 0)),
        pl.BlockSpec((1, 1, CONV_HIST, CONV_W), lambda b, t: (layer, b, 0, 0)),
        pl.BlockSpec((n_seq * n_tok, D_MIX), lambda b, t: (step(b, t), 0)),
        pl.BlockSpec((1, n_seq, RET_HEADS, RET_DK, RET_DV),
                     lambda b, t: (layer, step(b, t), 0, 0, 0)),
        pl.BlockSpec((1, n_seq, CONV_HIST, CONV_W), lambda b, t: (layer, step(b, t), 0, 0)),
    ]
    out_shape = [
        jax.ShapeDtypeStruct((batch, seq, D_MODEL), F32),
        jax.ShapeDtypeStruct((depth, batch, RET_HEADS, RET_DK, RET_DV), F32),
        jax.ShapeDtypeStruct((depth, batch, CONV_HIST, CONV_W), F32),
        jax.ShapeDtypeStruct((sbatch * n_tok, D_MIX), BF16),
        jax.ShapeDtypeStruct((depth, sbatch, RET_HEADS, RET_DK, RET_DV), F32),
        jax.ShapeDtypeStruct((depth, sbatch, CONV_HIST, CONV_W), F32),
    ]
    state_outputs = (1, 2, 4, 5)
    scratch = [
        pltpu.VMEM((RET_HEADS, RET_DK, RET_DV), F32),
        pltpu.VMEM((CONV_PAD + tm, CONV_W), F32),
        pltpu.VMEM((tm, CONV_W), F32),
        pltpu.VMEM((tm, RET_W), BF16),
        pltpu.VMEM((tm, RET_W), BF16),
        pltpu.VMEM((tm, RET_W), F32),
        pltpu.VMEM((tm, RET_W), BF16),
        pltpu.VMEM((tm, RET_W), F32),
        pltpu.VMEM((tm, CONV_W + 2 * XA_W), F32),
        pltpu.VMEM((tm, D_MIX), BF16),
    ]
    return pl.pallas_call(
        functools.partial(_layer_kernel, layer, final, n_tiles, n_alias, n_tok, n_seq),
        grid=(batch, n_tiles),
        in_specs=in_specs,
        out_specs=out_specs,
        out_shape=out_shape,
        scratch_shapes=scratch,
        input_output_aliases={n_fixed + k: state_outputs[k] for k in range(n_alias)},
        compiler_params=_params(dimension_semantics=("arbitrary", "arbitrary")),
        name=f"layer{layer}",
    )(cd, x, cosf, sins, mk, mv, ng, win, gn, cw, cb, lng, lnb, xan, wout, dec, wq, wk, fng,
      cd_s, z_s, s0, cbuf, mkt_s, mvt_s, cos_s, sin_s, dec_s, wq_s, wk_s, *prev_states)


def _sample_in_kernel(layer, x_ref, ng_ref, win_ref, z_ref, hb_scr):
    @pl.when(pl.program_id(0) == 0)
    def _():
        hb_scr[...] = _rms(x_ref[...], _layer_row(ng_ref, layer)).astype(BF16)

    z_ref[...] = _dot(hb_scr[...], win_ref[...])


def _sample_in(layer, x, ng, win):
    n = x.shape[0]
    cols = D_IN // SAMPLE_PROJ_STEPS
    whole = lambda a: pl.BlockSpec(a.shape, lambda j: (0,) * a.ndim)
    return pl.pallas_call(
        functools.partial(_sample_in_kernel, layer),
        grid=(SAMPLE_PROJ_STEPS,),
        in_specs=[whole(x), whole(ng),
                  pl.BlockSpec((None, D_MODEL, cols), lambda j: (layer, 0, j))],
        out_specs=pl.BlockSpec((n, cols), lambda j: (0, j)),
        out_shape=jax.ShapeDtypeStruct((n, D_IN), F32),
        scratch_shapes=[pltpu.VMEM((n, D_MODEL), BF16)],
        compiler_params=_params(dimension_semantics=("arbitrary",)),
        name="sample_in",
    )(x, ng, win)


def _sample_mid_kernel(layer, x_ref, mix_ref, wout_ref, ng_ref, win_ref, h_ref, z_ref, hb_scr):
    @pl.when(pl.program_id(0) == 0)
    def _():
        h = x_ref[...] + _dot(mix_ref[...], wout_ref[...])
        h_ref[...] = h
        hb_scr[...] = _rms(h, _layer_row(ng_ref, layer + 1)).astype(BF16)

    z_ref[...] = _dot(hb_scr[...], win_ref[...])


def _sample_mid(layer, x, mix, wout, ng, win):
    n = x.shape[0]
    cols = D_IN // SAMPLE_PROJ_STEPS
    whole = lambda a: pl.BlockSpec(a.shape, lambda j: (0,) * a.ndim)
    return pl.pallas_call(
        functools.partial(_sample_mid_kernel, layer),
        grid=(SAMPLE_PROJ_STEPS,),
        in_specs=[whole(x), whole(mix),
                  pl.BlockSpec((None, D_MIX, D_MODEL), lambda j: (layer, 0, 0)),
                  whole(ng),
                  pl.BlockSpec((None, D_MODEL, cols), lambda j: (layer + 1, 0, j))],
        out_specs=[pl.BlockSpec((n, D_MODEL), lambda j: (0, 0)),
                   pl.BlockSpec((n, cols), lambda j: (0, j))],
        out_shape=[jax.ShapeDtypeStruct((n, D_MODEL), F32),
                   jax.ShapeDtypeStruct((n, D_IN), F32)],
        scratch_shapes=[pltpu.VMEM((n, D_MODEL), BF16)],
        compiler_params=_params(dimension_semantics=("arbitrary",)),
        name="sample_mid",
    )(x, mix, wout, ng, win)


def _sample_out_kernel(x_ref, mix_ref, wout_ref, fng_ref, y_ref):
    y_ref[...] = _rms(x_ref[...] + _dot(mix_ref[...], wout_ref[...]), fng_ref[...])


def _sample_out(layer, x, mix, wout, fng):
    n = x.shape[0]
    whole = lambda a: pl.BlockSpec(a.shape, lambda j: (0,) * a.ndim)
    return pl.pallas_call(
        _sample_out_kernel,
        grid=(1,),
        in_specs=[whole(x), whole(mix),
                  pl.BlockSpec((None, D_MIX, D_MODEL), lambda j: (layer, 0, 0)), whole(fng)],
        out_specs=pl.BlockSpec((n, D_MODEL), lambda j: (0, 0)),
        out_shape=jax.ShapeDtypeStruct((n, D_MODEL), F32),
        compiler_params=_params(dimension_semantics=("arbitrary",)),
        name="sample_out",
    )(x, mix, wout, fng)


def _sample_mix_stages(layer, n_tok, n_seq,
                       cd_ref, z_ref, s0_ref, cbuf_ref, mkt_ref, mvt_ref, cos_ref, sin_ref,
                       gn_ref, cw_ref, cb_ref, lng_ref, lnb_ref, xan_ref,
                       dec_ref, wq_ref, wk_ref, mix_ref, sret_ref, sconv_ref):
    grp = SUBLANES // n_tok
    rows = grp * n_tok
    keep = CONV_HIST - n_tok
    cosf = cos_ref[...]
    sins = sin_ref[...]
    head = _head_lane_ids()
    row_id = lax.broadcasted_iota(jnp.int32, (rows, 1), 0)
    row_seq = row_id // n_tok
    row_tok = row_id - row_seq * n_tok
    row4_seq = lax.broadcasted_iota(jnp.int32, (XA_HEADS * rows, 1), 0) % rows // n_tok

    def pick(parts, seq_of_row):
        out = parts[0]
        for s in range(1, grp):
            out = jnp.where(seq_of_row == s, parts[s], out)
        return out

    groups = range(n_seq // grp)
    heads = range(RET_HEADS)
    seqs = [[g * grp + s for s in range(grp)] for g in groups]
    rs = [slice(rows * g, rows * (g + 1)) for g in groups]
    hcol = lambda c0, hh: slice(c0 + RET_DK * hh, c0 + RET_DK * (hh + 1))

    qh, kf, vh, sc, xsc = {}, {}, {}, {}, {}
    inner, cross, xo, xsum = {}, {}, {}, {}

    def first_matmuls():
        for g in groups:
            for hh in heads:
                qh[g, hh] = _rope(z_ref[rs[g], hcol(C_RQ, hh)], cosf, sins).astype(BF16)
                kf[g, hh] = _rope(z_ref[rs[g], hcol(C_RK, hh)], cosf, sins) * (RET_DK ** -0.5)
                vh[g, hh] = z_ref[rs[g], hcol(C_RV, hh)].astype(BF16)
                sc[g, hh] = _dot_nt(qh[g, hh], kf[g, hh].astype(BF16))
            xq = z_ref[rs[g], C_XQ:C_XQ + XA_W] * (LOG2_E * XA_DH ** -0.5)
            q4 = jnp.concatenate([jnp.where(head == hh, xq, 0.0) for hh in range(XA_HEADS)],
                                 axis=0).astype(BF16)
            xsc[g] = [_dot(q4, mkt_ref[0, b].astype(BF16)) for b in seqs[g]]

    def conv_module():
        for g in groups:
            z_a = z_ref[rs[g], C_CA:C_CA + CONV_W]
            u = z_a * _sigmoid(z_ref[rs[g], C_CB:C_CB + CONV_W])
            c = u * cw_ref[CONV_HIST:CONV_K, :] + _layer_row(cb_ref, layer)
            for lag in range(1, n_tok):
                tap = cw_ref[CONV_HIST - lag:CONV_K - lag, :]
                c = c + jnp.where(row_tok >= lag, pltpu.roll(u, lag, 0), 0.0) * tap
            for s, b in enumerate(seqs[g]):
                for i in range(n_tok):
                    window = cbuf_ref[0, b, i:CONV_HIST, :] * cw_ref[0:CONV_HIST - i, :]
                    hi = jnp.sum(window, axis=0, keepdims=True)
                    c = c + jnp.where(row_id == s * n_tok + i, hi, 0.0)
                sconv_ref[0, b, 0:keep, :] = cbuf_ref[0, b, n_tok:CONV_HIST, :]
                sconv_ref[0, b, keep:CONV_HIST, :] = u[s * n_tok:(s + 1) * n_tok, :]
            cn = _standardize(c) * _layer_row(lng_ref, layer) + _layer_row(lnb_ref, layer)
            gate = _silu(z_ref[rs[g], C_CG:C_CG + CONV_W])
            mix_ref[rs[g], RET_W:RET_W + CONV_W] = (_silu(cn) * gate).astype(BF16)

    def second_matmuls():
        for g in groups:
            for hh in heads:
                inner[g, hh] = _dot((sc[g, hh] * dec_ref[hh]).astype(BF16), vh[g, hh])
                kw = kf[g, hh] * wk_ref[hh]
                parts = []
                for s, b in enumerate(seqs[g]):
                    s_prev = s0_ref[0, b, hh]
                    parts.append(_dot(qh[g, hh], s_prev.astype(BF16)))
                    kv = _dot_tn(jnp.where(row_seq == s, kw, 0.0).astype(BF16), vh[g, hh])
                    sret_ref[0, b, hh] = cd_ref[hh] * s_prev + kv
                cross[g, hh] = pick(parts, row_seq) * wq_ref[hh]
            s4 = pick(xsc[g], row4_seq)
            e = jnp.exp2(s4 - jnp.max(s4, axis=-1, keepdims=True))
            xsum[g] = jnp.sum(e, axis=-1, keepdims=True)
            eb = e.astype(BF16)
            xo[g] = [_dot_nt(eb, mvt_ref[0, b].astype(BF16)) for b in seqs[g]]

    def norms_and_stores():
        for g in groups:
            for hh in heads:
                co = hcol(0, hh)
                o = _standardize(inner[g, hh] + cross[g, hh]) * gn_ref[layer:layer + 1, co]
                gate = _silu(z_ref[rs[g], hcol(C_RG, hh)])
                mix_ref[rs[g], co] = (o * gate).astype(BF16)
            o4 = pick(xo[g], row4_seq) / xsum[g]
            a = jnp.zeros((rows, XA_W), F32)
            for hh in range(XA_HEADS):
                a = jnp.where(head == hh, o4[rows * hh:rows * (hh + 1), :], a)
            gate = _silu(z_ref[rs[g], C_XG:C_XG + XA_W])
            mix_ref[rs[g], RET_W + CONV_W:D_MIX] = _xattn_norm_gate(
                a, _layer_row(xan_ref, layer), gate, rows).astype(BF16)

    return [first_matmuls, conv_module, second_matmuls, norms_and_stores]


def _rope_tables(pos):
    half = RET_DK // 2
    inv = np.float64(ROPE_BASE) ** (-np.arange(half, dtype=np.float64) / half)
    ang = pos.astype(np.float64)[:, None] * inv[None, :]
    cos, sin = np.cos(ang), np.sin(ang)
    return (np.concatenate([cos, cos], axis=-1).astype(np.float32),
            np.concatenate([-sin, sin], axis=-1).astype(np.float32))


def _decay_tables(chunk):
    lg = np.log(1.0 - np.exp2(-5.0 - np.arange(RET_HEADS, dtype=np.float64)))
    idx = np.arange(chunk, dtype=np.float64)
    diff = idx[:, None] - idx[None, :]
    dec = np.where(diff[None] >= 0, np.exp(np.maximum(diff, 0.0)[None] * lg[:, None, None]), 0.0)
    wk = np.exp((chunk - 1.0 - idx)[None, :] * lg[:, None])
    wq = np.exp((idx + 1.0)[None, :] * lg[:, None])
    cd = np.exp(chunk * lg)
    wk = np.broadcast_to(wk[:, :, None], (RET_HEADS, chunk, RET_DK))
    wq = np.broadcast_to(wq[:, :, None], (RET_HEADS, chunk, RET_DV))
    f32 = lambda a: np.ascontiguousarray(a, dtype=np.float32)
    return f32(dec), f32(wq), f32(wk), f32(cd)


def _group_tables(n_tok, pos0):
    grp = SUBLANES // n_tok
    cosf, sins = _rope_tables(pos0 + np.arange(n_tok))
    dec, wq, wk, cd = _decay_tables(n_tok)
    tile_rows = lambda a: np.concatenate([a] * grp, axis=-2)
    eye = np.eye(grp, dtype=np.float32)
    dec = np.einsum("st,hij->hsitj", eye, dec).reshape(RET_HEADS, grp * n_tok, grp * n_tok)
    return tile_rows(cosf), tile_rows(sins), dec, tile_rows(wq), tile_rows(wk), cd


def kernel(x_prompt, x_sample, mem_prompt, state_ret, state_conv, cache_mem_k, cache_mem_v,
           norm_g, w_in, ret_gn_g, conv_w, conv_b, conv_ln_g, conv_ln_b, xa_norm_g,
           mem_norm_g, w_mk, w_mv, w_out, final_norm_g):
    depth = w_in.shape[0]
    batch, seq, _ = x_prompt.shape
    dbatch, dseq, _ = x_sample.shape

    w_in_b = w_in.astype(BF16)
    w_out_b = w_out.astype(BF16)
    w_mkt_b = w_mk.swapaxes(1, 2).astype(BF16)
    w_mvt_b = w_mv.swapaxes(1, 2).astype(BF16)
    fng = final_norm_g.reshape(1, D_MODEL)

    p_mk, p_mv = _mem_kv(mem_prompt, mem_norm_g, w_mkt_b, w_mvt_b)
    cos_p, sin_p = _rope_tables(np.arange(seq))
    dec_p, wq_p, wk_p, cd_p = _decay_tables(RET_CHUNK if seq % RET_CHUNK == 0 else seq)

    assert SUBLANES % dseq == 0 and dseq % RET_CHUNK != 0
    cos_s, sin_s, dec_s, wq_s, wk_s, cd_s = _group_tables(dseq, PAST_LEN)
    to_hd_m = lambda c: c.transpose(0, 1, 3, 4, 2).reshape(depth, dbatch, XA_W, N_MEM)
    mkt_s, mvt_s = to_hd_m(cache_mem_k), to_hd_m(cache_mem_v)
    hs = x_sample.reshape(dbatch * dseq, D_MODEL)
    z = _sample_in(0, hs, norm_g, w_in_b)

    hp = x_prompt
    states = ()
    y_sample = None
    for l in range(depth):
        hp, p_ret, p_conv, mix, s_ret, s_conv = _layer(
            l, l == depth - 1, dseq, hp, cos_p, sin_p, p_mk, p_mv, norm_g, w_in_b, ret_gn_g,
            conv_w, conv_b, conv_ln_g, conv_ln_b, xa_norm_g, w_out_b, dec_p, wq_p, wk_p,
            cd_p, fng, z, state_ret, state_conv, mkt_s, mvt_s, cos_s, sin_s, dec_s, wq_s,
            wk_s, cd_s, states)
        states = (p_ret, p_conv, s_ret, s_conv)
        if l + 1 < depth:
            hs, z = _sample_mid(l, hs, mix, w_out_b, norm_g, w_in_b)
        else:
            y_sample = _sample_out(l, hs, mix, w_out_b, fng).reshape(dbatch, dseq, D_MODEL)
    y_prompt = hp

    from_hd_m = lambda c: c.reshape(depth, batch, XA_HEADS, XA_DH, N_MEM).transpose(0, 1, 4, 2, 3)
    return (y_prompt, y_sample, states[0], states[1],
            from_hd_m(p_mk), from_hd_m(p_mv), states[2], states[3])
```

```python
import functools

import numpy as np

import jax
import jax.numpy as jnp
from jax import lax
from jax.experimental import pallas as pl
from jax.experimental.pallas import tpu as pltpu

F32 = jnp.float32
BF16 = jnp.bfloat16

D_MODEL = 1024
N_MEM = 256
RET_HEADS = 4
RET_DK = 128
RET_DV = 128
RET_W = RET_HEADS * RET_DV
RET_CHUNK = 128
CONV_W = 256
CONV_K = 31
XA_HEADS = 4
XA_DH = 64
XA_W = XA_HEADS * XA_DH
D_MIX = RET_W + CONV_W + XA_W
ROPE_BASE = 10000.0
EPS = 1e-6
PAST_LEN = 16384
LOG2_E = 1.4426950408889634

C_RQ, C_RK, C_RV, C_RG = 0, 512, 1024, 1536
C_CA, C_CB, C_CG = 2048, 2304, 2560
C_XQ, C_XG = 2816, 3072
D_IN = 3328

VMEM_LIMIT_BYTES = 56 * 1024 * 1024
SUBLANES = 8
LANES = 128
CONV_PAD = 32
CONV_HIST = CONV_K - 1
PROMPT_TILE = 512
CONV_ROWS = 64
PROJ_BLOCK = 256
RET_STAGE_CHUNKS = 2
OUT_SLABS = 4
SAMPLE_PROJ_STEPS = 2


def _rms(x, g):
    return x * lax.rsqrt(jnp.mean(x * x, axis=-1, keepdims=True) + EPS) * g


def _standardize(x):
    mu = jnp.mean(x, axis=-1, keepdims=True)
    d = x - mu
    var = jnp.mean(d * d, axis=-1, keepdims=True)
    return d * lax.rsqrt(var + EPS)


def _sigmoid(x):
    return 1.0 / (1.0 + jnp.exp(-x))


def _silu(x):
    return x * _sigmoid(x)


def _dot(a, b):
    return jnp.dot(a, b, preferred_element_type=F32)


def _dot_nt(a, b):
    return lax.dot_general(a, b, (((1,), (1,)), ((), ())), preferred_element_type=F32)


def _dot_tn(a, b):
    return lax.dot_general(a, b, (((0,), (0,)), ((), ())), preferred_element_type=F32)


def _rope(x, cosf, sins):
    return x * cosf + pltpu.roll(x, RET_DK // 2, 1) * sins


def _zero_row_after(z):
    tile = z[z.shape[0] - SUBLANES:, z.shape[1] - LANES:]
    bits = pltpu.bitcast(tile, jnp.uint32)
    sixteen = jnp.uint32(16)
    zero = lax.shift_right_logical(lax.shift_right_logical(bits, sixteen), sixteen)
    row = pltpu.bitcast(zero, F32)[0:1, :]
    return jnp.concatenate([row, row], axis=1)


def _head_lane_ids():
    return lax.broadcasted_iota(jnp.int32, (1, XA_W), 1) // XA_DH


def _xattn_norm_gate(a, xan, gate, rows):
    head = _head_lane_ids()
    a2 = a * a
    ms = jnp.zeros((rows, XA_W), F32)
    for hh in range(XA_HEADS):
        m = head == hh
        ssq = jnp.sum(jnp.where(m, a2, 0.0), axis=-1, keepdims=True) * (1.0 / XA_DH)
        ms = jnp.where(m, ssq, ms)
    return a * lax.rsqrt(ms + EPS) * xan * gate


def _xattn_heads(q, mkt, mvt, xan, gate, rows):
    head = _head_lane_ids()
    q = q * (LOG2_E * XA_DH ** -0.5)
    scores = [_dot(jnp.where(head == hh, q, 0.0).astype(BF16), mkt) for hh in range(XA_HEADS)]
    pvs, sums = [], []
    for sc in scores:
        e = jnp.exp2(sc - jnp.max(sc, axis=-1, keepdims=True))
        sums.append(jnp.sum(e, axis=-1, keepdims=True))
        pvs.append(_dot_nt(e.astype(BF16), mvt))
    a = jnp.zeros((rows, XA_W), F32)
    for hh in range(XA_HEADS):
        a = jnp.where(head == hh, pvs[hh] / sums[hh], a)
    return _xattn_norm_gate(a, xan, gate, rows)


def _layer_row(ref, layer):
    return ref[layer:layer + 1, :]


def _params(**kw):
    return pltpu.CompilerParams(vmem_limit_bytes=VMEM_LIMIT_BYTES, **kw)


def _mem_kv_kernel(mem_ref, g_ref, wkt_ref, wvt_ref, kt_ref, vt_ref):
    layer = pl.program_id(0)
    gain = g_ref[pl.ds(layer, 1), :]
    for b in range(mem_ref.shape[0]):
        m = _rms(mem_ref[b], gain).astype(BF16)
        kt_ref[0, b] = _dot_nt(wkt_ref[...], m)
        vt_ref[0, b] = _dot_nt(wvt_ref[...], m)


def _mem_kv(mem, g, wkt, wvt):
    depth, batch = g.shape[0], mem.shape[0]
    out = jax.ShapeDtypeStruct((depth, batch, XA_W, N_MEM), F32)
    return pl.pallas_call(
        _mem_kv_kernel,
        grid=(depth,),
        in_specs=[
            pl.BlockSpec((batch, N_MEM, D_MODEL), lambda l: (0, 0, 0)),
            pl.BlockSpec((depth, D_MODEL), lambda l: (0, 0)),
            pl.BlockSpec((None, XA_W, D_MODEL), lambda l: (l, 0, 0)),
            pl.BlockSpec((None, XA_W, D_MODEL), lambda l: (l, 0, 0)),
        ],
        out_specs=[
            pl.BlockSpec((1, batch, XA_W, N_MEM), lambda l: (l, 0, 0, 0)),
            pl.BlockSpec((1, batch, XA_W, N_MEM), lambda l: (l, 0, 0, 0)),
        ],
        out_shape=[out, out],
        compiler_params=_params(dimension_semantics=("arbitrary",)),
        name="mem_kv",
    )(mem, g, wkt, wvt)


def _layer_kernel(layer, final, n_tiles, n_alias, n_tok, n_seq,
                  cd_ref, x_ref, cos_ref, sin_ref, mk_ref, mv_ref, ng_ref, win_ref,
                  gn_ref, cw_ref, cb_ref, lng_ref, lnb_ref, xan_ref, wout_ref,
                  dec_ref, wq_ref, wk_ref, fng_ref,
                  scd_ref, sz_ref, ss0_ref, scbuf_ref, smkt_ref, smvt_ref, scos_ref, ssin_ref,
                  sdec_ref, swq_ref, swk_ref, *refs):
    (y_ref, sret_ref, sconv_ref, smix_ref, ssret_ref, ssconv_ref,
     s_scr, ext_scr, c_scr, q_scr, qw_scr, k_scr, v_scr, g_scr, xg_scr, mix_scr) = refs[n_alias:]

    sample_stages = _sample_mix_stages(
        layer, n_tok, n_seq, scd_ref, sz_ref, ss0_ref, scbuf_ref, smkt_ref, smvt_ref,
        scos_ref, ssin_ref, gn_ref, cw_ref, cb_ref, lng_ref, lnb_ref, xan_ref, sdec_ref,
        swq_ref, swk_ref, smix_ref, ssret_ref, ssconv_ref)

    tm = PROMPT_TILE
    t = pl.program_id(1)

    @pl.when(t == 0)
    def _():
        s_scr[...] = jnp.zeros_like(s_scr)
        ext_scr[0:CONV_PAD, :] = jnp.zeros((CONV_PAD, CONV_W), F32)

    sample_stages[0]()
    sample_stages[1]()

    hb = _rms(x_ref[0], _layer_row(ng_ref, layer)).astype(BF16)

    def proj(a, b):
        return _dot(hb, win_ref[:, a:b])

    cosf = cos_ref[...]
    sins = sin_ref[...]

    u = proj(C_CA, C_CA + CONV_W) * _sigmoid(proj(C_CB, C_CB + CONV_W))
    ext_scr[CONV_PAD:CONV_PAD + tm, :] = u

    pb = PROJ_BLOCK

    def post_q(z, c):
        for i in range(pb // RET_DK):
            hh = c // RET_DK + i
            cols = slice(RET_DK * hh, RET_DK * (hh + 1))
            q = _rope(z[:, RET_DK * i:RET_DK * (i + 1)], cosf, sins)
            q_scr[:, cols] = q.astype(BF16)
            for ch in range(tm // RET_CHUNK):
                rows = slice(RET_CHUNK * ch, RET_CHUNK * (ch + 1))
                qw_scr[rows, cols] = (q[rows, :] * wq_ref[hh]).astype(BF16)

    def post_k(z, c):
        for i in range(pb // RET_DK):
            cols = slice(c + RET_DK * i, c + RET_DK * (i + 1))
            zs = z[:, RET_DK * i:RET_DK * (i + 1)]
            k_scr[:, cols] = _rope(zs, cosf, sins) * (RET_DK ** -0.5)

    def post_v(z, c):
        v_scr[:, c:c + pb] = z.astype(BF16)

    def post_g(z, c):
        g_scr[:, c:c + pb] = _silu(z)

    def post_cg(z, c):
        xg_scr[:, c:c + pb] = _silu(z)

    def post_xq(z, c):
        xg_scr[:, CONV_W + c:CONV_W + c + pb] = z

    def post_xg(z, c):
        xg_scr[:, CONV_W + XA_W + c:CONV_W + XA_W + c + pb] = _silu(z)

    blocks = []
    for col0, width, post in ((C_RQ, RET_W, post_q), (C_RK, RET_W, post_k),
                              (C_RV, RET_W, post_v), (C_RG, RET_W, post_g),
                              (C_CG, CONV_W, post_cg), (C_XQ, XA_W, post_xq),
                              (C_XG, XA_W, post_xg)):
        blocks += [(col0, c, post) for c in range(0, width, pb)]

    first = CONV_PAD - CONV_HIST
    conv_acc = {}
    conv_bias = _layer_row(cb_ref, layer)

    def tap_group(c0, r, wait_zero):
        n_rows = CONV_ROWS if r == 0 else CONV_ROWS + SUBLANES
        part = None
        for a in range((first + CONV_K - 1 - r) // SUBLANES + 1):
            j = SUBLANES * a + r - first
            if 0 <= j < CONV_K:
                w_j = cw_ref[j:j + 1, :]
                if wait_zero is not None:
                    w_j = w_j + wait_zero
                lo = c0 + SUBLANES * a
                term = ext_scr[lo:lo + n_rows, :] * w_j
                part = term if part is None else part + term
        if r == 0:
            conv_acc[c0] = part + conv_bias
        else:
            conv_acc[c0] = conv_acc[c0] + part[r:r + CONV_ROWS, :]
        if r == SUBLANES - 1:
            c_scr[c0:c0 + CONV_ROWS, :] = conv_acc.pop(c0)

    groups = [(c0, r) for c0 in range(0, tm, CONV_ROWS) for r in range(SUBLANES)]
    per_block = -(-len(groups) // len(blocks))
    z_prev = None
    for k, (col0, c, post) in enumerate(blocks):
        for n, (c0, r) in enumerate(groups[per_block * k:per_block * (k + 1)]):
            slab_end = tm * (n + 1) // per_block // SUBLANES * SUBLANES
            wait = None if z_prev is None else _zero_row_after(z_prev[0:slab_end, :])
            tap_group(c0, r, wait)
        z_prev = proj(col0 + c, col0 + c + pb)
        post(z_prev, c)

    sample_stages[2]()

    chunks = range(tm // RET_CHUNK)
    heads = range(RET_HEADS)
    crow = lambda c: slice(RET_CHUNK * c, RET_CHUNK * (c + 1))
    hcol = lambda hh: slice(RET_DK * hh, RET_DK * (hh + 1))
    mkt = mk_ref[0, 0].astype(BF16)
    mvt = mv_ref[0, 0].astype(BF16)
    prev_out = None
    for c0 in range(0, tm // RET_CHUNK, RET_STAGE_CHUNKS):
        part = chunks[c0:c0 + RET_STAGE_CHUNKS]
        half = slice(RET_CHUNK * c0, RET_CHUNK * (c0 + RET_STAGE_CHUNKS))
        n_half = RET_CHUNK * RET_STAGE_CHUNKS
        sc, kv, before, out = {}, {}, {}, {}
        for c in part:
            for hh in heads:
                kf = k_scr[crow(c), hcol(hh)]
                vh = v_scr[crow(c), hcol(hh)]
                sc[c, hh] = _dot_nt(q_scr[crow(c), hcol(hh)], kf.astype(BF16))
                kv[c, hh] = _dot_tn((kf * wk_ref[hh]).astype(BF16), vh)
        for hh in heads:
            state = s_scr[hh]
            for c in part:
                before[c, hh] = state.astype(BF16)
                state = cd_ref[hh] * state + kv[c, hh]
            s_scr[hh] = state
        for c in part:
            for hh in heads:
                lhs = jnp.concatenate([(sc[c, hh] * dec_ref[hh]).astype(BF16),
                                       qw_scr[crow(c), hcol(hh)]], axis=1)
                rhs = jnp.concatenate([v_scr[crow(c), hcol(hh)], before[c, hh]], axis=0)
                out[c, hh] = _dot(lhs, rhs)
        for c in part:
            for hh in heads:
                o = _standardize(out[c, hh]) * gn_ref[layer:layer + 1, hcol(hh)]
                mix_scr[crow(c), hcol(hh)] = (o * g_scr[crow(c), hcol(hh)]).astype(BF16)

        for n in range(OUT_SLABS):
            lo = half.start + n_half * n // OUT_SLABS
            slab = slice(lo, lo + n_half // OUT_SLABS)
            gain = _layer_row(lng_ref, layer)
            if prev_out is not None:
                src = slice(n_half * n // OUT_SLABS, n_half * (n + 1) // OUT_SLABS)
                gain = gain + _zero_row_after(prev_out[src, 0:2 * LANES])
            cn = _standardize(c_scr[slab, :]) * gain + _layer_row(lnb_ref, layer)
            mix_scr[slab, RET_W:RET_W + CONV_W] = (
                _silu(cn) * xg_scr[slab, 0:CONV_W]).astype(BF16)

        a = _xattn_heads(xg_scr[half, CONV_W:CONV_W + XA_W], mkt, mvt,
                         _layer_row(xan_ref, layer),
                         xg_scr[half, CONV_W + XA_W:CONV_W + 2 * XA_W], n_half)
        mix_scr[half, RET_W + CONV_W:D_MIX] = a.astype(BF16)

        prev_out = _dot(mix_scr[half, :], wout_ref[...])
        y = x_ref[0, half, :] + prev_out
        if final:
            y = _rms(y, fng_ref[...])
        y_ref[0, half, :] = y
        if c0 == 0:
            sample_stages[3]()

    @pl.when(t == n_tiles - 1)
    def _():
        sconv_ref[0, 0] = ext_scr[tm + first:tm + CONV_PAD, :]
        sret_ref[0, 0] = s_scr[...]

    ext_scr[0:CONV_PAD, :] = ext_scr[tm:tm + CONV_PAD, :]


def _layer(layer, final, n_tok, x, cosf, sins, mk, mv, ng, win, gn, cw, cb, lng, lnb, xan,
           wout, dec, wq, wk, cd, fng,
           z_s, s0, cbuf, mkt_s, mvt_s, cos_s, sin_s, dec_s, wq_s, wk_s, cd_s, prev_states):
    batch, seq, _ = x.shape
    depth, sbatch = s0.shape[0], s0.shape[1]
    tm = PROMPT_TILE
    n_tiles = seq // tm
    n_seq = sbatch // (batch * n_tiles)
    assert n_seq * batch * n_tiles == sbatch and (n_seq * n_tok) % (2 * SUBLANES) == 0
    whole = lambda a: pl.BlockSpec(a.shape, lambda b, t: (0,) * a.ndim)
    step = lambda b, t: b * n_tiles + t
    sample_specs = [
        pl.BlockSpec(memory_space=pltpu.SMEM),
        pl.BlockSpec((n_seq * n_tok, D_IN), lambda b, t: (step(b, t), 0)),
        pl.BlockSpec((1, n_seq, RET_HEADS, RET_DK, RET_DV),
                     lambda b, t: (layer, step(b, t), 0, 0, 0)),
        pl.BlockSpec((1, n_seq, CONV_HIST, CONV_W), lambda b, t: (layer, step(b, t), 0, 0)),
        pl.BlockSpec((1, n_seq, XA_W, N_MEM), lambda b, t: (layer, step(b, t), 0, 0)),
        pl.BlockSpec((1, n_seq, XA_W, N_MEM), lambda b, t: (layer, step(b, t), 0, 0)),
        whole(cos_s), whole(sin_s), whole(dec_s), whole(wq_s), whole(wk_s),
    ]
    in_specs = [
        pl.BlockSpec(memory_space=pltpu.SMEM),
        pl.BlockSpec((1, tm, D_MODEL), lambda b, t: (b, t, 0)),
        pl.BlockSpec((tm, RET_DK), lambda b, t: (t, 0)),
        pl.BlockSpec((tm, RET_DK), lambda b, t: (t, 0)),
        pl.BlockSpec((1, 1, XA_W, N_MEM), lambda b, t: (layer, b, 0, 0)),
        pl.BlockSpec((1, 1, XA_W, N_MEM), lambda b, t: (layer, b, 0, 0)),
        whole(ng),
        pl.BlockSpec((None, D_MODEL, D_IN), lambda b, t: (layer, 0, 0)),
        whole(gn),
        pl.BlockSpec((None, CONV_K, CONV_W), lambda b, t: (layer, 0, 0)),
        whole(cb), whole(lng), whole(lnb),
        whole(xan),
        pl.BlockSpec((None, D_MIX, D_MODEL), lambda b, t: (layer, 0, 0)),
        whole(dec), whole(wq), whole(wk),
        whole(fng),
    ] + sample_specs
    n_fixed = len(in_specs)
    n_alias = len(prev_states)
    in_specs += [pl.BlockSpec(memory_space=pl.ANY)] * n_alias
    out_specs = [
        pl.BlockSpec((1, tm, D_MODEL), lambda b, t: (b, t, 0)),
        pl.BlockSpec((1, 1, RET_HEADS, RET_DK, RET_DV), lambda b, t: (layer, b, 0, 0, 0)),
        pl.BlockSpec((1, 1, CONV_HIST, CONV_W), lambda b, t: (layer, b, 0, 0)),
        pl.BlockSpec((n_seq * n_tok, D_MIX), lambda b, t: (step(b, t), 0)),
        pl.BlockSpec((1, n_seq, RET_HEADS, RET_DK, RET_DV),
                     lambda b, t: (layer, step(b, t), 0, 0, 0)),
        pl.BlockSpec((1, n_seq, CONV_HIST, CONV_W), lambda b, t: (layer, step(b, t), 0, 0)),
    ]
    out_shape = [
        jax.ShapeDtypeStruct((batch, seq, D_MODEL), F32),
        jax.ShapeDtypeStruct((depth, batch, RET_HEADS, RET_DK, RET_DV), F32),
        jax.ShapeDtypeStruct((depth, batch, CONV_HIST, CONV_W), F32),
        jax.ShapeDtypeStruct((sbatch * n_tok, D_MIX), BF16),
        jax.ShapeDtypeStruct((depth, sbatch, RET_HEADS, RET_DK, RET_DV), F32),
        jax.ShapeDtypeStruct((depth, sbatch, CONV_HIST, CONV_W), F32),
    ]
    state_outputs = (1, 2, 4, 5)
    scratch = [
        pltpu.VMEM((RET_HEADS, RET_DK, RET_DV), F32),
        pltpu.VMEM((CONV_PAD + tm, CONV_W), F32),
        pltpu.VMEM((tm, CONV_W), F32),
        pltpu.VMEM((tm, RET_W), BF16),
        pltpu.VMEM((tm, RET_W), BF16),
        pltpu.VMEM((tm, RET_W), F32),
        pltpu.VMEM((tm, RET_W), BF16),
        pltpu.VMEM((tm, RET_W), F32),
        pltpu.VMEM((tm, CONV_W + 2 * XA_W), F32),
        pltpu.VMEM((tm, D_MIX), BF16),
    ]
    return pl.pallas_call(
        functools.partial(_layer_kernel, layer, final, n_tiles, n_alias, n_tok, n_seq),
        grid=(batch, n_tiles),
        in_specs=in_specs,
        out_specs=out_specs,
        out_shape=out_shape,
        scratch_shapes=scratch,
        input_output_aliases={n_fixed + k: state_outputs[k] for k in range(n_alias)},
        compiler_params=_params(dimension_semantics=("arbitrary", "arbitrary")),
        name=f"layer{layer}",
    )(cd, x, cosf, sins, mk, mv, ng, win, gn, cw, cb, lng, lnb, xan, wout, dec, wq, wk, fng,
      cd_s, z_s, s0, cbuf, mkt_s, mvt_s, cos_s, sin_s, dec_s, wq_s, wk_s, *prev_states)


def _sample_in_kernel(layer, x_ref, ng_ref, win_ref, z_ref, hb_scr):
    @pl.when(pl.program_id(0) == 0)
    def _():
        hb_scr[...] = _rms(x_ref[...], _layer_row(ng_ref, layer)).astype(BF16)

    z_ref[...] = _dot(hb_scr[...], win_ref[...])


def _sample_in(layer, x, ng, win):
    n = x.shape[0]
    cols = D_IN // SAMPLE_PROJ_STEPS
    whole = lambda a: pl.BlockSpec(a.shape, lambda j: (0,) * a.ndim)
    return pl.pallas_call(
        functools.partial(_sample_in_kernel, layer),
        grid=(SAMPLE_PROJ_STEPS,),
        in_specs=[whole(x), whole(ng),
                  pl.BlockSpec((None, D_MODEL, cols), lambda j: (layer, 0, j))],
        out_specs=pl.BlockSpec((n, cols), lambda j: (0, j)),
        out_shape=jax.ShapeDtypeStruct((n, D_IN), F32),
        scratch_shapes=[pltpu.VMEM((n, D_MODEL), BF16)],
        compiler_params=_params(dimension_semantics=("arbitrary",)),
        name="sample_in",
    )(x, ng, win)


def _sample_mid_kernel(layer, x_ref, mix_ref, wout_ref, ng_ref, win_ref, h_ref, z_ref, hb_scr):
    @pl.when(pl.program_id(0) == 0)
    def _():
        h = x_ref[...] + _dot(mix_ref[...], wout_ref[...])
        h_ref[...] = h
        hb_scr[...] = _rms(h, _layer_row(ng_ref, layer + 1)).astype(BF16)

    z_ref[...] = _dot(hb_scr[...], win_ref[...])


def _sample_mid(layer, x, mix, wout, ng, win):
    n = x.shape[0]
    cols = D_IN // SAMPLE_PROJ_STEPS
    whole = lambda a: pl.BlockSpec(a.shape, lambda j: (0,) * a.ndim)
    return pl.pallas_call(
        functools.partial(_sample_mid_kernel, layer),
        grid=(SAMPLE_PROJ_STEPS,),
        in_specs=[whole(x), whole(mix),
                  pl.BlockSpec((None, D_MIX, D_MODEL), lambda j: (layer, 0, 0)),
                  whole(ng),
                  pl.BlockSpec((None, D_MODEL, cols), lambda j: (layer + 1, 0, j))],
        out_specs=[pl.BlockSpec((n, D_MODEL), lambda j: (0, 0)),
                   pl.BlockSpec((n, cols), lambda j: (0, j))],
        out_shape=[jax.ShapeDtypeStruct((n, D_MODEL), F32),
                   jax.ShapeDtypeStruct((n, D_IN), F32)],
        scratch_shapes=[pltpu.VMEM((n, D_MODEL), BF16)],
        compiler_params=_params(dimension_semantics=("arbitrary",)),
        name="sample_mid",
    )(x, mix, wout, ng, win)


def _sample_out_kernel(x_ref, mix_ref, wout_ref, fng_ref, y_ref):
    y_ref[...] = _rms(x_ref[...] + _dot(mix_ref[...], wout_ref[...]), fng_ref[...])


def _sample_out(layer, x, mix, wout, fng):
    n = x.shape[0]
    whole = lambda a: pl.BlockSpec(a.shape, lambda j: (0,) * a.ndim)
    return pl.pallas_call(
        _sample_out_kernel,
        grid=(1,),
        in_specs=[whole(x), whole(mix),
                  pl.BlockSpec((None, D_MIX, D_MODEL), lambda j: (layer, 0, 0)), whole(fng)],
        out_specs=pl.BlockSpec((n, D_MODEL), lambda j: (0, 0)),
        out_shape=jax.ShapeDtypeStruct((n, D_MODEL), F32),
        compiler_params=_params(dimension_semantics=("arbitrary",)),
        name="sample_out",
    )(x, mix, wout, fng)


def _sample_mix_stages(layer, n_tok, n_seq,
                       cd_ref, z_ref, s0_ref, cbuf_ref, mkt_ref, mvt_ref, cos_ref, sin_ref,
                       gn_ref, cw_ref, cb_ref, lng_ref, lnb_ref, xan_ref,
                       dec_ref, wq_ref, wk_ref, mix_ref, sret_ref, sconv_ref):
    grp = SUBLANES // n_tok
    rows = grp * n_tok
    keep = CONV_HIST - n_tok
    cosf = cos_ref[...]
    sins = sin_ref[...]
    head = _head_lane_ids()
    row_id = lax.broadcasted_iota(jnp.int32, (rows, 1), 0)
    row_seq = row_id // n_tok
    row_tok = row_id - row_seq * n_tok
    row4_seq = lax.broadcasted_iota(jnp.int32, (XA_HEADS * rows, 1), 0) % rows // n_tok

    def pick(parts, seq_of_row):
        out = parts[0]
        for s in range(1, grp):
            out = jnp.where(seq_of_row == s, parts[s], out)
        return out

    groups = range(n_seq // grp)
    heads = range(RET_HEADS)
    seqs = [[g * grp + s for s in range(grp)] for g in groups]
    rs = [slice(rows * g, rows * (g + 1)) for g in groups]
    hcol = lambda c0, hh: slice(c0 + RET_DK * hh, c0 + RET_DK * (hh + 1))

    qh, kf, vh, sc, xsc = {}, {}, {}, {}, {}
    inner, cross, xo, xsum = {}, {}, {}, {}

    def first_matmuls():
        for g in groups:
            for hh in heads:
                qh[g, hh] = _rope(z_ref[rs[g], hcol(C_RQ, hh)], cosf, sins).astype(BF16)
                kf[g, hh] = _rope(z_ref[rs[g], hcol(C_RK, hh)], cosf, sins) * (RET_DK ** -0.5)
                vh[g, hh] = z_ref[rs[g], hcol(C_RV, hh)].astype(BF16)
                sc[g, hh] = _dot_nt(qh[g, hh], kf[g, hh].astype(BF16))
            xq = z_ref[rs[g], C_XQ:C_XQ + XA_W] * (LOG2_E * XA_DH ** -0.5)
            q4 = jnp.concatenate([jnp.where(head == hh, xq, 0.0) for hh in range(XA_HEADS)],
                                 axis=0).astype(BF16)
            xsc[g] = [_dot(q4, mkt_ref[0, b].astype(BF16)) for b in seqs[g]]

    def conv_module():
        for g in groups:
            z_a = z_ref[rs[g], C_CA:C_CA + CONV_W]
            u = z_a * _sigmoid(z_ref[rs[g], C_CB:C_CB + CONV_W])
            c = u * cw_ref[CONV_HIST:CONV_K, :] + _layer_row(cb_ref, layer)
            for lag in range(1, n_tok):
                tap = cw_ref[CONV_HIST - lag:CONV_K - lag, :]
                c = c + jnp.where(row_tok >= lag, pltpu.roll(u, lag, 0), 0.0) * tap
            for s, b in enumerate(seqs[g]):
                for i in range(n_tok):
                    window = cbuf_ref[0, b, i:CONV_HIST, :] * cw_ref[0:CONV_HIST - i, :]
                    hi = jnp.sum(window, axis=0, keepdims=True)
                    c = c + jnp.where(row_id == s * n_tok + i, hi, 0.0)
                sconv_ref[0, b, 0:keep, :] = cbuf_ref[0, b, n_tok:CONV_HIST, :]
                sconv_ref[0, b, keep:CONV_HIST, :] = u[s * n_tok:(s + 1) * n_tok, :]
            cn = _standardize(c) * _layer_row(lng_ref, layer) + _layer_row(lnb_ref, layer)
            gate = _silu(z_ref[rs[g], C_CG:C_CG + CONV_W])
            mix_ref[rs[g], RET_W:RET_W + CONV_W] = (_silu(cn) * gate).astype(BF16)

    def second_matmuls():
        for g in groups:
            for hh in heads:
                inner[g, hh] = _dot((sc[g, hh] * dec_ref[hh]).astype(BF16), vh[g, hh])
                kw = kf[g, hh] * wk_ref[hh]
                parts = []
                for s, b in enumerate(seqs[g]):
                    s_prev = s0_ref[0, b, hh]
                    parts.append(_dot(qh[g, hh], s_prev.astype(BF16)))
                    kv = _dot_tn(jnp.where(row_seq == s, kw, 0.0).astype(BF16), vh[g, hh])
                    sret_ref[0, b, hh] = cd_ref[hh] * s_prev + kv
                cross[g, hh] = pick(parts, row_seq) * wq_ref[hh]
            s4 = pick(xsc[g], row4_seq)
            e = jnp.exp2(s4 - jnp.max(s4, axis=-1, keepdims=True))
            xsum[g] = jnp.sum(e, axis=-1, keepdims=True)
            eb = e.astype(BF16)
            xo[g] = [_dot_nt(eb, mvt_ref[0, b].astype(BF16)) for b in seqs[g]]

    def norms_and_stores():
        for g in groups:
            for hh in heads:
                co = hcol(0, hh)
                o = _standardize(inner[g, hh] + cross[g, hh]) * gn_ref[layer:layer + 1, co]
                gate = _silu(z_ref[rs[g], hcol(C_RG, hh)])
                mix_ref[rs[g], co] = (o * gate).astype(BF16)
            o4 = pick(xo[g], row4_seq) / xsum[g]
            a = jnp.zeros((rows, XA_W), F32)
            for hh in range(XA_HEADS):
                a = jnp.where(head == hh, o4[rows * hh:rows * (hh + 1), :], a)
            gate = _silu(z_ref[rs[g], C_XG:C_XG + XA_W])
            mix_ref[rs[g], RET_W + CONV_W:D_MIX] = _xattn_norm_gate(
                a, _layer_row(xan_ref, layer), gate, rows).astype(BF16)

    return [first_matmuls, conv_module, second_matmuls, norms_and_stores]


def _rope_tables(pos):
    half = RET_DK // 2
    inv = np.float64(ROPE_BASE) ** (-np.arange(half, dtype=np.float64) / half)
    ang = pos.astype(np.float64)[:, None] * inv[None, :]
    cos, sin = np.cos(ang), np.sin(ang)
    return (np.concatenate([cos, cos], axis=-1).astype(np.float32),
            np.concatenate([-sin, sin], axis=-1).astype(np.float32))


def _decay_tables(chunk):
    lg = np.log(1.0 - np.exp2(-5.0 - np.arange(RET_HEADS, dtype=np.float64)))
    idx = np.arange(chunk, dtype=np.float64)
    diff = idx[:, None] - idx[None, :]
    dec = np.where(diff[None] >= 0, np.exp(np.maximum(diff, 0.0)[None] * lg[:, None, None]), 0.0)
    wk = np.exp((chunk - 1.0 - idx)[None, :] * lg[:, None])
    wq = np.exp((idx + 1.0)[None, :] * lg[:, None])
    cd = np.exp(chunk * lg)
    wk = np.broadcast_to(wk[:, :, None], (RET_HEADS, chunk, RET_DK))
    wq = np.broadcast_to(wq[:, :, None], (RET_HEADS, chunk, RET_DV))
    f32 = lambda a: np.ascontiguousarray(a, dtype=np.float32)
    return f32(dec), f32(wq), f32(wk), f32(cd)


def _group_tables(n_tok, pos0):
    grp = SUBLANES // n_tok
    cosf, sins = _rope_tables(pos0 + np.arange(n_tok))
    dec, wq, wk, cd = _decay_tables(n_tok)
    tile_rows = lambda a: np.concatenate([a] * grp, axis=-2)
    eye = np.eye(grp, dtype=np.float32)
    dec = np.einsum("st,hij->hsitj", eye, dec).reshape(RET_HEADS, grp * n_tok, grp * n_tok)
    return tile_rows(cosf), tile_rows(sins), dec, tile_rows(wq), tile_rows(wk), cd


def kernel(x_prompt, x_sample, mem_prompt, state_ret, state_conv, cache_mem_k, cache_mem_v,
           norm_g, w_in, ret_gn_g, conv_w, conv_b, conv_ln_g, conv_ln_b, xa_norm_g,
           mem_norm_g, w_mk, w_mv, w_out, final_norm_g):
    depth = w_in.shape[0]
    batch, seq, _ = x_prompt.shape
    dbatch, dseq, _ = x_sample.shape

    w_in_b = w_in.astype(BF16)
    w_out_b = w_out.astype(BF16)
    w_mkt_b = w_mk.swapaxes(1, 2).astype(BF16)
    w_mvt_b = w_mv.swapaxes(1, 2).astype(BF16)
    fng = final_norm_g.reshape(1, D_MODEL)

    p_mk, p_mv = _mem_kv(mem_prompt, mem_norm_g, w_mkt_b, w_mvt_b)
    cos_p, sin_p = _rope_tables(np.arange(seq))
    dec_p, wq_p, wk_p, cd_p = _decay_tables(RET_CHUNK if seq % RET_CHUNK == 0 else seq)

    assert SUBLANES % dseq == 0 and dseq % RET_CHUNK != 0
    cos_s, sin_s, dec_s, wq_s, wk_s, cd_s = _group_tables(dseq, PAST_LEN)
    to_hd_m = lambda c: c.transpose(0, 1, 3, 4, 2).reshape(depth, dbatch, XA_W, N_MEM)
    mkt_s, mvt_s = to_hd_m(cache_mem_k), to_hd_m(cache_mem_v)
    hs = x_sample.reshape(dbatch * dseq, D_MODEL)
    z = _sample_in(0, hs, norm_g, w_in_b)

    hp = x_prompt
    states = ()
    y_sample = None
    for l in range(depth):
        hp, p_ret, p_conv, mix, s_ret, s_conv = _layer(
            l, l == depth - 1, dseq, hp, cos_p, sin_p, p_mk, p_mv, norm_g, w_in_b, ret_gn_g,
            conv_w, conv_b, conv_ln_g, conv_ln_b, xa_norm_g, w_out_b, dec_p, wq_p, wk_p,
            cd_p, fng, z, state_ret, state_conv, mkt_s, mvt_s, cos_s, sin_s, dec_s, wq_s,
            wk_s, cd_s, states)
        states = (p_ret, p_conv, s_ret, s_conv)
        if l + 1 < depth:
            hs, z = _sample_mid(l, hs, mix, w_out_b, norm_g, w_in_b)
        else:
            y_sample = _sample_out(l, hs, mix, w_out_b, fng).reshape(dbatch, dseq, D_MODEL)
    y_prompt = hp

    from_hd_m = lambda c: c.reshape(depth, batch, XA_HEADS, XA_DH, N_MEM).transpose(0, 1, 4, 2, 3)
    return (y_prompt, y_sample, states[0], states[1],
            from_hd_m(p_mk), from_hd_m(p_mv), states[2], states[3])
```

```python
import functools

import numpy as np

import jax
import jax.numpy as jnp
from jax import lax
from jax.experimental import pallas as pl
from jax.experimental.pallas import tpu as pltpu

F32 = jnp.float32
BF16 = jnp.bfloat16

D_MODEL = 1024
N_MEM = 256
RET_HEADS = 4
RET_DK = 128
RET_DV = 128
RET_W = RET_HEADS * RET_DV
RET_CHUNK = 128
CONV_W = 256
CONV_K = 31
XA_HEADS = 4
XA_DH = 64
XA_W = XA_HEADS * XA_DH
D_MIX = RET_W + CONV_W + XA_W
ROPE_BASE = 10000.0
EPS = 1e-6
PAST_LEN = 16384
LOG2_E = 1.4426950408889634

C_RQ, C_RK, C_RV, C_RG = 0, 512, 1024, 1536
C_CA, C_CB, C_CG = 2048, 2304, 2560
C_XQ, C_XG = 2816, 3072
D_IN = 3328

VMEM_LIMIT_BYTES = 56 * 1024 * 1024
SUBLANES = 8
LANES = 128
CONV_PAD = 32
CONV_HIST = CONV_K - 1
PROMPT_TILE = 512
CONV_ROWS = 64
PROJ_BLOCK = 256
RET_STAGE_CHUNKS = 2
RET_EARLY_STAGES = 6
SAMPLE_PROJ_STEPS = 2


def _rms(x, g):
    return x * lax.rsqrt(jnp.mean(x * x, axis=-1, keepdims=True) + EPS) * g


def _standardize(x):
    mu = jnp.mean(x, axis=-1, keepdims=True)
    d = x - mu
    var = jnp.mean(d * d, axis=-1, keepdims=True)
    return d * lax.rsqrt(var + EPS)


def _sigmoid(x):
    return 1.0 / (1.0 + jnp.exp(-x))


def _silu(x):
    return x * _sigmoid(x)


def _dot(a, b):
    return jnp.dot(a, b, preferred_element_type=F32)


def _dot_nt(a, b):
    return lax.dot_general(a, b, (((1,), (1,)), ((), ())), preferred_element_type=F32)


def _dot_tn(a, b):
    return lax.dot_general(a, b, (((0,), (0,)), ((), ())), preferred_element_type=F32)


def _rope(x, cosf, sins):
    return x * cosf + pltpu.roll(x, RET_DK // 2, 1) * sins


def _zero_row_after(z):
    tile = z[z.shape[0] - SUBLANES:, z.shape[1] - LANES:]
    bits = pltpu.bitcast(tile, jnp.uint32)
    sixteen = jnp.uint32(16)
    zero = lax.shift_right_logical(lax.shift_right_logical(bits, sixteen), sixteen)
    row = pltpu.bitcast(zero, F32)[0:1, :]
    return jnp.concatenate([row, row], axis=1)


def _head_lane_ids():
    return lax.broadcasted_iota(jnp.int32, (1, XA_W), 1) // XA_DH


def _xattn_norm_gate(a, xan, gate, rows):
    head = _head_lane_ids()
    a2 = a * a
    ms = jnp.zeros((rows, XA_W), F32)
    for hh in range(XA_HEADS):
        m = head == hh
        ssq = jnp.sum(jnp.where(m, a2, 0.0), axis=-1, keepdims=True) * (1.0 / XA_DH)
        ms = jnp.where(m, ssq, ms)
    return a * lax.rsqrt(ms + EPS) * xan * gate


def _xattn_heads(q, mkt, mvt, xan, gate, rows):
    head = _head_lane_ids()
    q = q * (LOG2_E * XA_DH ** -0.5)
    scores = [_dot(jnp.where(head == hh, q, 0.0).astype(BF16), mkt) for hh in range(XA_HEADS)]
    pvs, sums = [], []
    for sc in scores:
        e = jnp.exp2(sc - jnp.max(sc, axis=-1, keepdims=True))
        sums.append(jnp.sum(e, axis=-1, keepdims=True))
        pvs.append(_dot_nt(e.astype(BF16), mvt))
    a = jnp.zeros((rows, XA_W), F32)
    for hh in range(XA_HEADS):
        a = jnp.where(head == hh, pvs[hh] / sums[hh], a)
    return _xattn_norm_gate(a, xan, gate, rows)


def _layer_row(ref, layer):
    return ref[layer:layer + 1, :]


def _params(**kw):
    return pltpu.CompilerParams(vmem_limit_bytes=VMEM_LIMIT_BYTES, **kw)


def _mem_kv_kernel(mem_ref, g_ref, wkt_ref, wvt_ref, kt_ref, vt_ref):
    layer = pl.program_id(0)
    gain = g_ref[pl.ds(layer, 1), :]
    for b in range(mem_ref.shape[0]):
        m = _rms(mem_ref[b], gain).astype(BF16)
        kt_ref[0, b] = _dot_nt(wkt_ref[...], m)
        vt_ref[0, b] = _dot_nt(wvt_ref[...], m)


def _mem_kv(mem, g, wkt, wvt):
    depth, batch = g.shape[0], mem.shape[0]
    out = jax.ShapeDtypeStruct((depth, batch, XA_W, N_MEM), F32)
    return pl.pallas_call(
        _mem_kv_kernel,
        grid=(depth,),
        in_specs=[
            pl.BlockSpec((batch, N_MEM, D_MODEL), lambda l: (0, 0, 0)),
            pl.BlockSpec((depth, D_MODEL), lambda l: (0, 0)),
            pl.BlockSpec((None, XA_W, D_MODEL), lambda l: (l, 0, 0)),
            pl.BlockSpec((None, XA_W, D_MODEL), lambda l: (l, 0, 0)),
        ],
        out_specs=[
            pl.BlockSpec((1, batch, XA_W, N_MEM), lambda l: (l, 0, 0, 0)),
            pl.BlockSpec((1, batch, XA_W, N_MEM), lambda l: (l, 0, 0, 0)),
        ],
        out_shape=[out, out],
        compiler_params=_params(dimension_semantics=("arbitrary",)),
        name="mem_kv",
    )(mem, g, wkt, wvt)


def _layer_kernel(layer, final, n_tiles, n_alias, n_tok, n_seq,
                  cd_ref, x_ref, cos_ref, sin_ref, mk_ref, mv_ref, ng_ref, win_ref,
                  gn_ref, cw_ref, cb_ref, lng_ref, lnb_ref, xan_ref, wout_ref,
                  dec_ref, wq_ref, wk_ref, fng_ref,
                  scd_ref, sz_ref, ss0_ref, scbuf_ref, smkt_ref, smvt_ref, scos_ref, ssin_ref,
                  sdec_ref, swq_ref, swk_ref, *refs):
    (y_ref, sret_ref, sconv_ref, smix_ref, ssret_ref, ssconv_ref,
     s_scr, ext_scr, c_scr, q_scr, qw_scr, k_scr, kw_scr, v_scr, g_scr, xg_scr,
     mix_scr) = refs[n_alias:]

    sample_stages = _sample_mix_stages(
        layer, n_tok, n_seq, scd_ref, sz_ref, ss0_ref, scbuf_ref, smkt_ref, smvt_ref,
        scos_ref, ssin_ref, gn_ref, cw_ref, cb_ref, lng_ref, lnb_ref, xan_ref, sdec_ref,
        swq_ref, swk_ref, smix_ref, ssret_ref, ssconv_ref)

    tm = PROMPT_TILE
    t = pl.program_id(1)

    @pl.when(t == 0)
    def _():
        s_scr[...] = jnp.zeros_like(s_scr)
        ext_scr[0:CONV_PAD, :] = jnp.zeros((CONV_PAD, CONV_W), F32)

    sample_stages[0]()
    sample_stages[1]()

    hb = _rms(x_ref[0], _layer_row(ng_ref, layer)).astype(BF16)

    def proj(a, b):
        return _dot(hb, win_ref[:, a:b])

    cosf = cos_ref[...]
    sins = sin_ref[...]

    u = proj(C_CA, C_CA + CONV_W) * _sigmoid(proj(C_CB, C_CB + CONV_W))
    ext_scr[CONV_PAD:CONV_PAD + tm, :] = u

    pb = PROJ_BLOCK

    def post_q(z, c):
        for i in range(pb // RET_DK):
            hh = c // RET_DK + i
            cols = slice(RET_DK * hh, RET_DK * (hh + 1))
            q = _rope(z[:, RET_DK * i:RET_DK * (i + 1)], cosf, sins)
            q_scr[:, cols] = q.astype(BF16)
            for ch in range(tm // RET_CHUNK):
                rows = slice(RET_CHUNK * ch, RET_CHUNK * (ch + 1))
                qw_scr[rows, cols] = (q[rows, :] * wq_ref[hh]).astype(BF16)

    def post_k(z, c):
        for i in range(pb // RET_DK):
            hh = c // RET_DK + i
            cols = slice(RET_DK * hh, RET_DK * (hh + 1))
            k = _rope(z[:, RET_DK * i:RET_DK * (i + 1)], cosf, sins) * (RET_DK ** -0.5)
            k_scr[:, cols] = k.astype(BF16)
            for ch in range(tm // RET_CHUNK):
                rows = slice(RET_CHUNK * ch, RET_CHUNK * (ch + 1))
                kw_scr[rows, cols] = (k[rows, :] * wk_ref[hh]).astype(BF16)

    def post_v(z, c):
        v_scr[:, c:c + pb] = z.astype(BF16)

    def post_g(z, c):
        g_scr[:, c:c + pb] = _silu(z)

    def post_cg(z, c):
        xg_scr[:, c:c + pb] = _silu(z)

    def post_xq(z, c):
        xg_scr[:, CONV_W + c:CONV_W + c + pb] = z

    def post_xg(z, c):
        xg_scr[:, CONV_W + XA_W + c:CONV_W + XA_W + c + pb] = _silu(z)

    blocks = []
    for col0, width, post in ((C_RQ, RET_W, post_q), (C_RK, RET_W, post_k),
                              (C_RV, RET_W, post_v), (C_RG, RET_W, post_g),
                              (C_CG, CONV_W, post_cg), (C_XQ, XA_W, post_xq),
                              (C_XG, XA_W, post_xg)):
        blocks += [(col0, c, post) for c in range(0, width, pb)]

    first = CONV_PAD - CONV_HIST
    conv_acc = {}
    conv_bias = _layer_row(cb_ref, layer)

    def tap_group(c0, r, wait_zero):
        n_rows = CONV_ROWS if r == 0 else CONV_ROWS + SUBLANES
        part = None
        for a in range((first + CONV_K - 1 - r) // SUBLANES + 1):
            j = SUBLANES * a + r - first
            if 0 <= j < CONV_K:
                w_j = cw_ref[j:j + 1, :]
                if wait_zero is not None:
                    w_j = w_j + wait_zero
                lo = c0 + SUBLANES * a
                term = ext_scr[lo:lo + n_rows, :] * w_j
                part = term if part is None else part + term
        if r == 0:
            conv_acc[c0] = part + conv_bias
        else:
            conv_acc[c0] = conv_acc[c0] + part[r:r + CONV_ROWS, :]
        if r == SUBLANES - 1:
            c_scr[c0:c0 + CONV_ROWS, :] = conv_acc.pop(c0)

    def conv_tail(lo, wait_zero):
        slab = slice(lo, lo + CONV_ROWS)
        gain = _layer_row(lng_ref, layer)
        if wait_zero is not None:
            gain = gain + wait_zero
        cn = _standardize(c_scr[slab, :]) * gain + _layer_row(lnb_ref, layer)
        mix_scr[slab, RET_W:RET_W + CONV_W] = (_silu(cn) * xg_scr[slab, 0:CONV_W]).astype(BF16)

    taps = [functools.partial(tap_group, c0, r)
            for c0 in range(0, tm, CONV_ROWS) for r in range(SUBLANES)]
    tails = [functools.partial(conv_tail, lo) for lo in range(0, tm, CONV_ROWS)]
    gate_block = [post for _, _, post in blocks].index(post_cg)
    taps_per = -(-len(taps) // gate_block)
    tails_per = -(-len(tails) // (len(blocks) - gate_block - 1))
    chunks = range(tm // RET_CHUNK)
    heads = range(RET_HEADS)
    crow = lambda c: slice(RET_CHUNK * c, RET_CHUNK * (c + 1))
    hcol = lambda hh: slice(RET_DK * hh, RET_DK * (hh + 1))

    def retention_stages(c0):
        part = chunks[c0:c0 + RET_STAGE_CHUNKS]
        sc, kv, before, out = {}, {}, {}, {}

        def scores_and_updates():
            for c in part:
                for hh in heads:
                    sc[c, hh] = _dot_nt(q_scr[crow(c), hcol(hh)], k_scr[crow(c), hcol(hh)])
                    kv[c, hh] = _dot_tn(kw_scr[crow(c), hcol(hh)], v_scr[crow(c), hcol(hh)])

        def recurrence_and_outputs():
            for hh in heads:
                state = s_scr[hh]
                for c in part:
                    before[c, hh] = state.astype(BF16)
                    state = cd_ref[hh] * state + kv[c, hh]
                s_scr[hh] = state
            for c in part:
                for hh in heads:
                    lhs = jnp.concatenate([(sc[c, hh] * dec_ref[hh]).astype(BF16),
                                           qw_scr[crow(c), hcol(hh)]], axis=1)
                    rhs = jnp.concatenate([v_scr[crow(c), hcol(hh)], before[c, hh]], axis=0)
                    out[c, hh] = _dot(lhs, rhs)

        def norms():
            for c in part:
                for hh in heads:
                    o = _standardize(out[c, hh]) * gn_ref[layer:layer + 1, hcol(hh)]
                    mix_scr[crow(c), hcol(hh)] = (o * g_scr[crow(c), hcol(hh)]).astype(BF16)

        return [scores_and_updates, recurrence_and_outputs, norms]

    n_parts = tm // RET_CHUNK // RET_STAGE_CHUNKS
    ret_stages = [retention_stages(RET_STAGE_CHUNKS * p) for p in range(n_parts)]
    pending = [stage for p in range(n_parts) for stage in ret_stages[p]]
    per_late_block = RET_EARLY_STAGES // (len(blocks) - gate_block)
    early_retention = {}
    for k in range(gate_block, len(blocks)):
        early_retention[k] = pending[:per_late_block]
        pending = pending[per_late_block:]
    z_prev = None
    for k, (col0, c, post) in enumerate(blocks):
        if k < gate_block:
            items = taps[taps_per * k:taps_per * (k + 1)]
        elif k == gate_block:
            items = []
        else:
            j = k - gate_block - 1
            items = tails[tails_per * j:tails_per * (j + 1)]
        for n, item in enumerate(items):
            slab_end = tm * (n + 1) // len(items) // SUBLANES * SUBLANES
            item(None if z_prev is None else _zero_row_after(z_prev[0:slab_end, :]))
        for stage in early_retention.get(k, []):
            stage()
        z_prev = proj(col0 + c, col0 + c + pb)
        post(z_prev, c)

    sample_stages[2]()
    for stage in pending:
        stage()

    mkt = mk_ref[0, 0].astype(BF16)
    mvt = mv_ref[0, 0].astype(BF16)
    for c0 in range(0, tm // RET_CHUNK, RET_STAGE_CHUNKS):
        half = slice(RET_CHUNK * c0, RET_CHUNK * (c0 + RET_STAGE_CHUNKS))
        n_half = RET_CHUNK * RET_STAGE_CHUNKS

        a = _xattn_heads(xg_scr[half, CONV_W:CONV_W + XA_W], mkt, mvt,
                         _layer_row(xan_ref, layer),
                         xg_scr[half, CONV_W + XA_W:CONV_W + 2 * XA_W], n_half)
        mix_scr[half, RET_W + CONV_W:D_MIX] = a.astype(BF16)

        y = x_ref[0, half, :] + _dot(mix_scr[half, :], wout_ref[...])
        if final:
            y = _rms(y, fng_ref[...])
        y_ref[0, half, :] = y
        if c0 == 0:
            sample_stages[3]()

    @pl.when(t == n_tiles - 1)
    def _():
        sconv_ref[0, 0] = ext_scr[tm + first:tm + CONV_PAD, :]
        sret_ref[0, 0] = s_scr[...]

    ext_scr[0:CONV_PAD, :] = ext_scr[tm:tm + CONV_PAD, :]


def _layer(layer, final, n_tok, x, cosf, sins, mk, mv, ng, win, gn, cw, cb, lng, lnb, xan,
           wout, dec, wq, wk, cd, fng,
           z_s, s0, cbuf, mkt_s, mvt_s, cos_s, sin_s, dec_s, wq_s, wk_s, cd_s, prev_states):
    batch, seq, _ = x.shape
    depth, sbatch = s0.shape[0], s0.shape[1]
    tm = PROMPT_TILE
    n_tiles = seq // tm
    n_seq = sbatch // (batch * n_tiles)
    assert n_seq * batch * n_tiles == sbatch and (n_seq * n_tok) % (2 * SUBLANES) == 0
    whole = lambda a: pl.BlockSpec(a.shape, lambda b, t: (0,) * a.ndim)
    step = lambda b, t: b * n_tiles + t
    sample_specs = [
        pl.BlockSpec(memory_space=pltpu.SMEM),
        pl.BlockSpec((n_seq * n_tok, D_IN), lambda b, t: (step(b, t), 0)),
        pl.BlockSpec((1, n_seq, RET_HEADS, RET_DK, RET_DV),
                     lambda b, t: (layer, step(b, t), 0, 0, 0)),
        pl.BlockSpec((1, n_seq, CONV_HIST, CONV_W), lambda b, t: (layer, step(b, t), 0, 0)),
        pl.BlockSpec((1, n_seq, XA_W, N_MEM), lambda b, t: (layer, step(b, t), 0, 0)),
        pl.BlockSpec((1, n_seq, XA_W, N_MEM), lambda b, t: (layer, step(b, t), 0, 0)),
        whole(cos_s), whole(sin_s), whole(dec_s), whole(wq_s), whole(wk_s),
    ]
    in_specs = [
        pl.BlockSpec(memory_space=pltpu.SMEM),
        pl.BlockSpec((1, tm, D_MODEL), lambda b, t: (b, t, 0)),
        pl.BlockSpec((tm, RET_DK), lambda b, t: (t, 0)),
        pl.BlockSpec((tm, RET_DK), lambda b, t: (t, 0)),
        pl.BlockSpec((1, 1, XA_W, N_MEM), lambda b, t: (layer, b, 0, 0)),
        pl.BlockSpec((1, 1, XA_W, N_MEM), lambda b, t: (layer, b, 0, 0)),
        whole(ng),
        pl.BlockSpec((None, D_MODEL, D_IN), lambda b, t: (layer, 0, 0)),
        whole(gn),
        pl.BlockSpec((None, CONV_K, CONV_W), lambda b, t: (layer, 0, 0)),
        whole(cb), whole(lng), whole(lnb),
        whole(xan),
        pl.BlockSpec((None, D_MIX, D_MODEL), lambda b, t: (layer, 0, 0)),
        whole(dec), whole(wq), whole(wk),
        whole(fng),
    ] + sample_specs
    n_fixed = len(in_specs)
    n_alias = len(prev_states)
    in_specs += [pl.BlockSpec(memory_space=pl.ANY)] * n_alias
    out_specs = [
        pl.BlockSpec((1, tm, D_MODEL), lambda b, t: (b, t, 0)),
        pl.BlockSpec((1, 1, RET_HEADS, RET_DK, RET_DV), lambda b, t: (layer, b, 0, 0, 0)),
        pl.BlockSpec((1, 1, CONV_HIST, CONV_W), lambda b, t: (layer, b, 0, 0)),
        pl.BlockSpec((n_seq * n_tok, D_MIX), lambda b, t: (step(b, t), 0)),
        pl.BlockSpec((1, n_seq, RET_HEADS, RET_DK, RET_DV),
                     lambda b, t: (layer, step(b, t), 0, 0, 0)),
        pl.BlockSpec((1, n_seq, CONV_HIST, CONV_W), lambda b, t: (layer, step(b, t), 0, 0)),
    ]
    out_shape = [
        jax.ShapeDtypeStruct((batch, seq, D_MODEL), F32),
        jax.ShapeDtypeStruct((depth, batch, RET_HEADS, RET_DK, RET_DV), F32),
        jax.ShapeDtypeStruct((depth, batch, CONV_HIST, CONV_W), F32),
        jax.ShapeDtypeStruct((sbatch * n_tok, D_MIX), BF16),
        jax.ShapeDtypeStruct((depth, sbatch, RET_HEADS, RET_DK, RET_DV), F32),
        jax.ShapeDtypeStruct((depth, sbatch, CONV_HIST, CONV_W), F32),
    ]
    state_outputs = (1, 2, 4, 5)
    scratch = [
        pltpu.VMEM((RET_HEADS, RET_DK, RET_DV), F32),
        pltpu.VMEM((CONV_PAD + tm, CONV_W), F32),
        pltpu.VMEM((tm, CONV_W), F32),
        pltpu.VMEM((tm, RET_W), BF16),
        pltpu.VMEM((tm, RET_W), BF16),
        pltpu.VMEM((tm, RET_W), BF16),
        pltpu.VMEM((tm, RET_W), BF16),
        pltpu.VMEM((tm, RET_W), BF16),
        pltpu.VMEM((tm, RET_W), F32),
        pltpu.VMEM((tm, CONV_W + 2 * XA_W), F32),
        pltpu.VMEM((tm, D_MIX), BF16),
    ]
    return pl.pallas_call(
        functools.partial(_layer_kernel, layer, final, n_tiles, n_alias, n_tok, n_seq),
        grid=(batch, n_tiles),
        in_specs=in_specs,
        out_specs=out_specs,
        out_shape=out_shape,
        scratch_shapes=scratch,
        input_output_aliases={n_fixed + k: state_outputs[k] for k in range(n_alias)},
        compiler_params=_params(dimension_semantics=("arbitrary", "arbitrary")),
        name=f"layer{layer}",
    )(cd, x, cosf, sins, mk, mv, ng, win, gn, cw, cb, lng, lnb, xan, wout, dec, wq, wk, fng,
      cd_s, z_s, s0, cbuf, mkt_s, mvt_s, cos_s, sin_s, dec_s, wq_s, wk_s, *prev_states)


def _sample_in_kernel(layer, x_ref, ng_ref, win_ref, z_ref, hb_scr):
    @pl.when(pl.program_id(0) == 0)
    def _():
        hb_scr[...] = _rms(x_ref[...], _layer_row(ng_ref, layer)).astype(BF16)

    z_ref[...] = _dot(hb_scr[...], win_ref[...])


def _sample_in(layer, x, ng, win):
    n = x.shape[0]
    cols = D_IN // SAMPLE_PROJ_STEPS
    whole = lambda a: pl.BlockSpec(a.shape, lambda j: (0,) * a.ndim)
    return pl.pallas_call(
        functools.partial(_sample_in_kernel, layer),
        grid=(SAMPLE_PROJ_STEPS,),
        in_specs=[whole(x), whole(ng),
                  pl.BlockSpec((None, D_MODEL, cols), lambda j: (layer, 0, j))],
        out_specs=pl.BlockSpec((n, cols), lambda j: (0, j)),
        out_shape=jax.ShapeDtypeStruct((n, D_IN), F32),
        scratch_shapes=[pltpu.VMEM((n, D_MODEL), BF16)],
        compiler_params=_params(dimension_semantics=("arbitrary",)),
        name="sample_in",
    )(x, ng, win)


def _sample_mid_kernel(layer, x_ref, mix_ref, wout_ref, ng_ref, win_ref, h_ref, z_ref, hb_scr):
    @pl.when(pl.program_id(0) == 0)
    def _():
        h = x_ref[...] + _dot(mix_ref[...], wout_ref[...])
        h_ref[...] = h
        hb_scr[...] = _rms(h, _layer_row(ng_ref, layer + 1)).astype(BF16)

    z_ref[...] = _dot(hb_scr[...], win_ref[...])


def _sample_mid(layer, x, mix, wout, ng, win):
    n = x.shape[0]
    cols = D_IN // SAMPLE_PROJ_STEPS
    whole = lambda a: pl.BlockSpec(a.shape, lambda j: (0,) * a.ndim)
    return pl.pallas_call(
        functools.partial(_sample_mid_kernel, layer),
        grid=(SAMPLE_PROJ_STEPS,),
        in_specs=[whole(x), whole(mix),
                  pl.BlockSpec((None, D_MIX, D_MODEL), lambda j: (layer, 0, 0)),
                  whole(ng),
                  pl.BlockSpec((None, D_MODEL, cols), lambda j: (layer + 1, 0, j))],
        out_specs=[pl.BlockSpec((n, D_MODEL), lambda j: (0, 0)),
                   pl.BlockSpec((n, cols), lambda j: (0, j))],
        out_shape=[jax.ShapeDtypeStruct((n, D_MODEL), F32),
                   jax.ShapeDtypeStruct((n, D_IN), F32)],
        scratch_shapes=[pltpu.VMEM((n, D_MODEL), BF16)],
        compiler_params=_params(dimension_semantics=("arbitrary",)),
        name="sample_mid",
    )(x, mix, wout, ng, win)


def _sample_out_kernel(x_ref, mix_ref, wout_ref, fng_ref, y_ref):
    y_ref[...] = _rms(x_ref[...] + _dot(mix_ref[...], wout_ref[...]), fng_ref[...])


def _sample_out(layer, x, mix, wout, fng):
    n = x.shape[0]
    whole = lambda a: pl.BlockSpec(a.shape, lambda j: (0,) * a.ndim)
    return pl.pallas_call(
        _sample_out_kernel,
        grid=(1,),
        in_specs=[whole(x), whole(mix),
                  pl.BlockSpec((None, D_MIX, D_MODEL), lambda j: (layer, 0, 0)), whole(fng)],
        out_specs=pl.BlockSpec((n, D_MODEL), lambda j: (0, 0)),
        out_shape=jax.ShapeDtypeStruct((n, D_MODEL), F32),
        compiler_params=_params(dimension_semantics=("arbitrary",)),
        name="sample_out",
    )(x, mix, wout, fng)


def _sample_mix_stages(layer, n_tok, n_seq,
                       cd_ref, z_ref, s0_ref, cbuf_ref, mkt_ref, mvt_ref, cos_ref, sin_ref,
                       gn_ref, cw_ref, cb_ref, lng_ref, lnb_ref, xan_ref,
                       dec_ref, wq_ref, wk_ref, mix_ref, sret_ref, sconv_ref):
    grp = SUBLANES // n_tok
    rows = grp * n_tok
    keep = CONV_HIST - n_tok
    cosf = cos_ref[...]
    sins = sin_ref[...]
    head = _head_lane_ids()
    row_id = lax.broadcasted_iota(jnp.int32, (rows, 1), 0)
    row_seq = row_id // n_tok
    row_tok = row_id - row_seq * n_tok
    row4_seq = lax.broadcasted_iota(jnp.int32, (XA_HEADS * rows, 1), 0) % rows // n_tok

    def pick(parts, seq_of_row):
        out = parts[0]
        for s in range(1, grp):
            out = jnp.where(seq_of_row == s, parts[s], out)
        return out

    groups = range(n_seq // grp)
    heads = range(RET_HEADS)
    seqs = [[g * grp + s for s in range(grp)] for g in groups]
    rs = [slice(rows * g, rows * (g + 1)) for g in groups]
    hcol = lambda c0, hh: slice(c0 + RET_DK * hh, c0 + RET_DK * (hh + 1))

    qh, kf, vh, sc, xsc = {}, {}, {}, {}, {}
    inner, cross, xo, xsum = {}, {}, {}, {}

    def first_matmuls():
        for g in groups:
            for hh in heads:
                qh[g, hh] = _rope(z_ref[rs[g], hcol(C_RQ, hh)], cosf, sins).astype(BF16)
                kf[g, hh] = _rope(z_ref[rs[g], hcol(C_RK, hh)], cosf, sins) * (RET_DK ** -0.5)
                vh[g, hh] = z_ref[rs[g], hcol(C_RV, hh)].astype(BF16)
                sc[g, hh] = _dot_nt(qh[g, hh], kf[g, hh].astype(BF16))
            xq = z_ref[rs[g], C_XQ:C_XQ + XA_W] * (LOG2_E * XA_DH ** -0.5)
            q4 = jnp.concatenate([jnp.where(head == hh, xq, 0.0) for hh in range(XA_HEADS)],
                                 axis=0).astype(BF16)
            xsc[g] = [_dot(q4, mkt_ref[0, b].astype(BF16)) for b in seqs[g]]

    def conv_module():
        for g in groups:
            z_a = z_ref[rs[g], C_CA:C_CA + CONV_W]
            u = z_a * _sigmoid(z_ref[rs[g], C_CB:C_CB + CONV_W])
            c = u * cw_ref[CONV_HIST:CONV_K, :] + _layer_row(cb_ref, layer)
            for lag in range(1, n_tok):
                tap = cw_ref[CONV_HIST - lag:CONV_K - lag, :]
                c = c + jnp.where(row_tok >= lag, pltpu.roll(u, lag, 0), 0.0) * tap
            for s, b in enumerate(seqs[g]):
                for i in range(n_tok):
                    window = cbuf_ref[0, b, i:CONV_HIST, :] * cw_ref[0:CONV_HIST - i, :]
                    hi = jnp.sum(window, axis=0, keepdims=True)
                    c = c + jnp.where(row_id == s * n_tok + i, hi, 0.0)
                sconv_ref[0, b, 0:keep, :] = cbuf_ref[0, b, n_tok:CONV_HIST, :]
                sconv_ref[0, b, keep:CONV_HIST, :] = u[s * n_tok:(s + 1) * n_tok, :]
            cn = _standardize(c) * _layer_row(lng_ref, layer) + _layer_row(lnb_ref, layer)
            gate = _silu(z_ref[rs[g], C_CG:C_CG + CONV_W])
            mix_ref[rs[g], RET_W:RET_W + CONV_W] = (_silu(cn) * gate).astype(BF16)

    def second_matmuls():
        for g in groups:
            for hh in heads:
                inner[g, hh] = _dot((sc[g, hh] * dec_ref[hh]).astype(BF16), vh[g, hh])
                kw = kf[g, hh] * wk_ref[hh]
                parts = []
                for s, b in enumerate(seqs[g]):
                    s_prev = s0_ref[0, b, hh]
                    parts.append(_dot(qh[g, hh], s_prev.astype(BF16)))
                    kv = _dot_tn(jnp.where(row_seq == s, kw, 0.0).astype(BF16), vh[g, hh])
                    sret_ref[0, b, hh] = cd_ref[hh] * s_prev + kv
                cross[g, hh] = pick(parts, row_seq) * wq_ref[hh]
            s4 = pick(xsc[g], row4_seq)
            e = jnp.exp2(s4 - jnp.max(s4, axis=-1, keepdims=True))
            xsum[g] = jnp.sum(e, axis=-1, keepdims=True)
            eb = e.astype(BF16)
            xo[g] = [_dot_nt(eb, mvt_ref[0, b].astype(BF16)) for b in seqs[g]]

    def norms_and_stores():
        for g in groups:
            for hh in heads:
                co = hcol(0, hh)
                o = _standardize(inner[g, hh] + cross[g, hh]) * gn_ref[layer:layer + 1, co]
                gate = _silu(z_ref[rs[g], hcol(C_RG, hh)])
                mix_ref[rs[g], co] = (o * gate).astype(BF16)
            o4 = pick(xo[g], row4_seq) / xsum[g]
            a = jnp.zeros((rows, XA_W), F32)
            for hh in range(XA_HEADS):
                a = jnp.where(head == hh, o4[rows * hh:rows * (hh + 1), :], a)
            gate = _silu(z_ref[rs[g], C_XG:C_XG + XA_W])
            mix_ref[rs[g], RET_W + CONV_W:D_MIX] = _xattn_norm_gate(
                a, _layer_row(xan_ref, layer), gate, rows).astype(BF16)

    return [first_matmuls, conv_module, second_matmuls, norms_and_stores]


def _rope_tables(pos):
    half = RET_DK // 2
    inv = np.float64(ROPE_BASE) ** (-np.arange(half, dtype=np.float64) / half)
    ang = pos.astype(np.float64)[:, None] * inv[None, :]
    cos, sin = np.cos(ang), np.sin(ang)
    return (np.concatenate([cos, cos], axis=-1).astype(np.float32),
            np.concatenate([-sin, sin], axis=-1).astype(np.float32))


def _decay_tables(chunk):
    lg = np.log(1.0 - np.exp2(-5.0 - np.arange(RET_HEADS, dtype=np.float64)))
    idx = np.arange(chunk, dtype=np.float64)
    diff = idx[:, None] - idx[None, :]
    dec = np.where(diff[None] >= 0, np.exp(np.maximum(diff, 0.0)[None] * lg[:, None, None]), 0.0)
    wk = np.exp((chunk - 1.0 - idx)[None, :] * lg[:, None])
    wq = np.exp((idx + 1.0)[None, :] * lg[:, None])
    cd = np.exp(chunk * lg)
    wk = np.broadcast_to(wk[:, :, None], (RET_HEADS, chunk, RET_DK))
    wq = np.broadcast_to(wq[:, :, None], (RET_HEADS, chunk, RET_DV))
    f32 = lambda a: np.ascontiguousarray(a, dtype=np.float32)
    return f32(dec), f32(wq), f32(wk), f32(cd)


def _group_tables(n_tok, pos0):
    grp = SUBLANES // n_tok
    cosf, sins = _rope_tables(pos0 + np.arange(n_tok))
    dec, wq, wk, cd = _decay_tables(n_tok)
    tile_rows = lambda a: np.concatenate([a] * grp, axis=-2)
    eye = np.eye(grp, dtype=np.float32)
    dec = np.einsum("st,hij->hsitj", eye, dec).reshape(RET_HEADS, grp * n_tok, grp * n_tok)
    return tile_rows(cosf), tile_rows(sins), dec, tile_rows(wq), tile_rows(wk), cd


def kernel(x_prompt, x_sample, mem_prompt, state_ret, state_conv, cache_mem_k, cache_mem_v,
           norm_g, w_in, ret_gn_g, conv_w, conv_b, conv_ln_g, conv_ln_b, xa_norm_g,
           mem_norm_g, w_mk, w_mv, w_out, final_norm_g):
    depth = w_in.shape[0]
    batch, seq, _ = x_prompt.shape
    dbatch, dseq, _ = x_sample.shape

    w_in_b = w_in.astype(BF16)
    w_out_b = w_out.astype(BF16)
    w_mkt_b = w_mk.swapaxes(1, 2).astype(BF16)
    w_mvt_b = w_mv.swapaxes(1, 2).astype(BF16)
    fng = final_norm_g.reshape(1, D_MODEL)

    p_mk, p_mv = _mem_kv(mem_prompt, mem_norm_g, w_mkt_b, w_mvt_b)
    cos_p, sin_p = _rope_tables(np.arange(seq))
    dec_p, wq_p, wk_p, cd_p = _decay_tables(RET_CHUNK if seq % RET_CHUNK == 0 else seq)

    assert SUBLANES % dseq == 0 and dseq % RET_CHUNK != 0
    cos_s, sin_s, dec_s, wq_s, wk_s, cd_s = _group_tables(dseq, PAST_LEN)
    to_hd_m = lambda c: c.transpose(0, 1, 3, 4, 2).reshape(depth, dbatch, XA_W, N_MEM)
    mkt_s, mvt_s = to_hd_m(cache_mem_k), to_hd_m(cache_mem_v)
    hs = x_sample.reshape(dbatch * dseq, D_MODEL)
    z = _sample_in(0, hs, norm_g, w_in_b)

    hp = x_prompt
    states = ()
    y_sample = None
    for l in range(depth):
        hp, p_ret, p_conv, mix, s_ret, s_conv = _layer(
            l, l == depth - 1, dseq, hp, cos_p, sin_p, p_mk, p_mv, norm_g, w_in_b, ret_gn_g,
            conv_w, conv_b, conv_ln_g, conv_ln_b, xa_norm_g, w_out_b, dec_p, wq_p, wk_p,
            cd_p, fng, z, state_ret, state_conv, mkt_s, mvt_s, cos_s, sin_s, dec_s, wq_s,
            wk_s, cd_s, states)
        states = (p_ret, p_conv, s_ret, s_conv)
        if l + 1 < depth:
            hs, z = _sample_mid(l, hs, mix, w_out_b, norm_g, w_in_b)
        else:
            y_sample = _sample_out(l, hs, mix, w_out_b, fng).reshape(dbatch, dseq, D_MODEL)
    y_prompt = hp

    from_hd_m = lambda c: c.reshape(depth, batch, XA_HEADS, XA_DH, N_MEM).transpose(0, 1, 4, 2, 3)
    return (y_prompt, y_sample, states[0], states[1],
            from_hd_m(p_mk), from_hd_m(p_mv), states[2], states[3])
```

```python
import functools

import numpy as np

import jax
import jax.numpy as jnp
from jax import lax
from jax.experimental import pallas as pl
from jax.experimental.pallas import tpu as pltpu

F32 = jnp.float32
BF16 = jnp.bfloat16

D_MODEL = 1024
N_MEM = 256
RET_HEADS = 4
RET_DK = 128
RET_DV = 128
RET_W = RET_HEADS * RET_DV
RET_CHUNK = 128
CONV_W = 256
CONV_K = 31
XA_HEADS = 4
XA_DH = 64
XA_W = XA_HEADS * XA_DH
D_MIX = RET_W + CONV_W + XA_W
ROPE_BASE = 10000.0
EPS = 1e-6
PAST_LEN = 16384
LOG2_E = 1.4426950408889634

C_RQ, C_RK, C_RV, C_RG = 0, 512, 1024, 1536
C_CA, C_CB, C_CG = 2048, 2304, 2560
C_XQ, C_XG = 2816, 3072
D_IN = 3328

VMEM_LIMIT_BYTES = 56 * 1024 * 1024
SUBLANES = 8
LANES = 128
CONV_PAD = 32
CONV_HIST = CONV_K - 1
PROMPT_TILE = 512
CONV_ROWS = 64
PROJ_BLOCK = 256
RET_STAGE_CHUNKS = 2
SAMPLE_PROJ_STEPS = 2


def _rms(x, g):
    return x * lax.rsqrt(jnp.mean(x * x, axis=-1, keepdims=True) + EPS) * g


def _standardize(x):
    mu = jnp.mean(x, axis=-1, keepdims=True)
    d = x - mu
    var = jnp.mean(d * d, axis=-1, keepdims=True)
    return d * lax.rsqrt(var + EPS)


def _sigmoid(x):
    return 1.0 / (1.0 + jnp.exp(-x))


def _silu(x):
    return x * _sigmoid(x)


def _dot(a, b):
    return jnp.dot(a, b, preferred_element_type=F32)


def _dot_nt(a, b):
    return lax.dot_general(a, b, (((1,), (1,)), ((), ())), preferred_element_type=F32)


def _dot_tn(a, b):
    return lax.dot_general(a, b, (((0,), (0,)), ((), ())), preferred_element_type=F32)


def _rope(x, cosf, sins):
    return x * cosf + pltpu.roll(x, RET_DK // 2, 1) * sins


def _zero_row_after(z):
    tile = z[z.shape[0] - SUBLANES:, z.shape[1] - LANES:]
    bits = pltpu.bitcast(tile, jnp.uint32)
    sixteen = jnp.uint32(16)
    zero = lax.shift_right_logical(lax.shift_right_logical(bits, sixteen), sixteen)
    row = pltpu.bitcast(zero, F32)[0:1, :]
    return jnp.concatenate([row, row], axis=1)


def _head_lane_ids():
    return lax.broadcasted_iota(jnp.int32, (1, XA_W), 1) // XA_DH


def _xattn_norm_gate(a, xan, gate, rows):
    head = _head_lane_ids()
    a2 = a * a
    ms = jnp.zeros((rows, XA_W), F32)
    for hh in range(XA_HEADS):
        m = head == hh
        ssq = jnp.sum(jnp.where(m, a2, 0.0), axis=-1, keepdims=True) * (1.0 / XA_DH)
        ms = jnp.where(m, ssq, ms)
    return a * lax.rsqrt(ms + EPS) * xan * gate


def _layer_row(ref, layer):
    return ref[layer:layer + 1, :]


def _params(**kw):
    return pltpu.CompilerParams(vmem_limit_bytes=VMEM_LIMIT_BYTES, **kw)


def _mem_kv_kernel(mem_ref, g_ref, wkt_ref, wvt_ref, kt_ref, vt_ref):
    layer = pl.program_id(0)
    gain = g_ref[pl.ds(layer, 1), :]
    for b in range(mem_ref.shape[0]):
        m = _rms(mem_ref[b], gain).astype(BF16)
        kt_ref[0, b] = _dot_nt(wkt_ref[...], m)
        vt_ref[0, b] = _dot_nt(wvt_ref[...], m)


def _mem_kv(mem, g, wkt, wvt):
    depth, batch = g.shape[0], mem.shape[0]
    out = jax.ShapeDtypeStruct((depth, batch, XA_W, N_MEM), F32)
    return pl.pallas_call(
        _mem_kv_kernel,
        grid=(depth,),
        in_specs=[
            pl.BlockSpec((batch, N_MEM, D_MODEL), lambda l: (0, 0, 0)),
            pl.BlockSpec((depth, D_MODEL), lambda l: (0, 0)),
            pl.BlockSpec((None, XA_W, D_MODEL), lambda l: (l, 0, 0)),
            pl.BlockSpec((None, XA_W, D_MODEL), lambda l: (l, 0, 0)),
        ],
        out_specs=[
            pl.BlockSpec((1, batch, XA_W, N_MEM), lambda l: (l, 0, 0, 0)),
            pl.BlockSpec((1, batch, XA_W, N_MEM), lambda l: (l, 0, 0, 0)),
        ],
        out_shape=[out, out],
        compiler_params=_params(dimension_semantics=("arbitrary",)),
        name="mem_kv",
    )(mem, g, wkt, wvt)


def _layer_kernel(layer, final, n_tiles, n_alias, n_tok, n_seq,
                  cd_ref, x_ref, cos_ref, sin_ref, mk_ref, mv_ref, ng_ref, win_ref,
                  gn_ref, cw_ref, cb_ref, lng_ref, lnb_ref, xan_ref, wout_ref,
                  dec_ref, wq_ref, wk_ref, fng_ref,
                  scd_ref, sz_ref, ss0_ref, scbuf_ref, smkt_ref, smvt_ref, scos_ref, ssin_ref,
                  sdec_ref, swq_ref, swk_ref, *refs):
    (y_ref, sret_ref, sconv_ref, smix_ref, ssret_ref, ssconv_ref,
     s_scr, ext_scr, c_scr, q_scr, qw_scr, k_scr, kw_scr, v_scr, g_scr, xg_scr,
     mix_scr) = refs[n_alias:]

    sample_stages = _sample_mix_stages(
        layer, n_tok, n_seq, scd_ref, sz_ref, ss0_ref, scbuf_ref, smkt_ref, smvt_ref,
        scos_ref, ssin_ref, gn_ref, cw_ref, cb_ref, lng_ref, lnb_ref, xan_ref, sdec_ref,
        swq_ref, swk_ref, smix_ref, ssret_ref, ssconv_ref)

    tm = PROMPT_TILE
    t = pl.program_id(1)

    @pl.when(t == 0)
    def _():
        s_scr[...] = jnp.zeros_like(s_scr)
        ext_scr[0:CONV_PAD, :] = jnp.zeros((CONV_PAD, CONV_W), F32)

    sample_stages[0]()
    sample_stages[1]()

    hb = _rms(x_ref[0], _layer_row(ng_ref, layer)).astype(BF16)

    def proj(a, b):
        return _dot(hb, win_ref[:, a:b])

    cosf = cos_ref[...]
    sins = sin_ref[...]

    u = proj(C_CA, C_CA + CONV_W) * _sigmoid(proj(C_CB, C_CB + CONV_W))
    ext_scr[CONV_PAD:CONV_PAD + tm, :] = u

    pb = PROJ_BLOCK

    def post_q(z, c):
        for i in range(pb // RET_DK):
            hh = c // RET_DK + i
            cols = slice(RET_DK * hh, RET_DK * (hh + 1))
            q = _rope(z[:, RET_DK * i:RET_DK * (i + 1)], cosf, sins)
            q_scr[:, cols] = q.astype(BF16)
            for ch in range(tm // RET_CHUNK):
                rows = slice(RET_CHUNK * ch, RET_CHUNK * (ch + 1))
                qw_scr[rows, cols] = (q[rows, :] * wq_ref[hh]).astype(BF16)

    def post_k(z, c):
        for i in range(pb // RET_DK):
            hh = c // RET_DK + i
            cols = slice(RET_DK * hh, RET_DK * (hh + 1))
            k = _rope(z[:, RET_DK * i:RET_DK * (i + 1)], cosf, sins) * (RET_DK ** -0.5)
            k_scr[:, cols] = k.astype(BF16)
            for ch in range(tm // RET_CHUNK):
                rows = slice(RET_CHUNK * ch, RET_CHUNK * (ch + 1))
                kw_scr[rows, cols] = (k[rows, :] * wk_ref[hh]).astype(BF16)

    def post_v(z, c):
        v_scr[:, c:c + pb] = z.astype(BF16)

    def post_g(z, c):
        g_scr[:, c:c + pb] = _silu(z)

    def post_cg(z, c):
        xg_scr[:, c:c + pb] = _silu(z)

    def post_xq(z, c):
        xg_scr[:, CONV_W + c:CONV_W + c + pb] = z

    def post_xg(z, c):
        xg_scr[:, CONV_W + XA_W + c:CONV_W + XA_W + c + pb] = _silu(z)

    blocks = []
    for col0, width, post in ((C_RQ, RET_W, post_q), (C_RK, RET_W, post_k),
                              (C_RV, RET_W, post_v), (C_RG, RET_W, post_g),
                              (C_CG, CONV_W, post_cg), (C_XQ, XA_W, post_xq),
                              (C_XG, XA_W, post_xg)):
        blocks += [(col0, c, post) for c in range(0, width, pb)]

    first = CONV_PAD - CONV_HIST
    conv_acc = {}
    conv_bias = _layer_row(cb_ref, layer)

    def tap_group(c0, r, wait_zero):
        n_rows = CONV_ROWS if r == 0 else CONV_ROWS + SUBLANES
        part = None
        for a in range((first + CONV_K - 1 - r) // SUBLANES + 1):
            j = SUBLANES * a + r - first
            if 0 <= j < CONV_K:
                w_j = cw_ref[j:j + 1, :]
                if wait_zero is not None:
                    w_j = w_j + wait_zero
                lo = c0 + SUBLANES * a
                term = ext_scr[lo:lo + n_rows, :] * w_j
                part = term if part is None else part + term
        if r == 0:
            conv_acc[c0] = part + conv_bias
        else:
            conv_acc[c0] = conv_acc[c0] + part[r:r + CONV_ROWS, :]
        if r == SUBLANES - 1:
            c_scr[c0:c0 + CONV_ROWS, :] = conv_acc.pop(c0)

    def conv_tail(lo, wait_zero):
        slab = slice(lo, lo + CONV_ROWS)
        gain = _layer_row(lng_ref, layer)
        if wait_zero is not None:
            gain = gain + wait_zero
        cn = _standardize(c_scr[slab, :]) * gain + _layer_row(lnb_ref, layer)
        mix_scr[slab, RET_W:RET_W + CONV_W] = (_silu(cn) * xg_scr[slab, 0:CONV_W]).astype(BF16)

    taps = [functools.partial(tap_group, c0, r)
            for c0 in range(0, tm, CONV_ROWS) for r in range(SUBLANES)]
    tails = [functools.partial(conv_tail, lo) for lo in range(0, tm, CONV_ROWS)]

    chunks = range(tm // RET_CHUNK)
    heads = range(RET_HEADS)
    crow = lambda c: slice(RET_CHUNK * c, RET_CHUNK * (c + 1))
    hcol = lambda hh: slice(RET_DK * hh, RET_DK * (hh + 1))

    def retention_stages(c0):
        part = chunks[c0:c0 + RET_STAGE_CHUNKS]
        sc, kv, before, out = {}, {}, {}, {}

        def scores_and_updates():
            for c in part:
                for hh in heads:
                    sc[c, hh] = _dot_nt(q_scr[crow(c), hcol(hh)], k_scr[crow(c), hcol(hh)])
                    kv[c, hh] = _dot_tn(kw_scr[crow(c), hcol(hh)], v_scr[crow(c), hcol(hh)])

        def recurrence_and_outputs():
            for hh in heads:
                state = s_scr[hh]
                for c in part:
                    before[c, hh] = state.astype(BF16)
                    state = cd_ref[hh] * state + kv[c, hh]
                s_scr[hh] = state
            for c in part:
                for hh in heads:
                    lhs = jnp.concatenate([(sc[c, hh] * dec_ref[hh]).astype(BF16),
                                           qw_scr[crow(c), hcol(hh)]], axis=1)
                    rhs = jnp.concatenate([v_scr[crow(c), hcol(hh)], before[c, hh]], axis=0)
                    out[c, hh] = _dot(lhs, rhs)

        def norms():
            for c in part:
                for hh in heads:
                    o = _standardize(out[c, hh]) * gn_ref[layer:layer + 1, hcol(hh)]
                    mix_scr[crow(c), hcol(hh)] = (o * g_scr[crow(c), hcol(hh)]).astype(BF16)

        return [scores_and_updates, recurrence_and_outputs, norms]

    n_half = RET_CHUNK * RET_STAGE_CHUNKS
    mkt = mk_ref[0, 0].astype(BF16)
    mvt = mv_ref[0, 0].astype(BF16)
    head = _head_lane_ids()

    def xattn_stages(half):
        scores, pvs, sums = [], [], []

        def query_key():
            q = xg_scr[half, CONV_W:CONV_W + XA_W] * (LOG2_E * XA_DH ** -0.5)
            for hh in range(XA_HEADS):
                scores.append(_dot(jnp.where(head == hh, q, 0.0).astype(BF16), mkt))

        def softmax_value():
            for sc in scores:
                e = jnp.exp2(sc - jnp.max(sc, axis=-1, keepdims=True))
                sums.append(jnp.sum(e, axis=-1, keepdims=True))
                pvs.append(_dot_nt(e.astype(BF16), mvt))

        def norm_gate():
            a = jnp.zeros((n_half, XA_W), F32)
            for hh in range(XA_HEADS):
                a = jnp.where(head == hh, pvs[hh] / sums[hh], a)
            gate = xg_scr[half, CONV_W + XA_W:CONV_W + 2 * XA_W]
            mix_scr[half, RET_W + CONV_W:D_MIX] = _xattn_norm_gate(
                a, _layer_row(xan_ref, layer), gate, n_half).astype(BF16)

        return [query_key, softmax_value, norm_gate]

    halves = [slice(n_half * p, n_half * (p + 1)) for p in range(tm // n_half)]
    pending = [stage for p in range(len(halves))
               for stage in retention_stages(RET_STAGE_CHUNKS * p)]
    gate_block = [post for _, _, post in blocks].index(post_cg)
    late_blocks = len(blocks) - gate_block
    taps_per = -(-len(taps) // gate_block)
    tails_per = -(-len(tails) // (late_blocks - 1))
    stages_per = len(pending) // late_blocks
    z_prev = None
    for k, (col0, c, post) in enumerate(blocks):
        if k < gate_block:
            items = taps[taps_per * k:taps_per * (k + 1)]
        elif k == gate_block:
            items = []
        else:
            j = k - gate_block - 1
            items = tails[tails_per * j:tails_per * (j + 1)]
        for n, item in enumerate(items):
            slab_end = tm * (n + 1) // len(items) // SUBLANES * SUBLANES
            item(None if z_prev is None else _zero_row_after(z_prev[0:slab_end, :]))
        if k >= gate_block:
            for stage in pending[:stages_per]:
                stage()
            pending = pending[stages_per:]
        z_prev = proj(col0 + c, col0 + c + pb)
        post(z_prev, c)

    sample_stages[2]()
    for stage in pending:
        stage()
    for p, half in enumerate(halves):
        for stage in xattn_stages(half):
            stage()
        y = x_ref[0, half, :] + _dot(mix_scr[half, :], wout_ref[...])
        if final:
            y = _rms(y, fng_ref[...])
        y_ref[0, half, :] = y
        if p == 0:
            sample_stages[3]()

    @pl.when(t == n_tiles - 1)
    def _():
        sconv_ref[0, 0] = ext_scr[tm + first:tm + CONV_PAD, :]
        sret_ref[0, 0] = s_scr[...]

    ext_scr[0:CONV_PAD, :] = ext_scr[tm:tm + CONV_PAD, :]


def _layer(layer, final, n_tok, x, cosf, sins, mk, mv, ng, win, gn, cw, cb, lng, lnb, xan,
           wout, dec, wq, wk, cd, fng,
           z_s, s0, cbuf, mkt_s, mvt_s, cos_s, sin_s, dec_s, wq_s, wk_s, cd_s, prev_states):
    batch, seq, _ = x.shape
    depth, sbatch = s0.shape[0], s0.shape[1]
    tm = PROMPT_TILE
    n_tiles = seq // tm
    n_seq = sbatch // (batch * n_tiles)
    assert n_seq * batch * n_tiles == sbatch and (n_seq * n_tok) % (2 * SUBLANES) == 0
    whole = lambda a: pl.BlockSpec(a.shape, lambda b, t: (0,) * a.ndim)
    step = lambda b, t: b * n_tiles + t
    sample_specs = [
        pl.BlockSpec(memory_space=pltpu.SMEM),
        pl.BlockSpec((n_seq * n_tok, D_IN), lambda b, t: (step(b, t), 0)),
        pl.BlockSpec((1, n_seq, RET_HEADS, RET_DK, RET_DV),
                     lambda b, t: (layer, step(b, t), 0, 0, 0)),
        pl.BlockSpec((1, n_seq, CONV_HIST, CONV_W), lambda b, t: (layer, step(b, t), 0, 0)),
        pl.BlockSpec((1, n_seq, XA_W, N_MEM), lambda b, t: (layer, step(b, t), 0, 0)),
        pl.BlockSpec((1, n_seq, XA_W, N_MEM), lambda b, t: (layer, step(b, t), 0, 0)),
        whole(cos_s), whole(sin_s), whole(dec_s), whole(wq_s), whole(wk_s),
    ]
    in_specs = [
        pl.BlockSpec(memory_space=pltpu.SMEM),
        pl.BlockSpec((1, tm, D_MODEL), lambda b, t: (b, t, 0)),
        pl.BlockSpec((tm, RET_DK), lambda b, t: (t, 0)),
        pl.BlockSpec((tm, RET_DK), lambda b, t: (t, 0)),
        pl.BlockSpec((1, 1, XA_W, N_MEM), lambda b, t: (layer, b, 0, 0)),
        pl.BlockSpec((1, 1, XA_W, N_MEM), lambda b, t: (layer, b, 0, 0)),
        whole(ng),
        pl.BlockSpec((None, D_MODEL, D_IN), lambda b, t: (layer, 0, 0)),
        whole(gn),
        pl.BlockSpec((None, CONV_K, CONV_W), lambda b, t: (layer, 0, 0)),
        whole(cb), whole(lng), whole(lnb),
        whole(xan),
        pl.BlockSpec((None, D_MIX, D_MODEL), lambda b, t: (layer, 0, 0)),
        whole(dec), whole(wq), whole(wk),
        whole(fng),
    ] + sample_specs
    n_fixed = len(in_specs)
    n_alias = len(prev_states)
    in_specs += [pl.BlockSpec(memory_space=pl.ANY)] * n_alias
    out_specs = [
        pl.BlockSpec((1, tm, D_MODEL), lambda b, t: (b, t, 0)),
        pl.BlockSpec((1, 1, RET_HEADS, RET_DK, RET_DV), lambda b, t: (layer, b, 0, 0, 0)),
        pl.BlockSpec((1, 1, CONV_HIST, CONV_W), lambda b, t: (layer, b, 0, 0)),
        pl.BlockSpec((n_seq * n_tok, D_MIX), lambda b, t: (step(b, t), 0)),
        pl.BlockSpec((1, n_seq, RET_HEADS, RET_DK, RET_DV),
                     lambda b, t: (layer, step(b, t), 0, 0, 0)),
        pl.BlockSpec((1, n_seq, CONV_HIST, CONV_W), lambda b, t: (layer, step(b, t), 0, 0)),
    ]
    out_shape = [
        jax.ShapeDtypeStruct((batch, seq, D_MODEL), F32),
        jax.ShapeDtypeStruct((depth, batch, RET_HEADS, RET_DK, RET_DV), F32),
        jax.ShapeDtypeStruct((depth, batch, CONV_HIST, CONV_W), F32),
        jax.ShapeDtypeStruct((sbatch * n_tok, D_MIX), BF16),
        jax.ShapeDtypeStruct((depth, sbatch, RET_HEADS, RET_DK, RET_DV), F32),
        jax.ShapeDtypeStruct((depth, sbatch, CONV_HIST, CONV_W), F32),
    ]
    state_outputs = (1, 2, 4, 5)
    scratch = [
        pltpu.VMEM((RET_HEADS, RET_DK, RET_DV), F32),
        pltpu.VMEM((CONV_PAD + tm, CONV_W), F32),
        pltpu.VMEM((tm, CONV_W), F32),
        pltpu.VMEM((tm, RET_W), BF16),
        pltpu.VMEM((tm, RET_W), BF16),
        pltpu.VMEM((tm, RET_W), BF16),
        pltpu.VMEM((tm, RET_W), BF16),
        pltpu.VMEM((tm, RET_W), BF16),
        pltpu.VMEM((tm, RET_W), F32),
        pltpu.VMEM((tm, CONV_W + 2 * XA_W), F32),
        pltpu.VMEM((tm, D_MIX), BF16),
    ]
    return pl.pallas_call(
        functools.partial(_layer_kernel, layer, final, n_tiles, n_alias, n_tok, n_seq),
        grid=(batch, n_tiles),
        in_specs=in_specs,
        out_specs=out_specs,
        out_shape=out_shape,
        scratch_shapes=scratch,
        input_output_aliases={n_fixed + k: state_outputs[k] for k in range(n_alias)},
        compiler_params=_params(dimension_semantics=("arbitrary", "arbitrary")),
        name=f"layer{layer}",
    )(cd, x, cosf, sins, mk, mv, ng, win, gn, cw, cb, lng, lnb, xan, wout, dec, wq, wk, fng,
      cd_s, z_s, s0, cbuf, mkt_s, mvt_s, cos_s, sin_s, dec_s, wq_s, wk_s, *prev_states)


def _sample_in_kernel(layer, x_ref, ng_ref, win_ref, z_ref, hb_scr):
    @pl.when(pl.program_id(0) == 0)
    def _():
        hb_scr[...] = _rms(x_ref[...], _layer_row(ng_ref, layer)).astype(BF16)

    z_ref[...] = _dot(hb_scr[...], win_ref[...])


def _sample_in(layer, x, ng, win):
    n = x.shape[0]
    cols = D_IN // SAMPLE_PROJ_STEPS
    whole = lambda a: pl.BlockSpec(a.shape, lambda j: (0,) * a.ndim)
    return pl.pallas_call(
        functools.partial(_sample_in_kernel, layer),
        grid=(SAMPLE_PROJ_STEPS,),
        in_specs=[whole(x), whole(ng),
                  pl.BlockSpec((None, D_MODEL, cols), lambda j: (layer, 0, j))],
        out_specs=pl.BlockSpec((n, cols), lambda j: (0, j)),
        out_shape=jax.ShapeDtypeStruct((n, D_IN), F32),
        scratch_shapes=[pltpu.VMEM((n, D_MODEL), BF16)],
        compiler_params=_params(dimension_semantics=("arbitrary",)),
        name="sample_in",
    )(x, ng, win)


def _sample_mid_kernel(layer, x_ref, mix_ref, wout_ref, ng_ref, win_ref, h_ref, z_ref, hb_scr):
    @pl.when(pl.program_id(0) == 0)
    def _():
        h = x_ref[...] + _dot(mix_ref[...], wout_ref[...])
        h_ref[...] = h
        hb_scr[...] = _rms(h, _layer_row(ng_ref, layer + 1)).astype(BF16)

    z_ref[...] = _dot(hb_scr[...], win_ref[...])


def _sample_mid(layer, x, mix, wout, ng, win):
    n = x.shape[0]
    cols = D_IN // SAMPLE_PROJ_STEPS
    whole = lambda a: pl.BlockSpec(a.shape, lambda j: (0,) * a.ndim)
    return pl.pallas_call(
        functools.partial(_sample_mid_kernel, layer),
        grid=(SAMPLE_PROJ_STEPS,),
        in_specs=[whole(x), whole(mix),
                  pl.BlockSpec((None, D_MIX, D_MODEL), lambda j: (layer, 0, 0)),
                  whole(ng),
                  pl.BlockSpec((None, D_MODEL, cols), lambda j: (layer + 1, 0, j))],
        out_specs=[pl.BlockSpec((n, D_MODEL), lambda j: (0, 0)),
                   pl.BlockSpec((n, cols), lambda j: (0, j))],
        out_shape=[jax.ShapeDtypeStruct((n, D_MODEL), F32),
                   jax.ShapeDtypeStruct((n, D_IN), F32)],
        scratch_shapes=[pltpu.VMEM((n, D_MODEL), BF16)],
        compiler_params=_params(dimension_semantics=("arbitrary",)),
        name="sample_mid",
    )(x, mix, wout, ng, win)


def _sample_out_kernel(x_ref, mix_ref, wout_ref, fng_ref, y_ref):
    y_ref[...] = _rms(x_ref[...] + _dot(mix_ref[...], wout_ref[...]), fng_ref[...])


def _sample_out(layer, x, mix, wout, fng):
    n = x.shape[0]
    whole = lambda a: pl.BlockSpec(a.shape, lambda j: (0,) * a.ndim)
    return pl.pallas_call(
        _sample_out_kernel,
        grid=(1,),
        in_specs=[whole(x), whole(mix),
                  pl.BlockSpec((None, D_MIX, D_MODEL), lambda j: (layer, 0, 0)), whole(fng)],
        out_specs=pl.BlockSpec((n, D_MODEL), lambda j: (0, 0)),
        out_shape=jax.ShapeDtypeStruct((n, D_MODEL), F32),
        compiler_params=_params(dimension_semantics=("arbitrary",)),
        name="sample_out",
    )(x, mix, wout, fng)


def _sample_mix_stages(layer, n_tok, n_seq,
                       cd_ref, z_ref, s0_ref, cbuf_ref, mkt_ref, mvt_ref, cos_ref, sin_ref,
                       gn_ref, cw_ref, cb_ref, lng_ref, lnb_ref, xan_ref,
                       dec_ref, wq_ref, wk_ref, mix_ref, sret_ref, sconv_ref):
    grp = SUBLANES // n_tok
    rows = grp * n_tok
    keep = CONV_HIST - n_tok
    cosf = cos_ref[...]
    sins = sin_ref[...]
    head = _head_lane_ids()
    row_id = lax.broadcasted_iota(jnp.int32, (rows, 1), 0)
    row_seq = row_id // n_tok
    row_tok = row_id - row_seq * n_tok
    row4_seq = lax.broadcasted_iota(jnp.int32, (XA_HEADS * rows, 1), 0) % rows // n_tok

    def pick(parts, seq_of_row):
        out = parts[0]
        for s in range(1, grp):
            out = jnp.where(seq_of_row == s, parts[s], out)
        return out

    groups = range(n_seq // grp)
    heads = range(RET_HEADS)
    seqs = [[g * grp + s for s in range(grp)] for g in groups]
    rs = [slice(rows * g, rows * (g + 1)) for g in groups]
    hcol = lambda c0, hh: slice(c0 + RET_DK * hh, c0 + RET_DK * (hh + 1))

    qh, kf, vh, sc, xsc = {}, {}, {}, {}, {}
    inner, cross, xo, xsum = {}, {}, {}, {}

    def first_matmuls():
        for g in groups:
            for hh in heads:
                qh[g, hh] = _rope(z_ref[rs[g], hcol(C_RQ, hh)], cosf, sins).astype(BF16)
                kf[g, hh] = _rope(z_ref[rs[g], hcol(C_RK, hh)], cosf, sins) * (RET_DK ** -0.5)
                vh[g, hh] = z_ref[rs[g], hcol(C_RV, hh)].astype(BF16)
                sc[g, hh] = _dot_nt(qh[g, hh], kf[g, hh].astype(BF16))
            xq = z_ref[rs[g], C_XQ:C_XQ + XA_W] * (LOG2_E * XA_DH ** -0.5)
            q4 = jnp.concatenate([jnp.where(head == hh, xq, 0.0) for hh in range(XA_HEADS)],
                                 axis=0).astype(BF16)
            xsc[g] = [_dot(q4, mkt_ref[0, b].astype(BF16)) for b in seqs[g]]

    def conv_module():
        for g in groups:
            z_a = z_ref[rs[g], C_CA:C_CA + CONV_W]
            u = z_a * _sigmoid(z_ref[rs[g], C_CB:C_CB + CONV_W])
            c = u * cw_ref[CONV_HIST:CONV_K, :] + _layer_row(cb_ref, layer)
            for lag in range(1, n_tok):
                tap = cw_ref[CONV_HIST - lag:CONV_K - lag, :]
                c = c + jnp.where(row_tok >= lag, pltpu.roll(u, lag, 0), 0.0) * tap
            for s, b in enumerate(seqs[g]):
                for i in range(n_tok):
                    window = cbuf_ref[0, b, i:CONV_HIST, :] * cw_ref[0:CONV_HIST - i, :]
                    hi = jnp.sum(window, axis=0, keepdims=True)
                    c = c + jnp.where(row_id == s * n_tok + i, hi, 0.0)
                sconv_ref[0, b, 0:keep, :] = cbuf_ref[0, b, n_tok:CONV_HIST, :]
                sconv_ref[0, b, keep:CONV_HIST, :] = u[s * n_tok:(s + 1) * n_tok, :]
            cn = _standardize(c) * _layer_row(lng_ref, layer) + _layer_row(lnb_ref, layer)
            gate = _silu(z_ref[rs[g], C_CG:C_CG + CONV_W])
            mix_ref[rs[g], RET_W:RET_W + CONV_W] = (_silu(cn) * gate).astype(BF16)

    def second_matmuls():
        for g in groups:
            for hh in heads:
                inner[g, hh] = _dot((sc[g, hh] * dec_ref[hh]).astype(BF16), vh[g, hh])
                kw = kf[g, hh] * wk_ref[hh]
                parts = []
                for s, b in enumerate(seqs[g]):
                    s_prev = s0_ref[0, b, hh]
                    parts.append(_dot(qh[g, hh], s_prev.astype(BF16)))
                    kv = _dot_tn(jnp.where(row_seq == s, kw, 0.0).astype(BF16), vh[g, hh])
                    sret_ref[0, b, hh] = cd_ref[hh] * s_prev + kv
                cross[g, hh] = pick(parts, row_seq) * wq_ref[hh]
            s4 = pick(xsc[g], row4_seq)
            e = jnp.exp2(s4 - jnp.max(s4, axis=-1, keepdims=True))
            xsum[g] = jnp.sum(e, axis=-1, keepdims=True)
            eb = e.astype(BF16)
            xo[g] = [_dot_nt(eb, mvt_ref[0, b].astype(BF16)) for b in seqs[g]]

    def norms_and_stores():
        for g in groups:
            for hh in heads:
                co = hcol(0, hh)
                o = _standardize(inner[g, hh] + cross[g, hh]) * gn_ref[layer:layer + 1, co]
                gate = _silu(z_ref[rs[g], hcol(C_RG, hh)])
                mix_ref[rs[g], co] = (o * gate).astype(BF16)
            o4 = pick(xo[g], row4_seq) / xsum[g]
            a = jnp.zeros((rows, XA_W), F32)
            for hh in range(XA_HEADS):
                a = jnp.where(head == hh, o4[rows * hh:rows * (hh + 1), :], a)
            gate = _silu(z_ref[rs[g], C_XG:C_XG + XA_W])
            mix_ref[rs[g], RET_W + CONV_W:D_MIX] = _xattn_norm_gate(
                a, _layer_row(xan_ref, layer), gate, rows).astype(BF16)

    return [first_matmuls, conv_module, second_matmuls, norms_and_stores]


def _rope_tables(pos):
    half = RET_DK // 2
    inv = np.float64(ROPE_BASE) ** (-np.arange(half, dtype=np.float64) / half)
    ang = pos.astype(np.float64)[:, None] * inv[None, :]
    cos, sin = np.cos(ang), np.sin(ang)
    return (np.concatenate([cos, cos], axis=-1).astype(np.float32),
            np.concatenate([-sin, sin], axis=-1).astype(np.float32))


def _decay_tables(chunk):
    lg = np.log(1.0 - np.exp2(-5.0 - np.arange(RET_HEADS, dtype=np.float64)))
    idx = np.arange(chunk, dtype=np.float64)
    diff = idx[:, None] - idx[None, :]
    dec = np.where(diff[None] >= 0, np.exp(np.maximum(diff, 0.0)[None] * lg[:, None, None]), 0.0)
    wk = np.exp((chunk - 1.0 - idx)[None, :] * lg[:, None])
    wq = np.exp((idx + 1.0)[None, :] * lg[:, None])
    cd = np.exp(chunk * lg)
    wk = np.broadcast_to(wk[:, :, None], (RET_HEADS, chunk, RET_DK))
    wq = np.broadcast_to(wq[:, :, None], (RET_HEADS, chunk, RET_DV))
    f32 = lambda a: np.ascontiguousarray(a, dtype=np.float32)
    return f32(dec), f32(wq), f32(wk), f32(cd)


def _group_tables(n_tok, pos0):
    grp = SUBLANES // n_tok
    cosf, sins = _rope_tables(pos0 + np.arange(n_tok))
    dec, wq, wk, cd = _decay_tables(n_tok)
    tile_rows = lambda a: np.concatenate([a] * grp, axis=-2)
    eye = np.eye(grp, dtype=np.float32)
    dec = np.einsum("st,hij->hsitj", eye, dec).reshape(RET_HEADS, grp * n_tok, grp * n_tok)
    return tile_rows(cosf), tile_rows(sins), dec, tile_rows(wq), tile_rows(wk), cd


def kernel(x_prompt, x_sample, mem_prompt, state_ret, state_conv, cache_mem_k, cache_mem_v,
           norm_g, w_in, ret_gn_g, conv_w, conv_b, conv_ln_g, conv_ln_b, xa_norm_g,
           mem_norm_g, w_mk, w_mv, w_out, final_norm_g):
    depth = w_in.shape[0]
    batch, seq, _ = x_prompt.shape
    dbatch, dseq, _ = x_sample.shape

    w_in_b = w_in.astype(BF16)
    w_out_b = w_out.astype(BF16)
    w_mkt_b = w_mk.swapaxes(1, 2).astype(BF16)
    w_mvt_b = w_mv.swapaxes(1, 2).astype(BF16)
    fng = final_norm_g.reshape(1, D_MODEL)

    p_mk, p_mv = _mem_kv(mem_prompt, mem_norm_g, w_mkt_b, w_mvt_b)
    cos_p, sin_p = _rope_tables(np.arange(seq))
    dec_p, wq_p, wk_p, cd_p = _decay_tables(RET_CHUNK if seq % RET_CHUNK == 0 else seq)

    assert SUBLANES % dseq == 0 and dseq % RET_CHUNK != 0
    cos_s, sin_s, dec_s, wq_s, wk_s, cd_s = _group_tables(dseq, PAST_LEN)
    to_hd_m = lambda c: c.transpose(0, 1, 3, 4, 2).reshape(depth, dbatch, XA_W, N_MEM)
    mkt_s, mvt_s = to_hd_m(cache_mem_k), to_hd_m(cache_mem_v)
    hs = x_sample.reshape(dbatch * dseq, D_MODEL)
    z = _sample_in(0, hs, norm_g, w_in_b)

    hp = x_prompt
    states = ()
    y_sample = None
    for l in range(depth):
        hp, p_ret, p_conv, mix, s_ret, s_conv = _layer(
            l, l == depth - 1, dseq, hp, cos_p, sin_p, p_mk, p_mv, norm_g, w_in_b, ret_gn_g,
            conv_w, conv_b, conv_ln_g, conv_ln_b, xa_norm_g, w_out_b, dec_p, wq_p, wk_p,
            cd_p, fng, z, state_ret, state_conv, mkt_s, mvt_s, cos_s, sin_s, dec_s, wq_s,
            wk_s, cd_s, states)
        states = (p_ret, p_conv, s_ret, s_conv)
        if l + 1 < depth:
            hs, z = _sample_mid(l, hs, mix, w_out_b, norm_g, w_in_b)
        else:
            y_sample = _sample_out(l, hs, mix, w_out_b, fng).reshape(dbatch, dseq, D_MODEL)
    y_prompt = hp

    from_hd_m = lambda c: c.reshape(depth, batch, XA_HEADS, XA_DH, N_MEM).transpose(0, 1, 4, 2, 3)
    return (y_prompt, y_sample, states[0], states[1],
            from_hd_m(p_mk), from_hd_m(p_mv), states[2], states[3])
```

```python
import functools

import numpy as np

import jax
import jax.numpy as jnp
from jax import lax
from jax.experimental import pallas as pl
from jax.experimental.pallas import tpu as pltpu

F32 = jnp.float32
BF16 = jnp.bfloat16

D_MODEL = 1024
N_MEM = 256
RET_HEADS = 4
RET_DK = 128
RET_DV = 128
RET_W = RET_HEADS * RET_DV
RET_CHUNK = 128
CONV_W = 256
CONV_K = 31
XA_HEADS = 4
XA_DH = 64
XA_W = XA_HEADS * XA_DH
D_MIX = RET_W + CONV_W + XA_W
ROPE_BASE = 10000.0
EPS = 1e-6
PAST_LEN = 16384
LOG2_E = 1.4426950408889634

C_RQ, C_RK, C_RV, C_RG = 0, 512, 1024, 1536
C_CA, C_CB, C_CG = 2048, 2304, 2560
C_XQ, C_XG = 2816, 3072
D_IN = 3328

VMEM_LIMIT_BYTES = 56 * 1024 * 1024
SUBLANES = 8
LANES = 128
CONV_PAD = 32
CONV_HIST = CONV_K - 1
PROMPT_TILE = 512
CONV_ROWS = 64
PROJ_BLOCK = 256
RET_STAGE_CHUNKS = 2
SAMPLE_PROJ_STEPS = 2


def _rms(x, g):
    return x * lax.rsqrt(jnp.mean(x * x, axis=-1, keepdims=True) + EPS) * g


def _standardize(x):
    mu = jnp.mean(x, axis=-1, keepdims=True)
    d = x - mu
    var = jnp.mean(d * d, axis=-1, keepdims=True)
    return d * lax.rsqrt(var + EPS)


def _sigmoid(x):
    return 1.0 / (1.0 + jnp.exp(-x))


def _silu(x):
    return x * _sigmoid(x)


def _dot(a, b):
    return jnp.dot(a, b, preferred_element_type=F32)


def _dot_nt(a, b):
    return lax.dot_general(a, b, (((1,), (1,)), ((), ())), preferred_element_type=F32)


def _dot_tn(a, b):
    return lax.dot_general(a, b, (((0,), (0,)), ((), ())), preferred_element_type=F32)


def _rope(x, cosf, sins):
    return x * cosf + pltpu.roll(x, RET_DK // 2, 1) * sins


def _zero_row_after(z):
    tile = z[z.shape[0] - SUBLANES:, z.shape[1] - LANES:]
    bits = pltpu.bitcast(tile, jnp.uint32)
    sixteen = jnp.uint32(16)
    zero = lax.shift_right_logical(lax.shift_right_logical(bits, sixteen), sixteen)
    row = pltpu.bitcast(zero, F32)[0:1, :]
    return jnp.concatenate([row, row], axis=1)


def _head_lane_ids():
    return lax.broadcasted_iota(jnp.int32, (1, XA_W), 1) // XA_DH


def _xattn_norm_gate(a, xan, gate, rows, denoms=None):
    head = _head_lane_ids()
    a2 = a * a
    ms = jnp.zeros((rows, XA_W), F32)
    for hh in range(XA_HEADS):
        m = head == hh
        ssq = jnp.sum(jnp.where(m, a2, 0.0), axis=-1, keepdims=True) * (1.0 / XA_DH)
        ssq = ssq + (EPS if denoms is None else EPS * denoms[hh] * denoms[hh])
        ms = jnp.where(m, ssq, ms)
    return a * lax.rsqrt(ms) * xan * gate


def _layer_row(ref, layer):
    return ref[layer:layer + 1, :]


def _params(**kw):
    return pltpu.CompilerParams(vmem_limit_bytes=VMEM_LIMIT_BYTES, **kw)


def _mem_kv_kernel(mem_ref, g_ref, wkt_ref, wvt_ref, kt_ref, vt_ref):
    layer = pl.program_id(0)
    gain = g_ref[pl.ds(layer, 1), :]
    for b in range(mem_ref.shape[0]):
        m = _rms(mem_ref[b], gain).astype(BF16)
        kt_ref[0, b] = _dot_nt(wkt_ref[...], m)
        vt_ref[0, b] = _dot_nt(wvt_ref[...], m)


def _mem_kv(mem, g, wkt, wvt):
    depth, batch = g.shape[0], mem.shape[0]
    out = jax.ShapeDtypeStruct((depth, batch, XA_W, N_MEM), F32)
    return pl.pallas_call(
        _mem_kv_kernel,
        grid=(depth,),
        in_specs=[
            pl.BlockSpec((batch, N_MEM, D_MODEL), lambda l: (0, 0, 0)),
            pl.BlockSpec((depth, D_MODEL), lambda l: (0, 0)),
            pl.BlockSpec((None, XA_W, D_MODEL), lambda l: (l, 0, 0)),
            pl.BlockSpec((None, XA_W, D_MODEL), lambda l: (l, 0, 0)),
        ],
        out_specs=[
            pl.BlockSpec((1, batch, XA_W, N_MEM), lambda l: (l, 0, 0, 0)),
            pl.BlockSpec((1, batch, XA_W, N_MEM), lambda l: (l, 0, 0, 0)),
        ],
        out_shape=[out, out],
        compiler_params=_params(dimension_semantics=("arbitrary",)),
        name="mem_kv",
    )(mem, g, wkt, wvt)


def _layer_kernel(layer, final, n_tiles, n_alias, n_tok, n_seq,
                  cd_ref, x_ref, cos_ref, sin_ref, mk_ref, mv_ref, ng_ref, win_ref,
                  gn_ref, cw_ref, cb_ref, lng_ref, lnb_ref, xan_ref, wout_ref,
                  dec_ref, wq_ref, wk_ref, fng_ref,
                  scd_ref, sz_ref, ss0_ref, scbuf_ref, smkt_ref, smvt_ref, scos_ref, ssin_ref,
                  sdec_ref, swq_ref, swk_ref, *refs):
    (y_ref, sret_ref, sconv_ref, smix_ref, ssret_ref, ssconv_ref,
     s_scr, ext_scr, c_scr, q_scr, qw_scr, k_scr, kw_scr, v_scr, g_scr, xg_scr,
     mix_scr) = refs[n_alias:]

    sample_stages = _sample_mix_stages(
        layer, n_tok, n_seq, scd_ref, sz_ref, ss0_ref, scbuf_ref, smkt_ref, smvt_ref,
        scos_ref, ssin_ref, gn_ref, cw_ref, cb_ref, lng_ref, lnb_ref, xan_ref, sdec_ref,
        swq_ref, swk_ref, smix_ref, ssret_ref, ssconv_ref)

    tm = PROMPT_TILE
    t = pl.program_id(1)

    @pl.when(t == 0)
    def _():
        s_scr[...] = jnp.zeros_like(s_scr)
        ext_scr[0:CONV_PAD, :] = jnp.zeros((CONV_PAD, CONV_W), F32)

    sample_stages[0]()
    sample_stages[1]()

    hb = _rms(x_ref[0], _layer_row(ng_ref, layer)).astype(BF16)

    def proj(a, b):
        return _dot(hb, win_ref[:, a:b])

    cosf = cos_ref[...]
    sins = sin_ref[...]

    u = proj(C_CA, C_CA + CONV_W) * _sigmoid(proj(C_CB, C_CB + CONV_W))
    ext_scr[CONV_PAD:CONV_PAD + tm, :] = u

    pb = PROJ_BLOCK

    def post_q(z, c):
        for i in range(pb // RET_DK):
            hh = c // RET_DK + i
            cols = slice(RET_DK * hh, RET_DK * (hh + 1))
            q = _rope(z[:, RET_DK * i:RET_DK * (i + 1)], cosf, sins)
            q_scr[:, cols] = q.astype(BF16)
            for ch in range(tm // RET_CHUNK):
                rows = slice(RET_CHUNK * ch, RET_CHUNK * (ch + 1))
                qw_scr[rows, cols] = (q[rows, :] * wq_ref[hh]).astype(BF16)

    def post_k(z, c):
        for i in range(pb // RET_DK):
            hh = c // RET_DK + i
            cols = slice(RET_DK * hh, RET_DK * (hh + 1))
            k = _rope(z[:, RET_DK * i:RET_DK * (i + 1)], cosf, sins) * (RET_DK ** -0.5)
            k_scr[:, cols] = k.astype(BF16)
            for ch in range(tm // RET_CHUNK):
                rows = slice(RET_CHUNK * ch, RET_CHUNK * (ch + 1))
                kw_scr[rows, cols] = (k[rows, :] * wk_ref[hh]).astype(BF16)

    def post_v(z, c):
        v_scr[:, c:c + pb] = z.astype(BF16)

    def post_g(z, c):
        g_scr[:, c:c + pb] = _silu(z)

    def post_cg(z, c):
        xg_scr[:, c:c + pb] = _silu(z)

    def post_xq(z, c):
        xg_scr[:, CONV_W + c:CONV_W + c + pb] = z

    def post_xg(z, c):
        xg_scr[:, CONV_W + XA_W + c:CONV_W + XA_W + c + pb] = _silu(z)

    blocks = []
    for col0, width, post in ((C_RQ, RET_W, post_q), (C_RK, RET_W, post_k),
                              (C_RV, RET_W, post_v), (C_RG, RET_W, post_g),
                              (C_CG, CONV_W, post_cg), (C_XQ, XA_W, post_xq),
                              (C_XG, XA_W, post_xg)):
        blocks += [(col0, c, post) for c in range(0, width, pb)]

    first = CONV_PAD - CONV_HIST
    conv_acc = {}
    conv_bias = _layer_row(cb_ref, layer)

    def tap_group(c0, r, wait_zero):
        n_rows = CONV_ROWS if r == 0 else CONV_ROWS + SUBLANES
        part = None
        for a in range((first + CONV_K - 1 - r) // SUBLANES + 1):
            j = SUBLANES * a + r - first
            if 0 <= j < CONV_K:
                w_j = cw_ref[j:j + 1, :]
                if wait_zero is not None:
                    w_j = w_j + wait_zero
                lo = c0 + SUBLANES * a
                term = ext_scr[lo:lo + n_rows, :] * w_j
                part = term if part is None else part + term
        if r == 0:
            conv_acc[c0] = part + conv_bias
        else:
            conv_acc[c0] = conv_acc[c0] + part[r:r + CONV_ROWS, :]
        if r == SUBLANES - 1:
            c_scr[c0:c0 + CONV_ROWS, :] = conv_acc.pop(c0)

    def conv_tail(lo, wait_zero):
        slab = slice(lo, lo + CONV_ROWS)
        gain = _layer_row(lng_ref, layer)
        if wait_zero is not None:
            gain = gain + wait_zero
        cn = _standardize(c_scr[slab, :]) * gain + _layer_row(lnb_ref, layer)
        mix_scr[slab, RET_W:RET_W + CONV_W] = (_silu(cn) * xg_scr[slab, 0:CONV_W]).astype(BF16)

    taps = [functools.partial(tap_group, c0, r)
            for c0 in range(0, tm, CONV_ROWS) for r in range(SUBLANES)]
    tails = [functools.partial(conv_tail, lo) for lo in range(0, tm, CONV_ROWS)]

    chunks = range(tm // RET_CHUNK)
    heads = range(RET_HEADS)
    crow = lambda c: slice(RET_CHUNK * c, RET_CHUNK * (c + 1))
    hcol = lambda hh: slice(RET_DK * hh, RET_DK * (hh + 1))

    def retention_stages(c0):
        part = chunks[c0:c0 + RET_STAGE_CHUNKS]
        sc, kv, before, out = {}, {}, {}, {}

        def scores_and_updates():
            for c in part:
                for hh in heads:
                    sc[c, hh] = _dot_nt(q_scr[crow(c), hcol(hh)], k_scr[crow(c), hcol(hh)])
                    kv[c, hh] = _dot_tn(kw_scr[crow(c), hcol(hh)], v_scr[crow(c), hcol(hh)])

        def recurrence_and_outputs():
            for hh in heads:
                state = s_scr[hh]
                for c in part:
                    before[c, hh] = state.astype(BF16)
                    state = cd_ref[hh] * state + kv[c, hh]
                s_scr[hh] = state
            for c in part:
                for hh in heads:
                    lhs = jnp.concatenate([(sc[c, hh] * dec_ref[hh]).astype(BF16),
                                           qw_scr[crow(c), hcol(hh)]], axis=1)
                    rhs = jnp.concatenate([v_scr[crow(c), hcol(hh)], before[c, hh]], axis=0)
                    out[c, hh] = _dot(lhs, rhs)

        def norms():
            for c in part:
                for hh in heads:
                    o = _standardize(out[c, hh]) * gn_ref[layer:layer + 1, hcol(hh)]
                    mix_scr[crow(c), hcol(hh)] = (o * g_scr[crow(c), hcol(hh)]).astype(BF16)

        return [scores_and_updates, recurrence_and_outputs, norms]

    n_half = RET_CHUNK * RET_STAGE_CHUNKS
    mkt = mk_ref[0, 0].astype(BF16)
    mvt = mv_ref[0, 0].astype(BF16)
    head = _head_lane_ids()

    def xattn_stages(half):
        scores, pvs, sums = [], [], []

        def query_key():
            q = xg_scr[half, CONV_W:CONV_W + XA_W] * (LOG2_E * XA_DH ** -0.5)
            for hh in range(XA_HEADS):
                scores.append(_dot(jnp.where(head == hh, q, 0.0).astype(BF16), mkt))

        def softmax_value():
            for sc in scores:
                e = jnp.exp2(sc - jnp.max(sc, axis=-1, keepdims=True))
                sums.append(jnp.sum(e, axis=-1, keepdims=True))
                pvs.append(_dot_nt(e.astype(BF16), mvt))

        def norm_gate():
            a = pvs[0]
            for hh in range(1, XA_HEADS):
                a = jnp.where(head == hh, pvs[hh], a)
            gate = xg_scr[half, CONV_W + XA_W:CONV_W + 2 * XA_W]
            mix_scr[half, RET_W + CONV_W:D_MIX] = _xattn_norm_gate(
                a, _layer_row(xan_ref, layer), gate, n_half, denoms=sums).astype(BF16)

        return [query_key, softmax_value, norm_gate]

    halves = [slice(n_half * p, n_half * (p + 1)) for p in range(tm // n_half)]
    pending = [stage for p in range(len(halves))
               for stage in retention_stages(RET_STAGE_CHUNKS * p)]
    gate_block = [post for _, _, post in blocks].index(post_cg)
    late_blocks = len(blocks) - gate_block
    taps_per = -(-len(taps) // gate_block)
    tails_per = -(-len(tails) // (late_blocks - 1))
    stages_per = len(pending) // late_blocks
    z_prev = None
    for k, (col0, c, post) in enumerate(blocks):
        if k < gate_block:
            items = taps[taps_per * k:taps_per * (k + 1)]
        elif k == gate_block:
            items = []
        else:
            j = k - gate_block - 1
            items = tails[tails_per * j:tails_per * (j + 1)]
        for n, item in enumerate(items):
            slab_end = tm * (n + 1) // len(items) // SUBLANES * SUBLANES
            item(None if z_prev is None else _zero_row_after(z_prev[0:slab_end, :]))
        if k >= gate_block:
            for stage in pending[:stages_per]:
                stage()
            pending = pending[stages_per:]
        z_prev = proj(col0 + c, col0 + c + pb)
        post(z_prev, c)

    sample_stages[2]()
    for stage in pending:
        stage()
    for p, half in enumerate(halves):
        for stage in xattn_stages(half):
            stage()
        y = x_ref[0, half, :] + _dot(mix_scr[half, :], wout_ref[...])
        if final:
            y = _rms(y, fng_ref[...])
        y_ref[0, half, :] = y
        if p == 0:
            sample_stages[3]()

    @pl.when(t == n_tiles - 1)
    def _():
        sconv_ref[0, 0] = ext_scr[tm + first:tm + CONV_PAD, :]
        sret_ref[0, 0] = s_scr[...]

    ext_scr[0:CONV_PAD, :] = ext_scr[tm:tm + CONV_PAD, :]


def _layer(layer, final, n_tok, x, cosf, sins, mk, mv, ng, win, gn, cw, cb, lng, lnb, xan,
           wout, dec, wq, wk, cd, fng,
           z_s, s0, cbuf, mkt_s, mvt_s, cos_s, sin_s, dec_s, wq_s, wk_s, cd_s, prev_states):
    batch, seq, _ = x.shape
    depth, sbatch = s0.shape[0], s0.shape[1]
    tm = PROMPT_TILE
    n_tiles = seq // tm
    n_seq = sbatch // (batch * n_tiles)
    assert n_seq * batch * n_tiles == sbatch and (n_seq * n_tok) % (2 * SUBLANES) == 0
    whole = lambda a: pl.BlockSpec(a.shape, lambda b, t: (0,) * a.ndim)
    step = lambda b, t: b * n_tiles + t
    sample_specs = [
        pl.BlockSpec(memory_space=pltpu.SMEM),
        pl.BlockSpec((n_seq * n_tok, D_IN), lambda b, t: (step(b, t), 0)),
        pl.BlockSpec((1, n_seq, RET_HEADS, RET_DK, RET_DV),
                     lambda b, t: (layer, step(b, t), 0, 0, 0)),
        pl.BlockSpec((1, n_seq, CONV_HIST, CONV_W), lambda b, t: (layer, step(b, t), 0, 0)),
        pl.BlockSpec((1, n_seq, XA_W, N_MEM), lambda b, t: (layer, step(b, t), 0, 0)),
        pl.BlockSpec((1, n_seq, XA_W, N_MEM), lambda b, t: (layer, step(b, t), 0, 0)),
        whole(cos_s), whole(sin_s), whole(dec_s), whole(wq_s), whole(wk_s),
    ]
    in_specs = [
        pl.BlockSpec(memory_space=pltpu.SMEM),
        pl.BlockSpec((1, tm, D_MODEL), lambda b, t: (b, t, 0)),
        pl.BlockSpec((tm, RET_DK), lambda b, t: (t, 0)),
        pl.BlockSpec((tm, RET_DK), lambda b, t: (t, 0)),
        pl.BlockSpec((1, 1, XA_W, N_MEM), lambda b, t: (layer, b, 0, 0)),
        pl.BlockSpec((1, 1, XA_W, N_MEM), lambda b, t: (layer, b, 0, 0)),
        whole(ng),
        whole(win),
        whole(gn),
        pl.BlockSpec((None, CONV_K, CONV_W), lambda b, t: (layer, 0, 0)),
        whole(cb), whole(lng), whole(lnb),
        whole(xan),
        whole(wout),
        whole(dec), whole(wq), whole(wk),
        whole(fng),
    ] + sample_specs
    n_fixed = len(in_specs)
    n_alias = len(prev_states)
    in_specs += [pl.BlockSpec(memory_space=pl.ANY)] * n_alias
    out_specs = [
        pl.BlockSpec((1, tm, D_MODEL), lambda b, t: (b, t, 0)),
        pl.BlockSpec((1, 1, RET_HEADS, RET_DK, RET_DV), lambda b, t: (layer, b, 0, 0, 0)),
        pl.BlockSpec((1, 1, CONV_HIST, CONV_W), lambda b, t: (layer, b, 0, 0)),
        pl.BlockSpec((n_seq * n_tok, D_MIX), lambda b, t: (step(b, t), 0)),
        pl.BlockSpec((1, n_seq, RET_HEADS, RET_DK, RET_DV),
                     lambda b, t: (layer, step(b, t), 0, 0, 0)),
        pl.BlockSpec((1, n_seq, CONV_HIST, CONV_W), lambda b, t: (layer, step(b, t), 0, 0)),
    ]
    out_shape = [
        jax.ShapeDtypeStruct((batch, seq, D_MODEL), F32),
        jax.ShapeDtypeStruct((depth, batch, RET_HEADS, RET_DK, RET_DV), F32),
        jax.ShapeDtypeStruct((depth, batch, CONV_HIST, CONV_W), F32),
        jax.ShapeDtypeStruct((sbatch * n_tok, D_MIX), BF16),
        jax.ShapeDtypeStruct((depth, sbatch, RET_HEADS, RET_DK, RET_DV), F32),
        jax.ShapeDtypeStruct((depth, sbatch, CONV_HIST, CONV_W), F32),
    ]
    state_outputs = (1, 2, 4, 5)
    scratch = [
        pltpu.VMEM((RET_HEADS, RET_DK, RET_DV), F32),
        pltpu.VMEM((CONV_PAD + tm, CONV_W), F32),
        pltpu.VMEM((tm, CONV_W), F32),
        pltpu.VMEM((tm, RET_W), BF16),
        pltpu.VMEM((tm, RET_W), BF16),
        pltpu.VMEM((tm, RET_W), BF16),
        pltpu.VMEM((tm, RET_W), BF16),
        pltpu.VMEM((tm, RET_W), BF16),
        pltpu.VMEM((tm, RET_W), F32),
        pltpu.VMEM((tm, CONV_W + 2 * XA_W), F32),
        pltpu.VMEM((tm, D_MIX), BF16),
    ]
    return pl.pallas_call(
        functools.partial(_layer_kernel, layer, final, n_tiles, n_alias, n_tok, n_seq),
        grid=(batch, n_tiles),
        in_specs=in_specs,
        out_specs=out_specs,
        out_shape=out_shape,
        scratch_shapes=scratch,
        input_output_aliases={n_fixed + k: state_outputs[k] for k in range(n_alias)},
        compiler_params=_params(dimension_semantics=("arbitrary", "arbitrary")),
        name=f"layer{layer}",
    )(cd, x, cosf, sins, mk, mv, ng, win, gn, cw, cb, lng, lnb, xan, wout, dec, wq, wk, fng,
      cd_s, z_s, s0, cbuf, mkt_s, mvt_s, cos_s, sin_s, dec_s, wq_s, wk_s, *prev_states)


def _project_and_cast(first_step, hb_scr, win_ref, wout_ref, z_ref, winb_ref, woutb_ref):
    @pl.when(first_step)
    def _():
        woutb_ref[...] = wout_ref[...].astype(BF16)

    w_block = win_ref[...].astype(BF16)
    winb_ref[...] = w_block
    z_ref[...] = _dot(hb_scr[...], w_block)


def _sample_in_kernel(layer, x_ref, ng_ref, win_ref, wout_ref, z_ref, winb_ref, woutb_ref,
                      hb_scr):
    first_step = pl.program_id(0) == 0

    @pl.when(first_step)
    def _():
        hb_scr[...] = _rms(x_ref[...], _layer_row(ng_ref, layer)).astype(BF16)

    _project_and_cast(first_step, hb_scr, win_ref, wout_ref, z_ref, winb_ref, woutb_ref)


def _weight_cast_specs(layer, cols):
    in_specs = [pl.BlockSpec((None, D_MODEL, cols), lambda j: (layer, 0, j)),
                pl.BlockSpec((None, D_MIX, D_MODEL), lambda j: (layer, 0, 0))]
    out_specs = [pl.BlockSpec((D_MODEL, cols), lambda j: (0, j)),
                 pl.BlockSpec((D_MIX, D_MODEL), lambda j: (0, 0))]
    out_shape = [jax.ShapeDtypeStruct((D_MODEL, D_IN), BF16),
                 jax.ShapeDtypeStruct((D_MIX, D_MODEL), BF16)]
    return in_specs, out_specs, out_shape


def _sample_in(layer, x, ng, win, wout):
    n = x.shape[0]
    cols = D_IN // SAMPLE_PROJ_STEPS
    whole = lambda a: pl.BlockSpec(a.shape, lambda j: (0,) * a.ndim)
    w_in_specs, w_out_specs, w_out_shape = _weight_cast_specs(layer, cols)
    return pl.pallas_call(
        functools.partial(_sample_in_kernel, layer),
        grid=(SAMPLE_PROJ_STEPS,),
        in_specs=[whole(x), whole(ng)] + w_in_specs,
        out_specs=[pl.BlockSpec((n, cols), lambda j: (0, j))] + w_out_specs,
        out_shape=[jax.ShapeDtypeStruct((n, D_IN), F32)] + w_out_shape,
        scratch_shapes=[pltpu.VMEM((n, D_MODEL), BF16)],
        compiler_params=_params(dimension_semantics=("arbitrary",)),
        name="sample_in",
    )(x, ng, win, wout)


def _sample_mid_kernel(layer, x_ref, mix_ref, woutb_prev_ref, ng_ref, win_ref, wout_ref,
                       h_ref, z_ref, winb_ref, woutb_ref, hb_scr):
    first_step = pl.program_id(0) == 0

    @pl.when(first_step)
    def _():
        h = x_ref[...] + _dot(mix_ref[...], woutb_prev_ref[...])
        h_ref[...] = h
        hb_scr[...] = _rms(h, _layer_row(ng_ref, layer + 1)).astype(BF16)

    _project_and_cast(first_step, hb_scr, win_ref, wout_ref, z_ref, winb_ref, woutb_ref)


def _sample_mid(layer, x, mix, woutb_prev, ng, win, wout):
    n = x.shape[0]
    cols = D_IN // SAMPLE_PROJ_STEPS
    whole = lambda a: pl.BlockSpec(a.shape, lambda j: (0,) * a.ndim)
    w_in_specs, w_out_specs, w_out_shape = _weight_cast_specs(layer + 1, cols)
    return pl.pallas_call(
        functools.partial(_sample_mid_kernel, layer),
        grid=(SAMPLE_PROJ_STEPS,),
        in_specs=[whole(x), whole(mix), whole(woutb_prev), whole(ng)] + w_in_specs,
        out_specs=[pl.BlockSpec((n, D_MODEL), lambda j: (0, 0)),
                   pl.BlockSpec((n, cols), lambda j: (0, j))] + w_out_specs,
        out_shape=[jax.ShapeDtypeStruct((n, D_MODEL), F32),
                   jax.ShapeDtypeStruct((n, D_IN), F32)] + w_out_shape,
        scratch_shapes=[pltpu.VMEM((n, D_MODEL), BF16)],
        compiler_params=_params(dimension_semantics=("arbitrary",)),
        name="sample_mid",
    )(x, mix, woutb_prev, ng, win, wout)


def _sample_out_kernel(x_ref, mix_ref, wout_ref, fng_ref, y_ref):
    y_ref[...] = _rms(x_ref[...] + _dot(mix_ref[...], wout_ref[...]), fng_ref[...])


def _sample_out(x, mix, wout, fng):
    n = x.shape[0]
    whole = lambda a: pl.BlockSpec(a.shape, lambda j: (0,) * a.ndim)
    return pl.pallas_call(
        _sample_out_kernel,
        grid=(1,),
        in_specs=[whole(x), whole(mix), whole(wout), whole(fng)],
        out_specs=pl.BlockSpec((n, D_MODEL), lambda j: (0, 0)),
        out_shape=jax.ShapeDtypeStruct((n, D_MODEL), F32),
        compiler_params=_params(dimension_semantics=("arbitrary",)),
        name="sample_out",
    )(x, mix, wout, fng)


def _sample_mix_stages(layer, n_tok, n_seq,
                       cd_ref, z_ref, s0_ref, cbuf_ref, mkt_ref, mvt_ref, cos_ref, sin_ref,
                       gn_ref, cw_ref, cb_ref, lng_ref, lnb_ref, xan_ref,
                       dec_ref, wq_ref, wk_ref, mix_ref, sret_ref, sconv_ref):
    grp = SUBLANES // n_tok
    rows = grp * n_tok
    keep = CONV_HIST - n_tok
    cosf = cos_ref[...]
    sins = sin_ref[...]
    head = _head_lane_ids()
    row_id = lax.broadcasted_iota(jnp.int32, (rows, 1), 0)
    row_seq = row_id // n_tok
    row_tok = row_id - row_seq * n_tok
    row4_seq = lax.broadcasted_iota(jnp.int32, (XA_HEADS * rows, 1), 0) % rows // n_tok

    def pick(parts, seq_of_row):
        out = parts[0]
        for s in range(1, grp):
            out = jnp.where(seq_of_row == s, parts[s], out)
        return out

    groups = range(n_seq // grp)
    heads = range(RET_HEADS)
    seqs = [[g * grp + s for s in range(grp)] for g in groups]
    rs = [slice(rows * g, rows * (g + 1)) for g in groups]
    hcol = lambda c0, hh: slice(c0 + RET_DK * hh, c0 + RET_DK * (hh + 1))

    qh, kf, vh, sc, xsc = {}, {}, {}, {}, {}
    inner, cross, xo, xsum = {}, {}, {}, {}

    def first_matmuls():
        for g in groups:
            for hh in heads:
                qh[g, hh] = _rope(z_ref[rs[g], hcol(C_RQ, hh)], cosf, sins).astype(BF16)
                kf[g, hh] = _rope(z_ref[rs[g], hcol(C_RK, hh)], cosf, sins) * (RET_DK ** -0.5)
                vh[g, hh] = z_ref[rs[g], hcol(C_RV, hh)].astype(BF16)
                sc[g, hh] = _dot_nt(qh[g, hh], kf[g, hh].astype(BF16))
            xq = z_ref[rs[g], C_XQ:C_XQ + XA_W] * (LOG2_E * XA_DH ** -0.5)
            q4 = jnp.concatenate([jnp.where(head == hh, xq, 0.0) for hh in range(XA_HEADS)],
                                 axis=0).astype(BF16)
            xsc[g] = [_dot(q4, mkt_ref[0, b].astype(BF16)) for b in seqs[g]]

    def conv_module():
        for g in groups:
            z_a = z_ref[rs[g], C_CA:C_CA + CONV_W]
            u = z_a * _sigmoid(z_ref[rs[g], C_CB:C_CB + CONV_W])
            c = u * cw_ref[CONV_HIST:CONV_K, :] + _layer_row(cb_ref, layer)
            for lag in range(1, n_tok):
                tap = cw_ref[CONV_HIST - lag:CONV_K - lag, :]
                c = c + jnp.where(row_tok >= lag, pltpu.roll(u, lag, 0), 0.0) * tap
            for s, b in enumerate(seqs[g]):
                for i in range(n_tok):
                    window = cbuf_ref[0, b, i:CONV_HIST, :] * cw_ref[0:CONV_HIST - i, :]
                    hi = jnp.sum(window, axis=0, keepdims=True)
                    c = c + jnp.where(row_id == s * n_tok + i, hi, 0.0)
                sconv_ref[0, b, 0:keep, :] = cbuf_ref[0, b, n_tok:CONV_HIST, :]
                sconv_ref[0, b, keep:CONV_HIST, :] = u[s * n_tok:(s + 1) * n_tok, :]
            cn = _standardize(c) * _layer_row(lng_ref, layer) + _layer_row(lnb_ref, layer)
            gate = _silu(z_ref[rs[g], C_CG:C_CG + CONV_W])
            mix_ref[rs[g], RET_W:RET_W + CONV_W] = (_silu(cn) * gate).astype(BF16)

    def second_matmuls():
        for g in groups:
            for hh in heads:
                inner[g, hh] = _dot((sc[g, hh] * dec_ref[hh]).astype(BF16), vh[g, hh])
                kw = kf[g, hh] * wk_ref[hh]
                parts = []
                for s, b in enumerate(seqs[g]):
                    s_prev = s0_ref[0, b, hh]
                    parts.append(_dot(qh[g, hh], s_prev.astype(BF16)))
                    kv = _dot_tn(jnp.where(row_seq == s, kw, 0.0).astype(BF16), vh[g, hh])
                    sret_ref[0, b, hh] = cd_ref[hh] * s_prev + kv
                cross[g, hh] = pick(parts, row_seq) * wq_ref[hh]
            s4 = pick(xsc[g], row4_seq)
            e = jnp.exp2(s4 - jnp.max(s4, axis=-1, keepdims=True))
            xsum[g] = jnp.sum(e, axis=-1, keepdims=True)
            eb = e.astype(BF16)
            xo[g] = [_dot_nt(eb, mvt_ref[0, b].astype(BF16)) for b in seqs[g]]

    def norms_and_stores():
        for g in groups:
            for hh in heads:
                co = hcol(0, hh)
                o = _standardize(inner[g, hh] + cross[g, hh]) * gn_ref[layer:layer + 1, co]
                gate = _silu(z_ref[rs[g], hcol(C_RG, hh)])
                mix_ref[rs[g], co] = (o * gate).astype(BF16)
            o4 = pick(xo[g], row4_seq) / xsum[g]
            a = jnp.zeros((rows, XA_W), F32)
            for hh in range(XA_HEADS):
                a = jnp.where(head == hh, o4[rows * hh:rows * (hh + 1), :], a)
            gate = _silu(z_ref[rs[g], C_XG:C_XG + XA_W])
            mix_ref[rs[g], RET_W + CONV_W:D_MIX] = _xattn_norm_gate(
                a, _layer_row(xan_ref, layer), gate, rows).astype(BF16)

    return [first_matmuls, conv_module, second_matmuls, norms_and_stores]


def _rope_tables(pos):
    half = RET_DK // 2
    inv = np.float64(ROPE_BASE) ** (-np.arange(half, dtype=np.float64) / half)
    ang = pos.astype(np.float64)[:, None] * inv[None, :]
    cos, sin = np.cos(ang), np.sin(ang)
    return (np.concatenate([cos, cos], axis=-1).astype(np.float32),
            np.concatenate([-sin, sin], axis=-1).astype(np.float32))


def _decay_tables(chunk):
    lg = np.log(1.0 - np.exp2(-5.0 - np.arange(RET_HEADS, dtype=np.float64)))
    idx = np.arange(chunk, dtype=np.float64)
    diff = idx[:, None] - idx[None, :]
    dec = np.where(diff[None] >= 0, np.exp(np.maximum(diff, 0.0)[None] * lg[:, None, None]), 0.0)
    wk = np.exp((chunk - 1.0 - idx)[None, :] * lg[:, None])
    wq = np.exp((idx + 1.0)[None, :] * lg[:, None])
    cd = np.exp(chunk * lg)
    wk = np.broadcast_to(wk[:, :, None], (RET_HEADS, chunk, RET_DK))
    wq = np.broadcast_to(wq[:, :, None], (RET_HEADS, chunk, RET_DV))
    f32 = lambda a: np.ascontiguousarray(a, dtype=np.float32)
    return f32(dec), f32(wq), f32(wk), f32(cd)


def _group_tables(n_tok, pos0):
    grp = SUBLANES // n_tok
    cosf, sins = _rope_tables(pos0 + np.arange(n_tok))
    dec, wq, wk, cd = _decay_tables(n_tok)
    tile_rows = lambda a: np.concatenate([a] * grp, axis=-2)
    eye = np.eye(grp, dtype=np.float32)
    dec = np.einsum("st,hij->hsitj", eye, dec).reshape(RET_HEADS, grp * n_tok, grp * n_tok)
    return tile_rows(cosf), tile_rows(sins), dec, tile_rows(wq), tile_rows(wk), cd


def kernel(x_prompt, x_sample, mem_prompt, state_ret, state_conv, cache_mem_k, cache_mem_v,
           norm_g, w_in, ret_gn_g, conv_w, conv_b, conv_ln_g, conv_ln_b, xa_norm_g,
           mem_norm_g, w_mk, w_mv, w_out, final_norm_g):
    depth = w_in.shape[0]
    batch, seq, _ = x_prompt.shape
    dbatch, dseq, _ = x_sample.shape

    w_mkt_b = w_mk.swapaxes(1, 2).astype(BF16)
    w_mvt_b = w_mv.swapaxes(1, 2).astype(BF16)
    fng = final_norm_g.reshape(1, D_MODEL)

    p_mk, p_mv = _mem_kv(mem_prompt, mem_norm_g, w_mkt_b, w_mvt_b)
    cos_p, sin_p = _rope_tables(np.arange(seq))
    dec_p, wq_p, wk_p, cd_p = _decay_tables(RET_CHUNK if seq % RET_CHUNK == 0 else seq)

    assert SUBLANES % dseq == 0 and dseq % RET_CHUNK != 0
    cos_s, sin_s, dec_s, wq_s, wk_s, cd_s = _group_tables(dseq, PAST_LEN)
    to_hd_m = lambda c: c.transpose(0, 1, 3, 4, 2).reshape(depth, dbatch, XA_W, N_MEM)
    mkt_s, mvt_s = to_hd_m(cache_mem_k), to_hd_m(cache_mem_v)
    hs = x_sample.reshape(dbatch * dseq, D_MODEL)
    z, w_in_b, w_out_b = _sample_in(0, hs, norm_g, w_in, w_out)

    hp = x_prompt
    states = ()
    y_sample = None
    for l in range(depth):
        hp, p_ret, p_conv, mix, s_ret, s_conv = _layer(
            l, l == depth - 1, dseq, hp, cos_p, sin_p, p_mk, p_mv, norm_g, w_in_b, ret_gn_g,
            conv_w, conv_b, conv_ln_g, conv_ln_b, xa_norm_g, w_out_b, dec_p, wq_p, wk_p,
            cd_p, fng, z, state_ret, state_conv, mkt_s, mvt_s, cos_s, sin_s, dec_s, wq_s,
            wk_s, cd_s, states)
        states = (p_ret, p_conv, s_ret, s_conv)
        if l + 1 < depth:
            hs, z, w_in_b, w_out_b = _sample_mid(l, hs, mix, w_out_b, norm_g, w_in, w_out)
        else:
            y_sample = _sample_out(hs, mix, w_out_b, fng).reshape(dbatch, dseq, D_MODEL)
    y_prompt = hp

    from_hd_m = lambda c: c.reshape(depth, batch, XA_HEADS, XA_DH, N_MEM).transpose(0, 1, 4, 2, 3)
    return (y_prompt, y_sample, states[0], states[1],
            from_hd_m(p_mk), from_hd_m(p_mv), states[2], states[3])
```

```python
import functools

import numpy as np

import jax
import jax.numpy as jnp
from jax import lax
from jax.experimental import pallas as pl
from jax.experimental.pallas import tpu as pltpu

F32 = jnp.float32
BF16 = jnp.bfloat16

D_MODEL = 1024
N_MEM = 256
RET_HEADS = 4
RET_DK = 128
RET_DV = 128
RET_W = RET_HEADS * RET_DV
RET_CHUNK = 128
CONV_W = 256
CONV_K = 31
XA_HEADS = 4
XA_DH = 64
XA_W = XA_HEADS * XA_DH
D_MIX = RET_W + CONV_W + XA_W
ROPE_BASE = 10000.0
EPS = 1e-6
PAST_LEN = 16384
LOG2_E = 1.4426950408889634

C_RQ, C_RK, C_RV, C_RG = 0, 512, 1024, 1536
C_CA, C_CB, C_CG = 2048, 2304, 2560
C_XQ, C_XG = 2816, 3072
D_IN = 3328

VMEM_LIMIT_BYTES = 56 * 1024 * 1024
SUBLANES = 8
LANES = 128
CONV_PAD = 32
CONV_HIST = CONV_K - 1
PROMPT_TILE = 512
CONV_ROWS = 64
PROJ_BLOCK = 256
RET_STAGE_CHUNKS = 2
SAMPLE_PROJ_STEPS = 2


def _rms(x, g):
    return x * lax.rsqrt(jnp.mean(x * x, axis=-1, keepdims=True) + EPS) * g


def _standardize(x):
    mu = jnp.mean(x, axis=-1, keepdims=True)
    d = x - mu
    var = jnp.mean(d * d, axis=-1, keepdims=True)
    return d * lax.rsqrt(var + EPS)


def _sigmoid(x):
    return 1.0 / (1.0 + jnp.exp(-x))


def _silu(x):
    return x * _sigmoid(x)


def _dot(a, b):
    return jnp.dot(a, b, preferred_element_type=F32)


def _dot_nt(a, b):
    return lax.dot_general(a, b, (((1,), (1,)), ((), ())), preferred_element_type=F32)


def _dot_tn(a, b):
    return lax.dot_general(a, b, (((0,), (0,)), ((), ())), preferred_element_type=F32)


def _rope(x, cosf, sins):
    return x * cosf + pltpu.roll(x, RET_DK // 2, 1) * sins


def _zero_row_after(z):
    tile = z[z.shape[0] - SUBLANES:, z.shape[1] - LANES:]
    bits = pltpu.bitcast(tile, jnp.uint32)
    sixteen = jnp.uint32(16)
    zero = lax.shift_right_logical(lax.shift_right_logical(bits, sixteen), sixteen)
    row = pltpu.bitcast(zero, F32)[0:1, :]
    return jnp.concatenate([row, row], axis=1)


def _head_lane_ids():
    return lax.broadcasted_iota(jnp.int32, (1, XA_W), 1) // XA_DH


def _xattn_norm_gate(a, xan, gate, rows, denoms=None):
    head = _head_lane_ids()
    a2 = a * a
    ms = jnp.zeros((rows, XA_W), F32)
    for hh in range(XA_HEADS):
        m = head == hh
        ssq = jnp.sum(jnp.where(m, a2, 0.0), axis=-1, keepdims=True) * (1.0 / XA_DH)
        ssq = ssq + (EPS if denoms is None else EPS * denoms[hh] * denoms[hh])
        ms = jnp.where(m, ssq, ms)
    return a * lax.rsqrt(ms) * xan * gate


def _layer_row(ref, layer):
    return ref[layer:layer + 1, :]


def _params(**kw):
    return pltpu.CompilerParams(vmem_limit_bytes=VMEM_LIMIT_BYTES, **kw)


def _mem_kv_kernel(mem_ref, g_ref, wk_ref, wv_ref, kt_ref, vt_ref):
    layer = pl.program_id(0)
    gain = g_ref[pl.ds(layer, 1), :]
    wkt = wk_ref[...].T.astype(BF16)
    wvt = wv_ref[...].T.astype(BF16)
    for b in range(mem_ref.shape[0]):
        m = _rms(mem_ref[b], gain).astype(BF16)
        kt_ref[0, b] = _dot_nt(wkt, m)
        vt_ref[0, b] = _dot_nt(wvt, m)


def _mem_kv(mem, g, wk, wv):
    depth, batch = g.shape[0], mem.shape[0]
    out = jax.ShapeDtypeStruct((depth, batch, XA_W, N_MEM), F32)
    return pl.pallas_call(
        _mem_kv_kernel,
        grid=(depth,),
        in_specs=[
            pl.BlockSpec((batch, N_MEM, D_MODEL), lambda l: (0, 0, 0)),
            pl.BlockSpec((depth, D_MODEL), lambda l: (0, 0)),
            pl.BlockSpec((None, D_MODEL, XA_W), lambda l: (l, 0, 0)),
            pl.BlockSpec((None, D_MODEL, XA_W), lambda l: (l, 0, 0)),
        ],
        out_specs=[
            pl.BlockSpec((1, batch, XA_W, N_MEM), lambda l: (l, 0, 0, 0)),
            pl.BlockSpec((1, batch, XA_W, N_MEM), lambda l: (l, 0, 0, 0)),
        ],
        out_shape=[out, out],
        compiler_params=_params(dimension_semantics=("arbitrary",)),
        name="mem_kv",
    )(mem, g, wk, wv)


def _layer_kernel(layer, final, n_tiles, n_alias, n_tok, n_seq,
                  cd_ref, x_ref, cos_ref, sin_ref, mk_ref, mv_ref, ng_ref, win_ref,
                  gn_ref, cw_ref, cb_ref, lng_ref, lnb_ref, xan_ref, wout_ref,
                  dec_ref, wq_ref, wk_ref, fng_ref,
                  scd_ref, sz_ref, szc_ref, ss0_ref, scbuf_ref, smkt_ref, smvt_ref, scos_ref,
                  ssin_ref, sdec_ref, swq_ref, swk_ref, *refs):
    (y_ref, sret_ref, sconv_ref, smix_ref, ssret_ref, ssconv_ref,
     s_scr, ext_scr, c_scr, q_scr, qw_scr, k_scr, kw_scr, v_scr, g_scr, xg_scr,
     mix_scr, us_scr, cs_scr) = refs[n_alias:]

    tm = PROMPT_TILE
    t = pl.program_id(1)
    step_in_block = (pl.program_id(0) * n_tiles + t) % (SUBLANES // n_seq)
    sample_stages = _sample_mix_stages(
        layer, n_tok, n_seq, step_in_block * (n_seq * n_tok), scd_ref, sz_ref, szc_ref,
        ss0_ref, scbuf_ref, smkt_ref, smvt_ref, scos_ref, ssin_ref, gn_ref, cw_ref, cb_ref,
        lng_ref, lnb_ref, xan_ref, sdec_ref, swq_ref, swk_ref, smix_ref, ssret_ref,
        ssconv_ref, us_scr, cs_scr)

    @pl.when(t == 0)
    def _():
        s_scr[...] = jnp.zeros_like(s_scr)
        ext_scr[0:CONV_PAD, :] = jnp.zeros((CONV_PAD, CONV_W), F32)

    sample_stages[0]()
    sample_stages[1]()

    hb = _rms(x_ref[0], _layer_row(ng_ref, layer)).astype(BF16)

    def proj(a, b):
        return _dot(hb, win_ref[:, a:b])

    cosf = cos_ref[...]
    sins = sin_ref[...]

    u = proj(C_CA, C_CA + CONV_W) * _sigmoid(proj(C_CB, C_CB + CONV_W))
    ext_scr[CONV_PAD:CONV_PAD + tm, :] = u

    pb = PROJ_BLOCK

    def post_q(z, c):
        for i in range(pb // RET_DK):
            hh = c // RET_DK + i
            cols = slice(RET_DK * hh, RET_DK * (hh + 1))
            q = _rope(z[:, RET_DK * i:RET_DK * (i + 1)], cosf, sins)
            q_scr[:, cols] = q.astype(BF16)
            for ch in range(tm // RET_CHUNK):
                rows = slice(RET_CHUNK * ch, RET_CHUNK * (ch + 1))
                qw_scr[rows, cols] = (q[rows, :] * wq_ref[hh]).astype(BF16)

    def post_k(z, c):
        for i in range(pb // RET_DK):
            hh = c // RET_DK + i
            cols = slice(RET_DK * hh, RET_DK * (hh + 1))
            k = _rope(z[:, RET_DK * i:RET_DK * (i + 1)], cosf, sins) * (RET_DK ** -0.5)
            k_scr[:, cols] = k.astype(BF16)
            for ch in range(tm // RET_CHUNK):
                rows = slice(RET_CHUNK * ch, RET_CHUNK * (ch + 1))
                kw_scr[rows, cols] = (k[rows, :] * wk_ref[hh]).astype(BF16)

    def post_v(z, c):
        v_scr[:, c:c + pb] = z.astype(BF16)

    def post_g(z, c):
        g_scr[:, c:c + pb] = _silu(z)

    def post_cg(z, c):
        xg_scr[:, c:c + pb] = _silu(z)

    def post_xq(z, c):
        xg_scr[:, CONV_W + c:CONV_W + c + pb] = z

    def post_xg(z, c):
        xg_scr[:, CONV_W + XA_W + c:CONV_W + XA_W + c + pb] = _silu(z)

    blocks = []
    for col0, width, post in ((C_RQ, RET_W, post_q), (C_RK, RET_W, post_k),
                              (C_RV, RET_W, post_v), (C_RG, RET_W, post_g),
                              (C_CG, CONV_W, post_cg), (C_XQ, XA_W, post_xq),
                              (C_XG, XA_W, post_xg)):
        blocks += [(col0, c, post) for c in range(0, width, pb)]

    first = CONV_PAD - CONV_HIST
    conv_acc = {}
    conv_bias = _layer_row(cb_ref, layer)

    def tap_group(c0, r, wait_zero):
        n_rows = CONV_ROWS if r == 0 else CONV_ROWS + SUBLANES
        part = None
        for a in range((first + CONV_K - 1 - r) // SUBLANES + 1):
            j = SUBLANES * a + r - first
            if 0 <= j < CONV_K:
                w_j = cw_ref[j:j + 1, :]
                if wait_zero is not None:
                    w_j = w_j + wait_zero
                lo = c0 + SUBLANES * a
                term = ext_scr[lo:lo + n_rows, :] * w_j
                part = term if part is None else part + term
        if r == 0:
            conv_acc[c0] = part + conv_bias
        else:
            conv_acc[c0] = conv_acc[c0] + part[r:r + CONV_ROWS, :]
        if r == SUBLANES - 1:
            c_scr[c0:c0 + CONV_ROWS, :] = conv_acc.pop(c0)

    def conv_tail(lo, wait_zero):
        slab = slice(lo, lo + CONV_ROWS)
        gain = _layer_row(lng_ref, layer)
        if wait_zero is not None:
            gain = gain + wait_zero
        cn = _standardize(c_scr[slab, :]) * gain + _layer_row(lnb_ref, layer)
        mix_scr[slab, RET_W:RET_W + CONV_W] = (_silu(cn) * xg_scr[slab, 0:CONV_W]).astype(BF16)

    taps = [functools.partial(tap_group, c0, r)
            for c0 in range(0, tm, CONV_ROWS) for r in range(SUBLANES)]
    tails = [functools.partial(conv_tail, lo) for lo in range(0, tm, CONV_ROWS)]

    chunks = range(tm // RET_CHUNK)
    heads = range(RET_HEADS)
    crow = lambda c: slice(RET_CHUNK * c, RET_CHUNK * (c + 1))
    hcol = lambda hh: slice(RET_DK * hh, RET_DK * (hh + 1))

    def retention_stages(c0):
        part = chunks[c0:c0 + RET_STAGE_CHUNKS]
        sc, kv, before, out = {}, {}, {}, {}

        def scores_and_updates():
            for c in part:
                for hh in heads:
                    sc[c, hh] = _dot_nt(q_scr[crow(c), hcol(hh)], k_scr[crow(c), hcol(hh)])
                    kv[c, hh] = _dot_tn(kw_scr[crow(c), hcol(hh)], v_scr[crow(c), hcol(hh)])

        def recurrence_and_outputs():
            for hh in heads:
                state = s_scr[hh]
                for c in part:
                    before[c, hh] = state.astype(BF16)
                    state = cd_ref[hh] * state + kv[c, hh]
                s_scr[hh] = state
            for c in part:
                for hh in heads:
                    lhs = jnp.concatenate([(sc[c, hh] * dec_ref[hh]).astype(BF16),
                                           qw_scr[crow(c), hcol(hh)]], axis=1)
                    rhs = jnp.concatenate([v_scr[crow(c), hcol(hh)], before[c, hh]], axis=0)
                    out[c, hh] = _dot(lhs, rhs)

        def norms():
            for c in part:
                for hh in heads:
                    o = _standardize(out[c, hh]) * gn_ref[layer:layer + 1, hcol(hh)]
                    mix_scr[crow(c), hcol(hh)] = (o * g_scr[crow(c), hcol(hh)]).astype(BF16)

        return [scores_and_updates, recurrence_and_outputs, norms]

    n_half = RET_CHUNK * RET_STAGE_CHUNKS
    mkt = mk_ref[0, 0].astype(BF16)
    mvt = mv_ref[0, 0].astype(BF16)
    head = _head_lane_ids()

    def xattn_stages(half):
        scores, pvs, sums = [], [], []

        def query_key():
            q = xg_scr[half, CONV_W:CONV_W + XA_W] * (LOG2_E * XA_DH ** -0.5)
            for hh in range(XA_HEADS):
                scores.append(_dot(jnp.where(head == hh, q, 0.0).astype(BF16), mkt))

        def softmax_value():
            for sc in scores:
                e = jnp.exp2(sc - jnp.max(sc, axis=-1, keepdims=True))
                sums.append(jnp.sum(e, axis=-1, keepdims=True))
                pvs.append(_dot_nt(e.astype(BF16), mvt))

        def norm_gate():
            a = pvs[0]
            for hh in range(1, XA_HEADS):
                a = jnp.where(head == hh, pvs[hh], a)
            gate = xg_scr[half, CONV_W + XA_W:CONV_W + 2 * XA_W]
            mix_scr[half, RET_W + CONV_W:D_MIX] = _xattn_norm_gate(
                a, _layer_row(xan_ref, layer), gate, n_half, denoms=sums).astype(BF16)

        return [query_key, softmax_value, norm_gate]

    halves = [slice(n_half * p, n_half * (p + 1)) for p in range(tm // n_half)]
    pending = [stage for p in range(len(halves))
               for stage in retention_stages(RET_STAGE_CHUNKS * p)]
    gate_block = [post for _, _, post in blocks].index(post_cg)
    late_blocks = len(blocks) - gate_block
    taps_per = -(-len(taps) // gate_block)
    tails_per = -(-len(tails) // (late_blocks - 1))
    stages_per = len(pending) // late_blocks
    z_prev = None
    for k, (col0, c, post) in enumerate(blocks):
        if k < gate_block:
            items = taps[taps_per * k:taps_per * (k + 1)]
        elif k == gate_block:
            items = []
        else:
            j = k - gate_block - 1
            items = tails[tails_per * j:tails_per * (j + 1)]
        for n, item in enumerate(items):
            slab_end = tm * (n + 1) // len(items) // SUBLANES * SUBLANES
            item(None if z_prev is None else _zero_row_after(z_prev[0:slab_end, :]))
        if k >= gate_block:
            for stage in pending[:stages_per]:
                stage()
            pending = pending[stages_per:]
        z_prev = proj(col0 + c, col0 + c + pb)
        post(z_prev, c)

    sample_stages[2]()
    for stage in pending:
        stage()
    for p, half in enumerate(halves):
        for stage in xattn_stages(half):
            stage()
        y = x_ref[0, half, :] + _dot(mix_scr[half, :], wout_ref[...])
        if final:
            y = _rms(y, fng_ref[...])
        y_ref[0, half, :] = y
        if p == 0:
            sample_stages[3]()

    @pl.when(t == n_tiles - 1)
    def _():
        sconv_ref[0, 0] = ext_scr[tm + first:tm + CONV_PAD, :]
        sret_ref[0, 0] = s_scr[...]

    ext_scr[0:CONV_PAD, :] = ext_scr[tm:tm + CONV_PAD, :]


def _layer(layer, final, n_tok, x, cosf, sins, mk, mv, ng, win, gn, cw, cb, lng, lnb, xan,
           wout, dec, wq, wk, cd, fng,
           z_s, s0, cbuf, mkt_s, mvt_s, cos_s, sin_s, dec_s, wq_s, wk_s, cd_s, prev_states):
    batch, seq, _ = x.shape
    depth, sbatch = s0.shape[0], s0.shape[1]
    tm = PROMPT_TILE
    n_tiles = seq // tm
    n_seq = sbatch // (batch * n_tiles)
    assert n_seq * batch * n_tiles == sbatch and (n_seq * n_tok) % (2 * SUBLANES) == 0
    assert SUBLANES % n_seq == 0 and C_CA % (2 * CONV_W) == 0 and C_CB == C_CA + CONV_W
    whole = lambda a: pl.BlockSpec(a.shape, lambda b, t: (0,) * a.ndim)
    step = lambda b, t: b * n_tiles + t
    conv_block = lambda b, t: step(b, t) // (SUBLANES // n_seq)
    conv_spec = pl.BlockSpec((None, CONV_HIST, SUBLANES, CONV_W),
                             lambda b, t: (layer, 0, conv_block(b, t), 0))
    sample_specs = [
        pl.BlockSpec(memory_space=pltpu.SMEM),
        pl.BlockSpec((n_seq * n_tok, D_IN), lambda b, t: (step(b, t), 0)),
        pl.BlockSpec((SUBLANES * n_tok, 2 * CONV_W),
                     lambda b, t: (conv_block(b, t), C_CA // (2 * CONV_W))),
        pl.BlockSpec((1, n_seq, RET_HEADS, RET_DK, RET_DV),
                     lambda b, t: (layer, step(b, t), 0, 0, 0)),
        conv_spec,
        pl.BlockSpec((1, n_seq, XA_W, N_MEM), lambda b, t: (layer, step(b, t), 0, 0)),
        pl.BlockSpec((1, n_seq, XA_W, N_MEM), lambda b, t: (layer, step(b, t), 0, 0)),
        whole(cos_s), whole(sin_s), whole(dec_s), whole(wq_s), whole(wk_s),
    ]
    in_specs = [
        pl.BlockSpec(memory_space=pltpu.SMEM),
        pl.BlockSpec((1, tm, D_MODEL), lambda b, t: (b, t, 0)),
        pl.BlockSpec((tm, RET_DK), lambda b, t: (t, 0)),
        pl.BlockSpec((tm, RET_DK), lambda b, t: (t, 0)),
        pl.BlockSpec((1, 1, XA_W, N_MEM), lambda b, t: (layer, b, 0, 0)),
        pl.BlockSpec((1, 1, XA_W, N_MEM), lambda b, t: (layer, b, 0, 0)),
        whole(ng),
        whole(win),
        whole(gn),
        pl.BlockSpec((None, CONV_K, CONV_W), lambda b, t: (layer, 0, 0)),
        whole(cb), whole(lng), whole(lnb),
        whole(xan),
        whole(wout),
        whole(dec), whole(wq), whole(wk),
        whole(fng),
    ] + sample_specs
    n_fixed = len(in_specs)
    n_alias = len(prev_states)
    in_specs += [pl.BlockSpec(memory_space=pl.ANY)] * n_alias
    out_specs = [
        pl.BlockSpec((1, tm, D_MODEL), lambda b, t: (b, t, 0)),
        pl.BlockSpec((1, 1, RET_HEADS, RET_DK, RET_DV), lambda b, t: (layer, b, 0, 0, 0)),
        pl.BlockSpec((1, 1, CONV_HIST, CONV_W), lambda b, t: (layer, b, 0, 0)),
        pl.BlockSpec((n_seq * n_tok, D_MIX), lambda b, t: (step(b, t), 0)),
        pl.BlockSpec((1, n_seq, RET_HEADS, RET_DK, RET_DV),
                     lambda b, t: (layer, step(b, t), 0, 0, 0)),
        conv_spec,
    ]
    out_shape = [
        jax.ShapeDtypeStruct((batch, seq, D_MODEL), F32),
        jax.ShapeDtypeStruct((depth, batch, RET_HEADS, RET_DK, RET_DV), F32),
        jax.ShapeDtypeStruct((depth, batch, CONV_HIST, CONV_W), F32),
        jax.ShapeDtypeStruct((sbatch * n_tok, D_MIX), BF16),
        jax.ShapeDtypeStruct((depth, sbatch, RET_HEADS, RET_DK, RET_DV), F32),
        jax.ShapeDtypeStruct((depth, CONV_HIST, sbatch, CONV_W), F32),
    ]
    state_outputs = (1, 2, 4, 5)
    scratch = [
        pltpu.VMEM((RET_HEADS, RET_DK, RET_DV), F32),
        pltpu.VMEM((CONV_PAD + tm, CONV_W), F32),
        pltpu.VMEM((tm, CONV_W), F32),
        pltpu.VMEM((tm, RET_W), BF16),
        pltpu.VMEM((tm, RET_W), BF16),
        pltpu.VMEM((tm, RET_W), BF16),
        pltpu.VMEM((tm, RET_W), BF16),
        pltpu.VMEM((tm, RET_W), BF16),
        pltpu.VMEM((tm, RET_W), F32),
        pltpu.VMEM((tm, CONV_W + 2 * XA_W), F32),
        pltpu.VMEM((tm, D_MIX), BF16),
        pltpu.VMEM((CONV_W // LANES, SUBLANES * n_tok, LANES), F32),
        pltpu.VMEM((CONV_W // LANES, SUBLANES * n_tok, LANES), F32),
    ]
    return pl.pallas_call(
        functools.partial(_layer_kernel, layer, final, n_tiles, n_alias, n_tok, n_seq),
        grid=(batch, n_tiles),
        in_specs=in_specs,
        out_specs=out_specs,
        out_shape=out_shape,
        scratch_shapes=scratch,
        input_output_aliases={n_fixed + k: state_outputs[k] for k in range(n_alias)},
        compiler_params=_params(dimension_semantics=("arbitrary", "arbitrary")),
        name=f"layer{layer}",
    )(cd, x, cosf, sins, mk, mv, ng, win, gn, cw, cb, lng, lnb, xan, wout, dec, wq, wk, fng,
      cd_s, z_s, z_s, s0, cbuf, mkt_s, mvt_s, cos_s, sin_s, dec_s, wq_s, wk_s, *prev_states)


def _project_and_cast(first_step, hb_scr, win_ref, wout_ref, z_ref, winb_ref, woutb_ref):
    @pl.when(first_step)
    def _():
        woutb_ref[...] = wout_ref[...].astype(BF16)

    w_block = win_ref[...].astype(BF16)
    winb_ref[...] = w_block
    z_ref[...] = _dot(hb_scr[...], w_block)


def _sample_in_kernel(layer, x_ref, ng_ref, win_ref, wout_ref, z_ref, winb_ref, woutb_ref,
                      hb_scr):
    first_step = pl.program_id(0) == 0

    @pl.when(first_step)
    def _():
        hb_scr[...] = _rms(x_ref[...], _layer_row(ng_ref, layer)).astype(BF16)

    _project_and_cast(first_step, hb_scr, win_ref, wout_ref, z_ref, winb_ref, woutb_ref)


def _weight_cast_specs(layer, cols):
    in_specs = [pl.BlockSpec((None, D_MODEL, cols), lambda j: (layer, 0, j)),
                pl.BlockSpec((None, D_MIX, D_MODEL), lambda j: (layer, 0, 0))]
    out_specs = [pl.BlockSpec((D_MODEL, cols), lambda j: (0, j)),
                 pl.BlockSpec((D_MIX, D_MODEL), lambda j: (0, 0))]
    out_shape = [jax.ShapeDtypeStruct((D_MODEL, D_IN), BF16),
                 jax.ShapeDtypeStruct((D_MIX, D_MODEL), BF16)]
    return in_specs, out_specs, out_shape


def _sample_in(layer, x, ng, win, wout):
    n = x.shape[0]
    cols = D_IN // SAMPLE_PROJ_STEPS
    whole = lambda a: pl.BlockSpec(a.shape, lambda j: (0,) * a.ndim)
    w_in_specs, w_out_specs, w_out_shape = _weight_cast_specs(layer, cols)
    return pl.pallas_call(
        functools.partial(_sample_in_kernel, layer),
        grid=(SAMPLE_PROJ_STEPS,),
        in_specs=[whole(x), whole(ng)] + w_in_specs,
        out_specs=[pl.BlockSpec((n, cols), lambda j: (0, j))] + w_out_specs,
        out_shape=[jax.ShapeDtypeStruct((n, D_IN), F32)] + w_out_shape,
        scratch_shapes=[pltpu.VMEM((n, D_MODEL), BF16)],
        compiler_params=_params(dimension_semantics=("arbitrary",)),
        name="sample_in",
    )(x, ng, win, wout)


def _sample_mid_kernel(layer, x_ref, mix_ref, woutb_prev_ref, ng_ref, win_ref, wout_ref,
                       h_ref, z_ref, winb_ref, woutb_ref, hb_scr):
    first_step = pl.program_id(0) == 0

    @pl.when(first_step)
    def _():
        h = x_ref[...] + _dot(mix_ref[...], woutb_prev_ref[...])
        h_ref[...] = h
        hb_scr[...] = _rms(h, _layer_row(ng_ref, layer + 1)).astype(BF16)

    _project_and_cast(first_step, hb_scr, win_ref, wout_ref, z_ref, winb_ref, woutb_ref)


def _sample_mid(layer, x, mix, woutb_prev, ng, win, wout):
    n = x.shape[0]
    cols = D_IN // SAMPLE_PROJ_STEPS
    whole = lambda a: pl.BlockSpec(a.shape, lambda j: (0,) * a.ndim)
    w_in_specs, w_out_specs, w_out_shape = _weight_cast_specs(layer + 1, cols)
    return pl.pallas_call(
        functools.partial(_sample_mid_kernel, layer),
        grid=(SAMPLE_PROJ_STEPS,),
        in_specs=[whole(x), whole(mix), whole(woutb_prev), whole(ng)] + w_in_specs,
        out_specs=[pl.BlockSpec((n, D_MODEL), lambda j: (0, 0)),
                   pl.BlockSpec((n, cols), lambda j: (0, j))] + w_out_specs,
        out_shape=[jax.ShapeDtypeStruct((n, D_MODEL), F32),
                   jax.ShapeDtypeStruct((n, D_IN), F32)] + w_out_shape,
        scratch_shapes=[pltpu.VMEM((n, D_MODEL), BF16)],
        compiler_params=_params(dimension_semantics=("arbitrary",)),
        name="sample_mid",
    )(x, mix, woutb_prev, ng, win, wout)


def _sample_out_kernel(x_ref, mix_ref, wout_ref, fng_ref, y_ref):
    y_ref[...] = _rms(x_ref[...] + _dot(mix_ref[...], wout_ref[...]), fng_ref[...])


def _sample_out(x, mix, wout, fng):
    n = x.shape[0]
    whole = lambda a: pl.BlockSpec(a.shape, lambda j: (0,) * a.ndim)
    return pl.pallas_call(
        _sample_out_kernel,
        grid=(1,),
        in_specs=[whole(x), whole(mix), whole(wout), whole(fng)],
        out_specs=pl.BlockSpec((n, D_MODEL), lambda j: (0, 0)),
        out_shape=jax.ShapeDtypeStruct((n, D_MODEL), F32),
        compiler_params=_params(dimension_semantics=("arbitrary",)),
        name="sample_out",
    )(x, mix, wout, fng)


def _sample_mix_stages(layer, n_tok, n_seq, own_row0,
                       cd_ref, z_ref, zc_ref, s0_ref, cbuf_ref, mkt_ref, mvt_ref, cos_ref,
                       sin_ref, gn_ref, cw_ref, cb_ref, lng_ref, lnb_ref, xan_ref,
                       dec_ref, wq_ref, wk_ref, mix_ref, sret_ref, sconv_ref, us_scr, cs_scr):
    grp = SUBLANES // n_tok
    rows = grp * n_tok
    cosf = cos_ref[...]
    sins = sin_ref[...]
    head = _head_lane_ids()
    row_id = lax.broadcasted_iota(jnp.int32, (rows, 1), 0)
    row_seq = row_id // n_tok
    row4_seq = lax.broadcasted_iota(jnp.int32, (XA_HEADS * rows, 1), 0) % rows // n_tok

    def pick(parts, seq_of_row):
        out = parts[0]
        for s in range(1, grp):
            out = jnp.where(seq_of_row == s, parts[s], out)
        return out

    groups = range(n_seq // grp)
    heads = range(RET_HEADS)
    seqs = [[g * grp + s for s in range(grp)] for g in groups]
    rs = [slice(rows * g, rows * (g + 1)) for g in groups]
    hcol = lambda c0, hh: slice(c0 + RET_DK * hh, c0 + RET_DK * (hh + 1))

    qh, kf, vh, sc, xsc = {}, {}, {}, {}, {}
    inner, cross, xo, xsum = {}, {}, {}, {}

    def first_matmuls():
        for g in groups:
            for hh in heads:
                qh[g, hh] = _rope(z_ref[rs[g], hcol(C_RQ, hh)], cosf, sins).astype(BF16)
                kf[g, hh] = _rope(z_ref[rs[g], hcol(C_RK, hh)], cosf, sins) * (RET_DK ** -0.5)
                vh[g, hh] = z_ref[rs[g], hcol(C_RV, hh)].astype(BF16)
                sc[g, hh] = _dot_nt(qh[g, hh], kf[g, hh].astype(BF16))
            xq = z_ref[rs[g], C_XQ:C_XQ + XA_W] * (LOG2_E * XA_DH ** -0.5)
            q4 = jnp.concatenate([jnp.where(head == hh, xq, 0.0) for hh in range(XA_HEADS)],
                                 axis=0).astype(BF16)
            xsc[g] = [_dot(q4, mkt_ref[0, b].astype(BF16)) for b in seqs[g]]

    def conv_module():
        pieces = range(CONV_W // LANES)
        lanes = lambda p: slice(LANES * p, LANES * (p + 1))
        u_rows = zc_ref[:, 0:CONV_W] * _sigmoid(zc_ref[:, CONV_W:2 * CONV_W])
        for p in pieces:
            us_scr[p] = u_rows[:, lanes(p)]
        u = []
        for i in range(n_tok):
            tok = pl.ds(i, SUBLANES, stride=n_tok)
            u.append(jnp.concatenate([us_scr[p, tok, :] for p in pieces], axis=-1))
        window = lambda k: cbuf_ref[k] if k < CONV_HIST else u[k - CONV_HIST]
        acc = [_layer_row(cb_ref, layer)] * n_tok
        for k in range(CONV_HIST + n_tok):
            w_k = window(k)
            for i in range(n_tok):
                if 0 <= k - i < CONV_K:
                    acc[i] = acc[i] + w_k * cw_ref[k - i:k - i + 1, :]
        for i in range(n_tok):
            for p in pieces:
                cs_scr[p, pl.ds(i, SUBLANES, stride=n_tok), :] = acc[i][:, lanes(p)]
        for k in range(CONV_HIST):
            sconv_ref[k] = window(k + n_tok)
        for g in groups:
            own = pl.ds(pl.multiple_of(own_row0 + rows * g, rows), rows)
            c = jnp.concatenate([cs_scr[p, own, :] for p in pieces], axis=-1)
            cn = _standardize(c) * _layer_row(lng_ref, layer) + _layer_row(lnb_ref, layer)
            gate = _silu(z_ref[rs[g], C_CG:C_CG + CONV_W])
            mix_ref[rs[g], RET_W:RET_W + CONV_W] = (_silu(cn) * gate).astype(BF16)

    def second_matmuls():
        for g in groups:
            for hh in heads:
                inner[g, hh] = _dot((sc[g, hh] * dec_ref[hh]).astype(BF16), vh[g, hh])
                kw = kf[g, hh] * wk_ref[hh]
                parts = []
                for s, b in enumerate(seqs[g]):
                    s_prev = s0_ref[0, b, hh]
                    parts.append(_dot(qh[g, hh], s_prev.astype(BF16)))
                    kv = _dot_tn(jnp.where(row_seq == s, kw, 0.0).astype(BF16), vh[g, hh])
                    sret_ref[0, b, hh] = cd_ref[hh] * s_prev + kv
                cross[g, hh] = pick(parts, row_seq) * wq_ref[hh]
            s4 = pick(xsc[g], row4_seq)
            e = jnp.exp2(s4 - jnp.max(s4, axis=-1, keepdims=True))
            xsum[g] = jnp.sum(e, axis=-1, keepdims=True)
            eb = e.astype(BF16)
            xo[g] = [_dot_nt(eb, mvt_ref[0, b].astype(BF16)) for b in seqs[g]]

    def norms_and_stores():
        for g in groups:
            for hh in heads:
                co = hcol(0, hh)
                o = _standardize(inner[g, hh] + cross[g, hh]) * gn_ref[layer:layer + 1, co]
                gate = _silu(z_ref[rs[g], hcol(C_RG, hh)])
                mix_ref[rs[g], co] = (o * gate).astype(BF16)
            o4 = pick(xo[g], row4_seq) / xsum[g]
            a = jnp.zeros((rows, XA_W), F32)
            for hh in range(XA_HEADS):
                a = jnp.where(head == hh, o4[rows * hh:rows * (hh + 1), :], a)
            gate = _silu(z_ref[rs[g], C_XG:C_XG + XA_W])
            mix_ref[rs[g], RET_W + CONV_W:D_MIX] = _xattn_norm_gate(
                a, _layer_row(xan_ref, layer), gate, rows).astype(BF16)

    return [first_matmuls, conv_module, second_matmuls, norms_and_stores]


def _rope_tables(pos):
    half = RET_DK // 2
    inv = np.float64(ROPE_BASE) ** (-np.arange(half, dtype=np.float64) / half)
    ang = pos.astype(np.float64)[:, None] * inv[None, :]
    cos, sin = np.cos(ang), np.sin(ang)
    return (np.concatenate([cos, cos], axis=-1).astype(np.float32),
            np.concatenate([-sin, sin], axis=-1).astype(np.float32))


def _decay_tables(chunk):
    lg = np.log(1.0 - np.exp2(-5.0 - np.arange(RET_HEADS, dtype=np.float64)))
    idx = np.arange(chunk, dtype=np.float64)
    diff = idx[:, None] - idx[None, :]
    dec = np.where(diff[None] >= 0, np.exp(np.maximum(diff, 0.0)[None] * lg[:, None, None]), 0.0)
    wk = np.exp((chunk - 1.0 - idx)[None, :] * lg[:, None])
    wq = np.exp((idx + 1.0)[None, :] * lg[:, None])
    cd = np.exp(chunk * lg)
    wk = np.broadcast_to(wk[:, :, None], (RET_HEADS, chunk, RET_DK))
    wq = np.broadcast_to(wq[:, :, None], (RET_HEADS, chunk, RET_DV))
    f32 = lambda a: np.ascontiguousarray(a, dtype=np.float32)
    return f32(dec), f32(wq), f32(wk), f32(cd)


def _group_tables(n_tok, pos0):
    grp = SUBLANES // n_tok
    cosf, sins = _rope_tables(pos0 + np.arange(n_tok))
    dec, wq, wk, cd = _decay_tables(n_tok)
    tile_rows = lambda a: np.concatenate([a] * grp, axis=-2)
    eye = np.eye(grp, dtype=np.float32)
    dec = np.einsum("st,hij->hsitj", eye, dec).reshape(RET_HEADS, grp * n_tok, grp * n_tok)
    return tile_rows(cosf), tile_rows(sins), dec, tile_rows(wq), tile_rows(wk), cd


def kernel(x_prompt, x_sample, mem_prompt, state_ret, state_conv, cache_mem_k, cache_mem_v,
           norm_g, w_in, ret_gn_g, conv_w, conv_b, conv_ln_g, conv_ln_b, xa_norm_g,
           mem_norm_g, w_mk, w_mv, w_out, final_norm_g):
    depth = w_in.shape[0]
    batch, seq, _ = x_prompt.shape
    dbatch, dseq, _ = x_sample.shape

    fng = final_norm_g.reshape(1, D_MODEL)

    p_mk, p_mv = _mem_kv(mem_prompt, mem_norm_g, w_mk, w_mv)
    cos_p, sin_p = _rope_tables(np.arange(seq))
    dec_p, wq_p, wk_p, cd_p = _decay_tables(RET_CHUNK if seq % RET_CHUNK == 0 else seq)

    assert SUBLANES % dseq == 0 and dseq % RET_CHUNK != 0
    cos_s, sin_s, dec_s, wq_s, wk_s, cd_s = _group_tables(dseq, PAST_LEN)
    to_hd_m = lambda c: c.transpose(0, 1, 3, 4, 2).reshape(depth, dbatch, XA_W, N_MEM)
    mkt_s, mvt_s = to_hd_m(cache_mem_k), to_hd_m(cache_mem_v)
    conv_s = state_conv.transpose(0, 2, 1, 3)
    hs = x_sample.reshape(dbatch * dseq, D_MODEL)
    z, w_in_b, w_out_b = _sample_in(0, hs, norm_g, w_in, w_out)

    hp = x_prompt
    states = ()
    y_sample = None
    for l in range(depth):
        hp, p_ret, p_conv, mix, s_ret, s_conv = _layer(
            l, l == depth - 1, dseq, hp, cos_p, sin_p, p_mk, p_mv, norm_g, w_in_b, ret_gn_g,
            conv_w, conv_b, conv_ln_g, conv_ln_b, xa_norm_g, w_out_b, dec_p, wq_p, wk_p,
            cd_p, fng, z, state_ret, conv_s, mkt_s, mvt_s, cos_s, sin_s, dec_s, wq_s,
            wk_s, cd_s, states)
        states = (p_ret, p_conv, s_ret, s_conv)
        if l + 1 < depth:
            hs, z, w_in_b, w_out_b = _sample_mid(l, hs, mix, w_out_b, norm_g, w_in, w_out)
        else:
            y_sample = _sample_out(hs, mix, w_out_b, fng).reshape(dbatch, dseq, D_MODEL)
    y_prompt = hp

    from_hd_m = lambda c: c.reshape(depth, batch, XA_HEADS, XA_DH, N_MEM).transpose(0, 1, 4, 2, 3)
    return (y_prompt, y_sample, states[0], states[1],
            from_hd_m(p_mk), from_hd_m(p_mv), states[2], states[3].transpose(0, 2, 1, 3))
```

```python
import functools

import numpy as np

import jax
import jax.numpy as jnp
from jax import lax
from jax.experimental import pallas as pl
from jax.experimental.pallas import tpu as pltpu

F32 = jnp.float32
BF16 = jnp.bfloat16

D_MODEL = 1024
N_MEM = 256
RET_HEADS = 4
RET_DK = 128
RET_DV = 128
RET_W = RET_HEADS * RET_DV
RET_CHUNK = 128
CONV_W = 256
CONV_K = 31
XA_HEADS = 4
XA_DH = 64
XA_W = XA_HEADS * XA_DH
D_MIX = RET_W + CONV_W + XA_W
ROPE_BASE = 10000.0
EPS = 1e-6
PAST_LEN = 16384
LOG2_E = 1.4426950408889634

C_RQ, C_RK, C_RV, C_RG = 0, 512, 1024, 1536
C_CA, C_CB, C_CG = 2048, 2304, 2560
C_XQ, C_XG = 2816, 3072
D_IN = 3328

VMEM_LIMIT_BYTES = 56 * 1024 * 1024
SUBLANES = 8
LANES = 128
CONV_PAD = 32
CONV_HIST = CONV_K - 1
PROMPT_TILE = 512
CONV_ROWS = 64
PROJ_BLOCK = 256
RET_STAGE_CHUNKS = 2
SAMPLE_PROJ_STEPS = 2


def _rms(x, g):
    return x * lax.rsqrt(jnp.mean(x * x, axis=-1, keepdims=True) + EPS) * g


def _standardize(x):
    mu = jnp.mean(x, axis=-1, keepdims=True)
    d = x - mu
    var = jnp.mean(d * d, axis=-1, keepdims=True)
    return d * lax.rsqrt(var + EPS)


def _sigmoid(x):
    return 1.0 / (1.0 + jnp.exp(-x))


def _silu(x):
    return x * _sigmoid(x)


def _dot(a, b):
    return jnp.dot(a, b, preferred_element_type=F32)


def _dot_nt(a, b):
    return lax.dot_general(a, b, (((1,), (1,)), ((), ())), preferred_element_type=F32)


def _dot_tn(a, b):
    return lax.dot_general(a, b, (((0,), (0,)), ((), ())), preferred_element_type=F32)


def _rope(x, cosf, sins):
    return x * cosf + pltpu.roll(x, RET_DK // 2, 1) * sins


def _zero_row_after(z):
    tile = z[z.shape[0] - SUBLANES:, z.shape[1] - LANES:]
    bits = pltpu.bitcast(tile, jnp.uint32)
    sixteen = jnp.uint32(16)
    zero = lax.shift_right_logical(lax.shift_right_logical(bits, sixteen), sixteen)
    row = pltpu.bitcast(zero, F32)[0:1, :]
    return jnp.concatenate([row, row], axis=1)


def _head_lane_ids():
    return lax.broadcasted_iota(jnp.int32, (1, XA_W), 1) // XA_DH


def _xattn_norm_gate(a, xan, gate, rows, denoms=None):
    head = _head_lane_ids()
    a2 = a * a
    ms = jnp.zeros((rows, XA_W), F32)
    for hh in range(XA_HEADS):
        m = head == hh
        ssq = jnp.sum(jnp.where(m, a2, 0.0), axis=-1, keepdims=True) * (1.0 / XA_DH)
        ssq = ssq + (EPS if denoms is None else EPS * denoms[hh] * denoms[hh])
        ms = jnp.where(m, ssq, ms)
    return a * lax.rsqrt(ms) * xan * gate


def _layer_row(ref, layer):
    return ref[layer:layer + 1, :]


def _params(**kw):
    return pltpu.CompilerParams(vmem_limit_bytes=VMEM_LIMIT_BYTES, **kw)


def _mem_kv_kernel(mem_ref, g_ref, wk_ref, wv_ref, kt_ref, vt_ref):
    layer = pl.program_id(0)
    gain = g_ref[pl.ds(layer, 1), :]
    wkt = wk_ref[...].T.astype(BF16)
    wvt = wv_ref[...].T.astype(BF16)
    for b in range(mem_ref.shape[0]):
        m = _rms(mem_ref[b], gain).astype(BF16)
        kt_ref[0, b] = _dot_nt(wkt, m)
        vt_ref[0, b] = _dot_nt(wvt, m)


def _mem_kv(mem, g, wk, wv):
    depth, batch = g.shape[0], mem.shape[0]
    out = jax.ShapeDtypeStruct((depth, batch, XA_W, N_MEM), F32)
    return pl.pallas_call(
        _mem_kv_kernel,
        grid=(depth,),
        in_specs=[
            pl.BlockSpec((batch, N_MEM, D_MODEL), lambda l: (0, 0, 0)),
            pl.BlockSpec((depth, D_MODEL), lambda l: (0, 0)),
            pl.BlockSpec((None, D_MODEL, XA_W), lambda l: (l, 0, 0)),
            pl.BlockSpec((None, D_MODEL, XA_W), lambda l: (l, 0, 0)),
        ],
        out_specs=[
            pl.BlockSpec((1, batch, XA_W, N_MEM), lambda l: (l, 0, 0, 0)),
            pl.BlockSpec((1, batch, XA_W, N_MEM), lambda l: (l, 0, 0, 0)),
        ],
        out_shape=[out, out],
        compiler_params=_params(dimension_semantics=("arbitrary",)),
        name="mem_kv",
    )(mem, g, wk, wv)


class _Blocks:
    def __init__(self, ref, row0, n, rows, lanes):
        self.ref, self.row0, self.n, self.rows, self.lanes = ref, row0, n, rows, lanes

    def __getitem__(self, i):
        if i is Ellipsis:
            return self.ref[self.row0:self.row0 + self.n * self.rows, 0:self.lanes]
        r0 = self.row0 + i * self.rows
        return self.ref[r0:r0 + self.rows, 0:self.lanes]


class _Scalars:
    def __init__(self, ref, first):
        self.ref, self.first = ref, first

    def __getitem__(self, i):
        return self.ref[self.first + i]


def _pack_tables(named):
    parts, layout, row0 = [], {}, 0
    for name, a in named:
        n, rows, lanes = a.shape
        assert rows % SUBLANES == 0 and lanes <= LANES
        wide = np.zeros((n * rows, LANES), np.float32)
        wide[:, :lanes] = a.reshape(n * rows, lanes)
        parts.append(wide)
        layout[name] = (row0, n, rows, lanes)
        row0 += n * rows
    return np.concatenate(parts), layout


def _layer_kernel(layer, final, n_tiles, n_alias, n_tok, n_seq, layout,
                  cd_all_ref, x_ref, cs_ref, mk_ref, mv_ref, ng_ref, win_ref,
                  gn_ref, cw_ref, cb_ref, lng_ref, lnb_ref, xan_ref, wout_ref,
                  tab_ref, fng_ref,
                  sz_ref, szc_ref, ss0_ref, scbuf_ref, smkt_ref, smvt_ref, *refs):
    (y_ref, sret_ref, sconv_ref, smix_ref, ssret_ref, ssconv_ref,
     s_scr, ext_scr, c_scr, q_scr, qw_scr, k_scr, kw_scr, v_scr, g_scr, xg_scr,
     mix_scr, us_scr, cs_scr) = refs[n_alias:]

    tm = PROMPT_TILE
    t = pl.program_id(1)
    table = lambda name: _Blocks(tab_ref, *layout[name])
    dec_ref, wq_ref, wk_ref = table("dec"), table("wq"), table("wk")
    cd_ref = _Scalars(cd_all_ref, 0)
    step_in_block = (pl.program_id(0) * n_tiles + t) % (SUBLANES // n_seq)
    sample_stages = _sample_mix_stages(
        layer, n_tok, n_seq, step_in_block * (n_seq * n_tok), _Scalars(cd_all_ref, RET_HEADS),
        sz_ref, szc_ref, ss0_ref, scbuf_ref, smkt_ref, smvt_ref, table("sample cos"),
        table("sample sin"), gn_ref, cw_ref, cb_ref, lng_ref, lnb_ref, xan_ref,
        table("sample dec"), table("sample wq"), table("sample wk"), smix_ref, ssret_ref,
        ssconv_ref, us_scr, cs_scr)

    @pl.when(t == 0)
    def _():
        s_scr[...] = jnp.zeros_like(s_scr)
        ext_scr[0:CONV_PAD, :] = jnp.zeros((CONV_PAD, CONV_W), F32)

    sample_stages[0]()
    sample_stages[1]()

    hb = _rms(x_ref[0], _layer_row(ng_ref, layer)).astype(BF16)

    def proj(a, b):
        return _dot(hb, win_ref[:, a:b])

    cosf = cs_ref[:, 0:RET_DK]
    sins = cs_ref[:, RET_DK:2 * RET_DK]

    u = proj(C_CA, C_CA + CONV_W) * _sigmoid(proj(C_CB, C_CB + CONV_W))
    ext_scr[CONV_PAD:CONV_PAD + tm, :] = u

    pb = PROJ_BLOCK

    def post_q(z, c):
        for i in range(pb // RET_DK):
            hh = c // RET_DK + i
            cols = slice(RET_DK * hh, RET_DK * (hh + 1))
            q = _rope(z[:, RET_DK * i:RET_DK * (i + 1)], cosf, sins)
            q_scr[:, cols] = q.astype(BF16)
            for ch in range(tm // RET_CHUNK):
                rows = slice(RET_CHUNK * ch, RET_CHUNK * (ch + 1))
                qw_scr[rows, cols] = (q[rows, :] * wq_ref[hh]).astype(BF16)

    def post_k(z, c):
        for i in range(pb // RET_DK):
            hh = c // RET_DK + i
            cols = slice(RET_DK * hh, RET_DK * (hh + 1))
            k = _rope(z[:, RET_DK * i:RET_DK * (i + 1)], cosf, sins) * (RET_DK ** -0.5)
            k_scr[:, cols] = k.astype(BF16)
            for ch in range(tm // RET_CHUNK):
                rows = slice(RET_CHUNK * ch, RET_CHUNK * (ch + 1))
                kw_scr[rows, cols] = (k[rows, :] * wk_ref[hh]).astype(BF16)

    def post_v(z, c):
        v_scr[:, c:c + pb] = z.astype(BF16)

    def post_g(z, c):
        g_scr[:, c:c + pb] = _silu(z)

    def post_cg(z, c):
        xg_scr[:, c:c + pb] = _silu(z)

    def post_xq(z, c):
        xg_scr[:, CONV_W + c:CONV_W + c + pb] = z

    def post_xg(z, c):
        xg_scr[:, CONV_W + XA_W + c:CONV_W + XA_W + c + pb] = _silu(z)

    blocks = []
    for col0, width, post in ((C_RQ, RET_W, post_q), (C_RK, RET_W, post_k),
                              (C_RV, RET_W, post_v), (C_RG, RET_W, post_g),
                              (C_CG, CONV_W, post_cg), (C_XQ, XA_W, post_xq),
                              (C_XG, XA_W, post_xg)):
        blocks += [(col0, c, post) for c in range(0, width, pb)]

    first = CONV_PAD - CONV_HIST
    conv_acc = {}
    conv_bias = _layer_row(cb_ref, layer)

    def tap_group(c0, r, wait_zero):
        n_rows = CONV_ROWS if r == 0 else CONV_ROWS + SUBLANES
        part = None
        for a in range((first + CONV_K - 1 - r) // SUBLANES + 1):
            j = SUBLANES * a + r - first
            if 0 <= j < CONV_K:
                w_j = cw_ref[j:j + 1, :]
                if wait_zero is not None:
                    w_j = w_j + wait_zero
                lo = c0 + SUBLANES * a
                term = ext_scr[lo:lo + n_rows, :] * w_j
                part = term if part is None else part + term
        if r == 0:
            conv_acc[c0] = part + conv_bias
        else:
            conv_acc[c0] = conv_acc[c0] + part[r:r + CONV_ROWS, :]
        if r == SUBLANES - 1:
            c_scr[c0:c0 + CONV_ROWS, :] = conv_acc.pop(c0)

    def conv_tail(lo, wait_zero):
        slab = slice(lo, lo + CONV_ROWS)
        gain = _layer_row(lng_ref, layer)
        if wait_zero is not None:
            gain = gain + wait_zero
        cn = _standardize(c_scr[slab, :]) * gain + _layer_row(lnb_ref, layer)
        mix_scr[slab, RET_W:RET_W + CONV_W] = (_silu(cn) * xg_scr[slab, 0:CONV_W]).astype(BF16)

    taps = [functools.partial(tap_group, c0, r)
            for c0 in range(0, tm, CONV_ROWS) for r in range(SUBLANES)]
    tails = [functools.partial(conv_tail, lo) for lo in range(0, tm, CONV_ROWS)]

    chunks = range(tm // RET_CHUNK)
    heads = range(RET_HEADS)
    crow = lambda c: slice(RET_CHUNK * c, RET_CHUNK * (c + 1))
    hcol = lambda hh: slice(RET_DK * hh, RET_DK * (hh + 1))

    def retention_stages(c0):
        part = chunks[c0:c0 + RET_STAGE_CHUNKS]
        sc, kv, before, out = {}, {}, {}, {}

        def scores_and_updates():
            for c in part:
                for hh in heads:
                    sc[c, hh] = _dot_nt(q_scr[crow(c), hcol(hh)], k_scr[crow(c), hcol(hh)])
                    kv[c, hh] = _dot_tn(kw_scr[crow(c), hcol(hh)], v_scr[crow(c), hcol(hh)])

        def recurrence_and_outputs():
            for hh in heads:
                state = s_scr[hh]
                for c in part:
                    before[c, hh] = state.astype(BF16)
                    state = cd_ref[hh] * state + kv[c, hh]
                s_scr[hh] = state
            for c in part:
                for hh in heads:
                    lhs = jnp.concatenate([(sc[c, hh] * dec_ref[hh]).astype(BF16),
                                           qw_scr[crow(c), hcol(hh)]], axis=1)
                    rhs = jnp.concatenate([v_scr[crow(c), hcol(hh)], before[c, hh]], axis=0)
                    out[c, hh] = _dot(lhs, rhs)

        def norms():
            for c in part:
                for hh in heads:
                    o = _standardize(out[c, hh]) * gn_ref[layer:layer + 1, hcol(hh)]
                    mix_scr[crow(c), hcol(hh)] = (o * g_scr[crow(c), hcol(hh)]).astype(BF16)

        return [scores_and_updates, recurrence_and_outputs, norms]

    n_half = RET_CHUNK * RET_STAGE_CHUNKS
    mkt = mk_ref[0, 0].astype(BF16)
    mvt = mv_ref[0, 0].astype(BF16)
    head = _head_lane_ids()

    def xattn_stages(half):
        scores, pvs, sums = [], [], []

        def query_key():
            q = xg_scr[half, CONV_W:CONV_W + XA_W] * (LOG2_E * XA_DH ** -0.5)
            for hh in range(XA_HEADS):
                scores.append(_dot(jnp.where(head == hh, q, 0.0).astype(BF16), mkt))

        def softmax_value():
            for sc in scores:
                e = jnp.exp2(sc - jnp.max(sc, axis=-1, keepdims=True))
                sums.append(jnp.sum(e, axis=-1, keepdims=True))
                pvs.append(_dot_nt(e.astype(BF16), mvt))

        def norm_gate():
            a = pvs[0]
            for hh in range(1, XA_HEADS):
                a = jnp.where(head == hh, pvs[hh], a)
            gate = xg_scr[half, CONV_W + XA_W:CONV_W + 2 * XA_W]
            mix_scr[half, RET_W + CONV_W:D_MIX] = _xattn_norm_gate(
                a, _layer_row(xan_ref, layer), gate, n_half, denoms=sums).astype(BF16)

        return [query_key, softmax_value, norm_gate]

    halves = [slice(n_half * p, n_half * (p + 1)) for p in range(tm // n_half)]
    pending = [stage for p in range(len(halves))
               for stage in retention_stages(RET_STAGE_CHUNKS * p)]
    gate_block = [post for _, _, post in blocks].index(post_cg)
    late_blocks = len(blocks) - gate_block
    taps_per = -(-len(taps) // gate_block)
    tails_per = -(-len(tails) // (late_blocks - 1))
    stages_per = len(pending) // late_blocks
    z_prev = None
    for k, (col0, c, post) in enumerate(blocks):
        if k < gate_block:
            items = taps[taps_per * k:taps_per * (k + 1)]
        elif k == gate_block:
            items = []
        else:
            j = k - gate_block - 1
            items = tails[tails_per * j:tails_per * (j + 1)]
        for n, item in enumerate(items):
            slab_end = tm * (n + 1) // len(items) // SUBLANES * SUBLANES
            item(None if z_prev is None else _zero_row_after(z_prev[0:slab_end, :]))
        if k >= gate_block:
            for stage in pending[:stages_per]:
                stage()
            pending = pending[stages_per:]
        z_prev = proj(col0 + c, col0 + c + pb)
        post(z_prev, c)

    sample_stages[2]()
    for stage in pending:
        stage()
    for p, half in enumerate(halves):
        for stage in xattn_stages(half):
            stage()
        y = x_ref[0, half, :] + _dot(mix_scr[half, :], wout_ref[...])
        if final:
            y = _rms(y, fng_ref[...])
        y_ref[0, half, :] = y
        if p == 0:
            sample_stages[3]()

    @pl.when(t == n_tiles - 1)
    def _():
        sconv_ref[0, 0] = ext_scr[tm + first:tm + CONV_PAD, :]
        sret_ref[0, 0] = s_scr[...]

    ext_scr[0:CONV_PAD, :] = ext_scr[tm:tm + CONV_PAD, :]


def _layer(layer, final, n_tok, x, cos_sin, mk, mv, ng, win, gn, cw, cb, lng, lnb, xan,
           wout, tables, layout, cd, fng, z_s, s0, cbuf, mkt_s, mvt_s, prev_states):
    batch, seq, _ = x.shape
    depth, sbatch = s0.shape[0], s0.shape[1]
    tm = PROMPT_TILE
    n_tiles = seq // tm
    n_seq = sbatch // (batch * n_tiles)
    assert n_seq * batch * n_tiles == sbatch and (n_seq * n_tok) % (2 * SUBLANES) == 0
    assert SUBLANES % n_seq == 0 and C_CA % (2 * CONV_W) == 0 and C_CB == C_CA + CONV_W
    whole = lambda a: pl.BlockSpec(a.shape, lambda b, t: (0,) * a.ndim)
    step = lambda b, t: b * n_tiles + t
    conv_block = lambda b, t: step(b, t) // (SUBLANES // n_seq)
    conv_spec = pl.BlockSpec((None, CONV_HIST, SUBLANES, CONV_W),
                             lambda b, t: (layer, 0, conv_block(b, t), 0))
    sample_specs = [
        pl.BlockSpec((n_seq * n_tok, D_IN), lambda b, t: (step(b, t), 0)),
        pl.BlockSpec((SUBLANES * n_tok, 2 * CONV_W),
                     lambda b, t: (conv_block(b, t), C_CA // (2 * CONV_W))),
        pl.BlockSpec((1, n_seq, RET_HEADS, RET_DK, RET_DV),
                     lambda b, t: (layer, step(b, t), 0, 0, 0)),
        conv_spec,
        pl.BlockSpec((1, n_seq, XA_W, N_MEM), lambda b, t: (layer, step(b, t), 0, 0)),
        pl.BlockSpec((1, n_seq, XA_W, N_MEM), lambda b, t: (layer, step(b, t), 0, 0)),
    ]
    in_specs = [
        pl.BlockSpec(memory_space=pltpu.SMEM),
        pl.BlockSpec((1, tm, D_MODEL), lambda b, t: (b, t, 0)),
        pl.BlockSpec((tm, 2 * RET_DK), lambda b, t: (t, 0)),
        pl.BlockSpec((1, 1, XA_W, N_MEM), lambda b, t: (layer, b, 0, 0)),
        pl.BlockSpec((1, 1, XA_W, N_MEM), lambda b, t: (layer, b, 0, 0)),
        whole(ng),
        whole(win),
        whole(gn),
        pl.BlockSpec((None, CONV_K, CONV_W), lambda b, t: (layer, 0, 0)),
        whole(cb), whole(lng), whole(lnb),
        whole(xan),
        whole(wout),
        whole(tables),
        whole(fng),
    ] + sample_specs
    n_fixed = len(in_specs)
    n_alias = len(prev_states)
    in_specs += [pl.BlockSpec(memory_space=pl.ANY)] * n_alias
    out_specs = [
        pl.BlockSpec((1, tm, D_MODEL), lambda b, t: (b, t, 0)),
        pl.BlockSpec((1, 1, RET_HEADS, RET_DK, RET_DV), lambda b, t: (layer, b, 0, 0, 0)),
        pl.BlockSpec((1, 1, CONV_HIST, CONV_W), lambda b, t: (layer, b, 0, 0)),
        pl.BlockSpec((n_seq * n_tok, D_MIX), lambda b, t: (step(b, t), 0)),
        pl.BlockSpec((1, n_seq, RET_HEADS, RET_DK, RET_DV),
                     lambda b, t: (layer, step(b, t), 0, 0, 0)),
        conv_spec,
    ]
    out_shape = [
        jax.ShapeDtypeStruct((batch, seq, D_MODEL), F32),
        jax.ShapeDtypeStruct((depth, batch, RET_HEADS, RET_DK, RET_DV), F32),
        jax.ShapeDtypeStruct((depth, batch, CONV_HIST, CONV_W), F32),
        jax.ShapeDtypeStruct((sbatch * n_tok, D_MIX), BF16),
        jax.ShapeDtypeStruct((depth, sbatch, RET_HEADS, RET_DK, RET_DV), F32),
        jax.ShapeDtypeStruct((depth, CONV_HIST, sbatch, CONV_W), F32),
    ]
    state_outputs = (1, 2, 4, 5)
    scratch = [
        pltpu.VMEM((RET_HEADS, RET_DK, RET_DV), F32),
        pltpu.VMEM((CONV_PAD + tm, CONV_W), F32),
        pltpu.VMEM((tm, CONV_W), F32),
        pltpu.VMEM((tm, RET_W), BF16),
        pltpu.VMEM((tm, RET_W), BF16),
        pltpu.VMEM((tm, RET_W), BF16),
        pltpu.VMEM((tm, RET_W), BF16),
        pltpu.VMEM((tm, RET_W), BF16),
        pltpu.VMEM((tm, RET_W), F32),
        pltpu.VMEM((tm, CONV_W + 2 * XA_W), F32),
        pltpu.VMEM((tm, D_MIX), BF16),
        pltpu.VMEM((CONV_W // LANES, SUBLANES * n_tok, LANES), F32),
        pltpu.VMEM((CONV_W // LANES, SUBLANES * n_tok, LANES), F32),
    ]
    return pl.pallas_call(
        functools.partial(_layer_kernel, layer, final, n_tiles, n_alias, n_tok, n_seq, layout),
        grid=(batch, n_tiles),
        in_specs=in_specs,
        out_specs=out_specs,
        out_shape=out_shape,
        scratch_shapes=scratch,
        input_output_aliases={n_fixed + k: state_outputs[k] for k in range(n_alias)},
        compiler_params=_params(dimension_semantics=("arbitrary", "arbitrary")),
        name=f"layer{layer}",
    )(cd, x, cos_sin, mk, mv, ng, win, gn, cw, cb, lng, lnb, xan, wout, tables, fng,
      z_s, z_s, s0, cbuf, mkt_s, mvt_s, *prev_states)


def _project_and_cast(first_step, hb_scr, win_ref, wout_ref, z_ref, winb_ref, woutb_ref):
    @pl.when(first_step)
    def _():
        woutb_ref[...] = wout_ref[...].astype(BF16)

    w_block = win_ref[...].astype(BF16)
    winb_ref[...] = w_block
    z_ref[...] = _dot(hb_scr[...], w_block)


def _sample_in_kernel(layer, x_ref, ng_ref, win_ref, wout_ref, z_ref, winb_ref, woutb_ref,
                      hb_scr):
    first_step = pl.program_id(0) == 0

    @pl.when(first_step)
    def _():
        hb_scr[...] = _rms(x_ref[...], _layer_row(ng_ref, layer)).astype(BF16)

    _project_and_cast(first_step, hb_scr, win_ref, wout_ref, z_ref, winb_ref, woutb_ref)


def _weight_cast_specs(layer, cols):
    in_specs = [pl.BlockSpec((None, D_MODEL, cols), lambda j: (layer, 0, j)),
                pl.BlockSpec((None, D_MIX, D_MODEL), lambda j: (layer, 0, 0))]
    out_specs = [pl.BlockSpec((D_MODEL, cols), lambda j: (0, j)),
                 pl.BlockSpec((D_MIX, D_MODEL), lambda j: (0, 0))]
    out_shape = [jax.ShapeDtypeStruct((D_MODEL, D_IN), BF16),
                 jax.ShapeDtypeStruct((D_MIX, D_MODEL), BF16)]
    return in_specs, out_specs, out_shape


def _sample_in(layer, x, ng, win, wout):
    n = x.shape[0]
    cols = D_IN // SAMPLE_PROJ_STEPS
    whole = lambda a: pl.BlockSpec(a.shape, lambda j: (0,) * a.ndim)
    w_in_specs, w_out_specs, w_out_shape = _weight_cast_specs(layer, cols)
    return pl.pallas_call(
        functools.partial(_sample_in_kernel, layer),
        grid=(SAMPLE_PROJ_STEPS,),
        in_specs=[whole(x), whole(ng)] + w_in_specs,
        out_specs=[pl.BlockSpec((n, cols), lambda j: (0, j))] + w_out_specs,
        out_shape=[jax.ShapeDtypeStruct((n, D_IN), F32)] + w_out_shape,
        scratch_shapes=[pltpu.VMEM((n, D_MODEL), BF16)],
        compiler_params=_params(dimension_semantics=("arbitrary",)),
        name="sample_in",
    )(x, ng, win, wout)


def _sample_mid_kernel(layer, x_ref, mix_ref, woutb_prev_ref, ng_ref, win_ref, wout_ref,
                       h_ref, z_ref, winb_ref, woutb_ref, hb_scr):
    first_step = pl.program_id(0) == 0

    @pl.when(first_step)
    def _():
        h = x_ref[...] + _dot(mix_ref[...], woutb_prev_ref[...])
        h_ref[...] = h
        hb_scr[...] = _rms(h, _layer_row(ng_ref, layer + 1)).astype(BF16)

    _project_and_cast(first_step, hb_scr, win_ref, wout_ref, z_ref, winb_ref, woutb_ref)


def _sample_mid(layer, x, mix, woutb_prev, ng, win, wout):
    n = x.shape[0]
    cols = D_IN // SAMPLE_PROJ_STEPS
    whole = lambda a: pl.BlockSpec(a.shape, lambda j: (0,) * a.ndim)
    w_in_specs, w_out_specs, w_out_shape = _weight_cast_specs(layer + 1, cols)
    return pl.pallas_call(
        functools.partial(_sample_mid_kernel, layer),
        grid=(SAMPLE_PROJ_STEPS,),
        in_specs=[whole(x), whole(mix), whole(woutb_prev), whole(ng)] + w_in_specs,
        out_specs=[pl.BlockSpec((n, D_MODEL), lambda j: (0, 0)),
                   pl.BlockSpec((n, cols), lambda j: (0, j))] + w_out_specs,
        out_shape=[jax.ShapeDtypeStruct((n, D_MODEL), F32),
                   jax.ShapeDtypeStruct((n, D_IN), F32)] + w_out_shape,
        scratch_shapes=[pltpu.VMEM((n, D_MODEL), BF16)],
        compiler_params=_params(dimension_semantics=("arbitrary",)),
        name="sample_mid",
    )(x, mix, woutb_prev, ng, win, wout)


def _sample_out_kernel(x_ref, mix_ref, wout_ref, fng_ref, y_ref):
    y_ref[...] = _rms(x_ref[...] + _dot(mix_ref[...], wout_ref[...]), fng_ref[...])


def _sample_out(x, mix, wout, fng):
    n = x.shape[0]
    whole = lambda a: pl.BlockSpec(a.shape, lambda j: (0,) * a.ndim)
    return pl.pallas_call(
        _sample_out_kernel,
        grid=(1,),
        in_specs=[whole(x), whole(mix), whole(wout), whole(fng)],
        out_specs=pl.BlockSpec((n, D_MODEL), lambda j: (0, 0)),
        out_shape=jax.ShapeDtypeStruct((n, D_MODEL), F32),
        compiler_params=_params(dimension_semantics=("arbitrary",)),
        name="sample_out",
    )(x, mix, wout, fng)


def _sample_mix_stages(layer, n_tok, n_seq, own_row0,
                       cd_ref, z_ref, zc_ref, s0_ref, cbuf_ref, mkt_ref, mvt_ref, cos_ref,
                       sin_ref, gn_ref, cw_ref, cb_ref, lng_ref, lnb_ref, xan_ref,
                       dec_ref, wq_ref, wk_ref, mix_ref, sret_ref, sconv_ref, us_scr, cs_scr):
    grp = SUBLANES // n_tok
    rows = grp * n_tok
    cosf = cos_ref[...]
    sins = sin_ref[...]
    head = _head_lane_ids()
    row_id = lax.broadcasted_iota(jnp.int32, (rows, 1), 0)
    row_seq = row_id // n_tok
    row4_seq = lax.broadcasted_iota(jnp.int32, (XA_HEADS * rows, 1), 0) % rows // n_tok

    def pick(parts, seq_of_row):
        out = parts[0]
        for s in range(1, grp):
            out = jnp.where(seq_of_row == s, parts[s], out)
        return out

    groups = range(n_seq // grp)
    heads = range(RET_HEADS)
    seqs = [[g * grp + s for s in range(grp)] for g in groups]
    rs = [slice(rows * g, rows * (g + 1)) for g in groups]
    hcol = lambda c0, hh: slice(c0 + RET_DK * hh, c0 + RET_DK * (hh + 1))

    qh, kf, vh, sc, xsc = {}, {}, {}, {}, {}
    inner, cross, xo, xsum = {}, {}, {}, {}

    def first_matmuls():
        for g in groups:
            for hh in heads:
                qh[g, hh] = _rope(z_ref[rs[g], hcol(C_RQ, hh)], cosf, sins).astype(BF16)
                kf[g, hh] = _rope(z_ref[rs[g], hcol(C_RK, hh)], cosf, sins) * (RET_DK ** -0.5)
                vh[g, hh] = z_ref[rs[g], hcol(C_RV, hh)].astype(BF16)
                sc[g, hh] = _dot_nt(qh[g, hh], kf[g, hh].astype(BF16))
            xq = z_ref[rs[g], C_XQ:C_XQ + XA_W] * (LOG2_E * XA_DH ** -0.5)
            q4 = jnp.concatenate([jnp.where(head == hh, xq, 0.0) for hh in range(XA_HEADS)],
                                 axis=0).astype(BF16)
            xsc[g] = [_dot(q4, mkt_ref[0, b].astype(BF16)) for b in seqs[g]]

    def conv_module():
        pieces = range(CONV_W // LANES)
        lanes = lambda p: slice(LANES * p, LANES * (p + 1))
        u_rows = zc_ref[:, 0:CONV_W] * _sigmoid(zc_ref[:, CONV_W:2 * CONV_W])
        for p in pieces:
            us_scr[p] = u_rows[:, lanes(p)]
        u = []
        for i in range(n_tok):
            tok = pl.ds(i, SUBLANES, stride=n_tok)
            u.append(jnp.concatenate([us_scr[p, tok, :] for p in pieces], axis=-1))
        window = lambda k: cbuf_ref[k] if k < CONV_HIST else u[k - CONV_HIST]
        acc = [_layer_row(cb_ref, layer)] * n_tok
        for k in range(CONV_HIST + n_tok):
            w_k = window(k)
            for i in range(n_tok):
                if 0 <= k - i < CONV_K:
                    acc[i] = acc[i] + w_k * cw_ref[k - i:k - i + 1, :]
        for i in range(n_tok):
            for p in pieces:
                cs_scr[p, pl.ds(i, SUBLANES, stride=n_tok), :] = acc[i][:, lanes(p)]
        for k in range(CONV_HIST):
            sconv_ref[k] = window(k + n_tok)
        for g in groups:
            own = pl.ds(pl.multiple_of(own_row0 + rows * g, rows), rows)
            c = jnp.concatenate([cs_scr[p, own, :] for p in pieces], axis=-1)
            cn = _standardize(c) * _layer_row(lng_ref, layer) + _layer_row(lnb_ref, layer)
            gate = _silu(z_ref[rs[g], C_CG:C_CG + CONV_W])
            mix_ref[rs[g], RET_W:RET_W + CONV_W] = (_silu(cn) * gate).astype(BF16)

    def second_matmuls():
        for g in groups:
            for hh in heads:
                inner[g, hh] = _dot((sc[g, hh] * dec_ref[hh]).astype(BF16), vh[g, hh])
                kw = kf[g, hh] * wk_ref[hh]
                parts = []
                for s, b in enumerate(seqs[g]):
                    s_prev = s0_ref[0, b, hh]
                    parts.append(_dot(qh[g, hh], s_prev.astype(BF16)))
                    kv = _dot_tn(jnp.where(row_seq == s, kw, 0.0).astype(BF16), vh[g, hh])
                    sret_ref[0, b, hh] = cd_ref[hh] * s_prev + kv
                cross[g, hh] = pick(parts, row_seq) * wq_ref[hh]
            s4 = pick(xsc[g], row4_seq)
            e = jnp.exp2(s4 - jnp.max(s4, axis=-1, keepdims=True))
            xsum[g] = jnp.sum(e, axis=-1, keepdims=True)
            eb = e.astype(BF16)
            xo[g] = [_dot_nt(eb, mvt_ref[0, b].astype(BF16)) for b in seqs[g]]

    def norms_and_stores():
        for g in groups:
            for hh in heads:
                co = hcol(0, hh)
                o = _standardize(inner[g, hh] + cross[g, hh]) * gn_ref[layer:layer + 1, co]
                gate = _silu(z_ref[rs[g], hcol(C_RG, hh)])
                mix_ref[rs[g], co] = (o * gate).astype(BF16)
            o4 = pick(xo[g], row4_seq) / xsum[g]
            a = jnp.zeros((rows, XA_W), F32)
            for hh in range(XA_HEADS):
                a = jnp.where(head == hh, o4[rows * hh:rows * (hh + 1), :], a)
            gate = _silu(z_ref[rs[g], C_XG:C_XG + XA_W])
            mix_ref[rs[g], RET_W + CONV_W:D_MIX] = _xattn_norm_gate(
                a, _layer_row(xan_ref, layer), gate, rows).astype(BF16)

    return [first_matmuls, conv_module, second_matmuls, norms_and_stores]


def _rope_tables(pos):
    half = RET_DK // 2
    inv = np.float64(ROPE_BASE) ** (-np.arange(half, dtype=np.float64) / half)
    ang = pos.astype(np.float64)[:, None] * inv[None, :]
    cos, sin = np.cos(ang), np.sin(ang)
    return (np.concatenate([cos, cos], axis=-1).astype(np.float32),
            np.concatenate([-sin, sin], axis=-1).astype(np.float32))


def _decay_tables(chunk):
    lg = np.log(1.0 - np.exp2(-5.0 - np.arange(RET_HEADS, dtype=np.float64)))
    idx = np.arange(chunk, dtype=np.float64)
    diff = idx[:, None] - idx[None, :]
    dec = np.where(diff[None] >= 0, np.exp(np.maximum(diff, 0.0)[None] * lg[:, None, None]), 0.0)
    wk = np.exp((chunk - 1.0 - idx)[None, :] * lg[:, None])
    wq = np.exp((idx + 1.0)[None, :] * lg[:, None])
    cd = np.exp(chunk * lg)
    wk = np.broadcast_to(wk[:, :, None], (RET_HEADS, chunk, RET_DK))
    wq = np.broadcast_to(wq[:, :, None], (RET_HEADS, chunk, RET_DV))
    f32 = lambda a: np.ascontiguousarray(a, dtype=np.float32)
    return f32(dec), f32(wq), f32(wk), f32(cd)


def _group_tables(n_tok, pos0):
    grp = SUBLANES // n_tok
    cosf, sins = _rope_tables(pos0 + np.arange(n_tok))
    dec, wq, wk, cd = _decay_tables(n_tok)
    tile_rows = lambda a: np.concatenate([a] * grp, axis=-2)
    eye = np.eye(grp, dtype=np.float32)
    dec = np.einsum("st,hij->hsitj", eye, dec).reshape(RET_HEADS, grp * n_tok, grp * n_tok)
    return tile_rows(cosf), tile_rows(sins), dec, tile_rows(wq), tile_rows(wk), cd


def kernel(x_prompt, x_sample, mem_prompt, state_ret, state_conv, cache_mem_k, cache_mem_v,
           norm_g, w_in, ret_gn_g, conv_w, conv_b, conv_ln_g, conv_ln_b, xa_norm_g,
           mem_norm_g, w_mk, w_mv, w_out, final_norm_g):
    depth = w_in.shape[0]
    batch, seq, _ = x_prompt.shape
    dbatch, dseq, _ = x_sample.shape

    fng = final_norm_g.reshape(1, D_MODEL)

    p_mk, p_mv = _mem_kv(mem_prompt, mem_norm_g, w_mk, w_mv)
    cos_p, sin_p = _rope_tables(np.arange(seq))
    dec_p, wq_p, wk_p, cd_p = _decay_tables(RET_CHUNK if seq % RET_CHUNK == 0 else seq)

    assert SUBLANES % dseq == 0 and dseq % RET_CHUNK != 0
    cos_s, sin_s, dec_s, wq_s, wk_s, cd_s = _group_tables(dseq, PAST_LEN)
    tables, layout = _pack_tables([
        ("dec", dec_p), ("wq", wq_p), ("wk", wk_p), ("sample cos", cos_s[None]),
        ("sample sin", sin_s[None]), ("sample dec", dec_s), ("sample wq", wq_s),
        ("sample wk", wk_s)])
    cos_sin = np.concatenate([cos_p, sin_p], axis=-1)
    cd = np.concatenate([cd_p, cd_s])
    to_hd_m = lambda c: c.transpose(0, 1, 3, 4, 2).reshape(depth, dbatch, XA_W, N_MEM)
    mkt_s, mvt_s = to_hd_m(cache_mem_k), to_hd_m(cache_mem_v)
    conv_s = state_conv.transpose(0, 2, 1, 3)
    hs = x_sample.reshape(dbatch * dseq, D_MODEL)
    z, w_in_b, w_out_b = _sample_in(0, hs, norm_g, w_in, w_out)

    hp = x_prompt
    states = ()
    y_sample = None
    for l in range(depth):
        hp, p_ret, p_conv, mix, s_ret, s_conv = _layer(
            l, l == depth - 1, dseq, hp, cos_sin, p_mk, p_mv, norm_g, w_in_b, ret_gn_g,
            conv_w, conv_b, conv_ln_g, conv_ln_b, xa_norm_g, w_out_b, tables, layout, cd,
            fng, z, state_ret, conv_s, mkt_s, mvt_s, states)
        states = (p_ret, p_conv, s_ret, s_conv)
        if l + 1 < depth:
            hs, z, w_in_b, w_out_b = _sample_mid(l, hs, mix, w_out_b, norm_g, w_in, w_out)
        else:
            y_sample = _sample_out(hs, mix, w_out_b, fng).reshape(dbatch, dseq, D_MODEL)
    y_prompt = hp

    from_hd_m = lambda c: c.reshape(depth, batch, XA_HEADS, XA_DH, N_MEM).transpose(0, 1, 4, 2, 3)
    return (y_prompt, y_sample, states[0], states[1],
            from_hd_m(p_mk), from_hd_m(p_mv), states[2], states[3].transpose(0, 2, 1, 3))
```

```python
import functools

import numpy as np

import jax
import jax.numpy as jnp
from jax import lax
from jax.experimental import pallas as pl
from jax.experimental.pallas import tpu as pltpu

F32 = jnp.float32
BF16 = jnp.bfloat16

D_MODEL = 1024
N_MEM = 256
RET_HEADS = 4
RET_DK = 128
RET_DV = 128
RET_W = RET_HEADS * RET_DV
RET_CHUNK = 128
CONV_W = 256
CONV_K = 31
XA_HEADS = 4
XA_DH = 64
XA_W = XA_HEADS * XA_DH
D_MIX = RET_W + CONV_W + XA_W
ROPE_BASE = 10000.0
EPS = 1e-6
PAST_LEN = 16384
LOG2_E = 1.4426950408889634

C_RQ, C_RK, C_RV, C_RG = 0, 512, 1024, 1536
C_CA, C_CB, C_CG = 2048, 2304, 2560
C_XQ, C_XG = 2816, 3072
D_IN = 3328

VMEM_LIMIT_BYTES = 56 * 1024 * 1024
SUBLANES = 8
LANES = 128
CONV_PAD = 32
CONV_HIST = CONV_K - 1
PROMPT_TILE = 512
CONV_ROWS = 64
PROJ_BLOCK = 256
RET_STAGE_CHUNKS = 2
SAMPLE_PROJ_STEPS = 2


def _rms(x, g):
    return x * lax.rsqrt(jnp.mean(x * x, axis=-1, keepdims=True) + EPS) * g


def _standardize(x):
    mu = jnp.mean(x, axis=-1, keepdims=True)
    d = x - mu
    var = jnp.mean(d * d, axis=-1, keepdims=True)
    return d * lax.rsqrt(var + EPS)


def _sigmoid(x):
    return 1.0 / (1.0 + jnp.exp(-x))


def _silu(x):
    return x * _sigmoid(x)


def _dot(a, b):
    return jnp.dot(a, b, preferred_element_type=F32)


def _dot_nt(a, b):
    return lax.dot_general(a, b, (((1,), (1,)), ((), ())), preferred_element_type=F32)


def _dot_tn(a, b):
    return lax.dot_general(a, b, (((0,), (0,)), ((), ())), preferred_element_type=F32)


def _rope(x, cosf, sins):
    return x * cosf + pltpu.roll(x, RET_DK // 2, 1) * sins


def _zero_row_after(z):
    tile = z[z.shape[0] - SUBLANES:, z.shape[1] - LANES:]
    bits = pltpu.bitcast(tile, jnp.uint32)
    sixteen = jnp.uint32(16)
    zero = lax.shift_right_logical(lax.shift_right_logical(bits, sixteen), sixteen)
    row = pltpu.bitcast(zero, F32)[0:1, :]
    return jnp.concatenate([row, row], axis=1)


def _head_lane_ids():
    return lax.broadcasted_iota(jnp.int32, (1, XA_W), 1) // XA_DH


def _xattn_norm_gate(a, xan, gate, rows, denoms=None):
    head = _head_lane_ids()
    a2 = a * a
    ms = jnp.zeros((rows, XA_W), F32)
    for hh in range(XA_HEADS):
        m = head == hh
        ssq = jnp.sum(jnp.where(m, a2, 0.0), axis=-1, keepdims=True) * (1.0 / XA_DH)
        ssq = ssq + (EPS if denoms is None else EPS * denoms[hh] * denoms[hh])
        ms = jnp.where(m, ssq, ms)
    return a * lax.rsqrt(ms) * xan * gate


def _layer_row(ref, layer):
    return ref[layer:layer + 1, :]


def _params(**kw):
    return pltpu.CompilerParams(vmem_limit_bytes=VMEM_LIMIT_BYTES, **kw)


def _mem_kv_kernel(mem_ref, g_ref, wk_ref, wv_ref, kt_ref, vt_ref):
    layer = pl.program_id(0)
    gain = g_ref[pl.ds(layer, 1), :]
    wkt = wk_ref[...].T.astype(BF16)
    wvt = wv_ref[...].T.astype(BF16)
    for b in range(mem_ref.shape[0]):
        m = _rms(mem_ref[b], gain).astype(BF16)
        kt_ref[0, b] = _dot_nt(wkt, m)
        vt_ref[0, b] = _dot_nt(wvt, m)


def _mem_kv(mem, g, wk, wv):
    depth, batch = g.shape[0], mem.shape[0]
    out = jax.ShapeDtypeStruct((depth, batch, XA_W, N_MEM), F32)
    return pl.pallas_call(
        _mem_kv_kernel,
        grid=(depth,),
        in_specs=[
            pl.BlockSpec((batch, N_MEM, D_MODEL), lambda l: (0, 0, 0)),
            pl.BlockSpec((depth, D_MODEL), lambda l: (0, 0)),
            pl.BlockSpec((None, D_MODEL, XA_W), lambda l: (l, 0, 0)),
            pl.BlockSpec((None, D_MODEL, XA_W), lambda l: (l, 0, 0)),
        ],
        out_specs=[
            pl.BlockSpec((1, batch, XA_W, N_MEM), lambda l: (l, 0, 0, 0)),
            pl.BlockSpec((1, batch, XA_W, N_MEM), lambda l: (l, 0, 0, 0)),
        ],
        out_shape=[out, out],
        compiler_params=_params(dimension_semantics=("arbitrary",)),
        name="mem_kv",
    )(mem, g, wk, wv)


class _Blocks:
    def __init__(self, ref, row0, n, rows, lanes):
        self.ref, self.row0, self.n, self.rows, self.lanes = ref, row0, n, rows, lanes

    def __getitem__(self, i):
        if i is Ellipsis:
            return self.ref[self.row0:self.row0 + self.n * self.rows, 0:self.lanes]
        r0 = self.row0 + i * self.rows
        return self.ref[r0:r0 + self.rows, 0:self.lanes]


class _Scalars:
    def __init__(self, ref, first):
        self.ref, self.first = ref, first

    def __getitem__(self, i):
        return self.ref[self.first + i]


def _pack_tables(named):
    parts, layout, row0 = [], {}, 0
    for name, a in named:
        n, rows, lanes = a.shape
        assert rows % SUBLANES == 0 and lanes <= LANES
        wide = np.zeros((n * rows, LANES), np.float32)
        wide[:, :lanes] = a.reshape(n * rows, lanes)
        parts.append(wide)
        layout[name] = (row0, n, rows, lanes)
        row0 += n * rows
    return np.concatenate(parts), layout


def _layer_kernel(layer, final, n_tiles, n_alias, n_tok, n_seq, layout,
                  cd_all_ref, x_ref, cs_ref, mk_ref, mv_ref, ng_ref, win_ref,
                  gn_ref, cw_ref, cb_ref, lng_ref, lnb_ref, xan_ref, wout_ref,
                  tab_ref, fng_ref,
                  sz_ref, szc_ref, ss0_ref, scbuf_ref, smkt_ref, smvt_ref, *refs):
    (y_ref, sret_ref, sconv_ref, smix_ref, ssret_ref, ssconv_ref,
     s_scr, ext_scr, c_scr, q_scr, qw_scr, k_scr, kw_scr, v_scr, g_scr, xg_scr,
     mix_scr, us_scr, cs_scr) = refs[n_alias:]

    tm = PROMPT_TILE
    t = pl.program_id(1)
    table = lambda name: _Blocks(tab_ref, *layout[name])
    dec_ref, wq_ref, wk_ref = table("dec"), table("wq"), table("wk")
    cd_ref = _Scalars(cd_all_ref, 0)
    step_in_block = (pl.program_id(0) * n_tiles + t) % (SUBLANES // n_seq)
    sample_stages = _sample_mix_stages(
        layer, n_tok, n_seq, step_in_block * (n_seq * n_tok), _Scalars(cd_all_ref, RET_HEADS),
        sz_ref, szc_ref, ss0_ref, scbuf_ref, smkt_ref, smvt_ref, table("sample cos"),
        table("sample sin"), gn_ref, cw_ref, cb_ref, lng_ref, lnb_ref, xan_ref,
        table("sample dec"), table("sample wq"), table("sample wk"), smix_ref, ssret_ref,
        ssconv_ref, us_scr, cs_scr)

    @pl.when(t == 0)
    def _():
        s_scr[...] = jnp.zeros_like(s_scr)
        ext_scr[0:CONV_PAD, :] = jnp.zeros((CONV_PAD, CONV_W), F32)

    sample_stages[0]()
    sample_stages[1]()

    hb = _rms(x_ref[0], _layer_row(ng_ref, layer)).astype(BF16)

    def proj(a, b):
        return _dot(hb, win_ref[:, a:b])

    cosf = cs_ref[:, 0:RET_DK]
    sins = cs_ref[:, RET_DK:2 * RET_DK]

    u = proj(C_CA, C_CA + CONV_W) * _sigmoid(proj(C_CB, C_CB + CONV_W))
    ext_scr[CONV_PAD:CONV_PAD + tm, :] = u

    pb = PROJ_BLOCK

    def post_q(z, c):
        for i in range(pb // RET_DK):
            hh = c // RET_DK + i
            cols = slice(RET_DK * hh, RET_DK * (hh + 1))
            q = _rope(z[:, RET_DK * i:RET_DK * (i + 1)], cosf, sins)
            q_scr[:, cols] = q.astype(BF16)
            for ch in range(tm // RET_CHUNK):
                rows = slice(RET_CHUNK * ch, RET_CHUNK * (ch + 1))
                qw_scr[rows, cols] = (q[rows, :] * wq_ref[hh]).astype(BF16)

    def post_k(z, c):
        for i in range(pb // RET_DK):
            hh = c // RET_DK + i
            cols = slice(RET_DK * hh, RET_DK * (hh + 1))
            k = _rope(z[:, RET_DK * i:RET_DK * (i + 1)], cosf, sins) * (RET_DK ** -0.5)
            k_scr[:, cols] = k.astype(BF16)
            for ch in range(tm // RET_CHUNK):
                rows = slice(RET_CHUNK * ch, RET_CHUNK * (ch + 1))
                kw_scr[rows, cols] = (k[rows, :] * wk_ref[hh]).astype(BF16)

    def post_v(z, c):
        v_scr[:, c:c + pb] = z.astype(BF16)

    def post_g(z, c):
        g_scr[:, c:c + pb] = _silu(z)

    def post_cg(z, c):
        xg_scr[:, c:c + pb] = _silu(z)

    def post_xq(z, c):
        xg_scr[:, CONV_W + c:CONV_W + c + pb] = z

    def post_xg(z, c):
        xg_scr[:, CONV_W + XA_W + c:CONV_W + XA_W + c + pb] = _silu(z)

    blocks = []
    for col0, width, post in ((C_RQ, RET_W, post_q), (C_RK, RET_W, post_k),
                              (C_RV, RET_W, post_v), (C_RG, RET_W, post_g),
                              (C_CG, CONV_W, post_cg), (C_XQ, XA_W, post_xq),
                              (C_XG, XA_W, post_xg)):
        blocks += [(col0, c, post) for c in range(0, width, pb)]

    first = CONV_PAD - CONV_HIST
    conv_acc = {}
    conv_bias = _layer_row(cb_ref, layer)

    def tap_group(c0, r, wait_zero):
        n_rows = CONV_ROWS if r == 0 else CONV_ROWS + SUBLANES
        part = None
        for a in range((first + CONV_K - 1 - r) // SUBLANES + 1):
            j = SUBLANES * a + r - first
            if 0 <= j < CONV_K:
                w_j = cw_ref[j:j + 1, :]
                if wait_zero is not None:
                    w_j = w_j + wait_zero
                lo = c0 + SUBLANES * a
                term = ext_scr[lo:lo + n_rows, :] * w_j
                part = term if part is None else part + term
        if r == 0:
            conv_acc[c0] = part + conv_bias
        else:
            conv_acc[c0] = conv_acc[c0] + part[r:r + CONV_ROWS, :]
        if r == SUBLANES - 1:
            c_scr[c0:c0 + CONV_ROWS, :] = conv_acc.pop(c0)

    def conv_tail(lo, wait_zero):
        slab = slice(lo, lo + CONV_ROWS)
        gain = _layer_row(lng_ref, layer)
        if wait_zero is not None:
            gain = gain + wait_zero
        cn = _standardize(c_scr[slab, :]) * gain + _layer_row(lnb_ref, layer)
        mix_scr[slab, RET_W:RET_W + CONV_W] = (_silu(cn) * xg_scr[slab, 0:CONV_W]).astype(BF16)

    taps = [functools.partial(tap_group, c0, r)
            for c0 in range(0, tm, CONV_ROWS) for r in range(SUBLANES)]
    tails = [functools.partial(conv_tail, lo) for lo in range(0, tm, CONV_ROWS)]

    chunks = range(tm // RET_CHUNK)
    heads = range(RET_HEADS)
    crow = lambda c: slice(RET_CHUNK * c, RET_CHUNK * (c + 1))
    hcol = lambda hh: slice(RET_DK * hh, RET_DK * (hh + 1))

    def retention_stages(c0):
        part = chunks[c0:c0 + RET_STAGE_CHUNKS]
        sc, kv, before, out = {}, {}, {}, {}

        def scores_and_updates():
            for c in part:
                for hh in heads:
                    sc[c, hh] = _dot_nt(q_scr[crow(c), hcol(hh)], k_scr[crow(c), hcol(hh)])
                    kv[c, hh] = _dot_tn(kw_scr[crow(c), hcol(hh)], v_scr[crow(c), hcol(hh)])

        def recurrence_and_outputs():
            for hh in heads:
                state = s_scr[hh]
                for c in part:
                    before[c, hh] = state.astype(BF16)
                    state = cd_ref[hh] * state + kv[c, hh]
                s_scr[hh] = state
            for c in part:
                for hh in heads:
                    lhs = jnp.concatenate([(sc[c, hh] * dec_ref[hh]).astype(BF16),
                                           qw_scr[crow(c), hcol(hh)]], axis=1)
                    rhs = jnp.concatenate([v_scr[crow(c), hcol(hh)], before[c, hh]], axis=0)
                    out[c, hh] = _dot(lhs, rhs)

        def norms():
            for c in part:
                for hh in heads:
                    o = _standardize(out[c, hh]) * gn_ref[layer:layer + 1, hcol(hh)]
                    mix_scr[crow(c), hcol(hh)] = (o * g_scr[crow(c), hcol(hh)]).astype(BF16)

        return [scores_and_updates, recurrence_and_outputs, norms]

    n_half = RET_CHUNK * RET_STAGE_CHUNKS
    mkt = mk_ref[0, 0].astype(BF16)
    mvt = mv_ref[0, 0].astype(BF16)
    head = _head_lane_ids()

    def xattn_stages(half):
        scores, pvs, sums = [], [], []

        def query_key():
            q = xg_scr[half, CONV_W:CONV_W + XA_W] * (LOG2_E * XA_DH ** -0.5)
            for hh in range(XA_HEADS):
                scores.append(_dot(jnp.where(head == hh, q, 0.0).astype(BF16), mkt))

        def softmax_value():
            for sc in scores:
                e = jnp.exp2(sc - jnp.max(sc, axis=-1, keepdims=True))
                sums.append(jnp.sum(e, axis=-1, keepdims=True))
                pvs.append(_dot_nt(e.astype(BF16), mvt))

        def norm_gate():
            a = pvs[0]
            for hh in range(1, XA_HEADS):
                a = jnp.where(head == hh, pvs[hh], a)
            gate = xg_scr[half, CONV_W + XA_W:CONV_W + 2 * XA_W]
            mix_scr[half, RET_W + CONV_W:D_MIX] = _xattn_norm_gate(
                a, _layer_row(xan_ref, layer), gate, n_half, denoms=sums).astype(BF16)

        return [query_key, softmax_value, norm_gate]

    halves = [slice(n_half * p, n_half * (p + 1)) for p in range(tm // n_half)]
    pending = [stage for p in range(len(halves))
               for stage in retention_stages(RET_STAGE_CHUNKS * p)]
    gate_block = [post for _, _, post in blocks].index(post_cg)
    late_blocks = len(blocks) - gate_block
    taps_per = -(-len(taps) // gate_block)
    tails_per = -(-len(tails) // (late_blocks - 1))
    stages_per = len(pending) // late_blocks
    z_prev = None
    for k, (col0, c, post) in enumerate(blocks):
        if k < gate_block:
            items = taps[taps_per * k:taps_per * (k + 1)]
        elif k == gate_block:
            items = []
        else:
            j = k - gate_block - 1
            items = tails[tails_per * j:tails_per * (j + 1)]
        for n, item in enumerate(items):
            slab_end = tm * (n + 1) // len(items) // SUBLANES * SUBLANES
            item(None if z_prev is None else _zero_row_after(z_prev[0:slab_end, :]))
        if k >= gate_block:
            for stage in pending[:stages_per]:
                stage()
            pending = pending[stages_per:]
        z_prev = proj(col0 + c, col0 + c + pb)
        post(z_prev, c)

    sample_stages[2]()
    for stage in pending:
        stage()
    for p, half in enumerate(halves):
        for stage in xattn_stages(half):
            stage()
        y = x_ref[0, half, :] + _dot(mix_scr[half, :], wout_ref[...])
        if final:
            y = _rms(y, fng_ref[...])
        y_ref[0, half, :] = y
        if p == 0:
            sample_stages[3]()

    @pl.when(t == n_tiles - 1)
    def _():
        sconv_ref[0, 0] = ext_scr[tm + first:tm + CONV_PAD, :]
        sret_ref[0, 0] = s_scr[...]

    ext_scr[0:CONV_PAD, :] = ext_scr[tm:tm + CONV_PAD, :]


def _layer(layer, final, n_tok, x, cos_sin, mk, mv, ng, win, gn, cw, cb, lng, lnb, xan,
           wout, tables, layout, cd, fng, z_s, s0, cbuf, mkt_s, mvt_s, prev_states):
    batch, seq, _ = x.shape
    depth, sbatch = s0.shape[0], s0.shape[1]
    tm = PROMPT_TILE
    n_tiles = seq // tm
    n_seq = sbatch // (batch * n_tiles)
    assert n_seq * batch * n_tiles == sbatch and (n_seq * n_tok) % (2 * SUBLANES) == 0
    assert SUBLANES % n_seq == 0 and C_CA % (2 * CONV_W) == 0 and C_CB == C_CA + CONV_W
    whole = lambda a: pl.BlockSpec(a.shape, lambda b, t: (0,) * a.ndim)
    step = lambda b, t: b * n_tiles + t
    conv_block = lambda b, t: step(b, t) // (SUBLANES // n_seq)
    conv_spec = pl.BlockSpec((None, CONV_HIST, SUBLANES, CONV_W),
                             lambda b, t: (layer, 0, conv_block(b, t), 0))
    sample_specs = [
        pl.BlockSpec((n_seq * n_tok, D_IN), lambda b, t: (step(b, t), 0)),
        pl.BlockSpec((SUBLANES * n_tok, 2 * CONV_W),
                     lambda b, t: (conv_block(b, t), C_CA // (2 * CONV_W))),
        pl.BlockSpec((1, n_seq, RET_HEADS, RET_DK, RET_DV),
                     lambda b, t: (layer, step(b, t), 0, 0, 0)),
        conv_spec,
        pl.BlockSpec((1, n_seq, XA_W, N_MEM), lambda b, t: (layer, step(b, t), 0, 0)),
        pl.BlockSpec((1, n_seq, XA_W, N_MEM), lambda b, t: (layer, step(b, t), 0, 0)),
    ]
    in_specs = [
        pl.BlockSpec(memory_space=pltpu.SMEM),
        pl.BlockSpec((1, tm, D_MODEL), lambda b, t: (b, t, 0)),
        pl.BlockSpec((tm, 2 * RET_DK), lambda b, t: (t, 0)),
        pl.BlockSpec((1, 1, XA_W, N_MEM), lambda b, t: (layer, b, 0, 0)),
        pl.BlockSpec((1, 1, XA_W, N_MEM), lambda b, t: (layer, b, 0, 0)),
        whole(ng),
        whole(win),
        whole(gn),
        pl.BlockSpec((None, CONV_K, CONV_W), lambda b, t: (layer, 0, 0)),
        whole(cb), whole(lng), whole(lnb),
        whole(xan),
        whole(wout),
        whole(tables),
        whole(fng),
    ] + sample_specs
    n_fixed = len(in_specs)
    n_alias = len(prev_states)
    in_specs += [pl.BlockSpec(memory_space=pl.ANY)] * n_alias
    out_specs = [
        pl.BlockSpec((1, tm, D_MODEL), lambda b, t: (b, t, 0)),
        pl.BlockSpec((1, 1, RET_HEADS, RET_DK, RET_DV), lambda b, t: (layer, b, 0, 0, 0)),
        pl.BlockSpec((1, 1, CONV_HIST, CONV_W), lambda b, t: (layer, b, 0, 0)),
        pl.BlockSpec((n_seq * n_tok, D_MIX), lambda b, t: (step(b, t), 0)),
        pl.BlockSpec((1, n_seq, RET_HEADS, RET_DK, RET_DV),
                     lambda b, t: (layer, step(b, t), 0, 0, 0)),
        conv_spec,
    ]
    out_shape = [
        jax.ShapeDtypeStruct((batch, seq, D_MODEL), F32),
        jax.ShapeDtypeStruct((depth, batch, RET_HEADS, RET_DK, RET_DV), F32),
        jax.ShapeDtypeStruct((depth, batch, CONV_HIST, CONV_W), F32),
        jax.ShapeDtypeStruct((sbatch * n_tok, D_MIX), BF16),
        jax.ShapeDtypeStruct((depth, sbatch, RET_HEADS, RET_DK, RET_DV), F32),
        jax.ShapeDtypeStruct((depth, CONV_HIST, sbatch, CONV_W), F32),
    ]
    state_outputs = (1, 2, 4, 5)
    scratch = [
        pltpu.VMEM((RET_HEADS, RET_DK, RET_DV), F32),
        pltpu.VMEM((CONV_PAD + tm, CONV_W), F32),
        pltpu.VMEM((tm, CONV_W), F32),
        pltpu.VMEM((tm, RET_W), BF16),
        pltpu.VMEM((tm, RET_W), BF16),
        pltpu.VMEM((tm, RET_W), BF16),
        pltpu.VMEM((tm, RET_W), BF16),
        pltpu.VMEM((tm, RET_W), BF16),
        pltpu.VMEM((tm, RET_W), F32),
        pltpu.VMEM((tm, CONV_W + 2 * XA_W), F32),
        pltpu.VMEM((tm, D_MIX), BF16),
        pltpu.VMEM((CONV_W // LANES, SUBLANES * n_tok, LANES), F32),
        pltpu.VMEM((CONV_W // LANES, SUBLANES * n_tok, LANES), F32),
    ]
    return pl.pallas_call(
        functools.partial(_layer_kernel, layer, final, n_tiles, n_alias, n_tok, n_seq, layout),
        grid=(batch, n_tiles),
        in_specs=in_specs,
        out_specs=out_specs,
        out_shape=out_shape,
        scratch_shapes=scratch,
        input_output_aliases={n_fixed + k: state_outputs[k] for k in range(n_alias)},
        compiler_params=_params(dimension_semantics=("arbitrary", "arbitrary")),
        name=f"layer{layer}",
    )(cd, x, cos_sin, mk, mv, ng, win, gn, cw, cb, lng, lnb, xan, wout, tables, fng,
      z_s, z_s, s0, cbuf, mkt_s, mvt_s, *prev_states)


def _project_and_cast(first_step, hb_scr, win_ref, wout_ref, z_ref, winb_ref, woutb_ref):
    @pl.when(first_step)
    def _():
        woutb_ref[...] = wout_ref[...].astype(BF16)

    w_block = win_ref[...].astype(BF16)
    winb_ref[...] = w_block
    z_ref[...] = _dot(hb_scr[...], w_block)


def _sample_in_kernel(layer, x_ref, ng_ref, win_ref, wout_ref, z_ref, winb_ref, woutb_ref,
                      hb_scr):
    first_step = pl.program_id(0) == 0

    @pl.when(first_step)
    def _():
        x = x_ref[...].reshape(-1, D_MODEL)
        hb_scr[...] = _rms(x, _layer_row(ng_ref, layer)).astype(BF16)

    _project_and_cast(first_step, hb_scr, win_ref, wout_ref, z_ref, winb_ref, woutb_ref)


def _weight_cast_specs(layer, cols):
    in_specs = [pl.BlockSpec((None, D_MODEL, cols), lambda j: (layer, 0, j)),
                pl.BlockSpec((None, D_MIX, D_MODEL), lambda j: (layer, 0, 0))]
    out_specs = [pl.BlockSpec((D_MODEL, cols), lambda j: (0, j)),
                 pl.BlockSpec((D_MIX, D_MODEL), lambda j: (0, 0))]
    out_shape = [jax.ShapeDtypeStruct((D_MODEL, D_IN), BF16),
                 jax.ShapeDtypeStruct((D_MIX, D_MODEL), BF16)]
    return in_specs, out_specs, out_shape


def _sample_in(layer, x, ng, win, wout):
    n = x.size // D_MODEL
    cols = D_IN // SAMPLE_PROJ_STEPS
    whole = lambda a: pl.BlockSpec(a.shape, lambda j: (0,) * a.ndim)
    w_in_specs, w_out_specs, w_out_shape = _weight_cast_specs(layer, cols)
    return pl.pallas_call(
        functools.partial(_sample_in_kernel, layer),
        grid=(SAMPLE_PROJ_STEPS,),
        in_specs=[whole(x), whole(ng)] + w_in_specs,
        out_specs=[pl.BlockSpec((n, cols), lambda j: (0, j))] + w_out_specs,
        out_shape=[jax.ShapeDtypeStruct((n, D_IN), F32)] + w_out_shape,
        scratch_shapes=[pltpu.VMEM((n, D_MODEL), BF16)],
        compiler_params=_params(dimension_semantics=("arbitrary",)),
        name="sample_in",
    )(x, ng, win, wout)


def _sample_mid_kernel(layer, x_ref, mix_ref, woutb_prev_ref, ng_ref, win_ref, wout_ref,
                       h_ref, z_ref, winb_ref, woutb_ref, hb_scr):
    first_step = pl.program_id(0) == 0

    @pl.when(first_step)
    def _():
        h = x_ref[...].reshape(-1, D_MODEL) + _dot(mix_ref[...], woutb_prev_ref[...])
        h_ref[...] = h
        hb_scr[...] = _rms(h, _layer_row(ng_ref, layer + 1)).astype(BF16)

    _project_and_cast(first_step, hb_scr, win_ref, wout_ref, z_ref, winb_ref, woutb_ref)


def _sample_mid(layer, x, mix, woutb_prev, ng, win, wout):
    n = x.size // D_MODEL
    cols = D_IN // SAMPLE_PROJ_STEPS
    whole = lambda a: pl.BlockSpec(a.shape, lambda j: (0,) * a.ndim)
    w_in_specs, w_out_specs, w_out_shape = _weight_cast_specs(layer + 1, cols)
    return pl.pallas_call(
        functools.partial(_sample_mid_kernel, layer),
        grid=(SAMPLE_PROJ_STEPS,),
        in_specs=[whole(x), whole(mix), whole(woutb_prev), whole(ng)] + w_in_specs,
        out_specs=[pl.BlockSpec((n, D_MODEL), lambda j: (0, 0)),
                   pl.BlockSpec((n, cols), lambda j: (0, j))] + w_out_specs,
        out_shape=[jax.ShapeDtypeStruct((n, D_MODEL), F32),
                   jax.ShapeDtypeStruct((n, D_IN), F32)] + w_out_shape,
        scratch_shapes=[pltpu.VMEM((n, D_MODEL), BF16)],
        compiler_params=_params(dimension_semantics=("arbitrary",)),
        name="sample_mid",
    )(x, mix, woutb_prev, ng, win, wout)


def _sample_out_kernel(x_ref, mix_ref, wout_ref, fng_ref, y_ref):
    x = x_ref[...].reshape(-1, D_MODEL)
    y = _rms(x + _dot(mix_ref[...], wout_ref[...]), fng_ref[...])
    y_ref[...] = y.reshape(y_ref.shape)


def _sample_out(x, mix, wout, fng, out_shape):
    whole = lambda a: pl.BlockSpec(a.shape, lambda j: (0,) * a.ndim)
    return pl.pallas_call(
        _sample_out_kernel,
        grid=(1,),
        in_specs=[whole(x), whole(mix), whole(wout), whole(fng)],
        out_specs=pl.BlockSpec(out_shape, lambda j: (0,) * len(out_shape)),
        out_shape=jax.ShapeDtypeStruct(out_shape, F32),
        compiler_params=_params(dimension_semantics=("arbitrary",)),
        name="sample_out",
    )(x, mix, wout, fng)


def _sample_mix_stages(layer, n_tok, n_seq, own_row0,
                       cd_ref, z_ref, zc_ref, s0_ref, cbuf_ref, mkt_ref, mvt_ref, cos_ref,
                       sin_ref, gn_ref, cw_ref, cb_ref, lng_ref, lnb_ref, xan_ref,
                       dec_ref, wq_ref, wk_ref, mix_ref, sret_ref, sconv_ref, us_scr, cs_scr):
    grp = SUBLANES // n_tok
    rows = grp * n_tok
    cosf = cos_ref[...]
    sins = sin_ref[...]
    head = _head_lane_ids()
    row_id = lax.broadcasted_iota(jnp.int32, (rows, 1), 0)
    row_seq = row_id // n_tok
    row4_seq = lax.broadcasted_iota(jnp.int32, (XA_HEADS * rows, 1), 0) % rows // n_tok

    def pick(parts, seq_of_row):
        out = parts[0]
        for s in range(1, grp):
            out = jnp.where(seq_of_row == s, parts[s], out)
        return out

    groups = range(n_seq // grp)
    heads = range(RET_HEADS)
    seqs = [[g * grp + s for s in range(grp)] for g in groups]
    rs = [slice(rows * g, rows * (g + 1)) for g in groups]
    hcol = lambda c0, hh: slice(c0 + RET_DK * hh, c0 + RET_DK * (hh + 1))

    qh, kf, vh, sc, xsc = {}, {}, {}, {}, {}
    inner, cross, xo, xsum = {}, {}, {}, {}

    def first_matmuls():
        for g in groups:
            for hh in heads:
                qh[g, hh] = _rope(z_ref[rs[g], hcol(C_RQ, hh)], cosf, sins).astype(BF16)
                kf[g, hh] = _rope(z_ref[rs[g], hcol(C_RK, hh)], cosf, sins) * (RET_DK ** -0.5)
                vh[g, hh] = z_ref[rs[g], hcol(C_RV, hh)].astype(BF16)
                sc[g, hh] = _dot_nt(qh[g, hh], kf[g, hh].astype(BF16))
            xq = z_ref[rs[g], C_XQ:C_XQ + XA_W] * (LOG2_E * XA_DH ** -0.5)
            q4 = jnp.concatenate([jnp.where(head == hh, xq, 0.0) for hh in range(XA_HEADS)],
                                 axis=0).astype(BF16)
            xsc[g] = [_dot(q4, mkt_ref[0, b].astype(BF16)) for b in seqs[g]]

    def conv_module():
        pieces = range(CONV_W // LANES)
        lanes = lambda p: slice(LANES * p, LANES * (p + 1))
        u_rows = zc_ref[:, 0:CONV_W] * _sigmoid(zc_ref[:, CONV_W:2 * CONV_W])
        for p in pieces:
            us_scr[p] = u_rows[:, lanes(p)]
        u = []
        for i in range(n_tok):
            tok = pl.ds(i, SUBLANES, stride=n_tok)
            u.append(jnp.concatenate([us_scr[p, tok, :] for p in pieces], axis=-1))
        window = lambda k: cbuf_ref[k] if k < CONV_HIST else u[k - CONV_HIST]
        acc = [_layer_row(cb_ref, layer)] * n_tok
        for k in range(CONV_HIST + n_tok):
            w_k = window(k)
            for i in range(n_tok):
                if 0 <= k - i < CONV_K:
                    acc[i] = acc[i] + w_k * cw_ref[k - i:k - i + 1, :]
        for i in range(n_tok):
            for p in pieces:
                cs_scr[p, pl.ds(i, SUBLANES, stride=n_tok), :] = acc[i][:, lanes(p)]
        for k in range(CONV_HIST):
            sconv_ref[k] = window(k + n_tok)
        for g in groups:
            own = pl.ds(pl.multiple_of(own_row0 + rows * g, rows), rows)
            c = jnp.concatenate([cs_scr[p, own, :] for p in pieces], axis=-1)
            cn = _standardize(c) * _layer_row(lng_ref, layer) + _layer_row(lnb_ref, layer)
            gate = _silu(z_ref[rs[g], C_CG:C_CG + CONV_W])
            mix_ref[rs[g], RET_W:RET_W + CONV_W] = (_silu(cn) * gate).astype(BF16)

    def second_matmuls():
        for g in groups:
            for hh in heads:
                inner[g, hh] = _dot((sc[g, hh] * dec_ref[hh]).astype(BF16), vh[g, hh])
                kw = kf[g, hh] * wk_ref[hh]
                parts = []
                for s, b in enumerate(seqs[g]):
                    s_prev = s0_ref[0, b, hh]
                    parts.append(_dot(qh[g, hh], s_prev.astype(BF16)))
                    kv = _dot_tn(jnp.where(row_seq == s, kw, 0.0).astype(BF16), vh[g, hh])
                    sret_ref[0, b, hh] = cd_ref[hh] * s_prev + kv
                cross[g, hh] = pick(parts, row_seq) * wq_ref[hh]
            s4 = pick(xsc[g], row4_seq)
            e = jnp.exp2(s4 - jnp.max(s4, axis=-1, keepdims=True))
            xsum[g] = jnp.sum(e, axis=-1, keepdims=True)
            eb = e.astype(BF16)
            xo[g] = [_dot_nt(eb, mvt_ref[0, b].astype(BF16)) for b in seqs[g]]

    def norms_and_stores():
        for g in groups:
            for hh in heads:
                co = hcol(0, hh)
                o = _standardize(inner[g, hh] + cross[g, hh]) * gn_ref[layer:layer + 1, co]
                gate = _silu(z_ref[rs[g], hcol(C_RG, hh)])
                mix_ref[rs[g], co] = (o * gate).astype(BF16)
            o4 = pick(xo[g], row4_seq) / xsum[g]
            a = jnp.zeros((rows, XA_W), F32)
            for hh in range(XA_HEADS):
                a = jnp.where(head == hh, o4[rows * hh:rows * (hh + 1), :], a)
            gate = _silu(z_ref[rs[g], C_XG:C_XG + XA_W])
            mix_ref[rs[g], RET_W + CONV_W:D_MIX] = _xattn_norm_gate(
                a, _layer_row(xan_ref, layer), gate, rows).astype(BF16)

    return [first_matmuls, conv_module, second_matmuls, norms_and_stores]


def _rope_tables(pos):
    half = RET_DK // 2
    inv = np.float64(ROPE_BASE) ** (-np.arange(half, dtype=np.float64) / half)
    ang = pos.astype(np.float64)[:, None] * inv[None, :]
    cos, sin = np.cos(ang), np.sin(ang)
    return (np.concatenate([cos, cos], axis=-1).astype(np.float32),
            np.concatenate([-sin, sin], axis=-1).astype(np.float32))


def _decay_tables(chunk):
    lg = np.log(1.0 - np.exp2(-5.0 - np.arange(RET_HEADS, dtype=np.float64)))
    idx = np.arange(chunk, dtype=np.float64)
    diff = idx[:, None] - idx[None, :]
    dec = np.where(diff[None] >= 0, np.exp(np.maximum(diff, 0.0)[None] * lg[:, None, None]), 0.0)
    wk = np.exp((chunk - 1.0 - idx)[None, :] * lg[:, None])
    wq = np.exp((idx + 1.0)[None, :] * lg[:, None])
    cd = np.exp(chunk * lg)
    wk = np.broadcast_to(wk[:, :, None], (RET_HEADS, chunk, RET_DK))
    wq = np.broadcast_to(wq[:, :, None], (RET_HEADS, chunk, RET_DV))
    f32 = lambda a: np.ascontiguousarray(a, dtype=np.float32)
    return f32(dec), f32(wq), f32(wk), f32(cd)


def _group_tables(n_tok, pos0):
    grp = SUBLANES // n_tok
    cosf, sins = _rope_tables(pos0 + np.arange(n_tok))
    dec, wq, wk, cd = _decay_tables(n_tok)
    tile_rows = lambda a: np.concatenate([a] * grp, axis=-2)
    eye = np.eye(grp, dtype=np.float32)
    dec = np.einsum("st,hij->hsitj", eye, dec).reshape(RET_HEADS, grp * n_tok, grp * n_tok)
    return tile_rows(cosf), tile_rows(sins), dec, tile_rows(wq), tile_rows(wk), cd


def kernel(x_prompt, x_sample, mem_prompt, state_ret, state_conv, cache_mem_k, cache_mem_v,
           norm_g, w_in, ret_gn_g, conv_w, conv_b, conv_ln_g, conv_ln_b, xa_norm_g,
           mem_norm_g, w_mk, w_mv, w_out, final_norm_g):
    depth = w_in.shape[0]
    batch, seq, _ = x_prompt.shape
    dbatch, dseq, _ = x_sample.shape

    fng = final_norm_g.reshape(1, D_MODEL)

    p_mk, p_mv = _mem_kv(mem_prompt, mem_norm_g, w_mk, w_mv)
    cos_p, sin_p = _rope_tables(np.arange(seq))
    dec_p, wq_p, wk_p, cd_p = _decay_tables(RET_CHUNK if seq % RET_CHUNK == 0 else seq)

    assert SUBLANES % dseq == 0 and dseq % RET_CHUNK != 0
    cos_s, sin_s, dec_s, wq_s, wk_s, cd_s = _group_tables(dseq, PAST_LEN)
    tables, layout = _pack_tables([
        ("dec", dec_p), ("wq", wq_p), ("wk", wk_p), ("sample cos", cos_s[None]),
        ("sample sin", sin_s[None]), ("sample dec", dec_s), ("sample wq", wq_s),
        ("sample wk", wk_s)])
    cos_sin = np.concatenate([cos_p, sin_p], axis=-1)
    cd = np.concatenate([cd_p, cd_s])
    to_hd_m = lambda c: c.transpose(0, 1, 3, 4, 2).reshape(depth, dbatch, XA_W, N_MEM)
    mkt_s, mvt_s = to_hd_m(cache_mem_k), to_hd_m(cache_mem_v)
    conv_s = state_conv.transpose(0, 2, 1, 3)
    hs = x_sample
    z, w_in_b, w_out_b = _sample_in(0, hs, norm_g, w_in, w_out)

    hp = x_prompt
    states = ()
    y_sample = None
    for l in range(depth):
        hp, p_ret, p_conv, mix, s_ret, s_conv = _layer(
            l, l == depth - 1, dseq, hp, cos_sin, p_mk, p_mv, norm_g, w_in_b, ret_gn_g,
            conv_w, conv_b, conv_ln_g, conv_ln_b, xa_norm_g, w_out_b, tables, layout, cd,
            fng, z, state_ret, conv_s, mkt_s, mvt_s, states)
        states = (p_ret, p_conv, s_ret, s_conv)
        if l + 1 < depth:
            hs, z, w_in_b, w_out_b = _sample_mid(l, hs, mix, w_out_b, norm_g, w_in, w_out)
        else:
            y_sample = _sample_out(hs, mix, w_out_b, fng, x_sample.shape)
    y_prompt = hp

    from_hd_m = lambda c: c.reshape(depth, batch, XA_HEADS, XA_DH, N_MEM).transpose(0, 1, 4, 2, 3)
    return (y_prompt, y_sample, states[0], states[1],
            from_hd_m(p_mk), from_hd_m(p_mv), states[2], states[3].transpose(0, 2, 1, 3))
```

```python
import functools

import numpy as np

import jax
import jax.numpy as jnp
from jax import lax
from jax.experimental import pallas as pl
from jax.experimental.pallas import tpu as pltpu

F32 = jnp.float32
BF16 = jnp.bfloat16

D_MODEL = 1024
N_MEM = 256
RET_HEADS = 4
RET_DK = 128
RET_DV = 128
RET_W = RET_HEADS * RET_DV
RET_CHUNK = 128
CONV_W = 256
CONV_K = 31
XA_HEADS = 4
XA_DH = 64
XA_W = XA_HEADS * XA_DH
D_MIX = RET_W + CONV_W + XA_W
ROPE_BASE = 10000.0
EPS = 1e-6
PAST_LEN = 16384
LOG2_E = 1.4426950408889634

C_RQ, C_RK, C_RV, C_RG = 0, 512, 1024, 1536
C_CA, C_CB, C_CG = 2048, 2304, 2560
C_XQ, C_XG = 2816, 3072
D_IN = 3328

VMEM_LIMIT_BYTES = 56 * 1024 * 1024
SUBLANES = 8
LANES = 128
CONV_PAD = 32
CONV_HIST = CONV_K - 1
PROMPT_TILE = 512
CONV_ROWS = 64
PROJ_BLOCK = 256
MEM_BATCHES = 2
RET_STAGE_CHUNKS = 2
SAMPLE_PROJ_STEPS = 2


def _rms(x, g):
    return x * lax.rsqrt(jnp.mean(x * x, axis=-1, keepdims=True) + EPS) * g


def _standardize(x):
    mu = jnp.mean(x, axis=-1, keepdims=True)
    d = x - mu
    var = jnp.mean(d * d, axis=-1, keepdims=True)
    return d * lax.rsqrt(var + EPS)


def _sigmoid(x):
    return 1.0 / (1.0 + jnp.exp(-x))


def _silu(x):
    return x * _sigmoid(x)


def _dot(a, b):
    return jnp.dot(a, b, preferred_element_type=F32)


def _dot_nt(a, b):
    return lax.dot_general(a, b, (((1,), (1,)), ((), ())), preferred_element_type=F32)


def _dot_tn(a, b):
    return lax.dot_general(a, b, (((0,), (0,)), ((), ())), preferred_element_type=F32)


def _rope(x, cosf, sins):
    return x * cosf + pltpu.roll(x, RET_DK // 2, 1) * sins


def _zero_row_after(z):
    tile = z[z.shape[0] - SUBLANES:, z.shape[1] - LANES:]
    bits = pltpu.bitcast(tile, jnp.uint32)
    sixteen = jnp.uint32(16)
    zero = lax.shift_right_logical(lax.shift_right_logical(bits, sixteen), sixteen)
    row = pltpu.bitcast(zero, F32)[0:1, :]
    return jnp.concatenate([row, row], axis=1)


def _head_lane_ids():
    return lax.broadcasted_iota(jnp.int32, (1, XA_W), 1) // XA_DH


def _xattn_norm_gate(a, xan, gate, rows, denoms=None):
    head = _head_lane_ids()
    a2 = a * a
    ms = jnp.zeros((rows, XA_W), F32)
    for hh in range(XA_HEADS):
        m = head == hh
        ssq = jnp.sum(jnp.where(m, a2, 0.0), axis=-1, keepdims=True) * (1.0 / XA_DH)
        ssq = ssq + (EPS if denoms is None else EPS * denoms[hh] * denoms[hh])
        ms = jnp.where(m, ssq, ms)
    return a * lax.rsqrt(ms) * xan * gate


def _layer_row(ref, layer):
    return ref[layer:layer + 1, :]


def _params(**kw):
    return pltpu.CompilerParams(vmem_limit_bytes=VMEM_LIMIT_BYTES, **kw)


def _mem_kv_kernel(mem_ref, g_ref, wk_ref, wv_ref, kt_ref, vt_ref, wkt_scr, wvt_scr):
    depth = g_ref.shape[0]

    @pl.when(pl.program_id(0) == 0)
    def _():
        for l in range(depth):
            wkt_scr[l] = wk_ref[l].T.astype(BF16)
            wvt_scr[l] = wv_ref[l].T.astype(BF16)

    for b in range(mem_ref.shape[0]):
        x = mem_ref[b]
        xn = x * lax.rsqrt(jnp.mean(x * x, axis=-1, keepdims=True) + EPS)
        for l in range(depth):
            m = (xn * g_ref[l:l + 1, :]).astype(BF16)
            kt_ref[l, b] = _dot_nt(wkt_scr[l], m)
            vt_ref[l, b] = _dot_nt(wvt_scr[l], m)


def _mem_kv(mem, g, wk, wv):
    depth, batch = g.shape[0], mem.shape[0]
    assert batch % MEM_BATCHES == 0
    whole = lambda a: pl.BlockSpec(a.shape, lambda i: (0,) * a.ndim)
    out = jax.ShapeDtypeStruct((depth, batch, XA_W, N_MEM), F32)
    out_spec = pl.BlockSpec((depth, MEM_BATCHES, XA_W, N_MEM), lambda i: (0, i, 0, 0))
    return pl.pallas_call(
        _mem_kv_kernel,
        grid=(batch // MEM_BATCHES,),
        in_specs=[pl.BlockSpec((MEM_BATCHES, N_MEM, D_MODEL), lambda i: (i, 0, 0)),
                  whole(g), whole(wk), whole(wv)],
        out_specs=[out_spec, out_spec],
        out_shape=[out, out],
        scratch_shapes=[pltpu.VMEM((depth, XA_W, D_MODEL), BF16),
                        pltpu.VMEM((depth, XA_W, D_MODEL), BF16)],
        compiler_params=_params(dimension_semantics=("arbitrary",)),
        name="mem_kv",
    )(mem, g, wk, wv)


class _Blocks:
    def __init__(self, ref, row0, n, rows, lanes):
        self.ref, self.row0, self.n, self.rows, self.lanes = ref, row0, n, rows, lanes

    def __getitem__(self, i):
        if i is Ellipsis:
            return self.ref[self.row0:self.row0 + self.n * self.rows, 0:self.lanes]
        r0 = self.row0 + i * self.rows
        return self.ref[r0:r0 + self.rows, 0:self.lanes]


class _Scalars:
    def __init__(self, ref, first):
        self.ref, self.first = ref, first

    def __getitem__(self, i):
        return self.ref[self.first + i]


def _pack_tables(named):
    parts, layout, row0 = [], {}, 0
    for name, a in named:
        n, rows, lanes = a.shape
        assert rows % SUBLANES == 0 and lanes <= LANES
        wide = np.zeros((n * rows, LANES), np.float32)
        wide[:, :lanes] = a.reshape(n * rows, lanes)
        parts.append(wide)
        layout[name] = (row0, n, rows, lanes)
        row0 += n * rows
    return np.concatenate(parts), layout


def _layer_kernel(layer, final, n_tiles, n_alias, n_tok, n_seq, layout,
                  cd_all_ref, x_ref, cs_ref, mk_ref, mv_ref, ng_ref, win_ref,
                  gn_ref, cw_ref, cb_ref, lng_ref, lnb_ref, xan_ref, wout_ref,
                  tab_ref, fng_ref,
                  sz_ref, szc_ref, ss0_ref, scbuf_ref, smkt_ref, smvt_ref, *refs):
    (y_ref, sret_ref, sconv_ref, smix_ref, ssret_ref, ssconv_ref,
     s_scr, ext_scr, c_scr, q_scr, qw_scr, k_scr, kw_scr, v_scr, g_scr, xg_scr,
     mix_scr, us_scr, cs_scr) = refs[n_alias:]

    tm = PROMPT_TILE
    t = pl.program_id(1)
    table = lambda name: _Blocks(tab_ref, *layout[name])
    dec_ref, wq_ref, wk_ref = table("dec"), table("wq"), table("wk")
    cd_ref = _Scalars(cd_all_ref, 0)
    step_in_block = (pl.program_id(0) * n_tiles + t) % (SUBLANES // n_seq)
    sample_stages = _sample_mix_stages(
        layer, n_tok, n_seq, step_in_block * (n_seq * n_tok), _Scalars(cd_all_ref, RET_HEADS),
        sz_ref, szc_ref, ss0_ref, scbuf_ref, smkt_ref, smvt_ref, table("sample cos"),
        table("sample sin"), gn_ref, cw_ref, cb_ref, lng_ref, lnb_ref, xan_ref,
        table("sample dec"), table("sample wq"), table("sample wk"), smix_ref, ssret_ref,
        ssconv_ref, us_scr, cs_scr)

    @pl.when(t == 0)
    def _():
        s_scr[...] = jnp.zeros_like(s_scr)
        ext_scr[0:CONV_PAD, :] = jnp.zeros((CONV_PAD, CONV_W), F32)

    sample_stages[0]()
    sample_stages[1]()

    hb = _rms(x_ref[0], _layer_row(ng_ref, layer)).astype(BF16)

    def proj(a, b):
        return _dot(hb, win_ref[:, a:b])

    cosf = cs_ref[:, 0:RET_DK]
    sins = cs_ref[:, RET_DK:2 * RET_DK]

    u = proj(C_CA, C_CA + CONV_W) * _sigmoid(proj(C_CB, C_CB + CONV_W))
    ext_scr[CONV_PAD:CONV_PAD + tm, :] = u

    pb = PROJ_BLOCK

    def post_q(z, c):
        for i in range(pb // RET_DK):
            hh = c // RET_DK + i
            cols = slice(RET_DK * hh, RET_DK * (hh + 1))
            q = _rope(z[:, RET_DK * i:RET_DK * (i + 1)], cosf, sins)
            q_scr[:, cols] = q.astype(BF16)
            for ch in range(tm // RET_CHUNK):
                rows = slice(RET_CHUNK * ch, RET_CHUNK * (ch + 1))
                qw_scr[rows, cols] = (q[rows, :] * wq_ref[hh]).astype(BF16)

    def post_k(z, c):
        for i in range(pb // RET_DK):
            hh = c // RET_DK + i
            cols = slice(RET_DK * hh, RET_DK * (hh + 1))
            k = _rope(z[:, RET_DK * i:RET_DK * (i + 1)], cosf, sins) * (RET_DK ** -0.5)
            k_scr[:, cols] = k.astype(BF16)
            for ch in range(tm // RET_CHUNK):
                rows = slice(RET_CHUNK * ch, RET_CHUNK * (ch + 1))
                kw_scr[rows, cols] = (k[rows, :] * wk_ref[hh]).astype(BF16)

    def post_v(z, c):
        v_scr[:, c:c + pb] = z.astype(BF16)

    def post_g(z, c):
        g_scr[:, c:c + pb] = _silu(z)

    def post_cg(z, c):
        xg_scr[:, c:c + pb] = _silu(z)

    def post_xq(z, c):
        xg_scr[:, CONV_W + c:CONV_W + c + pb] = z

    def post_xg(z, c):
        xg_scr[:, CONV_W + XA_W + c:CONV_W + XA_W + c + pb] = _silu(z)

    blocks = []
    for col0, width, post in ((C_RQ, RET_W, post_q), (C_RK, RET_W, post_k),
                              (C_RV, RET_W, post_v), (C_RG, RET_W, post_g),
                              (C_CG, CONV_W, post_cg), (C_XQ, XA_W, post_xq),
                              (C_XG, XA_W, post_xg)):
        blocks += [(col0, c, post) for c in range(0, width, pb)]

    first = CONV_PAD - CONV_HIST
    conv_acc = {}
    conv_bias = _layer_row(cb_ref, layer)

    def tap_group(c0, r, wait_zero):
        n_rows = CONV_ROWS if r == 0 else CONV_ROWS + SUBLANES
        part = None
        for a in range((first + CONV_K - 1 - r) // SUBLANES + 1):
            j = SUBLANES * a + r - first
            if 0 <= j < CONV_K:
                w_j = cw_ref[j:j + 1, :]
                if wait_zero is not None:
                    w_j = w_j + wait_zero
                lo = c0 + SUBLANES * a
                term = ext_scr[lo:lo + n_rows, :] * w_j
                part = term if part is None else part + term
        if r == 0:
            conv_acc[c0] = part + conv_bias
        else:
            conv_acc[c0] = conv_acc[c0] + part[r:r + CONV_ROWS, :]
        if r == SUBLANES - 1:
            c_scr[c0:c0 + CONV_ROWS, :] = conv_acc.pop(c0)

    def conv_tail(lo, wait_zero):
        slab = slice(lo, lo + CONV_ROWS)
        gain = _layer_row(lng_ref, layer)
        if wait_zero is not None:
            gain = gain + wait_zero
        cn = _standardize(c_scr[slab, :]) * gain + _layer_row(lnb_ref, layer)
        mix_scr[slab, RET_W:RET_W + CONV_W] = (_silu(cn) * xg_scr[slab, 0:CONV_W]).astype(BF16)

    taps = [functools.partial(tap_group, c0, r)
            for c0 in range(0, tm, CONV_ROWS) for r in range(SUBLANES)]
    tails = [functools.partial(conv_tail, lo) for lo in range(0, tm, CONV_ROWS)]

    chunks = range(tm // RET_CHUNK)
    heads = range(RET_HEADS)
    crow = lambda c: slice(RET_CHUNK * c, RET_CHUNK * (c + 1))
    hcol = lambda hh: slice(RET_DK * hh, RET_DK * (hh + 1))

    def retention_stages(c0):
        part = chunks[c0:c0 + RET_STAGE_CHUNKS]
        sc, kv, before, out = {}, {}, {}, {}

        def scores_and_updates():
            for c in part:
                for hh in heads:
                    sc[c, hh] = _dot_nt(q_scr[crow(c), hcol(hh)], k_scr[crow(c), hcol(hh)])
                    kv[c, hh] = _dot_tn(kw_scr[crow(c), hcol(hh)], v_scr[crow(c), hcol(hh)])

        def recurrence_and_outputs():
            for hh in heads:
                state = s_scr[hh]
                for c in part:
                    before[c, hh] = state.astype(BF16)
                    state = cd_ref[hh] * state + kv[c, hh]
                s_scr[hh] = state
            for c in part:
                for hh in heads:
                    lhs = jnp.concatenate([(sc[c, hh] * dec_ref[hh]).astype(BF16),
                                           qw_scr[crow(c), hcol(hh)]], axis=1)
                    rhs = jnp.concatenate([v_scr[crow(c), hcol(hh)], before[c, hh]], axis=0)
                    out[c, hh] = _dot(lhs, rhs)

        def norms():
            for c in part:
                for hh in heads:
                    o = _standardize(out[c, hh]) * gn_ref[layer:layer + 1, hcol(hh)]
                    mix_scr[crow(c), hcol(hh)] = (o * g_scr[crow(c), hcol(hh)]).astype(BF16)

        return [scores_and_updates, recurrence_and_outputs, norms]

    n_half = RET_CHUNK * RET_STAGE_CHUNKS
    mkt = mk_ref[0, 0].astype(BF16)
    mvt = mv_ref[0, 0].astype(BF16)
    head = _head_lane_ids()

    def xattn_stages(half):
        scores, pvs, sums = [], [], []

        def query_key():
            q = xg_scr[half, CONV_W:CONV_W + XA_W] * (LOG2_E * XA_DH ** -0.5)
            for hh in range(XA_HEADS):
                scores.append(_dot(jnp.where(head == hh, q, 0.0).astype(BF16), mkt))

        def softmax_value():
            for sc in scores:
                e = jnp.exp2(sc - jnp.max(sc, axis=-1, keepdims=True))
                sums.append(jnp.sum(e, axis=-1, keepdims=True))
                pvs.append(_dot_nt(e.astype(BF16), mvt))

        def norm_gate():
            a = pvs[0]
            for hh in range(1, XA_HEADS):
                a = jnp.where(head == hh, pvs[hh], a)
            gate = xg_scr[half, CONV_W + XA_W:CONV_W + 2 * XA_W]
            mix_scr[half, RET_W + CONV_W:D_MIX] = _xattn_norm_gate(
                a, _layer_row(xan_ref, layer), gate, n_half, denoms=sums).astype(BF16)

        return [query_key, softmax_value, norm_gate]

    halves = [slice(n_half * p, n_half * (p + 1)) for p in range(tm // n_half)]
    pending = [stage for p in range(len(halves))
               for stage in retention_stages(RET_STAGE_CHUNKS * p)]
    gate_block = [post for _, _, post in blocks].index(post_cg)
    late_blocks = len(blocks) - gate_block
    taps_per = -(-len(taps) // gate_block)
    tails_per = -(-len(tails) // (late_blocks - 1))
    stages_per = len(pending) // late_blocks
    z_prev = None
    for k, (col0, c, post) in enumerate(blocks):
        if k < gate_block:
            items = taps[taps_per * k:taps_per * (k + 1)]
        elif k == gate_block:
            items = []
        else:
            j = k - gate_block - 1
            items = tails[tails_per * j:tails_per * (j + 1)]
        for n, item in enumerate(items):
            slab_end = tm * (n + 1) // len(items) // SUBLANES * SUBLANES
            item(None if z_prev is None else _zero_row_after(z_prev[0:slab_end, :]))
        if k >= gate_block:
            for stage in pending[:stages_per]:
                stage()
            pending = pending[stages_per:]
        z_prev = proj(col0 + c, col0 + c + pb)
        post(z_prev, c)

    sample_stages[2]()
    for stage in pending:
        stage()
    for p, half in enumerate(halves):
        for stage in xattn_stages(half):
            stage()
        y = x_ref[0, half, :] + _dot(mix_scr[half, :], wout_ref[...])
        if final:
            y = _rms(y, fng_ref[...])
        y_ref[0, half, :] = y
        if p == 0:
            sample_stages[3]()

    @pl.when(t == n_tiles - 1)
    def _():
        sconv_ref[0, 0] = ext_scr[tm + first:tm + CONV_PAD, :]
        sret_ref[0, 0] = s_scr[...]

    ext_scr[0:CONV_PAD, :] = ext_scr[tm:tm + CONV_PAD, :]


def _layer(layer, final, n_tok, x, cos_sin, mk, mv, ng, win, gn, cw, cb, lng, lnb, xan,
           wout, tables, layout, cd, fng, z_s, s0, cbuf, mkt_s, mvt_s, prev_states):
    batch, seq, _ = x.shape
    depth, sbatch = s0.shape[0], s0.shape[1]
    tm = PROMPT_TILE
    n_tiles = seq // tm
    n_seq = sbatch // (batch * n_tiles)
    assert n_seq * batch * n_tiles == sbatch and (n_seq * n_tok) % (2 * SUBLANES) == 0
    assert SUBLANES % n_seq == 0 and C_CA % (2 * CONV_W) == 0 and C_CB == C_CA + CONV_W
    whole = lambda a: pl.BlockSpec(a.shape, lambda b, t: (0,) * a.ndim)
    step = lambda b, t: b * n_tiles + t
    conv_block = lambda b, t: step(b, t) // (SUBLANES // n_seq)
    conv_spec = pl.BlockSpec((None, CONV_HIST, SUBLANES, CONV_W),
                             lambda b, t: (layer, 0, conv_block(b, t), 0))
    sample_specs = [
        pl.BlockSpec((n_seq * n_tok, D_IN), lambda b, t: (step(b, t), 0)),
        pl.BlockSpec((SUBLANES * n_tok, 2 * CONV_W),
                     lambda b, t: (conv_block(b, t), C_CA // (2 * CONV_W))),
        pl.BlockSpec((1, n_seq, RET_HEADS, RET_DK, RET_DV),
                     lambda b, t: (layer, step(b, t), 0, 0, 0)),
        conv_spec,
        pl.BlockSpec((1, n_seq, XA_W, N_MEM), lambda b, t: (layer, step(b, t), 0, 0)),
        pl.BlockSpec((1, n_seq, XA_W, N_MEM), lambda b, t: (layer, step(b, t), 0, 0)),
    ]
    in_specs = [
        pl.BlockSpec(memory_space=pltpu.SMEM),
        pl.BlockSpec((1, tm, D_MODEL), lambda b, t: (b, t, 0)),
        pl.BlockSpec((tm, 2 * RET_DK), lambda b, t: (t, 0)),
        pl.BlockSpec((1, 1, XA_W, N_MEM), lambda b, t: (layer, b, 0, 0)),
        pl.BlockSpec((1, 1, XA_W, N_MEM), lambda b, t: (layer, b, 0, 0)),
        whole(ng),
        whole(win),
        whole(gn),
        pl.BlockSpec((None, CONV_K, CONV_W), lambda b, t: (layer, 0, 0)),
        whole(cb), whole(lng), whole(lnb),
        whole(xan),
        whole(wout),
        whole(tables),
        whole(fng),
    ] + sample_specs
    n_fixed = len(in_specs)
    n_alias = len(prev_states)
    in_specs += [pl.BlockSpec(memory_space=pl.ANY)] * n_alias
    out_specs = [
        pl.BlockSpec((1, tm, D_MODEL), lambda b, t: (b, t, 0)),
        pl.BlockSpec((1, 1, RET_HEADS, RET_DK, RET_DV), lambda b, t: (layer, b, 0, 0, 0)),
        pl.BlockSpec((1, 1, CONV_HIST, CONV_W), lambda b, t: (layer, b, 0, 0)),
        pl.BlockSpec((n_seq * n_tok, D_MIX), lambda b, t: (step(b, t), 0)),
        pl.BlockSpec((1, n_seq, RET_HEADS, RET_DK, RET_DV),
                     lambda b, t: (layer, step(b, t), 0, 0, 0)),
        conv_spec,
    ]
    out_shape = [
        jax.ShapeDtypeStruct((batch, seq, D_MODEL), F32),
        jax.ShapeDtypeStruct((depth, batch, RET_HEADS, RET_DK, RET_DV), F32),
        jax.ShapeDtypeStruct((depth, batch, CONV_HIST, CONV_W), F32),
        jax.ShapeDtypeStruct((sbatch * n_tok, D_MIX), BF16),
        jax.ShapeDtypeStruct((depth, sbatch, RET_HEADS, RET_DK, RET_DV), F32),
        jax.ShapeDtypeStruct((depth, CONV_HIST, sbatch, CONV_W), F32),
    ]
    state_outputs = (1, 2, 4, 5)
    scratch = [
        pltpu.VMEM((RET_HEADS, RET_DK, RET_DV), F32),
        pltpu.VMEM((CONV_PAD + tm, CONV_W), F32),
        pltpu.VMEM((tm, CONV_W), F32),
        pltpu.VMEM((tm, RET_W), BF16),
        pltpu.VMEM((tm, RET_W), BF16),
        pltpu.VMEM((tm, RET_W), BF16),
        pltpu.VMEM((tm, RET_W), BF16),
        pltpu.VMEM((tm, RET_W), BF16),
        pltpu.VMEM((tm, RET_W), F32),
        pltpu.VMEM((tm, CONV_W + 2 * XA_W), F32),
        pltpu.VMEM((tm, D_MIX), BF16),
        pltpu.VMEM((CONV_W // LANES, SUBLANES * n_tok, LANES), F32),
        pltpu.VMEM((CONV_W // LANES, SUBLANES * n_tok, LANES), F32),
    ]
    return pl.pallas_call(
        functools.partial(_layer_kernel, layer, final, n_tiles, n_alias, n_tok, n_seq, layout),
        grid=(batch, n_tiles),
        in_specs=in_specs,
        out_specs=out_specs,
        out_shape=out_shape,
        scratch_shapes=scratch,
        input_output_aliases={n_fixed + k: state_outputs[k] for k in range(n_alias)},
        compiler_params=_params(dimension_semantics=("arbitrary", "arbitrary")),
        name=f"layer{layer}",
    )(cd, x, cos_sin, mk, mv, ng, win, gn, cw, cb, lng, lnb, xan, wout, tables, fng,
      z_s, z_s, s0, cbuf, mkt_s, mvt_s, *prev_states)


def _project_and_cast(hb_scr, win_ref, wout_ref, z_ref, winb_ref, woutb_ref):
    woutb_ref[...] = wout_ref[...].astype(BF16)
    w_block = win_ref[...].astype(BF16)
    winb_ref[...] = w_block
    z_ref[...] = _dot(hb_scr[...], w_block)


def _sample_in_kernel(layer, x_ref, ng_ref, win_ref, wout_ref, z_ref, winb_ref, woutb_ref,
                      hb_scr):
    first_step = pl.program_id(0) == 0

    @pl.when(first_step)
    def _():
        x = x_ref[...].reshape(-1, D_MODEL)
        hb_scr[...] = _rms(x, _layer_row(ng_ref, layer)).astype(BF16)

    _project_and_cast(hb_scr, win_ref, wout_ref, z_ref, winb_ref, woutb_ref)


def _weight_cast_specs(layer, cols):
    rows = D_MIX // SAMPLE_PROJ_STEPS
    in_specs = [pl.BlockSpec((None, D_MODEL, cols), lambda j: (layer, 0, j)),
                pl.BlockSpec((None, rows, D_MODEL), lambda j: (layer, j, 0))]
    out_specs = [pl.BlockSpec((D_MODEL, cols), lambda j: (0, j)),
                 pl.BlockSpec((rows, D_MODEL), lambda j: (j, 0))]
    out_shape = [jax.ShapeDtypeStruct((D_MODEL, D_IN), BF16),
                 jax.ShapeDtypeStruct((D_MIX, D_MODEL), BF16)]
    return in_specs, out_specs, out_shape


def _sample_in(layer, x, ng, win, wout):
    n = x.size // D_MODEL
    cols = D_IN // SAMPLE_PROJ_STEPS
    whole = lambda a: pl.BlockSpec(a.shape, lambda j: (0,) * a.ndim)
    w_in_specs, w_out_specs, w_out_shape = _weight_cast_specs(layer, cols)
    return pl.pallas_call(
        functools.partial(_sample_in_kernel, layer),
        grid=(SAMPLE_PROJ_STEPS,),
        in_specs=[whole(x), whole(ng)] + w_in_specs,
        out_specs=[pl.BlockSpec((n, cols), lambda j: (0, j))] + w_out_specs,
        out_shape=[jax.ShapeDtypeStruct((n, D_IN), F32)] + w_out_shape,
        scratch_shapes=[pltpu.VMEM((n, D_MODEL), BF16)],
        compiler_params=_params(dimension_semantics=("arbitrary",)),
        name="sample_in",
    )(x, ng, win, wout)


def _sample_mid_kernel(layer, x_ref, mix_ref, woutb_prev_ref, ng_ref, win_ref, wout_ref,
                       h_ref, z_ref, winb_ref, woutb_ref, hb_scr):
    first_step = pl.program_id(0) == 0

    @pl.when(first_step)
    def _():
        h = x_ref[...].reshape(-1, D_MODEL) + _dot(mix_ref[...], woutb_prev_ref[...])
        h_ref[...] = h
        hb_scr[...] = _rms(h, _layer_row(ng_ref, layer + 1)).astype(BF16)

    _project_and_cast(hb_scr, win_ref, wout_ref, z_ref, winb_ref, woutb_ref)


def _sample_mid(layer, x, mix, woutb_prev, ng, win, wout):
    n = x.size // D_MODEL
    cols = D_IN // SAMPLE_PROJ_STEPS
    whole = lambda a: pl.BlockSpec(a.shape, lambda j: (0,) * a.ndim)
    w_in_specs, w_out_specs, w_out_shape = _weight_cast_specs(layer + 1, cols)
    return pl.pallas_call(
        functools.partial(_sample_mid_kernel, layer),
        grid=(SAMPLE_PROJ_STEPS,),
        in_specs=[whole(x), whole(mix), whole(woutb_prev), whole(ng)] + w_in_specs,
        out_specs=[pl.BlockSpec((n, D_MODEL), lambda j: (0, 0)),
                   pl.BlockSpec((n, cols), lambda j: (0, j))] + w_out_specs,
        out_shape=[jax.ShapeDtypeStruct((n, D_MODEL), F32),
                   jax.ShapeDtypeStruct((n, D_IN), F32)] + w_out_shape,
        scratch_shapes=[pltpu.VMEM((n, D_MODEL), BF16)],
        compiler_params=_params(dimension_semantics=("arbitrary",)),
        name="sample_mid",
    )(x, mix, woutb_prev, ng, win, wout)


def _sample_out_kernel(x_ref, mix_ref, wout_ref, fng_ref, y_ref):
    x = x_ref[...].reshape(-1, D_MODEL)
    y = _rms(x + _dot(mix_ref[...], wout_ref[...]), fng_ref[...])
    y_ref[...] = y.reshape(y_ref.shape)


def _sample_out(x, mix, wout, fng, out_shape):
    whole = lambda a: pl.BlockSpec(a.shape, lambda j: (0,) * a.ndim)
    return pl.pallas_call(
        _sample_out_kernel,
        grid=(1,),
        in_specs=[whole(x), whole(mix), whole(wout), whole(fng)],
        out_specs=pl.BlockSpec(out_shape, lambda j: (0,) * len(out_shape)),
        out_shape=jax.ShapeDtypeStruct(out_shape, F32),
        compiler_params=_params(dimension_semantics=("arbitrary",)),
        name="sample_out",
    )(x, mix, wout, fng)


def _sample_mix_stages(layer, n_tok, n_seq, own_row0,
                       cd_ref, z_ref, zc_ref, s0_ref, cbuf_ref, mkt_ref, mvt_ref, cos_ref,
                       sin_ref, gn_ref, cw_ref, cb_ref, lng_ref, lnb_ref, xan_ref,
                       dec_ref, wq_ref, wk_ref, mix_ref, sret_ref, sconv_ref, us_scr, cs_scr):
    grp = SUBLANES // n_tok
    rows = grp * n_tok
    cosf = cos_ref[...]
    sins = sin_ref[...]
    head = _head_lane_ids()
    row_id = lax.broadcasted_iota(jnp.int32, (rows, 1), 0)
    row_seq = row_id // n_tok
    row4_seq = lax.broadcasted_iota(jnp.int32, (XA_HEADS * rows, 1), 0) % rows // n_tok

    def pick(parts, seq_of_row):
        out = parts[0]
        for s in range(1, grp):
            out = jnp.where(seq_of_row == s, parts[s], out)
        return out

    groups = range(n_seq // grp)
    heads = range(RET_HEADS)
    seqs = [[g * grp + s for s in range(grp)] for g in groups]
    rs = [slice(rows * g, rows * (g + 1)) for g in groups]
    hcol = lambda c0, hh: slice(c0 + RET_DK * hh, c0 + RET_DK * (hh + 1))

    qh, kf, vh, sc, xsc = {}, {}, {}, {}, {}
    inner, cross, xo, xsum = {}, {}, {}, {}

    def first_matmuls():
        for g in groups:
            for hh in heads:
                qh[g, hh] = _rope(z_ref[rs[g], hcol(C_RQ, hh)], cosf, sins).astype(BF16)
                kf[g, hh] = _rope(z_ref[rs[g], hcol(C_RK, hh)], cosf, sins) * (RET_DK ** -0.5)
                vh[g, hh] = z_ref[rs[g], hcol(C_RV, hh)].astype(BF16)
                sc[g, hh] = _dot_nt(qh[g, hh], kf[g, hh].astype(BF16))
            xq = z_ref[rs[g], C_XQ:C_XQ + XA_W] * (LOG2_E * XA_DH ** -0.5)
            q4 = jnp.concatenate([jnp.where(head == hh, xq, 0.0) for hh in range(XA_HEADS)],
                                 axis=0).astype(BF16)
            xsc[g] = [_dot(q4, mkt_ref[0, b].astype(BF16)) for b in seqs[g]]

    def conv_module():
        pieces = range(CONV_W // LANES)
        lanes = lambda p: slice(LANES * p, LANES * (p + 1))
        u_rows = zc_ref[:, 0:CONV_W] * _sigmoid(zc_ref[:, CONV_W:2 * CONV_W])
        for p in pieces:
            us_scr[p] = u_rows[:, lanes(p)]
        u = []
        for i in range(n_tok):
            tok = pl.ds(i, SUBLANES, stride=n_tok)
            u.append(jnp.concatenate([us_scr[p, tok, :] for p in pieces], axis=-1))
        window = lambda k: cbuf_ref[k] if k < CONV_HIST else u[k - CONV_HIST]
        acc = [_layer_row(cb_ref, layer)] * n_tok
        for k in range(CONV_HIST + n_tok):
            w_k = window(k)
            for i in range(n_tok):
                if 0 <= k - i < CONV_K:
                    acc[i] = acc[i] + w_k * cw_ref[k - i:k - i + 1, :]
        for i in range(n_tok):
            for p in pieces:
                cs_scr[p, pl.ds(i, SUBLANES, stride=n_tok), :] = acc[i][:, lanes(p)]
        for k in range(CONV_HIST):
            sconv_ref[k] = window(k + n_tok)
        for g in groups:
            own = pl.ds(pl.multiple_of(own_row0 + rows * g, rows), rows)
            c = jnp.concatenate([cs_scr[p, own, :] for p in pieces], axis=-1)
            cn = _standardize(c) * _layer_row(lng_ref, layer) + _layer_row(lnb_ref, layer)
            gate = _silu(z_ref[rs[g], C_CG:C_CG + CONV_W])
            mix_ref[rs[g], RET_W:RET_W + CONV_W] = (_silu(cn) * gate).astype(BF16)

    def second_matmuls():
        for g in groups:
            for hh in heads:
                inner[g, hh] = _dot((sc[g, hh] * dec_ref[hh]).astype(BF16), vh[g, hh])
                kw = kf[g, hh] * wk_ref[hh]
                parts = []
                for s, b in enumerate(seqs[g]):
                    s_prev = s0_ref[0, b, hh]
                    parts.append(_dot(qh[g, hh], s_prev.astype(BF16)))
                    kv = _dot_tn(jnp.where(row_seq == s, kw, 0.0).astype(BF16), vh[g, hh])
                    sret_ref[0, b, hh] = cd_ref[hh] * s_prev + kv
                cross[g, hh] = pick(parts, row_seq) * wq_ref[hh]
            s4 = pick(xsc[g], row4_seq)
            e = jnp.exp2(s4 - jnp.max(s4, axis=-1, keepdims=True))
            xsum[g] = jnp.sum(e, axis=-1, keepdims=True)
            eb = e.astype(BF16)
            xo[g] = [_dot_nt(eb, mvt_ref[0, b].astype(BF16)) for b in seqs[g]]

    def norms_and_stores():
        for g in groups:
            for hh in heads:
                co = hcol(0, hh)
                o = _standardize(inner[g, hh] + cross[g, hh]) * gn_ref[layer:layer + 1, co]
                gate = _silu(z_ref[rs[g], hcol(C_RG, hh)])
                mix_ref[rs[g], co] = (o * gate).astype(BF16)
            o4 = pick(xo[g], row4_seq) / xsum[g]
            a = jnp.zeros((rows, XA_W), F32)
            for hh in range(XA_HEADS):
                a = jnp.where(head == hh, o4[rows * hh:rows * (hh + 1), :], a)
            gate = _silu(z_ref[rs[g], C_XG:C_XG + XA_W])
            mix_ref[rs[g], RET_W + CONV_W:D_MIX] = _xattn_norm_gate(
                a, _layer_row(xan_ref, layer), gate, rows).astype(BF16)

    return [first_matmuls, conv_module, second_matmuls, norms_and_stores]


def _rope_tables(pos):
    half = RET_DK // 2
    inv = np.float64(ROPE_BASE) ** (-np.arange(half, dtype=np.float64) / half)
    ang = pos.astype(np.float64)[:, None] * inv[None, :]
    cos, sin = np.cos(ang), np.sin(ang)
    return (np.concatenate([cos, cos], axis=-1).astype(np.float32),
            np.concatenate([-sin, sin], axis=-1).astype(np.float32))


def _decay_tables(chunk):
    lg = np.log(1.0 - np.exp2(-5.0 - np.arange(RET_HEADS, dtype=np.float64)))
    idx = np.arange(chunk, dtype=np.float64)
    diff = idx[:, None] - idx[None, :]
    dec = np.where(diff[None] >= 0, np.exp(np.maximum(diff, 0.0)[None] * lg[:, None, None]), 0.0)
    wk = np.exp((chunk - 1.0 - idx)[None, :] * lg[:, None])
    wq = np.exp((idx + 1.0)[None, :] * lg[:, None])
    cd = np.exp(chunk * lg)
    wk = np.broadcast_to(wk[:, :, None], (RET_HEADS, chunk, RET_DK))
    wq = np.broadcast_to(wq[:, :, None], (RET_HEADS, chunk, RET_DV))
    f32 = lambda a: np.ascontiguousarray(a, dtype=np.float32)
    return f32(dec), f32(wq), f32(wk), f32(cd)


def _group_tables(n_tok, pos0):
    grp = SUBLANES // n_tok
    cosf, sins = _rope_tables(pos0 + np.arange(n_tok))
    dec, wq, wk, cd = _decay_tables(n_tok)
    tile_rows = lambda a: np.concatenate([a] * grp, axis=-2)
    eye = np.eye(grp, dtype=np.float32)
    dec = np.einsum("st,hij->hsitj", eye, dec).reshape(RET_HEADS, grp * n_tok, grp * n_tok)
    return tile_rows(cosf), tile_rows(sins), dec, tile_rows(wq), tile_rows(wk), cd


def kernel(x_prompt, x_sample, mem_prompt, state_ret, state_conv, cache_mem_k, cache_mem_v,
           norm_g, w_in, ret_gn_g, conv_w, conv_b, conv_ln_g, conv_ln_b, xa_norm_g,
           mem_norm_g, w_mk, w_mv, w_out, final_norm_g):
    depth = w_in.shape[0]
    batch, seq, _ = x_prompt.shape
    dbatch, dseq, _ = x_sample.shape

    fng = final_norm_g.reshape(1, D_MODEL)

    p_mk, p_mv = _mem_kv(mem_prompt, mem_norm_g, w_mk, w_mv)
    cos_p, sin_p = _rope_tables(np.arange(seq))
    dec_p, wq_p, wk_p, cd_p = _decay_tables(RET_CHUNK if seq % RET_CHUNK == 0 else seq)

    assert SUBLANES % dseq == 0 and dseq % RET_CHUNK != 0
    cos_s, sin_s, dec_s, wq_s, wk_s, cd_s = _group_tables(dseq, PAST_LEN)
    tables, layout = _pack_tables([
        ("dec", dec_p), ("wq", wq_p), ("wk", wk_p), ("sample cos", cos_s[None]),
        ("sample sin", sin_s[None]), ("sample dec", dec_s), ("sample wq", wq_s),
        ("sample wk", wk_s)])
    cos_sin = np.concatenate([cos_p, sin_p], axis=-1)
    cd = np.concatenate([cd_p, cd_s])
    to_hd_m = lambda c: c.transpose(0, 1, 3, 4, 2).reshape(depth, dbatch, XA_W, N_MEM)
    mkt_s, mvt_s = to_hd_m(cache_mem_k), to_hd_m(cache_mem_v)
    conv_s = state_conv.transpose(0, 2, 1, 3)
    hs = x_sample
    z, w_in_b, w_out_b = _sample_in(0, hs, norm_g, w_in, w_out)

    hp = x_prompt
    states = ()
    y_sample = None
    for l in range(depth):
        hp, p_ret, p_conv, mix, s_ret, s_conv = _layer(
            l, l == depth - 1, dseq, hp, cos_sin, p_mk, p_mv, norm_g, w_in_b, ret_gn_g,
            conv_w, conv_b, conv_ln_g, conv_ln_b, xa_norm_g, w_out_b, tables, layout, cd,
            fng, z, state_ret, conv_s, mkt_s, mvt_s, states)
        states = (p_ret, p_conv, s_ret, s_conv)
        if l + 1 < depth:
            hs, z, w_in_b, w_out_b = _sample_mid(l, hs, mix, w_out_b, norm_g, w_in, w_out)
        else:
            y_sample = _sample_out(hs, mix, w_out_b, fng, x_sample.shape)
    y_prompt = hp

    from_hd_m = lambda c: c.reshape(depth, batch, XA_HEADS, XA_DH, N_MEM).transpose(0, 1, 4, 2, 3)
    return (y_prompt, y_sample, states[0], states[1],
            from_hd_m(p_mk), from_hd_m(p_mv), states[2], states[3].transpose(0, 2, 1, 3))
```

```python
import functools

import numpy as np

import jax
import jax.numpy as jnp
from jax import lax
from jax.experimental import pallas as pl
from jax.experimental.pallas import tpu as pltpu

F32 = jnp.float32
BF16 = jnp.bfloat16

D_MODEL = 1024
N_MEM = 256
RET_HEADS = 4
RET_DK = 128
RET_DV = 128
RET_W = RET_HEADS * RET_DV
RET_CHUNK = 128
CONV_W = 256
CONV_K = 31
XA_HEADS = 4
XA_DH = 64
XA_W = XA_HEADS * XA_DH
D_MIX = RET_W + CONV_W + XA_W
ROPE_BASE = 10000.0
EPS = 1e-6
PAST_LEN = 16384
LOG2_E = 1.4426950408889634

C_RQ, C_RK, C_RV, C_RG = 0, 512, 1024, 1536
C_CA, C_CB, C_CG = 2048, 2304, 2560
C_XQ, C_XG = 2816, 3072
D_IN = 3328

VMEM_LIMIT_BYTES = 56 * 1024 * 1024
SUBLANES = 8
LANES = 128
CONV_PAD = 32
CONV_HIST = CONV_K - 1
PROMPT_TILE = 512
CONV_ROWS = 64
PROJ_BLOCK = 256
RET_STAGE_CHUNKS = 2
STREAM_BUFFERS = 3


def _rms(x, g):
    return x * lax.rsqrt(jnp.mean(x * x, axis=-1, keepdims=True) + EPS) * g


def _standardize(x):
    mu = jnp.mean(x, axis=-1, keepdims=True)
    d = x - mu
    var = jnp.mean(d * d, axis=-1, keepdims=True)
    return d * lax.rsqrt(var + EPS)


def _sigmoid(x):
    return 1.0 / (1.0 + jnp.exp(-x))


def _silu(x):
    return x * _sigmoid(x)


def _dot(a, b):
    return jnp.dot(a, b, preferred_element_type=F32)


def _dot_nt(a, b):
    return lax.dot_general(a, b, (((1,), (1,)), ((), ())), preferred_element_type=F32)


def _dot_tn(a, b):
    return lax.dot_general(a, b, (((0,), (0,)), ((), ())), preferred_element_type=F32)


def _rope(x, cosf, sins):
    return x * cosf + pltpu.roll(x, RET_DK // 2, 1) * sins


def _zero_row_after(z):
    tile = z[z.shape[0] - SUBLANES:, z.shape[1] - LANES:]
    bits = pltpu.bitcast(tile, jnp.uint32)
    sixteen = jnp.uint32(16)
    zero = lax.shift_right_logical(lax.shift_right_logical(bits, sixteen), sixteen)
    row = pltpu.bitcast(zero, F32)[0:1, :]
    return jnp.concatenate([row, row], axis=1)


def _head_lane_ids():
    return lax.broadcasted_iota(jnp.int32, (1, XA_W), 1) // XA_DH


def _xattn_norm_gate(a, xan, gate, rows, denoms=None):
    head = _head_lane_ids()
    a2 = a * a
    ms = jnp.zeros((rows, XA_W), F32)
    for hh in range(XA_HEADS):
        m = head == hh
        ssq = jnp.sum(jnp.where(m, a2, 0.0), axis=-1, keepdims=True) * (1.0 / XA_DH)
        ssq = ssq + (EPS if denoms is None else EPS * denoms[hh] * denoms[hh])
        ms = jnp.where(m, ssq, ms)
    return a * lax.rsqrt(ms) * xan * gate


def _layer_row(ref, layer):
    return ref[layer:layer + 1, :]


def _params(**kw):
    return pltpu.CompilerParams(vmem_limit_bytes=VMEM_LIMIT_BYTES, **kw)


def _mem_kv_kernel(mem_ref, g_ref, wk_ref, wv_ref, kt_ref, vt_ref):
    layer = pl.program_id(0)
    gain = g_ref[pl.ds(layer, 1), :]
    wkt = wk_ref[...].T.astype(BF16)
    wvt = wv_ref[...].T.astype(BF16)
    for b in range(mem_ref.shape[0]):
        m = _rms(mem_ref[b], gain).astype(BF16)
        kt_ref[0, b] = _dot_nt(wkt, m)
        vt_ref[0, b] = _dot_nt(wvt, m)


def _mem_kv(mem, g, wk, wv):
    depth, batch = g.shape[0], mem.shape[0]
    out = jax.ShapeDtypeStruct((depth, batch, XA_W, N_MEM), F32)
    return pl.pallas_call(
        _mem_kv_kernel,
        grid=(depth,),
        in_specs=[
            pl.BlockSpec((batch, N_MEM, D_MODEL), lambda l: (0, 0, 0)),
            pl.BlockSpec((depth, D_MODEL), lambda l: (0, 0)),
            pl.BlockSpec((None, D_MODEL, XA_W), lambda l: (l, 0, 0)),
            pl.BlockSpec((None, D_MODEL, XA_W), lambda l: (l, 0, 0)),
        ],
        out_specs=[
            pl.BlockSpec((1, batch, XA_W, N_MEM), lambda l: (l, 0, 0, 0)),
            pl.BlockSpec((1, batch, XA_W, N_MEM), lambda l: (l, 0, 0, 0)),
        ],
        out_shape=[out, out],
        compiler_params=_params(dimension_semantics=("arbitrary",)),
        name="mem_kv",
    )(mem, g, wk, wv)


class _Blocks:
    def __init__(self, ref, row0, n, rows, lanes):
        self.ref, self.row0, self.n, self.rows, self.lanes = ref, row0, n, rows, lanes

    def __getitem__(self, i):
        if i is Ellipsis:
            return self.ref[self.row0:self.row0 + self.n * self.rows, 0:self.lanes]
        r0 = self.row0 + i * self.rows
        return self.ref[r0:r0 + self.rows, 0:self.lanes]


class _Scalars:
    def __init__(self, ref, first):
        self.ref, self.first = ref, first

    def __getitem__(self, i):
        return self.ref[self.first + i]


def _pack_tables(named):
    parts, layout, row0 = [], {}, 0
    for name, a in named:
        n, rows, lanes = a.shape
        assert rows % SUBLANES == 0 and lanes <= LANES
        wide = np.zeros((n * rows, LANES), np.float32)
        wide[:, :lanes] = a.reshape(n * rows, lanes)
        parts.append(wide)
        layout[name] = (row0, n, rows, lanes)
        row0 += n * rows
    return np.concatenate(parts), layout


def _layer_kernel(layer, final, n_tiles, n_alias, n_tok, n_seq, layout,
                  cd_all_ref, x_ref, cs_ref, mk_ref, mv_ref, ng_ref, win_ref,
                  gn_ref, cw_ref, cb_ref, lng_ref, lnb_ref, xan_ref, wout_ref,
                  tab_ref, fng_ref,
                  sz_ref, szc_ref, ss0_ref, scbuf_ref, smkt_ref, smvt_ref, *refs):
    (y_ref, sret_ref, sconv_ref, smix_ref, ssret_ref, ssconv_ref,
     s_scr, ext_scr, c_scr, q_scr, qw_scr, k_scr, kw_scr, v_scr, g_scr, xg_scr,
     mix_scr, us_scr, cs_scr) = refs[n_alias:]

    tm = PROMPT_TILE
    t = pl.program_id(1)
    table = lambda name: _Blocks(tab_ref, *layout[name])
    dec_ref, wq_ref, wk_ref = table("dec"), table("wq"), table("wk")
    cd_ref = _Scalars(cd_all_ref, 0)
    step_in_block = (pl.program_id(0) * n_tiles + t) % (SUBLANES // n_seq)
    sample_stages = _sample_mix_stages(
        layer, n_tok, n_seq, step_in_block * (n_seq * n_tok), _Scalars(cd_all_ref, RET_HEADS),
        sz_ref, szc_ref, ss0_ref, scbuf_ref, smkt_ref, smvt_ref, table("sample cos"),
        table("sample sin"), gn_ref, cw_ref, cb_ref, lng_ref, lnb_ref, xan_ref,
        table("sample dec"), table("sample wq"), table("sample wk"), smix_ref, ssret_ref,
        ssconv_ref, us_scr, cs_scr)

    @pl.when(t == 0)
    def _():
        s_scr[...] = jnp.zeros_like(s_scr)
        ext_scr[0:CONV_PAD, :] = jnp.zeros((CONV_PAD, CONV_W), F32)

    sample_stages[0]()
    sample_stages[1]()

    hb = _rms(x_ref[0], _layer_row(ng_ref, layer)).astype(BF16)

    def proj(a, b):
        return _dot(hb, win_ref[:, a:b])

    cosf = cs_ref[:, 0:RET_DK]
    sins = cs_ref[:, RET_DK:2 * RET_DK]

    u = proj(C_CA, C_CA + CONV_W) * _sigmoid(proj(C_CB, C_CB + CONV_W))
    ext_scr[CONV_PAD:CONV_PAD + tm, :] = u

    pb = PROJ_BLOCK

    def post_q(z, c):
        for i in range(pb // RET_DK):
            hh = c // RET_DK + i
            cols = slice(RET_DK * hh, RET_DK * (hh + 1))
            q = _rope(z[:, RET_DK * i:RET_DK * (i + 1)], cosf, sins)
            q_scr[:, cols] = q.astype(BF16)
            for ch in range(tm // RET_CHUNK):
                rows = slice(RET_CHUNK * ch, RET_CHUNK * (ch + 1))
                qw_scr[rows, cols] = (q[rows, :] * wq_ref[hh]).astype(BF16)

    def post_k(z, c):
        for i in range(pb // RET_DK):
            hh = c // RET_DK + i
            cols = slice(RET_DK * hh, RET_DK * (hh + 1))
            k = _rope(z[:, RET_DK * i:RET_DK * (i + 1)], cosf, sins) * (RET_DK ** -0.5)
            k_scr[:, cols] = k.astype(BF16)
            for ch in range(tm // RET_CHUNK):
                rows = slice(RET_CHUNK * ch, RET_CHUNK * (ch + 1))
                kw_scr[rows, cols] = (k[rows, :] * wk_ref[hh]).astype(BF16)

    def post_v(z, c):
        v_scr[:, c:c + pb] = z.astype(BF16)

    def post_g(z, c):
        g_scr[:, c:c + pb] = _silu(z)

    def post_cg(z, c):
        xg_scr[:, c:c + pb] = _silu(z)

    def post_xq(z, c):
        xg_scr[:, CONV_W + c:CONV_W + c + pb] = z

    def post_xg(z, c):
        xg_scr[:, CONV_W + XA_W + c:CONV_W + XA_W + c + pb] = _silu(z)

    blocks = []
    for col0, width, post in ((C_RQ, RET_W, post_q), (C_RK, RET_W, post_k),
                              (C_RV, RET_W, post_v), (C_RG, RET_W, post_g),
                              (C_CG, CONV_W, post_cg), (C_XQ, XA_W, post_xq),
                              (C_XG, XA_W, post_xg)):
        blocks += [(col0, c, post) for c in range(0, width, pb)]

    first = CONV_PAD - CONV_HIST
    conv_acc = {}
    conv_bias = _layer_row(cb_ref, layer)

    def tap_group(c0, r, wait_zero):
        n_rows = CONV_ROWS if r == 0 else CONV_ROWS + SUBLANES
        part = None
        for a in range((first + CONV_K - 1 - r) // SUBLANES + 1):
            j = SUBLANES * a + r - first
            if 0 <= j < CONV_K:
                w_j = cw_ref[j:j + 1, :]
                if wait_zero is not None:
                    w_j = w_j + wait_zero
                lo = c0 + SUBLANES * a
                term = ext_scr[lo:lo + n_rows, :] * w_j
                part = term if part is None else part + term
        if r == 0:
            conv_acc[c0] = part + conv_bias
        else:
            conv_acc[c0] = conv_acc[c0] + part[r:r + CONV_ROWS, :]
        if r == SUBLANES - 1:
            c_scr[c0:c0 + CONV_ROWS, :] = conv_acc.pop(c0)

    def conv_tail(lo, wait_zero):
        slab = slice(lo, lo + CONV_ROWS)
        gain = _layer_row(lng_ref, layer)
        if wait_zero is not None:
            gain = gain + wait_zero
        cn = _standardize(c_scr[slab, :]) * gain + _layer_row(lnb_ref, layer)
        mix_scr[slab, RET_W:RET_W + CONV_W] = (_silu(cn) * xg_scr[slab, 0:CONV_W]).astype(BF16)

    taps = [functools.partial(tap_group, c0, r)
            for c0 in range(0, tm, CONV_ROWS) for r in range(SUBLANES)]
    tails = [functools.partial(conv_tail, lo) for lo in range(0, tm, CONV_ROWS)]

    chunks = range(tm // RET_CHUNK)
    heads = range(RET_HEADS)
    crow = lambda c: slice(RET_CHUNK * c, RET_CHUNK * (c + 1))
    hcol = lambda hh: slice(RET_DK * hh, RET_DK * (hh + 1))

    def retention_stages(c0):
        part = chunks[c0:c0 + RET_STAGE_CHUNKS]
        sc, kv, before, out = {}, {}, {}, {}

        def scores_and_updates():
            for c in part:
                for hh in heads:
                    sc[c, hh] = _dot_nt(q_scr[crow(c), hcol(hh)], k_scr[crow(c), hcol(hh)])
                    kv[c, hh] = _dot_tn(kw_scr[crow(c), hcol(hh)], v_scr[crow(c), hcol(hh)])

        def recurrence_and_outputs():
            for hh in heads:
                state = s_scr[hh]
                for c in part:
                    before[c, hh] = state.astype(BF16)
                    state = cd_ref[hh] * state + kv[c, hh]
                s_scr[hh] = state
            for c in part:
                for hh in heads:
                    lhs = jnp.concatenate([(sc[c, hh] * dec_ref[hh]).astype(BF16),
                                           qw_scr[crow(c), hcol(hh)]], axis=1)
                    rhs = jnp.concatenate([v_scr[crow(c), hcol(hh)], before[c, hh]], axis=0)
                    out[c, hh] = _dot(lhs, rhs)

        def norms():
            for c in part:
                for hh in heads:
                    o = _standardize(out[c, hh]) * gn_ref[layer:layer + 1, hcol(hh)]
                    mix_scr[crow(c), hcol(hh)] = (o * g_scr[crow(c), hcol(hh)]).astype(BF16)

        return [scores_and_updates, recurrence_and_outputs, norms]

    n_half = RET_CHUNK * RET_STAGE_CHUNKS
    mkt = mk_ref[0, 0].astype(BF16)
    mvt = mv_ref[0, 0].astype(BF16)
    head = _head_lane_ids()

    def xattn_stages(half):
        scores, pvs, sums = [], [], []

        def query_key():
            q = xg_scr[half, CONV_W:CONV_W + XA_W] * (LOG2_E * XA_DH ** -0.5)
            for hh in range(XA_HEADS):
                scores.append(_dot(jnp.where(head == hh, q, 0.0).astype(BF16), mkt))

        def softmax_value():
            for sc in scores:
                e = jnp.exp2(sc - jnp.max(sc, axis=-1, keepdims=True))
                sums.append(jnp.sum(e, axis=-1, keepdims=True))
                pvs.append(_dot_nt(e.astype(BF16), mvt))

        def norm_gate():
            a = pvs[0]
            for hh in range(1, XA_HEADS):
                a = jnp.where(head == hh, pvs[hh], a)
            gate = xg_scr[half, CONV_W + XA_W:CONV_W + 2 * XA_W]
            mix_scr[half, RET_W + CONV_W:D_MIX] = _xattn_norm_gate(
                a, _layer_row(xan_ref, layer), gate, n_half, denoms=sums).astype(BF16)

        return [query_key, softmax_value, norm_gate]

    halves = [slice(n_half * p, n_half * (p + 1)) for p in range(tm // n_half)]
    pending = [stage for p in range(len(halves))
               for stage in retention_stages(RET_STAGE_CHUNKS * p)]
    gate_block = [post for _, _, post in blocks].index(post_cg)
    late_blocks = len(blocks) - gate_block
    taps_per = -(-len(taps) // gate_block)
    tails_per = -(-len(tails) // (late_blocks - 1))
    stages_per = len(pending) // late_blocks
    z_prev = None
    for k, (col0, c, post) in enumerate(blocks):
        if k < gate_block:
            items = taps[taps_per * k:taps_per * (k + 1)]
        elif k == gate_block:
            items = []
        else:
            j = k - gate_block - 1
            items = tails[tails_per * j:tails_per * (j + 1)]
        for n, item in enumerate(items):
            slab_end = tm * (n + 1) // len(items) // SUBLANES * SUBLANES
            item(None if z_prev is None else _zero_row_after(z_prev[0:slab_end, :]))
        if k >= gate_block:
            for stage in pending[:stages_per]:
                stage()
            pending = pending[stages_per:]
        z_prev = proj(col0 + c, col0 + c + pb)
        post(z_prev, c)

    sample_stages[2]()
    for stage in pending:
        stage()
    for p, half in enumerate(halves):
        for stage in xattn_stages(half):
            stage()
        y = x_ref[0, half, :] + _dot(mix_scr[half, :], wout_ref[...])
        if final:
            y = _rms(y, fng_ref[...])
        y_ref[0, half, :] = y
        if p == 0:
            sample_stages[3]()

    @pl.when(t == n_tiles - 1)
    def _():
        sconv_ref[0, 0] = ext_scr[tm + first:tm + CONV_PAD, :]
        sret_ref[0, 0] = s_scr[...]

    ext_scr[0:CONV_PAD, :] = ext_scr[tm:tm + CONV_PAD, :]


def _layer(layer, final, n_tok, x, cos_sin, mk, mv, ng, win, gn, cw, cb, lng, lnb, xan,
           wout, tables, layout, cd, fng, z_s, s0, cbuf, mkt_s, mvt_s, prev_states):
    batch, seq, _ = x.shape
    depth, sbatch = s0.shape[0], s0.shape[1]
    tm = PROMPT_TILE
    n_tiles = seq // tm
    n_seq = sbatch // (batch * n_tiles)
    assert n_seq * batch * n_tiles == sbatch and (n_seq * n_tok) % (2 * SUBLANES) == 0
    assert SUBLANES % n_seq == 0 and C_CA % (2 * CONV_W) == 0 and C_CB == C_CA + CONV_W
    whole = lambda a: pl.BlockSpec(a.shape, lambda b, t: (0,) * a.ndim)
    step = lambda b, t: b * n_tiles + t
    conv_block = lambda b, t: step(b, t) // (SUBLANES // n_seq)
    conv_spec = pl.BlockSpec((None, CONV_HIST, SUBLANES, CONV_W),
                             lambda b, t: (layer, 0, conv_block(b, t), 0))
    sample_specs = [
        pl.BlockSpec((n_seq * n_tok, D_IN), lambda b, t: (step(b, t), 0)),
        pl.BlockSpec((SUBLANES * n_tok, 2 * CONV_W),
                     lambda b, t: (conv_block(b, t), C_CA // (2 * CONV_W))),
        pl.BlockSpec((1, n_seq, RET_HEADS, RET_DK, RET_DV),
                     lambda b, t: (layer, step(b, t), 0, 0, 0)),
        conv_spec,
        pl.BlockSpec((1, n_seq, XA_W, N_MEM), lambda b, t: (layer, step(b, t), 0, 0)),
        pl.BlockSpec((1, n_seq, XA_W, N_MEM), lambda b, t: (layer, step(b, t), 0, 0)),
    ]
    in_specs = [
        pl.BlockSpec(memory_space=pltpu.SMEM),
        pl.BlockSpec((1, tm, D_MODEL), lambda b, t: (b, t, 0)),
        pl.BlockSpec((tm, 2 * RET_DK), lambda b, t: (t, 0)),
        pl.BlockSpec((1, 1, XA_W, N_MEM), lambda b, t: (layer, b, 0, 0)),
        pl.BlockSpec((1, 1, XA_W, N_MEM), lambda b, t: (layer, b, 0, 0)),
        whole(ng),
        whole(win),
        whole(gn),
        pl.BlockSpec((None, CONV_K, CONV_W), lambda b, t: (layer, 0, 0)),
        whole(cb), whole(lng), whole(lnb),
        whole(xan),
        whole(wout),
        whole(tables),
        whole(fng),
    ] + sample_specs
    n_fixed = len(in_specs)
    n_alias = len(prev_states)
    in_specs += [pl.BlockSpec(memory_space=pl.ANY)] * n_alias
    out_specs = [
        pl.BlockSpec((1, tm, D_MODEL), lambda b, t: (b, t, 0)),
        pl.BlockSpec((1, 1, RET_HEADS, RET_DK, RET_DV), lambda b, t: (layer, b, 0, 0, 0)),
        pl.BlockSpec((1, 1, CONV_HIST, CONV_W), lambda b, t: (layer, b, 0, 0)),
        pl.BlockSpec((n_seq * n_tok, D_MIX), lambda b, t: (step(b, t), 0)),
        pl.BlockSpec((1, n_seq, RET_HEADS, RET_DK, RET_DV),
                     lambda b, t: (layer, step(b, t), 0, 0, 0)),
        conv_spec,
    ]
    out_shape = [
        jax.ShapeDtypeStruct((batch, seq, D_MODEL), F32),
        jax.ShapeDtypeStruct((depth, batch, RET_HEADS, RET_DK, RET_DV), F32),
        jax.ShapeDtypeStruct((depth, batch, CONV_HIST, CONV_W), F32),
        jax.ShapeDtypeStruct((sbatch * n_tok, D_MIX), BF16),
        jax.ShapeDtypeStruct((depth, sbatch, RET_HEADS, RET_DK, RET_DV), F32),
        jax.ShapeDtypeStruct((depth, CONV_HIST, sbatch, CONV_W), F32),
    ]
    state_outputs = (1, 2, 4, 5)
    scratch = [
        pltpu.VMEM((RET_HEADS, RET_DK, RET_DV), F32),
        pltpu.VMEM((CONV_PAD + tm, CONV_W), F32),
        pltpu.VMEM((tm, CONV_W), F32),
        pltpu.VMEM((tm, RET_W), BF16),
        pltpu.VMEM((tm, RET_W), BF16),
        pltpu.VMEM((tm, RET_W), BF16),
        pltpu.VMEM((tm, RET_W), BF16),
        pltpu.VMEM((tm, RET_W), BF16),
        pltpu.VMEM((tm, RET_W), F32),
        pltpu.VMEM((tm, CONV_W + 2 * XA_W), F32),
        pltpu.VMEM((tm, D_MIX), BF16),
        pltpu.VMEM((CONV_W // LANES, SUBLANES * n_tok, LANES), F32),
        pltpu.VMEM((CONV_W // LANES, SUBLANES * n_tok, LANES), F32),
    ]
    return pl.pallas_call(
        functools.partial(_layer_kernel, layer, final, n_tiles, n_alias, n_tok, n_seq, layout),
        grid=(batch, n_tiles),
        in_specs=in_specs,
        out_specs=out_specs,
        out_shape=out_shape,
        scratch_shapes=scratch,
        input_output_aliases={n_fixed + k: state_outputs[k] for k in range(n_alias)},
        compiler_params=_params(dimension_semantics=("arbitrary", "arbitrary")),
        name=f"layer{layer}",
    )(cd, x, cos_sin, mk, mv, ng, win, gn, cw, cb, lng, lnb, xan, wout, tables, fng,
      z_s, z_s, s0, cbuf, mkt_s, mvt_s, *prev_states)


def _stream_project(layer, prepare, hb_scr, win_hbm, wout_hbm, z_hbm, winb_hbm, woutb_ref,
                    w_buf, z_buf, wb_buf, wo_buf, w_sem, z_sem, wb_sem, wo_sem):
    n_chunks = D_IN // PROJ_BLOCK
    ring = STREAM_BUFFERS
    cols = lambda j: pl.ds(PROJ_BLOCK * j, PROJ_BLOCK)
    fetch = lambda j: pltpu.make_async_copy(
        win_hbm.at[layer, :, cols(j)], w_buf.at[j % ring], w_sem.at[j % ring])
    put_z = lambda j: pltpu.make_async_copy(
        z_buf.at[j % 2], z_hbm.at[:, cols(j)], z_sem.at[j % 2])
    put_w = lambda j: pltpu.make_async_copy(
        wb_buf.at[j % 2], winb_hbm.at[:, cols(j)], wb_sem.at[j % 2])
    fetch_wout = pltpu.make_async_copy(wout_hbm.at[layer], wo_buf, wo_sem.at[0])

    for j in range(min(ring, n_chunks)):
        fetch(j).start()
    fetch_wout.start()
    prepare()
    for j in range(n_chunks):
        fetch(j).wait()
        if j >= 2:
            put_z(j - 2).wait()
            put_w(j - 2).wait()
        w = w_buf[j % ring].astype(BF16)
        wb_buf[j % 2] = w
        z_buf[j % 2] = _dot(hb_scr[...], w)
        put_w(j).start()
        put_z(j).start()
        if j + ring < n_chunks:
            fetch(j + ring).start()
        if j == n_chunks // 2:
            fetch_wout.wait()
            woutb_ref[...] = wo_buf[...].astype(BF16)
    for j in range(max(n_chunks - 2, 0), n_chunks):
        put_z(j).wait()
        put_w(j).wait()


def _stream_call(kernel, name, n, operands, extra_out_specs, extra_out_shape, win, wout):
    whole = lambda a: pl.BlockSpec(a.shape, lambda j: (0,) * a.ndim)
    in_hbm = pl.BlockSpec(memory_space=pl.ANY)
    return pl.pallas_call(
        kernel,
        grid=(1,),
        in_specs=[whole(a) for a in operands] + [in_hbm, in_hbm],
        out_specs=extra_out_specs + [in_hbm, in_hbm,
                                     pl.BlockSpec((D_MIX, D_MODEL), lambda j: (0, 0))],
        out_shape=extra_out_shape + [jax.ShapeDtypeStruct((n, D_IN), F32),
                                     jax.ShapeDtypeStruct((D_MODEL, D_IN), BF16),
                                     jax.ShapeDtypeStruct((D_MIX, D_MODEL), BF16)],
        scratch_shapes=[
            pltpu.VMEM((n, D_MODEL), BF16),
            pltpu.VMEM((STREAM_BUFFERS, D_MODEL, PROJ_BLOCK), F32),
            pltpu.VMEM((2, n, PROJ_BLOCK), F32),
            pltpu.VMEM((2, D_MODEL, PROJ_BLOCK), BF16),
            pltpu.VMEM((D_MIX, D_MODEL), F32),
            pltpu.SemaphoreType.DMA((STREAM_BUFFERS,)),
            pltpu.SemaphoreType.DMA((2,)),
            pltpu.SemaphoreType.DMA((2,)),
            pltpu.SemaphoreType.DMA((1,)),
        ],
        compiler_params=_params(dimension_semantics=("arbitrary",)),
        name=name,
    )(*operands, win, wout)


def _sample_in_kernel(layer, x_ref, ng_ref, win_hbm, wout_hbm, z_hbm, winb_hbm, woutb_ref,
                      hb_scr, *stream):
    def prepare():
        x = x_ref[...].reshape(-1, D_MODEL)
        hb_scr[...] = _rms(x, _layer_row(ng_ref, layer)).astype(BF16)

    _stream_project(layer, prepare, hb_scr, win_hbm, wout_hbm, z_hbm, winb_hbm, woutb_ref,
                    *stream)


def _sample_in(layer, x, ng, win, wout):
    return _stream_call(functools.partial(_sample_in_kernel, layer), "sample_in",
                        x.size // D_MODEL, [x, ng], [], [], win, wout)


def _sample_mid_kernel(layer, x_ref, mix_ref, woutb_prev_ref, ng_ref, win_hbm, wout_hbm,
                       h_ref, z_hbm, winb_hbm, woutb_ref, hb_scr, *stream):
    def prepare():
        h = x_ref[...].reshape(-1, D_MODEL) + _dot(mix_ref[...], woutb_prev_ref[...])
        h_ref[...] = h
        hb_scr[...] = _rms(h, _layer_row(ng_ref, layer + 1)).astype(BF16)

    _stream_project(layer + 1, prepare, hb_scr, win_hbm, wout_hbm, z_hbm, winb_hbm, woutb_ref,
                    *stream)


def _sample_mid(layer, x, mix, woutb_prev, ng, win, wout):
    n = x.size // D_MODEL
    return _stream_call(functools.partial(_sample_mid_kernel, layer), "sample_mid", n,
                        [x, mix, woutb_prev, ng],
                        [pl.BlockSpec((n, D_MODEL), lambda j: (0, 0))],
                        [jax.ShapeDtypeStruct((n, D_MODEL), F32)], win, wout)


def _sample_out_kernel(x_ref, mix_ref, wout_ref, fng_ref, y_ref):
    x = x_ref[...].reshape(-1, D_MODEL)
    y = _rms(x + _dot(mix_ref[...], wout_ref[...]), fng_ref[...])
    y_ref[...] = y.reshape(y_ref.shape)


def _sample_out(x, mix, wout, fng, out_shape):
    whole = lambda a: pl.BlockSpec(a.shape, lambda j: (0,) * a.ndim)
    return pl.pallas_call(
        _sample_out_kernel,
        grid=(1,),
        in_specs=[whole(x), whole(mix), whole(wout), whole(fng)],
        out_specs=pl.BlockSpec(out_shape, lambda j: (0,) * len(out_shape)),
        out_shape=jax.ShapeDtypeStruct(out_shape, F32),
        compiler_params=_params(dimension_semantics=("arbitrary",)),
        name="sample_out",
    )(x, mix, wout, fng)


def _sample_mix_stages(layer, n_tok, n_seq, own_row0,
                       cd_ref, z_ref, zc_ref, s0_ref, cbuf_ref, mkt_ref, mvt_ref, cos_ref,
                       sin_ref, gn_ref, cw_ref, cb_ref, lng_ref, lnb_ref, xan_ref,
                       dec_ref, wq_ref, wk_ref, mix_ref, sret_ref, sconv_ref, us_scr, cs_scr):
    grp = SUBLANES // n_tok
    rows = grp * n_tok
    cosf = cos_ref[...]
    sins = sin_ref[...]
    head = _head_lane_ids()
    row_id = lax.broadcasted_iota(jnp.int32, (rows, 1), 0)
    row_seq = row_id // n_tok
    row4_seq = lax.broadcasted_iota(jnp.int32, (XA_HEADS * rows, 1), 0) % rows // n_tok

    def pick(parts, seq_of_row):
        out = parts[0]
        for s in range(1, grp):
            out = jnp.where(seq_of_row == s, parts[s], out)
        return out

    groups = range(n_seq // grp)
    heads = range(RET_HEADS)
    seqs = [[g * grp + s for s in range(grp)] for g in groups]
    rs = [slice(rows * g, rows * (g + 1)) for g in groups]
    hcol = lambda c0, hh: slice(c0 + RET_DK * hh, c0 + RET_DK * (hh + 1))

    qh, kf, vh, sc, xsc = {}, {}, {}, {}, {}
    inner, cross, xo, xsum = {}, {}, {}, {}

    def first_matmuls():
        for g in groups:
            for hh in heads:
                qh[g, hh] = _rope(z_ref[rs[g], hcol(C_RQ, hh)], cosf, sins).astype(BF16)
                kf[g, hh] = _rope(z_ref[rs[g], hcol(C_RK, hh)], cosf, sins) * (RET_DK ** -0.5)
                vh[g, hh] = z_ref[rs[g], hcol(C_RV, hh)].astype(BF16)
                sc[g, hh] = _dot_nt(qh[g, hh], kf[g, hh].astype(BF16))
            xq = z_ref[rs[g], C_XQ:C_XQ + XA_W] * (LOG2_E * XA_DH ** -0.5)
            q4 = jnp.concatenate([jnp.where(head == hh, xq, 0.0) for hh in range(XA_HEADS)],
                                 axis=0).astype(BF16)
            xsc[g] = [_dot(q4, mkt_ref[0, b].astype(BF16)) for b in seqs[g]]

    def conv_module():
        pieces = range(CONV_W // LANES)
        lanes = lambda p: slice(LANES * p, LANES * (p + 1))
        u_rows = zc_ref[:, 0:CONV_W] * _sigmoid(zc_ref[:, CONV_W:2 * CONV_W])
        for p in pieces:
            us_scr[p] = u_rows[:, lanes(p)]
        u = []
        for i in range(n_tok):
            tok = pl.ds(i, SUBLANES, stride=n_tok)
            u.append(jnp.concatenate([us_scr[p, tok, :] for p in pieces], axis=-1))
        window = lambda k: cbuf_ref[k] if k < CONV_HIST else u[k - CONV_HIST]
        acc = [_layer_row(cb_ref, layer)] * n_tok
        for k in range(CONV_HIST + n_tok):
            w_k = window(k)
            for i in range(n_tok):
                if 0 <= k - i < CONV_K:
                    acc[i] = acc[i] + w_k * cw_ref[k - i:k - i + 1, :]
        for i in range(n_tok):
            for p in pieces:
                cs_scr[p, pl.ds(i, SUBLANES, stride=n_tok), :] = acc[i][:, lanes(p)]
        for k in range(CONV_HIST):
            sconv_ref[k] = window(k + n_tok)
        for g in groups:
            own = pl.ds(pl.multiple_of(own_row0 + rows * g, rows), rows)
            c = jnp.concatenate([cs_scr[p, own, :] for p in pieces], axis=-1)
            cn = _standardize(c) * _layer_row(lng_ref, layer) + _layer_row(lnb_ref, layer)
            gate = _silu(z_ref[rs[g], C_CG:C_CG + CONV_W])
            mix_ref[rs[g], RET_W:RET_W + CONV_W] = (_silu(cn) * gate).astype(BF16)

    def second_matmuls():
        for g in groups:
            for hh in heads:
                inner[g, hh] = _dot((sc[g, hh] * dec_ref[hh]).astype(BF16), vh[g, hh])
                kw = kf[g, hh] * wk_ref[hh]
                parts = []
                for s, b in enumerate(seqs[g]):
                    s_prev = s0_ref[0, b, hh]
                    parts.append(_dot(qh[g, hh], s_prev.astype(BF16)))
                    kv = _dot_tn(jnp.where(row_seq == s, kw, 0.0).astype(BF16), vh[g, hh])
                    sret_ref[0, b, hh] = cd_ref[hh] * s_prev + kv
                cross[g, hh] = pick(parts, row_seq) * wq_ref[hh]
            s4 = pick(xsc[g], row4_seq)
            e = jnp.exp2(s4 - jnp.max(s4, axis=-1, keepdims=True))
            xsum[g] = jnp.sum(e, axis=-1, keepdims=True)
            eb = e.astype(BF16)
            xo[g] = [_dot_nt(eb, mvt_ref[0, b].astype(BF16)) for b in seqs[g]]

    def norms_and_stores():
        for g in groups:
            for hh in heads:
                co = hcol(0, hh)
                o = _standardize(inner[g, hh] + cross[g, hh]) * gn_ref[layer:layer + 1, co]
                gate = _silu(z_ref[rs[g], hcol(C_RG, hh)])
                mix_ref[rs[g], co] = (o * gate).astype(BF16)
            o4 = pick(xo[g], row4_seq) / xsum[g]
            a = jnp.zeros((rows, XA_W), F32)
            for hh in range(XA_HEADS):
                a = jnp.where(head == hh, o4[rows * hh:rows * (hh + 1), :], a)
            gate = _silu(z_ref[rs[g], C_XG:C_XG + XA_W])
            mix_ref[rs[g], RET_W + CONV_W:D_MIX] = _xattn_norm_gate(
                a, _layer_row(xan_ref, layer), gate, rows).astype(BF16)

    return [first_matmuls, conv_module, second_matmuls, norms_and_stores]


def _rope_tables(pos):
    half = RET_DK // 2
    inv = np.float64(ROPE_BASE) ** (-np.arange(half, dtype=np.float64) / half)
    ang = pos.astype(np.float64)[:, None] * inv[None, :]
    cos, sin = np.cos(ang), np.sin(ang)
    return (np.concatenate([cos, cos], axis=-1).astype(np.float32),
            np.concatenate([-sin, sin], axis=-1).astype(np.float32))


def _decay_tables(chunk):
    lg = np.log(1.0 - np.exp2(-5.0 - np.arange(RET_HEADS, dtype=np.float64)))
    idx = np.arange(chunk, dtype=np.float64)
    diff = idx[:, None] - idx[None, :]
    dec = np.where(diff[None] >= 0, np.exp(np.maximum(diff, 0.0)[None] * lg[:, None, None]), 0.0)
    wk = np.exp((chunk - 1.0 - idx)[None, :] * lg[:, None])
    wq = np.exp((idx + 1.0)[None, :] * lg[:, None])
    cd = np.exp(chunk * lg)
    wk = np.broadcast_to(wk[:, :, None], (RET_HEADS, chunk, RET_DK))
    wq = np.broadcast_to(wq[:, :, None], (RET_HEADS, chunk, RET_DV))
    f32 = lambda a: np.ascontiguousarray(a, dtype=np.float32)
    return f32(dec), f32(wq), f32(wk), f32(cd)


def _group_tables(n_tok, pos0):
    grp = SUBLANES // n_tok
    cosf, sins = _rope_tables(pos0 + np.arange(n_tok))
    dec, wq, wk, cd = _decay_tables(n_tok)
    tile_rows = lambda a: np.concatenate([a] * grp, axis=-2)
    eye = np.eye(grp, dtype=np.float32)
    dec = np.einsum("st,hij->hsitj", eye, dec).reshape(RET_HEADS, grp * n_tok, grp * n_tok)
    return tile_rows(cosf), tile_rows(sins), dec, tile_rows(wq), tile_rows(wk), cd


def kernel(x_prompt, x_sample, mem_prompt, state_ret, state_conv, cache_mem_k, cache_mem_v,
           norm_g, w_in, ret_gn_g, conv_w, conv_b, conv_ln_g, conv_ln_b, xa_norm_g,
           mem_norm_g, w_mk, w_mv, w_out, final_norm_g):
    depth = w_in.shape[0]
    batch, seq, _ = x_prompt.shape
    dbatch, dseq, _ = x_sample.shape

    fng = final_norm_g.reshape(1, D_MODEL)

    p_mk, p_mv = _mem_kv(mem_prompt, mem_norm_g, w_mk, w_mv)
    cos_p, sin_p = _rope_tables(np.arange(seq))
    dec_p, wq_p, wk_p, cd_p = _decay_tables(RET_CHUNK if seq % RET_CHUNK == 0 else seq)

    assert SUBLANES % dseq == 0 and dseq % RET_CHUNK != 0
    cos_s, sin_s, dec_s, wq_s, wk_s, cd_s = _group_tables(dseq, PAST_LEN)
    tables, layout = _pack_tables([
        ("dec", dec_p), ("wq", wq_p), ("wk", wk_p), ("sample cos", cos_s[None]),
        ("sample sin", sin_s[None]), ("sample dec", dec_s), ("sample wq", wq_s),
        ("sample wk", wk_s)])
    cos_sin = np.concatenate([cos_p, sin_p], axis=-1)
    cd = np.concatenate([cd_p, cd_s])
    to_hd_m = lambda c: c.transpose(0, 1, 3, 4, 2).reshape(depth, dbatch, XA_W, N_MEM)
    mkt_s, mvt_s = to_hd_m(cache_mem_k), to_hd_m(cache_mem_v)
    conv_s = state_conv.transpose(0, 2, 1, 3)
    hs = x_sample
    z, w_in_b, w_out_b = _sample_in(0, hs, norm_g, w_in, w_out)

    hp = x_prompt
    states = ()
    y_sample = None
    for l in range(depth):
        hp, p_ret, p_conv, mix, s_ret, s_conv = _layer(
            l, l == depth - 1, dseq, hp, cos_sin, p_mk, p_mv, norm_g, w_in_b, ret_gn_g,
            conv_w, conv_b, conv_ln_g, conv_ln_b, xa_norm_g, w_out_b, tables, layout, cd,
            fng, z, state_ret, conv_s, mkt_s, mvt_s, states)
        states = (p_ret, p_conv, s_ret, s_conv)
        if l + 1 < depth:
            hs, z, w_in_b, w_out_b = _sample_mid(l, hs, mix, w_out_b, norm_g, w_in, w_out)
        else:
            y_sample = _sample_out(hs, mix, w_out_b, fng, x_sample.shape)
    y_prompt = hp

    from_hd_m = lambda c: c.reshape(depth, batch, XA_HEADS, XA_DH, N_MEM).transpose(0, 1, 4, 2, 3)
    return (y_prompt, y_sample, states[0], states[1],
            from_hd_m(p_mk), from_hd_m(p_mv), states[2], states[3].transpose(0, 2, 1, 3))
```

```python
import functools

import numpy as np

import jax
import jax.numpy as jnp
from jax import lax
from jax.experimental import pallas as pl
from jax.experimental.pallas import tpu as pltpu

F32 = jnp.float32
BF16 = jnp.bfloat16

D_MODEL = 1024
N_MEM = 256
RET_HEADS = 4
RET_DK = 128
RET_DV = 128
RET_W = RET_HEADS * RET_DV
RET_CHUNK = 128
CONV_W = 256
CONV_K = 31
XA_HEADS = 4
XA_DH = 64
XA_W = XA_HEADS * XA_DH
D_MIX = RET_W + CONV_W + XA_W
ROPE_BASE = 10000.0
EPS = 1e-6
PAST_LEN = 16384
LOG2_E = 1.4426950408889634

C_RQ, C_RK, C_RV, C_RG = 0, 512, 1024, 1536
C_CA, C_CB, C_CG = 2048, 2304, 2560
C_XQ, C_XG = 2816, 3072
D_IN = 3328

VMEM_LIMIT_BYTES = 56 * 1024 * 1024
SUBLANES = 8
LANES = 128
CONV_PAD = 32
CONV_HIST = CONV_K - 1
PROMPT_TILE = 512
CONV_ROWS = 64
PROJ_BLOCK = 256
RET_STAGE_CHUNKS = 2
STREAM_BUFFERS = 2
STREAM_COLS = 1664


def _rms(x, g):
    return x * lax.rsqrt(jnp.mean(x * x, axis=-1, keepdims=True) + EPS) * g


def _standardize(x):
    mu = jnp.mean(x, axis=-1, keepdims=True)
    d = x - mu
    var = jnp.mean(d * d, axis=-1, keepdims=True)
    return d * lax.rsqrt(var + EPS)


def _sigmoid(x):
    return 1.0 / (1.0 + jnp.exp(-x))


def _silu(x):
    return x * _sigmoid(x)


def _dot(a, b):
    return jnp.dot(a, b, preferred_element_type=F32)


def _dot_nt(a, b):
    return lax.dot_general(a, b, (((1,), (1,)), ((), ())), preferred_element_type=F32)


def _dot_tn(a, b):
    return lax.dot_general(a, b, (((0,), (0,)), ((), ())), preferred_element_type=F32)


def _rope(x, cosf, sins):
    return x * cosf + pltpu.roll(x, RET_DK // 2, 1) * sins


def _zero_row_after(z):
    tile = z[z.shape[0] - SUBLANES:, z.shape[1] - LANES:]
    bits = pltpu.bitcast(tile, jnp.uint32)
    sixteen = jnp.uint32(16)
    zero = lax.shift_right_logical(lax.shift_right_logical(bits, sixteen), sixteen)
    row = pltpu.bitcast(zero, F32)[0:1, :]
    return jnp.concatenate([row, row], axis=1)


def _head_lane_ids():
    return lax.broadcasted_iota(jnp.int32, (1, XA_W), 1) // XA_DH


def _xattn_norm_gate(a, xan, gate, rows, denoms=None):
    head = _head_lane_ids()
    a2 = a * a
    ms = jnp.zeros((rows, XA_W), F32)
    for hh in range(XA_HEADS):
        m = head == hh
        ssq = jnp.sum(jnp.where(m, a2, 0.0), axis=-1, keepdims=True) * (1.0 / XA_DH)
        ssq = ssq + (EPS if denoms is None else EPS * denoms[hh] * denoms[hh])
        ms = jnp.where(m, ssq, ms)
    return a * lax.rsqrt(ms) * xan * gate


def _layer_row(ref, layer):
    return ref[layer:layer + 1, :]


def _params(**kw):
    return pltpu.CompilerParams(vmem_limit_bytes=VMEM_LIMIT_BYTES, **kw)


def _mem_kv_kernel(mem_ref, g_ref, wk_ref, wv_ref, kt_ref, vt_ref):
    layer = pl.program_id(0)
    gain = g_ref[pl.ds(layer, 1), :]
    wkt = wk_ref[...].T.astype(BF16)
    wvt = wv_ref[...].T.astype(BF16)
    for b in range(mem_ref.shape[0]):
        m = _rms(mem_ref[b], gain).astype(BF16)
        kt_ref[0, b] = _dot_nt(wkt, m)
        vt_ref[0, b] = _dot_nt(wvt, m)


def _mem_kv(mem, g, wk, wv):
    depth, batch = g.shape[0], mem.shape[0]
    out = jax.ShapeDtypeStruct((depth, batch, XA_W, N_MEM), F32)
    return pl.pallas_call(
        _mem_kv_kernel,
        grid=(depth,),
        in_specs=[
            pl.BlockSpec((batch, N_MEM, D_MODEL), lambda l: (0, 0, 0)),
            pl.BlockSpec((depth, D_MODEL), lambda l: (0, 0)),
            pl.BlockSpec((None, D_MODEL, XA_W), lambda l: (l, 0, 0)),
            pl.BlockSpec((None, D_MODEL, XA_W), lambda l: (l, 0, 0)),
        ],
        out_specs=[
            pl.BlockSpec((1, batch, XA_W, N_MEM), lambda l: (l, 0, 0, 0)),
            pl.BlockSpec((1, batch, XA_W, N_MEM), lambda l: (l, 0, 0, 0)),
        ],
        out_shape=[out, out],
        compiler_params=_params(dimension_semantics=("arbitrary",)),
        name="mem_kv",
    )(mem, g, wk, wv)


class _Blocks:
    def __init__(self, ref, row0, n, rows, lanes):
        self.ref, self.row0, self.n, self.rows, self.lanes = ref, row0, n, rows, lanes

    def __getitem__(self, i):
        if i is Ellipsis:
            return self.ref[self.row0:self.row0 + self.n * self.rows, 0:self.lanes]
        r0 = self.row0 + i * self.rows
        return self.ref[r0:r0 + self.rows, 0:self.lanes]


class _Scalars:
    def __init__(self, ref, first):
        self.ref, self.first = ref, first

    def __getitem__(self, i):
        return self.ref[self.first + i]


def _pack_tables(named):
    parts, layout, row0 = [], {}, 0
    for name, a in named:
        n, rows, lanes = a.shape
        assert rows % SUBLANES == 0 and lanes <= LANES
        wide = np.zeros((n * rows, LANES), np.float32)
        wide[:, :lanes] = a.reshape(n * rows, lanes)
        parts.append(wide)
        layout[name] = (row0, n, rows, lanes)
        row0 += n * rows
    return np.concatenate(parts), layout


def _layer_kernel(layer, final, n_tiles, n_alias, n_tok, n_seq, layout,
                  cd_all_ref, x_ref, cs_ref, mk_ref, mv_ref, ng_ref, win_ref,
                  gn_ref, cw_ref, cb_ref, lng_ref, lnb_ref, xan_ref, wout_ref,
                  tab_ref, fng_ref,
                  sz_ref, szc_ref, ss0_ref, scbuf_ref, smkt_ref, smvt_ref, *refs):
    (y_ref, sret_ref, sconv_ref, smix_ref, ssret_ref, ssconv_ref,
     s_scr, ext_scr, c_scr, q_scr, qw_scr, k_scr, kw_scr, v_scr, g_scr, xg_scr,
     mix_scr, us_scr, cs_scr) = refs[n_alias:]

    tm = PROMPT_TILE
    t = pl.program_id(1)
    table = lambda name: _Blocks(tab_ref, *layout[name])
    dec_ref, wq_ref, wk_ref = table("dec"), table("wq"), table("wk")
    cd_ref = _Scalars(cd_all_ref, 0)
    step_in_block = (pl.program_id(0) * n_tiles + t) % (SUBLANES // n_seq)
    sample_stages = _sample_mix_stages(
        layer, n_tok, n_seq, step_in_block * (n_seq * n_tok), _Scalars(cd_all_ref, RET_HEADS),
        sz_ref, szc_ref, ss0_ref, scbuf_ref, smkt_ref, smvt_ref, table("sample cos"),
        table("sample sin"), gn_ref, cw_ref, cb_ref, lng_ref, lnb_ref, xan_ref,
        table("sample dec"), table("sample wq"), table("sample wk"), smix_ref, ssret_ref,
        ssconv_ref, us_scr, cs_scr)

    @pl.when(t == 0)
    def _():
        s_scr[...] = jnp.zeros_like(s_scr)
        ext_scr[0:CONV_PAD, :] = jnp.zeros((CONV_PAD, CONV_W), F32)

    sample_stages[0]()
    sample_stages[1]()

    hb = _rms(x_ref[0], _layer_row(ng_ref, layer)).astype(BF16)

    def proj(a, b):
        return _dot(hb, win_ref[:, a:b])

    cosf = cs_ref[:, 0:RET_DK]
    sins = cs_ref[:, RET_DK:2 * RET_DK]

    u = proj(C_CA, C_CA + CONV_W) * _sigmoid(proj(C_CB, C_CB + CONV_W))
    ext_scr[CONV_PAD:CONV_PAD + tm, :] = u

    pb = PROJ_BLOCK

    def post_q(z, c):
        for i in range(pb // RET_DK):
            hh = c // RET_DK + i
            cols = slice(RET_DK * hh, RET_DK * (hh + 1))
            q = _rope(z[:, RET_DK * i:RET_DK * (i + 1)], cosf, sins)
            q_scr[:, cols] = q.astype(BF16)
            for ch in range(tm // RET_CHUNK):
                rows = slice(RET_CHUNK * ch, RET_CHUNK * (ch + 1))
                qw_scr[rows, cols] = (q[rows, :] * wq_ref[hh]).astype(BF16)

    def post_k(z, c):
        for i in range(pb // RET_DK):
            hh = c // RET_DK + i
            cols = slice(RET_DK * hh, RET_DK * (hh + 1))
            k = _rope(z[:, RET_DK * i:RET_DK * (i + 1)], cosf, sins) * (RET_DK ** -0.5)
            k_scr[:, cols] = k.astype(BF16)
            for ch in range(tm // RET_CHUNK):
                rows = slice(RET_CHUNK * ch, RET_CHUNK * (ch + 1))
                kw_scr[rows, cols] = (k[rows, :] * wk_ref[hh]).astype(BF16)

    def post_v(z, c):
        v_scr[:, c:c + pb] = z.astype(BF16)

    def post_g(z, c):
        g_scr[:, c:c + pb] = _silu(z)

    def post_cg(z, c):
        xg_scr[:, c:c + pb] = _silu(z)

    def post_xq(z, c):
        xg_scr[:, CONV_W + c:CONV_W + c + pb] = z

    def post_xg(z, c):
        xg_scr[:, CONV_W + XA_W + c:CONV_W + XA_W + c + pb] = _silu(z)

    blocks = []
    for col0, width, post in ((C_RQ, RET_W, post_q), (C_RK, RET_W, post_k),
                              (C_RV, RET_W, post_v), (C_RG, RET_W, post_g),
                              (C_CG, CONV_W, post_cg), (C_XQ, XA_W, post_xq),
                              (C_XG, XA_W, post_xg)):
        blocks += [(col0, c, post) for c in range(0, width, pb)]

    first = CONV_PAD - CONV_HIST
    conv_acc = {}
    conv_bias = _layer_row(cb_ref, layer)

    def tap_group(c0, r, wait_zero):
        n_rows = CONV_ROWS if r == 0 else CONV_ROWS + SUBLANES
        part = None
        for a in range((first + CONV_K - 1 - r) // SUBLANES + 1):
            j = SUBLANES * a + r - first
            if 0 <= j < CONV_K:
                w_j = cw_ref[j:j + 1, :]
                if wait_zero is not None:
                    w_j = w_j + wait_zero
                lo = c0 + SUBLANES * a
                term = ext_scr[lo:lo + n_rows, :] * w_j
                part = term if part is None else part + term
        if r == 0:
            conv_acc[c0] = part + conv_bias
        else:
            conv_acc[c0] = conv_acc[c0] + part[r:r + CONV_ROWS, :]
        if r == SUBLANES - 1:
            c_scr[c0:c0 + CONV_ROWS, :] = conv_acc.pop(c0)

    def conv_tail(lo, wait_zero):
        slab = slice(lo, lo + CONV_ROWS)
        gain = _layer_row(lng_ref, layer)
        if wait_zero is not None:
            gain = gain + wait_zero
        cn = _standardize(c_scr[slab, :]) * gain + _layer_row(lnb_ref, layer)
        mix_scr[slab, RET_W:RET_W + CONV_W] = (_silu(cn) * xg_scr[slab, 0:CONV_W]).astype(BF16)

    taps = [functools.partial(tap_group, c0, r)
            for c0 in range(0, tm, CONV_ROWS) for r in range(SUBLANES)]
    tails = [functools.partial(conv_tail, lo) for lo in range(0, tm, CONV_ROWS)]

    chunks = range(tm // RET_CHUNK)
    heads = range(RET_HEADS)
    crow = lambda c: slice(RET_CHUNK * c, RET_CHUNK * (c + 1))
    hcol = lambda hh: slice(RET_DK * hh, RET_DK * (hh + 1))

    def retention_stages(c0):
        part = chunks[c0:c0 + RET_STAGE_CHUNKS]
        sc, kv, before, out = {}, {}, {}, {}

        def scores_and_updates():
            for c in part:
                for hh in heads:
                    sc[c, hh] = _dot_nt(q_scr[crow(c), hcol(hh)], k_scr[crow(c), hcol(hh)])
                    kv[c, hh] = _dot_tn(kw_scr[crow(c), hcol(hh)], v_scr[crow(c), hcol(hh)])

        def recurrence_and_outputs():
            for hh in heads:
                state = s_scr[hh]
                for c in part:
                    before[c, hh] = state.astype(BF16)
                    state = cd_ref[hh] * state + kv[c, hh]
                s_scr[hh] = state
            for c in part:
                for hh in heads:
                    lhs = jnp.concatenate([(sc[c, hh] * dec_ref[hh]).astype(BF16),
                                           qw_scr[crow(c), hcol(hh)]], axis=1)
                    rhs = jnp.concatenate([v_scr[crow(c), hcol(hh)], before[c, hh]], axis=0)
                    out[c, hh] = _dot(lhs, rhs)

        def norms():
            for c in part:
                for hh in heads:
                    o = _standardize(out[c, hh]) * gn_ref[layer:layer + 1, hcol(hh)]
                    mix_scr[crow(c), hcol(hh)] = (o * g_scr[crow(c), hcol(hh)]).astype(BF16)

        return [scores_and_updates, recurrence_and_outputs, norms]

    n_half = RET_CHUNK * RET_STAGE_CHUNKS
    mkt = mk_ref[0, 0].astype(BF16)
    mvt = mv_ref[0, 0].astype(BF16)
    head = _head_lane_ids()

    def xattn_stages(half):
        scores, pvs, sums = [], [], []

        def query_key():
            q = xg_scr[half, CONV_W:CONV_W + XA_W] * (LOG2_E * XA_DH ** -0.5)
            for hh in range(XA_HEADS):
                scores.append(_dot(jnp.where(head == hh, q, 0.0).astype(BF16), mkt))

        def softmax_value():
            for sc in scores:
                e = jnp.exp2(sc - jnp.max(sc, axis=-1, keepdims=True))
                sums.append(jnp.sum(e, axis=-1, keepdims=True))
                pvs.append(_dot_nt(e.astype(BF16), mvt))

        def norm_gate():
            a = pvs[0]
            for hh in range(1, XA_HEADS):
                a = jnp.where(head == hh, pvs[hh], a)
            gate = xg_scr[half, CONV_W + XA_W:CONV_W + 2 * XA_W]
            mix_scr[half, RET_W + CONV_W:D_MIX] = _xattn_norm_gate(
                a, _layer_row(xan_ref, layer), gate, n_half, denoms=sums).astype(BF16)

        return [query_key, softmax_value, norm_gate]

    halves = [slice(n_half * p, n_half * (p + 1)) for p in range(tm // n_half)]
    pending = [stage for p in range(len(halves))
               for stage in retention_stages(RET_STAGE_CHUNKS * p)]
    gate_block = [post for _, _, post in blocks].index(post_cg)
    late_blocks = len(blocks) - gate_block
    taps_per = -(-len(taps) // gate_block)
    tails_per = -(-len(tails) // (late_blocks - 1))
    stages_per = len(pending) // late_blocks
    z_prev = None
    for k, (col0, c, post) in enumerate(blocks):
        if k < gate_block:
            items = taps[taps_per * k:taps_per * (k + 1)]
        elif k == gate_block:
            items = []
        else:
            j = k - gate_block - 1
            items = tails[tails_per * j:tails_per * (j + 1)]
        for n, item in enumerate(items):
            slab_end = tm * (n + 1) // len(items) // SUBLANES * SUBLANES
            item(None if z_prev is None else _zero_row_after(z_prev[0:slab_end, :]))
        if k >= gate_block:
            for stage in pending[:stages_per]:
                stage()
            pending = pending[stages_per:]
        z_prev = proj(col0 + c, col0 + c + pb)
        post(z_prev, c)

    sample_stages[2]()
    for stage in pending:
        stage()
    for p, half in enumerate(halves):
        for stage in xattn_stages(half):
            stage()
        y = x_ref[0, half, :] + _dot(mix_scr[half, :], wout_ref[...])
        if final:
            y = _rms(y, fng_ref[...])
        y_ref[0, half, :] = y
        if p == 0:
            sample_stages[3]()

    @pl.when(t == n_tiles - 1)
    def _():
        sconv_ref[0, 0] = ext_scr[tm + first:tm + CONV_PAD, :]
        sret_ref[0, 0] = s_scr[...]

    ext_scr[0:CONV_PAD, :] = ext_scr[tm:tm + CONV_PAD, :]


def _layer(layer, final, n_tok, x, cos_sin, mk, mv, ng, win, gn, cw, cb, lng, lnb, xan,
           wout, tables, layout, cd, fng, z_s, s0, cbuf, mkt_s, mvt_s, prev_states):
    batch, seq, _ = x.shape
    depth, sbatch = s0.shape[0], s0.shape[1]
    tm = PROMPT_TILE
    n_tiles = seq // tm
    n_seq = sbatch // (batch * n_tiles)
    assert n_seq * batch * n_tiles == sbatch and (n_seq * n_tok) % (2 * SUBLANES) == 0
    assert SUBLANES % n_seq == 0 and C_CA % (2 * CONV_W) == 0 and C_CB == C_CA + CONV_W
    whole = lambda a: pl.BlockSpec(a.shape, lambda b, t: (0,) * a.ndim)
    step = lambda b, t: b * n_tiles + t
    conv_block = lambda b, t: step(b, t) // (SUBLANES // n_seq)
    conv_spec = pl.BlockSpec((None, CONV_HIST, SUBLANES, CONV_W),
                             lambda b, t: (layer, 0, conv_block(b, t), 0))
    sample_specs = [
        pl.BlockSpec((n_seq * n_tok, D_IN), lambda b, t: (step(b, t), 0)),
        pl.BlockSpec((SUBLANES * n_tok, 2 * CONV_W),
                     lambda b, t: (conv_block(b, t), C_CA // (2 * CONV_W))),
        pl.BlockSpec((1, n_seq, RET_HEADS, RET_DK, RET_DV),
                     lambda b, t: (layer, step(b, t), 0, 0, 0)),
        conv_spec,
        pl.BlockSpec((1, n_seq, XA_W, N_MEM), lambda b, t: (layer, step(b, t), 0, 0)),
        pl.BlockSpec((1, n_seq, XA_W, N_MEM), lambda b, t: (layer, step(b, t), 0, 0)),
    ]
    in_specs = [
        pl.BlockSpec(memory_space=pltpu.SMEM),
        pl.BlockSpec((1, tm, D_MODEL), lambda b, t: (b, t, 0)),
        pl.BlockSpec((tm, 2 * RET_DK), lambda b, t: (t, 0)),
        pl.BlockSpec((1, 1, XA_W, N_MEM), lambda b, t: (layer, b, 0, 0)),
        pl.BlockSpec((1, 1, XA_W, N_MEM), lambda b, t: (layer, b, 0, 0)),
        whole(ng),
        whole(win),
        whole(gn),
        pl.BlockSpec((None, CONV_K, CONV_W), lambda b, t: (layer, 0, 0)),
        whole(cb), whole(lng), whole(lnb),
        whole(xan),
        whole(wout),
        whole(tables),
        whole(fng),
    ] + sample_specs
    n_fixed = len(in_specs)
    n_alias = len(prev_states)
    in_specs += [pl.BlockSpec(memory_space=pl.ANY)] * n_alias
    out_specs = [
        pl.BlockSpec((1, tm, D_MODEL), lambda b, t: (b, t, 0)),
        pl.BlockSpec((1, 1, RET_HEADS, RET_DK, RET_DV), lambda b, t: (layer, b, 0, 0, 0)),
        pl.BlockSpec((1, 1, CONV_HIST, CONV_W), lambda b, t: (layer, b, 0, 0)),
        pl.BlockSpec((n_seq * n_tok, D_MIX), lambda b, t: (step(b, t), 0)),
        pl.BlockSpec((1, n_seq, RET_HEADS, RET_DK, RET_DV),
                     lambda b, t: (layer, step(b, t), 0, 0, 0)),
        conv_spec,
    ]
    out_shape = [
        jax.ShapeDtypeStruct((batch, seq, D_MODEL), F32),
        jax.ShapeDtypeStruct((depth, batch, RET_HEADS, RET_DK, RET_DV), F32),
        jax.ShapeDtypeStruct((depth, batch, CONV_HIST, CONV_W), F32),
        jax.ShapeDtypeStruct((sbatch * n_tok, D_MIX), BF16),
        jax.ShapeDtypeStruct((depth, sbatch, RET_HEADS, RET_DK, RET_DV), F32),
        jax.ShapeDtypeStruct((depth, CONV_HIST, sbatch, CONV_W), F32),
    ]
    state_outputs = (1, 2, 4, 5)
    scratch = [
        pltpu.VMEM((RET_HEADS, RET_DK, RET_DV), F32),
        pltpu.VMEM((CONV_PAD + tm, CONV_W), F32),
        pltpu.VMEM((tm, CONV_W), F32),
        pltpu.VMEM((tm, RET_W), BF16),
        pltpu.VMEM((tm, RET_W), BF16),
        pltpu.VMEM((tm, RET_W), BF16),
        pltpu.VMEM((tm, RET_W), BF16),
        pltpu.VMEM((tm, RET_W), BF16),
        pltpu.VMEM((tm, RET_W), F32),
        pltpu.VMEM((tm, CONV_W + 2 * XA_W), F32),
        pltpu.VMEM((tm, D_MIX), BF16),
        pltpu.VMEM((CONV_W // LANES, SUBLANES * n_tok, LANES), F32),
        pltpu.VMEM((CONV_W // LANES, SUBLANES * n_tok, LANES), F32),
    ]
    return pl.pallas_call(
        functools.partial(_layer_kernel, layer, final, n_tiles, n_alias, n_tok, n_seq, layout),
        grid=(batch, n_tiles),
        in_specs=in_specs,
        out_specs=out_specs,
        out_shape=out_shape,
        scratch_shapes=scratch,
        input_output_aliases={n_fixed + k: state_outputs[k] for k in range(n_alias)},
        compiler_params=_params(dimension_semantics=("arbitrary", "arbitrary")),
        name=f"layer{layer}",
    )(cd, x, cos_sin, mk, mv, ng, win, gn, cw, cb, lng, lnb, xan, wout, tables, fng,
      z_s, z_s, s0, cbuf, mkt_s, mvt_s, *prev_states)


def _stream_project(layer, prepare, hb_scr, win_hbm, wout_hbm, z_hbm, winb_hbm, woutb_ref,
                    w_buf, z_buf, wb_buf, wo_buf, w_sem, z_sem, wb_sem, wo_sem):
    n_chunks = D_IN // STREAM_COLS
    ring = STREAM_BUFFERS
    cols = lambda j: pl.ds(STREAM_COLS * j, STREAM_COLS)
    fetch = lambda j: pltpu.make_async_copy(
        win_hbm.at[layer, :, cols(j)], w_buf.at[j % ring], w_sem.at[j % ring])
    put_z = lambda j: pltpu.make_async_copy(
        z_buf.at[j % 2], z_hbm.at[:, cols(j)], z_sem.at[j % 2])
    put_w = lambda j: pltpu.make_async_copy(
        wb_buf.at[j % 2], winb_hbm.at[:, cols(j)], wb_sem.at[j % 2])
    fetch_wout = pltpu.make_async_copy(wout_hbm.at[layer], wo_buf, wo_sem.at[0])

    for j in range(min(ring, n_chunks)):
        fetch(j).start()
    fetch_wout.start()
    prepare()
    for j in range(n_chunks):
        fetch(j).wait()
        if j >= 2:
            put_z(j - 2).wait()
            put_w(j - 2).wait()
        w = w_buf[j % ring].astype(BF16)
        wb_buf[j % 2] = w
        z_buf[j % 2] = _dot(hb_scr[...], w)
        put_w(j).start()
        put_z(j).start()
        if j + ring < n_chunks:
            fetch(j + ring).start()
        if j == n_chunks // 2:
            fetch_wout.wait()
            woutb_ref[...] = wo_buf[...].astype(BF16)
    for j in range(max(n_chunks - 2, 0), n_chunks):
        put_z(j).wait()
        put_w(j).wait()


def _stream_call(kernel, name, n, operands, extra_out_specs, extra_out_shape, win, wout):
    whole = lambda a: pl.BlockSpec(a.shape, lambda j: (0,) * a.ndim)
    in_hbm = pl.BlockSpec(memory_space=pl.ANY)
    return pl.pallas_call(
        kernel,
        grid=(1,),
        in_specs=[whole(a) for a in operands] + [in_hbm, in_hbm],
        out_specs=extra_out_specs + [in_hbm, in_hbm,
                                     pl.BlockSpec((D_MIX, D_MODEL), lambda j: (0, 0))],
        out_shape=extra_out_shape + [jax.ShapeDtypeStruct((n, D_IN), F32),
                                     jax.ShapeDtypeStruct((D_MODEL, D_IN), BF16),
                                     jax.ShapeDtypeStruct((D_MIX, D_MODEL), BF16)],
        scratch_shapes=[
            pltpu.VMEM((n, D_MODEL), BF16),
            pltpu.VMEM((STREAM_BUFFERS, D_MODEL, STREAM_COLS), F32),
            pltpu.VMEM((2, n, STREAM_COLS), F32),
            pltpu.VMEM((2, D_MODEL, STREAM_COLS), BF16),
            pltpu.VMEM((D_MIX, D_MODEL), F32),
            pltpu.SemaphoreType.DMA((STREAM_BUFFERS,)),
            pltpu.SemaphoreType.DMA((2,)),
            pltpu.SemaphoreType.DMA((2,)),
            pltpu.SemaphoreType.DMA((1,)),
        ],
        compiler_params=_params(dimension_semantics=("arbitrary",)),
        name=name,
    )(*operands, win, wout)


def _sample_in_kernel(layer, x_ref, ng_ref, win_hbm, wout_hbm, z_hbm, winb_hbm, woutb_ref,
                      hb_scr, *stream):
    def prepare():
        x = x_ref[...].reshape(-1, D_MODEL)
        hb_scr[...] = _rms(x, _layer_row(ng_ref, layer)).astype(BF16)

    _stream_project(layer, prepare, hb_scr, win_hbm, wout_hbm, z_hbm, winb_hbm, woutb_ref,
                    *stream)


def _sample_in(layer, x, ng, win, wout):
    return _stream_call(functools.partial(_sample_in_kernel, layer), "sample_in",
                        x.size // D_MODEL, [x, ng], [], [], win, wout)


def _sample_mid_kernel(layer, x_ref, mix_ref, woutb_prev_ref, ng_ref, win_hbm, wout_hbm,
                       h_ref, z_hbm, winb_hbm, woutb_ref, hb_scr, *stream):
    def prepare():
        h = x_ref[...].reshape(-1, D_MODEL) + _dot(mix_ref[...], woutb_prev_ref[...])
        h_ref[...] = h
        hb_scr[...] = _rms(h, _layer_row(ng_ref, layer + 1)).astype(BF16)

    _stream_project(layer + 1, prepare, hb_scr, win_hbm, wout_hbm, z_hbm, winb_hbm, woutb_ref,
                    *stream)


def _sample_mid(layer, x, mix, woutb_prev, ng, win, wout):
    n = x.size // D_MODEL
    return _stream_call(functools.partial(_sample_mid_kernel, layer), "sample_mid", n,
                        [x, mix, woutb_prev, ng],
                        [pl.BlockSpec((n, D_MODEL), lambda j: (0, 0))],
                        [jax.ShapeDtypeStruct((n, D_MODEL), F32)], win, wout)


def _sample_out_kernel(x_ref, mix_ref, wout_ref, fng_ref, y_ref):
    x = x_ref[...].reshape(-1, D_MODEL)
    y = _rms(x + _dot(mix_ref[...], wout_ref[...]), fng_ref[...])
    y_ref[...] = y.reshape(y_ref.shape)


def _sample_out(x, mix, wout, fng, out_shape):
    whole = lambda a: pl.BlockSpec(a.shape, lambda j: (0,) * a.ndim)
    return pl.pallas_call(
        _sample_out_kernel,
        grid=(1,),
        in_specs=[whole(x), whole(mix), whole(wout), whole(fng)],
        out_specs=pl.BlockSpec(out_shape, lambda j: (0,) * len(out_shape)),
        out_shape=jax.ShapeDtypeStruct(out_shape, F32),
        compiler_params=_params(dimension_semantics=("arbitrary",)),
        name="sample_out",
    )(x, mix, wout, fng)


def _sample_mix_stages(layer, n_tok, n_seq, own_row0,
                       cd_ref, z_ref, zc_ref, s0_ref, cbuf_ref, mkt_ref, mvt_ref, cos_ref,
                       sin_ref, gn_ref, cw_ref, cb_ref, lng_ref, lnb_ref, xan_ref,
                       dec_ref, wq_ref, wk_ref, mix_ref, sret_ref, sconv_ref, us_scr, cs_scr):
    grp = SUBLANES // n_tok
    rows = grp * n_tok
    cosf = cos_ref[...]
    sins = sin_ref[...]
    head = _head_lane_ids()
    row_id = lax.broadcasted_iota(jnp.int32, (rows, 1), 0)
    row_seq = row_id // n_tok
    row4_seq = lax.broadcasted_iota(jnp.int32, (XA_HEADS * rows, 1), 0) % rows // n_tok

    def pick(parts, seq_of_row):
        out = parts[0]
        for s in range(1, grp):
            out = jnp.where(seq_of_row == s, parts[s], out)
        return out

    groups = range(n_seq // grp)
    heads = range(RET_HEADS)
    seqs = [[g * grp + s for s in range(grp)] for g in groups]
    rs = [slice(rows * g, rows * (g + 1)) for g in groups]
    hcol = lambda c0, hh: slice(c0 + RET_DK * hh, c0 + RET_DK * (hh + 1))

    qh, kf, vh, sc, xsc = {}, {}, {}, {}, {}
    inner, cross, xo, xsum = {}, {}, {}, {}

    def first_matmuls():
        for g in groups:
            for hh in heads:
                qh[g, hh] = _rope(z_ref[rs[g], hcol(C_RQ, hh)], cosf, sins).astype(BF16)
                kf[g, hh] = _rope(z_ref[rs[g], hcol(C_RK, hh)], cosf, sins) * (RET_DK ** -0.5)
                vh[g, hh] = z_ref[rs[g], hcol(C_RV, hh)].astype(BF16)
                sc[g, hh] = _dot_nt(qh[g, hh], kf[g, hh].astype(BF16))
            xq = z_ref[rs[g], C_XQ:C_XQ + XA_W] * (LOG2_E * XA_DH ** -0.5)
            q4 = jnp.concatenate([jnp.where(head == hh, xq, 0.0) for hh in range(XA_HEADS)],
                                 axis=0).astype(BF16)
            xsc[g] = [_dot(q4, mkt_ref[0, b].astype(BF16)) for b in seqs[g]]

    def conv_module():
        pieces = range(CONV_W // LANES)
        lanes = lambda p: slice(LANES * p, LANES * (p + 1))
        u_rows = zc_ref[:, 0:CONV_W] * _sigmoid(zc_ref[:, CONV_W:2 * CONV_W])
        for p in pieces:
            us_scr[p] = u_rows[:, lanes(p)]
        u = []
        for i in range(n_tok):
            tok = pl.ds(i, SUBLANES, stride=n_tok)
            u.append(jnp.concatenate([us_scr[p, tok, :] for p in pieces], axis=-1))
        window = lambda k: cbuf_ref[k] if k < CONV_HIST else u[k - CONV_HIST]
        acc = [_layer_row(cb_ref, layer)] * n_tok
        for k in range(CONV_HIST + n_tok):
            w_k = window(k)
            for i in range(n_tok):
                if 0 <= k - i < CONV_K:
                    acc[i] = acc[i] + w_k * cw_ref[k - i:k - i + 1, :]
        for i in range(n_tok):
            for p in pieces:
                cs_scr[p, pl.ds(i, SUBLANES, stride=n_tok), :] = acc[i][:, lanes(p)]
        for k in range(CONV_HIST):
            sconv_ref[k] = window(k + n_tok)
        for g in groups:
            own = pl.ds(pl.multiple_of(own_row0 + rows * g, rows), rows)
            c = jnp.concatenate([cs_scr[p, own, :] for p in pieces], axis=-1)
            cn = _standardize(c) * _layer_row(lng_ref, layer) + _layer_row(lnb_ref, layer)
            gate = _silu(z_ref[rs[g], C_CG:C_CG + CONV_W])
            mix_ref[rs[g], RET_W:RET_W + CONV_W] = (_silu(cn) * gate).astype(BF16)

    def second_matmuls():
        for g in groups:
            for hh in heads:
                inner[g, hh] = _dot((sc[g, hh] * dec_ref[hh]).astype(BF16), vh[g, hh])
                kw = kf[g, hh] * wk_ref[hh]
                parts = []
                for s, b in enumerate(seqs[g]):
                    s_prev = s0_ref[0, b, hh]
                    parts.append(_dot(qh[g, hh], s_prev.astype(BF16)))
                    kv = _dot_tn(jnp.where(row_seq == s, kw, 0.0).astype(BF16), vh[g, hh])
                    sret_ref[0, b, hh] = cd_ref[hh] * s_prev + kv
                cross[g, hh] = pick(parts, row_seq) * wq_ref[hh]
            s4 = pick(xsc[g], row4_seq)
            e = jnp.exp2(s4 - jnp.max(s4, axis=-1, keepdims=True))
            xsum[g] = jnp.sum(e, axis=-1, keepdims=True)
            eb = e.astype(BF16)
            xo[g] = [_dot_nt(eb, mvt_ref[0, b].astype(BF16)) for b in seqs[g]]

    def norms_and_stores():
        for g in groups:
            for hh in heads:
                co = hcol(0, hh)
                o = _standardize(inner[g, hh] + cross[g, hh]) * gn_ref[layer:layer + 1, co]
                gate = _silu(z_ref[rs[g], hcol(C_RG, hh)])
                mix_ref[rs[g], co] = (o * gate).astype(BF16)
            o4 = pick(xo[g], row4_seq) / xsum[g]
            a = jnp.zeros((rows, XA_W), F32)
            for hh in range(XA_HEADS):
                a = jnp.where(head == hh, o4[rows * hh:rows * (hh + 1), :], a)
            gate = _silu(z_ref[rs[g], C_XG:C_XG + XA_W])
            mix_ref[rs[g], RET_W + CONV_W:D_MIX] = _xattn_norm_gate(
                a, _layer_row(xan_ref, layer), gate, rows).astype(BF16)

    return [first_matmuls, conv_module, second_matmuls, norms_and_stores]


def _rope_tables(pos):
    half = RET_DK // 2
    inv = np.float64(ROPE_BASE) ** (-np.arange(half, dtype=np.float64) / half)
    ang = pos.astype(np.float64)[:, None] * inv[None, :]
    cos, sin = np.cos(ang), np.sin(ang)
    return (np.concatenate([cos, cos], axis=-1).astype(np.float32),
            np.concatenate([-sin, sin], axis=-1).astype(np.float32))


def _decay_tables(chunk):
    lg = np.log(1.0 - np.exp2(-5.0 - np.arange(RET_HEADS, dtype=np.float64)))
    idx = np.arange(chunk, dtype=np.float64)
    diff = idx[:, None] - idx[None, :]
    dec = np.where(diff[None] >= 0, np.exp(np.maximum(diff, 0.0)[None] * lg[:, None, None]), 0.0)
    wk = np.exp((chunk - 1.0 - idx)[None, :] * lg[:, None])
    wq = np.exp((idx + 1.0)[None, :] * lg[:, None])
    cd = np.exp(chunk * lg)
    wk = np.broadcast_to(wk[:, :, None], (RET_HEADS, chunk, RET_DK))
    wq = np.broadcast_to(wq[:, :, None], (RET_HEADS, chunk, RET_DV))
    f32 = lambda a: np.ascontiguousarray(a, dtype=np.float32)
    return f32(dec), f32(wq), f32(wk), f32(cd)


def _group_tables(n_tok, pos0):
    grp = SUBLANES // n_tok
    cosf, sins = _rope_tables(pos0 + np.arange(n_tok))
    dec, wq, wk, cd = _decay_tables(n_tok)
    tile_rows = lambda a: np.concatenate([a] * grp, axis=-2)
    eye = np.eye(grp, dtype=np.float32)
    dec = np.einsum("st,hij->hsitj", eye, dec).reshape(RET_HEADS, grp * n_tok, grp * n_tok)
    return tile_rows(cosf), tile_rows(sins), dec, tile_rows(wq), tile_rows(wk), cd


def kernel(x_prompt, x_sample, mem_prompt, state_ret, state_conv, cache_mem_k, cache_mem_v,
           norm_g, w_in, ret_gn_g, conv_w, conv_b, conv_ln_g, conv_ln_b, xa_norm_g,
           mem_norm_g, w_mk, w_mv, w_out, final_norm_g):
    depth = w_in.shape[0]
    batch, seq, _ = x_prompt.shape
    dbatch, dseq, _ = x_sample.shape

    fng = final_norm_g.reshape(1, D_MODEL)

    p_mk, p_mv = _mem_kv(mem_prompt, mem_norm_g, w_mk, w_mv)
    cos_p, sin_p = _rope_tables(np.arange(seq))
    dec_p, wq_p, wk_p, cd_p = _decay_tables(RET_CHUNK if seq % RET_CHUNK == 0 else seq)

    assert SUBLANES % dseq == 0 and dseq % RET_CHUNK != 0
    cos_s, sin_s, dec_s, wq_s, wk_s, cd_s = _group_tables(dseq, PAST_LEN)
    tables, layout = _pack_tables([
        ("dec", dec_p), ("wq", wq_p), ("wk", wk_p), ("sample cos", cos_s[None]),
        ("sample sin", sin_s[None]), ("sample dec", dec_s), ("sample wq", wq_s),
        ("sample wk", wk_s)])
    cos_sin = np.concatenate([cos_p, sin_p], axis=-1)
    cd = np.concatenate([cd_p, cd_s])
    to_hd_m = lambda c: c.transpose(0, 1, 3, 4, 2).reshape(depth, dbatch, XA_W, N_MEM)
    mkt_s, mvt_s = to_hd_m(cache_mem_k), to_hd_m(cache_mem_v)
    conv_s = state_conv.transpose(0, 2, 1, 3)
    hs = x_sample
    z, w_in_b, w_out_b = _sample_in(0, hs, norm_g, w_in, w_out)

    hp = x_prompt
    states = ()
    y_sample = None
    for l in range(depth):
        hp, p_ret, p_conv, mix, s_ret, s_conv = _layer(
            l, l == depth - 1, dseq, hp, cos_sin, p_mk, p_mv, norm_g, w_in_b, ret_gn_g,
            conv_w, conv_b, conv_ln_g, conv_ln_b, xa_norm_g, w_out_b, tables, layout, cd,
            fng, z, state_ret, conv_s, mkt_s, mvt_s, states)
        states = (p_ret, p_conv, s_ret, s_conv)
        if l + 1 < depth:
            hs, z, w_in_b, w_out_b = _sample_mid(l, hs, mix, w_out_b, norm_g, w_in, w_out)
        else:
            y_sample = _sample_out(hs, mix, w_out_b, fng, x_sample.shape)
    y_prompt = hp

    from_hd_m = lambda c: c.reshape(depth, batch, XA_HEADS, XA_DH, N_MEM).transpose(0, 1, 4, 2, 3)
    return (y_prompt, y_sample, states[0], states[1],
            from_hd_m(p_mk), from_hd_m(p_mv), states[2], states[3].transpose(0, 2, 1, 3))
```
